```python
import jax
import jax.numpy as jnp
from jax import lax
import numpy as np

D_MODEL = 1024
BATCH = 16
SEQ = 4096
DEPTH = 1

D_MIX = D_MODEL
NORM_EPS = 1e-6

RWKV_HEADS = 8
RWKV_HEAD_DIM = 64
D_RWKV = RWKV_HEADS * RWKV_HEAD_DIM
DECAY_LORA = 32
ICLR_LORA = 32
GATE_LORA = 96
GN_EPS = 64e-5
RWKV_IN_SIZES = (D_RWKV, D_RWKV, D_RWKV, DECAY_LORA, ICLR_LORA, GATE_LORA)
D_RWKV_IN = sum(RWKV_IN_SIZES)

MLA_HEADS = 8
QK_NOPE_DIM = 64
QK_ROPE_DIM = 32
QK_HEAD_DIM = QK_NOPE_DIM + QK_ROPE_DIM
V_HEAD_DIM = 64
D_MLA = MLA_HEADS * V_HEAD_DIM
Q_LORA_RANK = 256
KV_LORA_RANK = 128
ROPE_THETA = 10000.0
Q_BLOCK = 128
MAX_POS_OFFSET = 4096
D_MLA_IN = Q_LORA_RANK + KV_LORA_RANK + QK_ROPE_DIM

D_IN = D_RWKV_IN + D_MLA_IN

N_EXPERTS = 256
TOP_K = 8
N_GROUPS = 8
TOPK_GROUPS = 4
D_EXPERT = 256
D_SHARED = 256
ROUTED_SCALE = 2.5
MOE_BLOCK = 256

kernel_name = 'hybrid_rwkv7_mla_moe_adaln'


def rms_norm(x, gain):
    xf = x.astype(jnp.float32)
    y = xf * lax.rsqrt(jnp.mean(xf * xf, axis=-1, keepdims=True) + NORM_EPS)
    return (y * gain.astype(jnp.float32)).astype(x.dtype)


def swiglu(gu):
    g, u = jnp.split(gu, 2, axis=-1)
    return jax.nn.silu(g) * u


def token_shift(u, mu):
    prev = jnp.pad(u, ((0, 0), (1, 0), (0, 0)))[:, :-1]
    return u + (prev - u) * mu


def rope_tables(positions):
    inv_freq = ROPE_THETA ** (-jnp.arange(0, QK_ROPE_DIM, 2, dtype=jnp.float32) / QK_ROPE_DIM)
    ang = positions.astype(jnp.float32)[..., None] * inv_freq
    return jnp.cos(ang)[:, :, None, :], jnp.sin(ang)[:, :, None, :]


def apply_rope_tail(x, cos, sin):
    x_nope, x_pe = jnp.split(x.astype(jnp.float32), [QK_NOPE_DIM], axis=-1)
    x1, x2 = jnp.split(x_pe, 2, axis=-1)
    x_pe = jnp.concatenate([x1 * cos - x2 * sin, x2 * cos + x1 * sin], axis=-1)
    return jnp.concatenate([x_nope, x_pe], axis=-1).astype(x.dtype)


def rwkv7_scan(r, decay, k, v, kk, a):
    B, T, H, N = r.shape

    def step(S, inp):
        r_t, w_t, k_t, v_t, kk_t, a_t = inp
        sa = jnp.einsum('bhvk,bhk->bhv', S, -kk_t)
        S = (S * w_t[:, :, None, :]
             + sa[..., None] * (kk_t * a_t)[:, :, None, :]
             + v_t[..., None] * k_t[:, :, None, :])
        y = jnp.einsum('bhvk,bhk->bhv', S, r_t)
        return S, y

    xs = tuple(jnp.moveaxis(t, 1, 0) for t in (r, decay, k, v, kk, a))
    S0 = jnp.zeros((B, H, N, N), jnp.float32)
    _, ys = lax.scan(step, S0, xs)
    return jnp.moveaxis(ys, 0, 1)


def rwkv7_group(u, mu, decay_w0, decay_up, iclr_a0, iclr_up, gate_up, k_k, k_a, r_k, ln_w, ln_b):
    B, T, _ = u.shape
    H, N = RWKV_HEADS, RWKV_HEAD_DIM
    out_dtype = u.dtype
    u = token_shift(u.astype(jnp.float32), mu.astype(jnp.float32))
    r, k, v, xw, xa, xg = jnp.split(u, np.cumsum(RWKV_IN_SIZES)[:-1].tolist(), axis=-1)
    log_w = -jax.nn.softplus(-(decay_w0 + jnp.tanh(xw) @ decay_up)) - 0.5
    decay = jnp.exp(-jnp.exp(log_w))
    a = jax.nn.sigmoid(iclr_a0 + xa @ iclr_up)
    g = jax.nn.sigmoid(xg) @ gate_up
    heads = lambda t: t.reshape(B, T, H, N)
    kk = heads(k * k_k)
    kk = kk * lax.rsqrt(jnp.maximum(jnp.sum(kk * kk, axis=-1, keepdims=True), 1e-24))
    k = k * (1.0 + (a - 1.0) * k_a)
    r, k, v, decay, a = heads(r), heads(k), heads(v), heads(decay), heads(a)
    y = rwkv7_scan(r, decay, k, v, kk, a)
    mean = jnp.mean(y, axis=-1, keepdims=True)
    var = jnp.mean(jnp.square(y - mean), axis=-1, keepdims=True)
    y = ((y - mean) * lax.rsqrt(var + GN_EPS)).reshape(B, T, D_RWKV) * ln_w + ln_b
    bonus = jnp.sum(r * k * r_k, axis=-1, keepdims=True) * v
    y = (y + bonus.reshape(B, T, D_RWKV)) * g
    return y.astype(out_dtype)


def causal_block_attention(q, k, v):
    T = q.shape[1]
    scale = QK_HEAD_DIM ** -0.5
    outs = []
    for start in range(0, T, Q_BLOCK):
        end = min(start + Q_BLOCK, T)
        s = jnp.einsum('bqhd,bkhd->bhqk', q[:, start:end], k[:, :end]).astype(jnp.float32) * scale
        q_pos = jnp.arange(start, end)[:, None]
        k_pos = jnp.arange(end)[None, :]
        s = jnp.where(k_pos <= q_pos, s, -jnp.inf)
        p = jax.nn.softmax(s, axis=-1).astype(v.dtype)
        outs.append(jnp.einsum('bhqk,bkhd->bqhd', p, v[:, :end]))
    return jnp.concatenate(outs, axis=1)


def mla_group(u, cos, sin, q_a_norm, w_q_b, kv_a_norm, w_kv_b, q_norm, k_norm):
    B, T, _ = u.shape
    H = MLA_HEADS
    out_dtype = u.dtype
    q_lat, kv_lat, k_pe = jnp.split(u, [Q_LORA_RANK, Q_LORA_RANK + KV_LORA_RANK], axis=-1)
    q = (rms_norm(q_lat, q_a_norm) @ w_q_b).reshape(B, T, H, QK_HEAD_DIM)
    kv = (rms_norm(kv_lat, kv_a_norm) @ w_kv_b).reshape(B, T, H, QK_NOPE_DIM + V_HEAD_DIM)
    k_nope, v = jnp.split(kv, [QK_NOPE_DIM], axis=-1)
    k = jnp.concatenate([k_nope, jnp.broadcast_to(k_pe[:, :, None, :], (B, T, H, QK_ROPE_DIM))], axis=-1)
    q = apply_rope_tail(rms_norm(q, q_norm), cos, sin)
    k = apply_rope_tail(rms_norm(k, k_norm), cos, sin)
    o = causal_block_attention(q, k, v)
    return o.reshape(B, T, D_MLA).astype(out_dtype)


def moe_ffn(h, w_router, router_bias, w_e_gate_up, w_e_down, w_sh_gate_up, w_sh_down):
    B, T, D = h.shape
    N = B * T
    NK = N * TOP_K
    xt = h.reshape(N, D)
    scores = jax.nn.sigmoid((xt @ w_router).astype(jnp.float32))
    sel = scores + router_bias.astype(jnp.float32)
    grp = sel.reshape(N, N_GROUPS, N_EXPERTS // N_GROUPS)
    grp_score = jnp.sum(lax.top_k(grp, 2)[0], axis=-1)
    _, top_grp = lax.top_k(grp_score, TOPK_GROUPS)
    grp_mask = jnp.any(top_grp[..., None] == jnp.arange(N_GROUPS), axis=1)
    exp_mask = jnp.repeat(grp_mask, N_EXPERTS // N_GROUPS, axis=-1)
    _, top_e = lax.top_k(jnp.where(exp_mask, sel, -jnp.inf), TOP_K)
    w = jnp.take_along_axis(scores, top_e, axis=-1)
    w = w / jnp.sum(w, axis=-1, keepdims=True) * ROUTED_SCALE

    e_flat = top_e.reshape(NK).astype(jnp.int32)
    tok_flat = jnp.repeat(jnp.arange(N, dtype=jnp.int32), TOP_K)
    w_flat = w.reshape(NK)
    order = jnp.argsort(e_flat, stable=True)
    e_sorted, tok_sorted, w_sorted = e_flat[order], tok_flat[order], w_flat[order]
    counts = jnp.bincount(e_flat, length=N_EXPERTS).astype(jnp.int32)
    starts = jnp.cumsum(counts) - counts
    padded = (counts + MOE_BLOCK - 1) // MOE_BLOCK * MOE_BLOCK
    pad_ends = jnp.cumsum(padded)
    pad_starts = pad_ends - padded
    dest = pad_starts[e_sorted] + (jnp.arange(NK, dtype=jnp.int32) - starts[e_sorted])
    n_blocks = (NK + N_EXPERTS * (MOE_BLOCK - 1)) // MOE_BLOCK
    P = n_blocks * MOE_BLOCK
    buf_tok = jnp.full((P,), N, jnp.int32).at[dest].set(tok_sorted)
    buf_w = jnp.zeros((P,), jnp.float32).at[dest].set(w_sorted)
    block_expert = jnp.minimum(
        jnp.searchsorted(pad_ends, jnp.arange(n_blocks, dtype=jnp.int32) * MOE_BLOCK, side='right'),
        N_EXPERTS - 1).astype(jnp.int32)

    x_pad = jnp.concatenate([xt, jnp.zeros((1, D), xt.dtype)], axis=0)

    def body(acc, blk):
        idx, wts, e = blk
        xb = x_pad[idx]
        yb = swiglu(xb @ w_e_gate_up[e]) @ w_e_down[e]
        return acc.at[idx].add(yb.astype(jnp.float32) * wts[:, None]), None

    acc0 = jnp.zeros((N + 1, D), jnp.float32)
    acc, _ = lax.scan(body, acc0, (buf_tok.reshape(n_blocks, MOE_BLOCK),
                                   buf_w.reshape(n_blocks, MOE_BLOCK), block_expert))
    shared = swiglu(xt @ w_sh_gate_up) @ w_sh_down
    return (acc[:N].astype(h.dtype) + shared).reshape(B, T, D)


def hybrid_layer(x, c_act, cos, sin, ada_w, ada_b, norm_mix, w_in, rwkv_mu, decay_w0, decay_up,
                 iclr_a0, iclr_up, gate_up, rwkv_k_k, rwkv_k_a, rwkv_r_k, ln_x_w, ln_x_b,
                 q_a_norm, w_q_b, kv_a_norm, w_kv_b, q_norm, k_norm, w_out, norm_ffn,
                 w_router, router_bias, w_e_gate_up, w_e_down, w_sh_gate_up, w_sh_down):
    mod = c_act @ ada_w + ada_b
    sh_a, sc_a, g_a, sh_f, sc_f, g_f = [m[:, None, :] for m in jnp.split(mod, 6, axis=-1)]
    h = rms_norm(x, norm_mix) * (1.0 + sc_a) + sh_a
    u = h @ w_in
    y_r = rwkv7_group(u[..., :D_RWKV_IN], rwkv_mu, decay_w0, decay_up, iclr_a0, iclr_up, gate_up,
                      rwkv_k_k, rwkv_k_a, rwkv_r_k, ln_x_w, ln_x_b)
    y_m = mla_group(u[..., D_RWKV_IN:], cos, sin, q_a_norm, w_q_b, kv_a_norm, w_kv_b,
                    q_norm, k_norm)
    x = x + g_a * (jnp.concatenate([y_r, y_m], axis=-1) @ w_out)
    h = rms_norm(x, norm_ffn) * (1.0 + sc_f) + sh_f
    x = x + g_f * moe_ffn(h, w_router, router_bias, w_e_gate_up, w_e_down, w_sh_gate_up, w_sh_down)
    return x


def setup_inputs(seed: int = 0) -> dict:
    key = jax.random.key(seed)
    ks = iter(jax.random.split(key, 40))
    f32 = jnp.float32
    L, D = DEPTH, D_MODEL

    def nrm(shape, scale):
        return jax.random.normal(next(ks), shape, f32) * scale

    def gain(shape):
        return 1.0 + nrm(shape, 0.02)

    x = nrm((BATCH, SEQ, D), 1.0)
    c = nrm((BATCH, D), 1.0)
    start = jax.random.randint(next(ks), (BATCH, 1), 0, MAX_POS_OFFSET, jnp.int32)
    positions = start + jnp.arange(SEQ, dtype=jnp.int32)[None, :]
    return {
        'x': x,
        'c': c,
        'positions': positions,
        'ada_w': nrm((L, D, 6 * D), 0.5 * D ** -0.5),
        'ada_b': nrm((L, 6 * D), 0.02),
        'norm_mix': gain((L, D)),
        'w_in': nrm((L, D, D_IN), D ** -0.5),
        'rwkv_mu': jax.random.uniform(next(ks), (L, D_RWKV_IN), f32),
        'decay_w0': jax.random.uniform(next(ks), (L, D_RWKV), f32, minval=-5.0, maxval=1.0),
        'decay_up': nrm((L, DECAY_LORA, D_RWKV), DECAY_LORA ** -0.5),
        'iclr_a0': nrm((L, D_RWKV), 0.1),
        'iclr_up': nrm((L, ICLR_LORA, D_RWKV), ICLR_LORA ** -0.5),
        'gate_up': nrm((L, GATE_LORA, D_RWKV), GATE_LORA ** -0.5),
        'rwkv_k_k': 0.85 + nrm((L, D_RWKV), 0.05),
        'rwkv_k_a': 1.0 + nrm((L, D_RWKV), 0.05),
        'rwkv_r_k': nrm((L, RWKV_HEADS, RWKV_HEAD_DIM), 0.1),
        'ln_x_w': gain((L, D_RWKV)),
        'ln_x_b': nrm((L, D_RWKV), 0.02),
        'q_a_norm': gain((L, Q_LORA_RANK)),
        'w_q_b': nrm((L, Q_LORA_RANK, MLA_HEADS * QK_HEAD_DIM), Q_LORA_RANK ** -0.5),
        'kv_a_norm': gain((L, KV_LORA_RANK)),
        'w_kv_b': nrm((L, KV_LORA_RANK, MLA_HEADS * (QK_NOPE_DIM + V_HEAD_DIM)), KV_LORA_RANK ** -0.5),
        'q_norm': gain((L, QK_HEAD_DIM)),
        'k_norm': gain((L, QK_HEAD_DIM)),
        'w_out': nrm((L, D_MIX, D), D_MIX ** -0.5),
        'norm_ffn': gain((L, D)),
        'w_router': nrm((L, D, N_EXPERTS), D ** -0.5),
        'router_bias': nrm((L, N_EXPERTS), 0.01),
        'w_e_gate_up': nrm((L, N_EXPERTS, D, 2 * D_EXPERT), D ** -0.5),
        'w_e_down': nrm((L, N_EXPERTS, D_EXPERT, D), D_EXPERT ** -0.5),
        'w_sh_gate_up': nrm((L, D, 2 * D_SHARED), D ** -0.5),
        'w_sh_down': nrm((L, D_SHARED, D), D_SHARED ** -0.5),
    }


def reference(x, c, positions, ada_w, ada_b, norm_mix, w_in, rwkv_mu, decay_w0, decay_up,
              iclr_a0, iclr_up, gate_up, rwkv_k_k, rwkv_k_a, rwkv_r_k, ln_x_w, ln_x_b,
              q_a_norm, w_q_b, kv_a_norm, w_kv_b, q_norm, k_norm, w_out, norm_ffn,
              w_router, router_bias, w_e_gate_up, w_e_down, w_sh_gate_up, w_sh_down):
    cos, sin = rope_tables(positions)
    c_act = jax.nn.silu(c)
    for l in range(DEPTH):
        x = hybrid_layer(x, c_act, cos, sin, ada_w[l], ada_b[l], norm_mix[l], w_in[l], rwkv_mu[l],
                         decay_w0[l], decay_up[l], iclr_a0[l], iclr_up[l], gate_up[l],
                         rwkv_k_k[l], rwkv_k_a[l], rwkv_r_k[l], ln_x_w[l], ln_x_b[l],
                         q_a_norm[l], w_q_b[l], kv_a_norm[l], w_kv_b[l], q_norm[l], k_norm[l],
                         w_out[l], norm_ffn[l], w_router[l], router_bias[l], w_e_gate_up[l],
                         w_e_down[l], w_sh_gate_up[l], w_sh_down[l])
    return x
```

```python
import functools
import math

import jax
import jax.numpy as jnp
import numpy as np
from jax import lax
from jax.experimental import pallas as pl
from jax.experimental.pallas import tpu as pltpu

F32 = jnp.float32
BF16 = jnp.bfloat16
I32 = jnp.int32

NORM_EPS = 1e-6
GN_EPS = 64e-5
RWKV_HEADS = 8
RWKV_HEAD_DIM = 64
D_RWKV = 512
DECAY_LORA = 32
ICLR_LORA = 32
GATE_LORA = 96
MLA_HEADS = 8
QK_NOPE_DIM = 64
QK_ROPE_DIM = 32
QK_HEAD_DIM = 96
V_HEAD_DIM = 64
Q_LORA_RANK = 256
KV_LORA_RANK = 128
ROPE_THETA = 10000.0
N_EXPERTS = 256
TOP_K = 8
N_GROUPS = 8
TOPK_GROUPS = 4
GROUP_SIZE = N_EXPERTS // N_GROUPS
D_EXPERT = 256
ROUTED_SCALE = 2.5
MOE_BLOCK = 256

LANES = 128
HEAD_PAD = 128
VMEM_LIMIT = 56 * 1024 * 1024

SCAN_CHUNK = 64
NEG_INF = float("-inf")


def _cparams(*sem):
    return pltpu.CompilerParams(dimension_semantics=sem, vmem_limit_bytes=VMEM_LIMIT)


def _split2(a):
    hi = a.astype(BF16)
    lo = (a - hi.astype(F32)).astype(BF16)
    return hi, lo


def _split3(a):
    hi = a.astype(BF16)
    r1 = a - hi.astype(F32)
    mid = r1.astype(BF16)
    lo = (r1 - mid.astype(F32)).astype(BF16)
    return hi, mid, lo


def _dot(a, b, dims=None):
    if dims is None:
        return jnp.dot(a, b, preferred_element_type=F32)
    return lax.dot_general(a, b, (dims, ((), ())), preferred_element_type=F32)


def _mm(a, b, dims=None):
    return _dot(a.astype(BF16), b.astype(BF16), dims)


def _mm3(a, b, dims=None):
    ah, al = _split2(a)
    bh, bl = _split2(b)
    return _dot(ah, bh, dims) + (_dot(ah, bl, dims) + _dot(al, bh, dims))


def _mm_exact_rhs(a, b_exact_bf16, dims=None):
    h, m, l = _split3(a)
    return _dot(h, b_exact_bf16, dims) + (_dot(m, b_exact_bf16, dims) + _dot(l, b_exact_bf16, dims))


NT = ((1,), (1,))
TN = ((0,), (0,))


def _sigmoid(z):
    return 1.0 / (1.0 + jnp.exp(-z))


def _silu(z):
    return z * _sigmoid(z)


def _seg_ones(width, seg):
    r = lax.broadcasted_iota(I32, (width, width), 0) // seg
    c = lax.broadcasted_iota(I32, (width, width), 1) // seg
    return jnp.where(r == c, 1.0, 0.0).astype(BF16)


def _segsum(a, ones_bd):
    hi, lo = _split2(a)
    return _dot(hi, ones_bd) + _dot(lo, ones_bd)


def _mod_kernel(c_ref, w_ref, b_ref, o_ref):
    ca = _silu(c_ref[...])
    o_ref[...] = _mm3(ca, w_ref[...]) + b_ref[...]


def _mod_call(c, ada_w, ada_b):
    B, D = c.shape
    n6 = ada_w.shape[1]
    tn = D
    return pl.pallas_call(
        _mod_kernel,
        grid=(n6 // tn,),
        in_specs=[pl.BlockSpec((B, D), lambda j: (0, 0)),
                  pl.BlockSpec((D, tn), lambda j: (0, j)),
                  pl.BlockSpec((1, tn), lambda j: (0, j))],
        out_specs=pl.BlockSpec((B, tn), lambda j: (0, j)),
        out_shape=jax.ShapeDtypeStruct((B, n6), F32),
        compiler_params=_cparams("arbitrary"),
        name="mod",
    )(c, ada_w, ada_b.reshape(1, n6))


def _pre_kernel(x_ref, mod_ref, pos_ref, nmix_ref, wrkv_ref, wlora_ref, wmla_ref,
                mu_rkv_ref, mu_lora_ref, wup_ref, w0_ref, a0_ref, kk_ref, ka_ref, rk_ref,
                qan_ref, wqb_ref, kvan_ref, wkb_ref, wvb_ref, qn_ref, kn_ref, invf_ref,
                r_ref, lw_ref, k_ref, v_ref, kkn_ref, akk_ref, g_ref, bonus_ref,
                q_ref, kout_ref, vout_ref,
                carry_rkv, carry_lora):
    ti = pl.program_id(1)
    tm = x_ref.shape[1]

    @pl.when(ti == 0)
    def _():
        carry_rkv[...] = jnp.zeros_like(carry_rkv)
        carry_lora[...] = jnp.zeros_like(carry_lora)

    xb = x_ref[0]
    sh_a = mod_ref[0, 0:1, :]
    sc_a = mod_ref[0, 1:2, :]
    ms = jnp.mean(xb * xb, axis=-1, keepdims=True)
    h = xb * lax.rsqrt(ms + NORM_EPS) * nmix_ref[...] * (1.0 + sc_a) + sh_a
    hb = h.astype(BF16)
    u_rkv = _dot(hb, wrkv_ref[...])
    u_lora = _dot(hb, wlora_ref[...])
    u_mla = _dot(hb, wmla_ref[...])

    row0 = lax.broadcasted_iota(I32, (tm, 1), 0) == 0

    def shift(u, carry):
        prev = jnp.where(row0, carry[...], pltpu.roll(u, 1, 0))
        carry[...] = u[tm - 1:tm, :]
        return prev

    prev_rkv = shift(u_rkv, carry_rkv)
    prev_lora = shift(u_lora, carry_lora)
    us = u_rkv + (prev_rkv - u_rkv) * mu_rkv_ref[...]
    ul = u_lora + (prev_lora - u_lora) * mu_lora_ref[...]
    r = us[:, 0:D_RWKV]
    k = us[:, D_RWKV:2 * D_RWKV]
    v = us[:, 2 * D_RWKV:3 * D_RWKV]

    lane_l = lax.broadcasted_iota(I32, ul.shape, 1)
    t_in = jnp.where(lane_l < DECAY_LORA, jnp.tanh(ul),
                     jnp.where(lane_l < DECAY_LORA + ICLR_LORA, ul, _sigmoid(ul)))
    up = _mm(t_in, wup_ref[...])
    z = w0_ref[...] + up[:, 0:D_RWKV]
    lw = (-math.exp(-0.5)) * _sigmoid(z)
    a = _sigmoid(a0_ref[...] + up[:, D_RWKV:2 * D_RWKV])
    g = up[:, 2 * D_RWKV:3 * D_RWKV]

    ones64 = _seg_ones(D_RWKV, RWKV_HEAD_DIM)
    kk = k * kk_ref[...]
    ss = _segsum(kk * kk, ones64)
    kk = kk * lax.rsqrt(jnp.maximum(ss, 1e-24))
    k2 = k * (1.0 + (a - 1.0) * ka_ref[...])
    bonus = _segsum(r * k2 * rk_ref[...], ones64) * v

    r_ref[0] = r
    lw_ref[0] = lw
    k_ref[0] = k2
    v_ref[0] = v
    kkn_ref[0] = kk
    akk_ref[0] = a * kk
    g_ref[0] = g
    bonus_ref[0] = bonus

    q_lat = u_mla[:, 0:Q_LORA_RANK]
    kv_lat = u_mla[:, Q_LORA_RANK:Q_LORA_RANK + KV_LORA_RANK]
    kpe_tile = u_mla[:, Q_LORA_RANK + KV_LORA_RANK:]
    qn = q_lat * lax.rsqrt(jnp.mean(q_lat * q_lat, axis=-1, keepdims=True) + NORM_EPS) * qan_ref[...]
    kvn = kv_lat * lax.rsqrt(jnp.mean(kv_lat * kv_lat, axis=-1, keepdims=True) + NORM_EPS) * kvan_ref[...]
    q_raw = _mm(qn, wqb_ref[...])
    kvb = kvn.astype(BF16)
    k_raw = _dot(kvb, wkb_ref[...])
    v_pad = _dot(kvb, wvb_ref[...])
    kpe_h = pltpu.roll(kpe_tile, QK_NOPE_DIM, 1)

    lane = lax.broadcasted_iota(I32, (tm, HEAD_PAD), 1)
    ang = pos_ref[0].astype(F32) * invf_ref[...]
    in_x1 = (lane >= QK_NOPE_DIM) & (lane < QK_NOPE_DIM + QK_ROPE_DIM // 2)
    in_x2 = (lane >= QK_NOPE_DIM + QK_ROPE_DIM // 2) & (lane < QK_HEAD_DIM)
    cos_t = jnp.cos(ang)
    sin_t = jnp.sin(ang)
    s1 = jnp.where(in_x1, -sin_t, 0.0)
    s2 = jnp.where(in_x2, sin_t, 0.0)
    half = QK_ROPE_DIM // 2

    def norm_rope(xh, gain):
        ssq = jnp.sum(xh * xh, axis=-1, keepdims=True) * (1.0 / QK_HEAD_DIM)
        xh = xh * lax.rsqrt(ssq + NORM_EPS) * gain
        return xh * cos_t + pltpu.roll(xh, HEAD_PAD - half, 1) * s1 + pltpu.roll(xh, half, 1) * s2

    for hh in range(MLA_HEADS):
        sl = slice(hh * HEAD_PAD, (hh + 1) * HEAD_PAD)
        q_ref[0, :, sl] = norm_rope(q_raw[:, sl], qn_ref[...]).astype(BF16)
        kout_ref[0, :, sl] = norm_rope(k_raw[:, sl] + kpe_h, kn_ref[...]).astype(BF16)
    vout_ref[0] = v_pad.astype(BF16)


def _pad_heads(w, n_heads, width):
    kdim = w.shape[0]
    w = w.reshape(kdim, n_heads, width)
    w = jnp.pad(w, ((0, 0), (0, 0), (0, HEAD_PAD - width)))
    return w.reshape(kdim, n_heads * HEAD_PAD)


def _pre_call(x, mod3, positions, norm_mix, w_in, rwkv_mu, decay_w0, decay_up, iclr_a0, iclr_up,
              gate_up, k_k, k_a, r_k, q_a_norm, w_q_b, kv_a_norm, w_kv_b, q_norm, k_norm, tm):
    B, T, D = x.shape
    n_rkv = 3 * D_RWKV
    n_lora = DECAY_LORA + ICLR_LORA + GATE_LORA
    LORA_PAD = 256
    MLA_PAD = 512
    n_mla = Q_LORA_RANK + KV_LORA_RANK + QK_ROPE_DIM
    w_rkv = w_in[:, :n_rkv].astype(BF16)
    w_lora = jnp.pad(w_in[:, n_rkv:n_rkv + n_lora], ((0, 0), (0, LORA_PAD - n_lora))).astype(BF16)
    w_mla = jnp.pad(w_in[:, n_rkv + n_lora:], ((0, 0), (0, MLA_PAD - n_mla))).astype(BF16)
    mu_rkv = rwkv_mu[:n_rkv].reshape(1, n_rkv)
    mu_lora = jnp.pad(rwkv_mu[n_rkv:], (0, LORA_PAD - n_lora)).reshape(1, LORA_PAD)
    w_up = jnp.zeros((LORA_PAD, n_rkv), F32)
    w_up = w_up.at[0:DECAY_LORA, 0:D_RWKV].set(decay_up)
    w_up = w_up.at[DECAY_LORA:DECAY_LORA + ICLR_LORA, D_RWKV:2 * D_RWKV].set(iclr_up)
    w_up = w_up.at[DECAY_LORA + ICLR_LORA:n_lora, 2 * D_RWKV:].set(gate_up)
    w_up = w_up.astype(BF16)
    w_qb = _pad_heads(w_q_b, MLA_HEADS, QK_HEAD_DIM).astype(BF16)
    w_kv3 = w_kv_b.reshape(KV_LORA_RANK, MLA_HEADS, QK_NOPE_DIM + V_HEAD_DIM)
    w_kb = _pad_heads(w_kv3[:, :, :QK_NOPE_DIM].reshape(KV_LORA_RANK, -1), MLA_HEADS, QK_NOPE_DIM).astype(BF16)
    w_vb = _pad_heads(w_kv3[:, :, QK_NOPE_DIM:].reshape(KV_LORA_RANK, -1), MLA_HEADS, V_HEAD_DIM).astype(BF16)
    qn_pad = jnp.pad(q_norm, (0, HEAD_PAD - QK_HEAD_DIM)).reshape(1, HEAD_PAD)
    kn_pad = jnp.pad(k_norm, (0, HEAD_PAD - QK_HEAD_DIM)).reshape(1, HEAD_PAD)
    inv_freq = ROPE_THETA ** (-jnp.arange(0, QK_ROPE_DIM, 2, dtype=F32) / QK_ROPE_DIM)
    invf = jnp.zeros((HEAD_PAD,), F32).at[QK_NOPE_DIM:QK_HEAD_DIM].set(jnp.tile(inv_freq, 2)).reshape(1, HEAD_PAD)
    pos3 = positions.reshape(B, T, 1)
    HP = MLA_HEADS * HEAD_PAD

    row = lambda n: pl.BlockSpec((1, n), lambda b, t: (0, 0))
    full = lambda a: pl.BlockSpec(a.shape, lambda b, t: (0,) * a.ndim)
    tok = lambda n: pl.BlockSpec((1, tm, n), lambda b, t: (b, t, 0))
    outs = ([jax.ShapeDtypeStruct((B, T, D_RWKV), F32)] * 8
            + [jax.ShapeDtypeStruct((B, T, HP), BF16)] * 3)
    return pl.pallas_call(
        _pre_kernel,
        grid=(B, T // tm),
        in_specs=[tok(D),
                  pl.BlockSpec((1, 6, D), lambda b, t: (b, 0, 0)),
                  tok(1),
                  row(D), full(w_rkv), full(w_lora), full(w_mla),
                  row(n_rkv), row(LORA_PAD), full(w_up), row(D_RWKV), row(D_RWKV),
                  row(D_RWKV), row(D_RWKV), row(D_RWKV),
                  row(Q_LORA_RANK), full(w_qb), row(KV_LORA_RANK), full(w_kb), full(w_vb),
                  row(HEAD_PAD), row(HEAD_PAD), row(HEAD_PAD)],
        out_specs=[tok(D_RWKV)] * 8 + [tok(HP)] * 3,
        out_shape=outs,
        scratch_shapes=[pltpu.VMEM((1, n_rkv), F32), pltpu.VMEM((1, LORA_PAD), F32)],
        compiler_params=_cparams("arbitrary", "arbitrary"),
        name="pre",
    )(x, mod3, pos3, norm_mix.reshape(1, D), w_rkv, w_lora, w_mla, mu_rkv, mu_lora, w_up,
      decay_w0.reshape(1, -1), iclr_a0.reshape(1, -1), k_k.reshape(1, -1), k_a.reshape(1, -1),
      r_k.reshape(1, -1), q_a_norm.reshape(1, -1), w_qb, kv_a_norm.reshape(1, -1), w_kb, w_vb,
      qn_pad, kn_pad, invf)


def _scan_kernel(r_ref, lw_ref, k_ref, v_ref, kk_ref, akk_ref, y_ref, state):
    C = SCAN_CHUNK
    N = RWKV_HEAD_DIM

    @pl.when(pl.program_id(1) == 0)
    def _():
        state[...] = jnp.zeros_like(state)

    lw = lw_ref[0]
    ri = lax.broadcasted_iota(I32, (C, C), 0)
    ci = lax.broadcasted_iota(I32, (C, C), 1)
    tri_incl = jnp.where(ci <= ri, 1.0, 0.0).astype(BF16)
    cum = _mm_exact_rhs_left(tri_incl, lw)
    cum_end = cum[C - 1:C, :]
    e_pos = jnp.exp(cum)
    e_neg = jnp.exp(-cum)
    e_prev = jnp.exp(cum - lw)
    e_end = jnp.exp(cum_end - cum)
    rr = r_ref[0]
    kk = kk_ref[0]
    k2 = k_ref[0]
    pneg = -akk_ref[0]
    vv = v_ref[0]
    rt = rr * e_pos
    bt = kk * e_prev
    pt = pneg * e_neg
    kt = k2 * e_neg
    ph = pneg * e_end
    kh = k2 * e_end
    ones_c = jnp.ones((C, N), BF16)
    strict = ci < ri
    incl = ci <= ri
    eye = jnp.where(ci == ri, 1.0, 0.0)

    for hh in range(RWKV_HEADS):
        sl = slice(hh * N, (hh + 1) * N)
        bt_h, pt_h, kt_h, rt_h = bt[:, sl], pt[:, sl], kt[:, sl], rt[:, sl]
        ph_h, kh_h, v_h = ph[:, sl], kh[:, sl], vv[:, sl]
        a_ab = jnp.where(strict, _mm3(bt_h, pt_h, NT), 0.0)
        a_ak = jnp.where(strict, _mm3(bt_h, kt_h, NT), 0.0)
        b_rp = jnp.where(incl, _mm3(rt_h, pt_h, NT), 0.0)
        b_rk = jnp.where(incl, _mm3(rt_h, kt_h, NT), 0.0)
        tinv = eye + a_ab
        apow = a_ab
        for _ in range(int(math.log2(C)) - 1):
            apow = _mm3(apow, apow)
            tinv = tinv + _mm3(tinv, apow)
        s0 = state[hh]
        x_in = _mm3(bt_h, s0) + _mm3(a_ak, v_h)
        u = _mm3(tinv, x_in)
        y = _mm3(rt_h, s0) + _mm3(b_rp, u) + _mm3(b_rk, v_h)
        wlog = _mm_exact_rhs(lw[:, sl], ones_c, TN)
        state[hh] = jnp.exp(wlog) * s0 + _mm3(ph_h, u, TN) + _mm3(kh_h, v_h, TN)
        y_ref[0, :, sl] = y


def _mm_exact_rhs_left(b_exact_bf16, a):
    h, m, l = _split3(a)
    return _dot(b_exact_bf16, h) + (_dot(b_exact_bf16, m) + _dot(b_exact_bf16, l))


def _scan_call(r, lw, k2, v, kk, akk):
    B, T, W = r.shape
    C = SCAN_CHUNK
    spec = pl.BlockSpec((1, C, W), lambda b, c: (b, c, 0))
    return pl.pallas_call(
        _scan_kernel,
        grid=(B, T // C),
        in_specs=[spec] * 6,
        out_specs=spec,
        out_shape=jax.ShapeDtypeStruct((B, T, W), F32),
        scratch_shapes=[pltpu.VMEM((RWKV_HEADS, RWKV_HEAD_DIM, RWKV_HEAD_DIM), F32)],
        compiler_params=_cparams("arbitrary", "arbitrary"),
        name="scan",
    )(r, lw, k2, v, kk, akk)


def _attn_kernel(qi_ref, ki_ref, q_ref, k_ref, v_ref, o_ref, m_sc, l_sc, acc_sc):
    s_id = pl.program_id(2)
    qi = qi_ref[s_id]
    ki = ki_ref[s_id]
    tq = q_ref.shape[1]
    tk = k_ref.shape[1]

    @pl.when(ki == 0)
    def _():
        m_sc[...] = jnp.full_like(m_sc, NEG_INF)
        l_sc[...] = jnp.zeros_like(l_sc)
        acc_sc[...] = jnp.zeros_like(acc_sc)

    s = _dot(q_ref[0], k_ref[0], NT) * (QK_HEAD_DIM ** -0.5)
    q_pos = qi * tq + lax.broadcasted_iota(I32, (tq, tk), 0)
    k_pos = ki * tk + lax.broadcasted_iota(I32, (tq, tk), 1)
    s = jnp.where(k_pos <= q_pos, s, NEG_INF)
    m_old = m_sc[...]
    m_new = jnp.maximum(m_old, jnp.max(s, axis=-1, keepdims=True))
    alpha = jnp.exp(m_old - m_new)
    p = jnp.exp(s - m_new)
    l_sc[...] = alpha * l_sc[...] + jnp.sum(p, axis=-1, keepdims=True)
    acc_sc[...] = alpha * acc_sc[...] + _dot(p.astype(BF16), v_ref[0])
    m_sc[...] = m_new

    @pl.when(ki == qi)
    def _():
        o_ref[0] = (acc_sc[...] / l_sc[...]).astype(o_ref.dtype)


def _attn_call(q, k, v, tq):
    B, T, HP = q.shape
    nq = T // tq
    pairs = [(i, j) for i in range(nq) for j in range(i + 1)]
    qi_tab = jnp.asarray([p[0] for p in pairs], I32)
    ki_tab = jnp.asarray([p[1] for p in pairs], I32)
    qspec = pl.BlockSpec((1, tq, HEAD_PAD), lambda b, h, s, qt, kt: (b, qt[s], h))
    kspec = pl.BlockSpec((1, tq, HEAD_PAD), lambda b, h, s, qt, kt: (b, kt[s], h))
    return pl.pallas_call(
        _attn_kernel,
        grid_spec=pltpu.PrefetchScalarGridSpec(
            num_scalar_prefetch=2,
            grid=(B, MLA_HEADS, len(pairs)),
            in_specs=[qspec, kspec, kspec],
            out_specs=qspec,
            scratch_shapes=[pltpu.VMEM((tq, 1), F32), pltpu.VMEM((tq, 1), F32),
                            pltpu.VMEM((tq, HEAD_PAD), F32)]),
        out_shape=jax.ShapeDtypeStruct((B, T, HP), BF16),
        compiler_params=_cparams("arbitrary", "arbitrary", "arbitrary"),
        name="attn",
    )(qi_tab, ki_tab, q, k, v)


def _post_kernel(y_ref, bonus_ref, g_ref, o_ref, x_ref, mod_ref, lnw_ref, lnb_ref,
                 wo_r_ref, wo_m_ref, nffn_ref, x1_ref, h2_ref):
    y = y_ref[0]
    ones64 = _seg_ones(D_RWKV, RWKV_HEAD_DIM)
    mean = _segsum(y, ones64) * (1.0 / RWKV_HEAD_DIM)
    yc = y - mean
    var = _segsum(yc * yc, ones64) * (1.0 / RWKV_HEAD_DIM)
    yn = yc * lax.rsqrt(var + GN_EPS) * lnw_ref[...] + lnb_ref[...]
    yr = (yn + bonus_ref[0]) * g_ref[0]
    mix = _mm(yr, wo_r_ref[...]) + _dot(o_ref[0], wo_m_ref[...])
    g_a = mod_ref[0, 2:3, :]
    sh_f = mod_ref[0, 3:4, :]
    sc_f = mod_ref[0, 4:5, :]
    x1 = x_ref[0] + g_a * mix
    x1_ref[0] = x1
    ms = jnp.mean(x1 * x1, axis=-1, keepdims=True)
    h2_ref[0] = x1 * lax.rsqrt(ms + NORM_EPS) * nffn_ref[...] * (1.0 + sc_f) + sh_f


def _post_call(y, bonus, g, o_pad, x, mod3, ln_w, ln_b, w_out, norm_ffn, tm):
    B, T, D = x.shape
    HP = MLA_HEADS * HEAD_PAD
    wo_r = w_out[:D_RWKV].astype(BF16)
    wo_m = jnp.pad(w_out[D_RWKV:].reshape(MLA_HEADS, V_HEAD_DIM, D),
                   ((0, 0), (0, HEAD_PAD - V_HEAD_DIM), (0, 0))).reshape(HP, D).astype(BF16)
    tok = lambda n: pl.BlockSpec((1, tm, n), lambda b, t: (b, t, 0))
    row = lambda n: pl.BlockSpec((1, n), lambda b, t: (0, 0))
    full = lambda a: pl.BlockSpec(a.shape, lambda b, t: (0,) * a.ndim)
    return pl.pallas_call(
        _post_kernel,
        grid=(B, T // tm),
        in_specs=[tok(D_RWKV), tok(D_RWKV), tok(D_RWKV), tok(HP), tok(D),
                  pl.BlockSpec((1, 6, D), lambda b, t: (b, 0, 0)),
                  row(D_RWKV), row(D_RWKV), full(wo_r), full(wo_m), row(D)],
        out_specs=[tok(D), tok(D)],
        out_shape=[jax.ShapeDtypeStruct((B, T, D), F32)] * 2,
        compiler_params=_cparams("arbitrary", "arbitrary"),
        name="post",
    )(y, bonus, g, o_pad, x, mod3, ln_w.reshape(1, -1), ln_b.reshape(1, -1), wo_r, wo_m,
      norm_ffn.reshape(1, D))


def _first_index(mask, iota, size, axis):
    return jnp.min(jnp.where(mask, iota, size), axis=axis, keepdims=True)


def _route_kernel(h_ref, wr_ref, bias_ref, e_ref, w_ref, rank_ref, cnt_ref, base):
    tr = h_ref.shape[0]
    E = N_EXPERTS

    @pl.when(pl.program_id(0) == 0)
    def _():
        base[...] = jnp.zeros_like(base)

    logits = _mm3(wr_ref[...], h_ref[...], NT)
    scores = _sigmoid(logits)
    sel = scores + bias_ref[...]
    iota_g = lax.broadcasted_iota(I32, (GROUP_SIZE, tr), 0)
    gs_rows = []
    for gi in range(N_GROUPS):
        blk = sel[gi * GROUP_SIZE:(gi + 1) * GROUP_SIZE, :]
        m1 = jnp.max(blk, axis=0, keepdims=True)
        i1 = _first_index(blk == m1, iota_g, GROUP_SIZE, 0)
        m2 = jnp.max(jnp.where(iota_g == i1, NEG_INF, blk), axis=0, keepdims=True)
        gs_rows.append(m1 + m2)
    gs = jnp.concatenate(gs_rows, axis=0)
    iota8 = lax.broadcasted_iota(I32, (N_GROUPS, tr), 0)
    gmask = jnp.zeros((N_GROUPS, tr), jnp.bool_)
    for _ in range(TOPK_GROUPS):
        mg = jnp.max(gs, axis=0, keepdims=True)
        ig = _first_index(gs == mg, iota8, N_GROUPS, 0)
        hit = iota8 == ig
        gmask = gmask | hit
        gs = jnp.where(hit, NEG_INF, gs)
    msel = jnp.concatenate(
        [jnp.where(gmask[gi:gi + 1, :], sel[gi * GROUP_SIZE:(gi + 1) * GROUP_SIZE, :], NEG_INF)
         for gi in range(N_GROUPS)], axis=0)
    iota_e = lax.broadcasted_iota(I32, (E, tr), 0)
    e_rows, w_rows = [], []
    onehot = jnp.zeros((E, tr), F32)
    for _ in range(TOP_K):
        mv = jnp.max(msel, axis=0, keepdims=True)
        ie = _first_index(msel == mv, iota_e, E, 0)
        hit = iota_e == ie
        e_rows.append(ie)
        w_rows.append(jnp.sum(jnp.where(hit, scores, 0.0), axis=0, keepdims=True))
        onehot = jnp.where(hit, 1.0, onehot)
        msel = jnp.where(hit, NEG_INF, msel)
    top_e = jnp.concatenate(e_rows, axis=0)
    wts = jnp.concatenate(w_rows, axis=0)
    wts = wts / jnp.sum(wts, axis=0, keepdims=True) * ROUTED_SCALE
    ti = lax.broadcasted_iota(I32, (tr, tr), 0)
    tj = lax.broadcasted_iota(I32, (tr, tr), 1)
    upper = jnp.where(ti < tj, 1.0, 0.0).astype(BF16)
    pos = _dot(onehot.astype(BF16), upper) + base[...]
    rank_rows = [jnp.sum(jnp.where(iota_e == e_rows[j], pos, 0.0), axis=0, keepdims=True)
                 for j in range(TOP_K)]
    base[...] = base[...] + jnp.sum(onehot, axis=1, keepdims=True)
    e_ref[...] = top_e
    w_ref[...] = wts
    rank_ref[...] = jnp.concatenate(rank_rows, axis=0).astype(I32)
    cnt_ref[...] = base[...].astype(I32)


def _route_call(h2, w_router, router_bias, tr):
    N, D = h2.shape
    E = N_EXPERTS
    out_kn = pl.BlockSpec((TOP_K, tr), lambda i: (0, i))
    return pl.pallas_call(
        _route_kernel,
        grid=(N // tr,),
        in_specs=[pl.BlockSpec((tr, D), lambda i: (i, 0)),
                  pl.BlockSpec((E, D), lambda i: (0, 0)),
                  pl.BlockSpec((E, 1), lambda i: (0, 0))],
        out_specs=[out_kn, out_kn, out_kn, pl.BlockSpec((E, 1), lambda i: (0, 0))],
        out_shape=[jax.ShapeDtypeStruct((TOP_K, N), I32), jax.ShapeDtypeStruct((TOP_K, N), F32),
                   jax.ShapeDtypeStruct((TOP_K, N), I32), jax.ShapeDtypeStruct((E, 1), I32)],
        scratch_shapes=[pltpu.VMEM((E, 1), F32)],
        compiler_params=_cparams("arbitrary"),
        name="route",
    )(h2, w_router.T, router_bias.reshape(E, 1))


def _dest_kernel(e_ref, rank_ref, start_ref, d_ref):
    tr = e_ref.shape[1]
    iota_e = lax.broadcasted_iota(I32, (N_EXPERTS, tr), 0)
    starts = start_ref[...]
    rows = [jnp.sum(jnp.where(iota_e == e_ref[j:j + 1, :], starts, 0), axis=0, keepdims=True)
            for j in range(TOP_K)]
    d_ref[...] = jnp.concatenate(rows, axis=0) + rank_ref[...]


def _dest_call(top_e, rank, pad_starts, tr):
    K, N = top_e.shape
    spec = pl.BlockSpec((K, tr), lambda i: (0, i))
    return pl.pallas_call(
        _dest_kernel,
        grid=(N // tr,),
        in_specs=[spec, spec, pl.BlockSpec((N_EXPERTS, 1), lambda i: (0, 0))],
        out_specs=spec,
        out_shape=jax.ShapeDtypeStruct((K, N), I32),
        compiler_params=_cparams("arbitrary"),
        name="dest",
    )(top_e, rank, pad_starts.reshape(N_EXPERTS, 1))


def _row_copy(src, s_row, dst, d_row, sem):
    return pltpu.make_async_copy(src.at[pl.ds(s_row, 1), :], dst.at[pl.ds(d_row, 1), :], sem)


def _dispatch_kernel(dest_hbm, h_ref, xs_in, xs_out, idx, sem_idx, sem_rows):
    del xs_in
    i = pl.program_id(0)
    td = h_ref.shape[0]
    cp = pltpu.make_async_copy(dest_hbm.at[i], idx, sem_idx)
    cp.start()
    cp.wait()

    def issue(t, carry):
        for j in range(TOP_K):
            _row_copy(h_ref, t, xs_out, idx[j, t], sem_rows).start()
        return carry

    lax.fori_loop(0, td, issue, 0)

    def drain(t, carry):
        for j in range(TOP_K):
            _row_copy(h_ref, 0, xs_out, 0, sem_rows).wait()
        return carry

    lax.fori_loop(0, td, drain, 0)


def _dispatch_call(dest3, h2, xs_zero, td):
    N, D = h2.shape
    P = xs_zero.shape[0]
    return pl.pallas_call(
        _dispatch_kernel,
        grid=(N // td,),
        in_specs=[pl.BlockSpec(memory_space=pl.ANY),
                  pl.BlockSpec((td, D), lambda i: (i, 0)),
                  pl.BlockSpec(memory_space=pl.ANY)],
        out_specs=pl.BlockSpec(memory_space=pl.ANY),
        out_shape=jax.ShapeDtypeStruct((P, D), h2.dtype),
        scratch_shapes=[pltpu.SMEM((TOP_K, td), I32), pltpu.SemaphoreType.DMA,
                        pltpu.SemaphoreType.DMA],
        input_output_aliases={2: 0},
        compiler_params=_cparams("arbitrary"),
        name="dispatch",
    )(dest3, h2, xs_zero)


def _moe_kernel(be_ref, nu_ref, xs_ref, wgu_ref, wdn_ref, y_ref):
    del be_ref

    @pl.when(pl.program_id(0) < nu_ref[0])
    def _():
        gu = _mm(xs_ref[...], wgu_ref[0])
        act = _silu(gu[:, :D_EXPERT]) * gu[:, D_EXPERT:]
        y_ref[...] = _mm(act, wdn_ref[0])


def _moe_call(block_expert, n_used, xs, w_gu, w_dn):
    P, D = xs.shape
    nb = P // MOE_BLOCK
    blk = lambda i, be, nu: (jnp.minimum(i, nu[0] - 1), 0)
    return pl.pallas_call(
        _moe_kernel,
        grid_spec=pltpu.PrefetchScalarGridSpec(
            num_scalar_prefetch=2,
            grid=(nb,),
            in_specs=[pl.BlockSpec((MOE_BLOCK, D), blk),
                      pl.BlockSpec((1, D, 2 * D_EXPERT), lambda i, be, nu: (be[i], 0, 0)),
                      pl.BlockSpec((1, D_EXPERT, D), lambda i, be, nu: (be[i], 0, 0))],
            out_specs=pl.BlockSpec((MOE_BLOCK, D), blk)),
        out_shape=jax.ShapeDtypeStruct((P, D), F32),
        compiler_params=_cparams("arbitrary"),
        name="moe",
    )(block_expert, n_used, xs, w_gu, w_dn)


def _combine_kernel(dest_hbm, w_ref, y_hbm, h_ref, x1_ref, mod_ref, wsg_ref, wsd_ref, o_ref,
                    idx, rows, sem_idx, sem_rows):
    i = pl.program_id(0)
    tc = h_ref.shape[0]
    cp = pltpu.make_async_copy(dest_hbm.at[i], idx, sem_idx)
    cp.start()
    cp.wait()

    def issue(t, carry):
        for j in range(TOP_K):
            _row_copy(y_hbm, idx[j, t], rows.at[j], t, sem_rows).start()
        return carry

    lax.fori_loop(0, tc, issue, 0)
    gu = _mm(h_ref[...], wsg_ref[...])
    act = _silu(gu[:, :D_EXPERT]) * gu[:, D_EXPERT:]
    ffn = _mm(act, wsd_ref[...])

    def drain(t, carry):
        for j in range(TOP_K):
            _row_copy(y_hbm, 0, rows.at[j], 0, sem_rows).wait()
        return carry

    lax.fori_loop(0, tc, drain, 0)
    wts = w_ref[...]
    for j in range(TOP_K):
        ffn = ffn + rows[j] * wts[:, j:j + 1]
    g_f = mod_ref[0, 5:6, :]
    o_ref[...] = x1_ref[...] + g_f * ffn


def _combine_call(dest3, w_nk, y, h2, x1, mod3, w_sh_gu, w_sh_dn, tokens_per_batch, tc):
    N, D = h2.shape
    tiles_per_batch = tokens_per_batch // tc
    tok = pl.BlockSpec((tc, D), lambda i: (i, 0))
    wsg = w_sh_gu.astype(BF16)
    wsd = w_sh_dn.astype(BF16)
    return pl.pallas_call(
        _combine_kernel,
        grid=(N // tc,),
        in_specs=[pl.BlockSpec(memory_space=pl.ANY),
                  pl.BlockSpec((tc, TOP_K), lambda i: (i, 0)),
                  pl.BlockSpec(memory_space=pl.ANY),
                  tok, tok,
                  pl.BlockSpec((1, 6, D), lambda i: (i // tiles_per_batch, 0, 0)),
                  pl.BlockSpec(wsg.shape, lambda i: (0, 0)),
                  pl.BlockSpec(wsd.shape, lambda i: (0, 0))],
        out_specs=tok,
        out_shape=jax.ShapeDtypeStruct((N, D), F32),
        scratch_shapes=[pltpu.SMEM((TOP_K, tc), I32), pltpu.VMEM((TOP_K, tc, D), F32),
                        pltpu.SemaphoreType.DMA, pltpu.SemaphoreType.DMA],
        compiler_params=_cparams("arbitrary"),
        name="combine",
    )(dest3, w_nk, y, h2, x1, mod3, wsg, wsd)


def _tile(n, pref):
    t = min(n, pref)
    assert n % t == 0, (n, t)
    return t


def _layer(x, mod3, positions, norm_mix, w_in, rwkv_mu, decay_w0, decay_up, iclr_a0, iclr_up,
           gate_up, rwkv_k_k, rwkv_k_a, rwkv_r_k, ln_x_w, ln_x_b, q_a_norm, w_q_b, kv_a_norm,
           w_kv_b, q_norm, k_norm, w_out, norm_ffn, w_router, router_bias, w_e_gate_up, w_e_down,
           w_sh_gate_up, w_sh_down):
    B, T, D = x.shape
    N = B * T
    assert T % SCAN_CHUNK == 0
    (r, lw, k2, v, kk, akk, g, bonus, q_pad, k_pad, v_pad) = _pre_call(
        x, mod3, positions, norm_mix, w_in, rwkv_mu, decay_w0, decay_up, iclr_a0, iclr_up,
        gate_up, rwkv_k_k, rwkv_k_a, rwkv_r_k, q_a_norm, w_q_b, kv_a_norm, w_kv_b, q_norm, k_norm,
        tm=_tile(T, 512))
    y = _scan_call(r, lw, k2, v, kk, akk)
    o_pad = _attn_call(q_pad, k_pad, v_pad, tq=_tile(T, 512))
    x1, h2 = _post_call(y, bonus, g, o_pad, x, mod3, ln_x_w, ln_x_b, w_out, norm_ffn,
                        tm=_tile(T, 512))
    x1 = x1.reshape(N, D)
    h2 = h2.reshape(N, D)

    tr = _tile(N, 512)
    top_e, wts, rank, counts = _route_call(h2, w_router, router_bias, tr)
    counts = counts.reshape(N_EXPERTS)
    padded = (counts + MOE_BLOCK - 1) // MOE_BLOCK * MOE_BLOCK
    pad_ends = jnp.cumsum(padded)
    pad_starts = pad_ends - padded
    n_blocks = (N * TOP_K + N_EXPERTS * (MOE_BLOCK - 1)) // MOE_BLOCK
    block_expert = jnp.minimum(
        jnp.searchsorted(pad_ends, jnp.arange(n_blocks, dtype=I32) * MOE_BLOCK, side='right'),
        N_EXPERTS - 1).astype(I32)
    n_used = (pad_ends[-1:] // MOE_BLOCK).astype(I32)
    dest = _dest_call(top_e, rank, pad_starts.astype(I32), tr)

    td = _tile(T, 256)
    dest3 = dest.reshape(TOP_K, N // td, td).transpose(1, 0, 2)
    xs = _dispatch_call(dest3, h2, jnp.zeros((n_blocks * MOE_BLOCK, D), F32), td)
    y_e = _moe_call(block_expert, n_used, xs, w_e_gate_up, w_e_down)
    out = _combine_call(dest3, wts.T, y_e, h2, x1, mod3, w_sh_gate_up, w_sh_down, T, td)
    return out.reshape(B, T, D)


def kernel(x, c, positions, ada_w, ada_b, norm_mix, w_in, rwkv_mu, decay_w0, decay_up, iclr_a0, iclr_up, gate_up, rwkv_k_k, rwkv_k_a, rwkv_r_k, ln_x_w, ln_x_b, q_a_norm, w_q_b, kv_a_norm, w_kv_b, q_norm, k_norm, w_out, norm_ffn, w_router, router_bias, w_e_gate_up, w_e_down, w_sh_gate_up, w_sh_down):
    B, T, D = x.shape
    depth = ada_w.shape[0]
    for l in range(depth):
        mod3 = _mod_call(c, ada_w[l], ada_b[l]).reshape(B, 6, D)
        x = _layer(x, mod3, positions, norm_mix[l], w_in[l], rwkv_mu[l], decay_w0[l], decay_up[l],
                   iclr_a0[l], iclr_up[l], gate_up[l], rwkv_k_k[l], rwkv_k_a[l], rwkv_r_k[l],
                   ln_x_w[l], ln_x_b[l], q_a_norm[l], w_q_b[l], kv_a_norm[l], w_kv_b[l],
                   q_norm[l], k_norm[l], w_out[l], norm_ffn[l], w_router[l], router_bias[l],
                   w_e_gate_up[l], w_e_down[l], w_sh_gate_up[l], w_sh_down[l])
    return x
```

```python
import functools
import math

import jax
import jax.numpy as jnp
import numpy as np
from jax import lax
from jax.experimental import pallas as pl
from jax.experimental.pallas import tpu as pltpu

F32 = jnp.float32
BF16 = jnp.bfloat16
I32 = jnp.int32

NORM_EPS = 1e-6
GN_EPS = 64e-5
RWKV_HEADS = 8
RWKV_HEAD_DIM = 64
D_RWKV = 512
DECAY_LORA = 32
ICLR_LORA = 32
GATE_LORA = 96
MLA_HEADS = 8
QK_NOPE_DIM = 64
QK_ROPE_DIM = 32
QK_HEAD_DIM = 96
V_HEAD_DIM = 64
Q_LORA_RANK = 256
KV_LORA_RANK = 128
ROPE_THETA = 10000.0
N_EXPERTS = 256
TOP_K = 8
N_GROUPS = 8
TOPK_GROUPS = 4
GROUP_SIZE = N_EXPERTS // N_GROUPS
D_EXPERT = 256
ROUTED_SCALE = 2.5
MOE_BLOCK = 256

LANES = 128
SUBLANES = 8
HEAD_PAD = 128
VMEM_LIMIT = 56 * 1024 * 1024

SCAN_CHUNK = 64
SCAN_BLOCK = 256
ATTN_TILE = 512
ATTN_Q_SCALE = QK_HEAD_DIM ** -0.5 * math.log2(math.e)
NEG_INF = float("-inf")


def _cparams(*sem):
    return pltpu.CompilerParams(dimension_semantics=sem, vmem_limit_bytes=VMEM_LIMIT)


def _split2(a):
    hi = a.astype(BF16)
    lo = (a - hi.astype(F32)).astype(BF16)
    return hi, lo


def _split3(a):
    hi = a.astype(BF16)
    r1 = a - hi.astype(F32)
    mid = r1.astype(BF16)
    lo = (r1 - mid.astype(F32)).astype(BF16)
    return hi, mid, lo


def _dot(a, b, dims=None):
    if dims is None:
        return jnp.dot(a, b, preferred_element_type=F32)
    return lax.dot_general(a, b, (dims, ((), ())), preferred_element_type=F32)


def _mm(a, b, dims=None):
    return _dot(a.astype(BF16), b.astype(BF16), dims)


def _mm3(a, b, dims=None):
    ah, al = _split2(a)
    bh, bl = _split2(b)
    return _dot(ah, bh, dims) + (_dot(ah, bl, dims) + _dot(al, bh, dims))


def _mm_exact_rhs(a, b_exact_bf16, dims=None):
    h, m, l = _split3(a)
    return _dot(h, b_exact_bf16, dims) + (_dot(m, b_exact_bf16, dims) + _dot(l, b_exact_bf16, dims))


NT = ((1,), (1,))
TN = ((0,), (0,))


def _sigmoid(z):
    return 1.0 / (1.0 + jnp.exp(-z))


def _silu(z):
    return z * _sigmoid(z)


def _seg_ones(width, seg):
    r = lax.broadcasted_iota(I32, (width, width), 0) // seg
    c = lax.broadcasted_iota(I32, (width, width), 1) // seg
    return jnp.where(r == c, 1.0, 0.0).astype(BF16)


def _segsum(a, ones_bd):
    hi, lo = _split2(a)
    return _dot(hi, ones_bd) + _dot(lo, ones_bd)


def _mod_kernel(c_ref, w_ref, b_ref, o_ref):
    ca = _silu(c_ref[...])
    o_ref[...] = _mm3(ca, w_ref[...]) + b_ref[...]


def _mod_call(c, ada_w, ada_b):
    B, D = c.shape
    n6 = ada_w.shape[1]
    tn = D
    return pl.pallas_call(
        _mod_kernel,
        grid=(n6 // tn,),
        in_specs=[pl.BlockSpec((B, D), lambda j: (0, 0)),
                  pl.BlockSpec((D, tn), lambda j: (0, j)),
                  pl.BlockSpec((1, tn), lambda j: (0, j))],
        out_specs=pl.BlockSpec((B, tn), lambda j: (0, j)),
        out_shape=jax.ShapeDtypeStruct((B, n6), F32),
        compiler_params=_cparams("arbitrary"),
        name="mod",
    )(c, ada_w, ada_b.reshape(1, n6))


def _pre_kernel(x_ref, mod_ref, pos_ref, nmix_ref, wrkv_ref, wlora_ref, wmla_ref,
                mu_rkv_ref, mu_lora_ref, wup_ref, w0_ref, a0_ref, kk_ref, ka_ref, rk_ref,
                qan_ref, wqb_ref, kvan_ref, wkb_ref, wvb_ref, qn_ref, kn_ref, invf_ref,
                r_ref, lw_ref, k_ref, v_ref, kkn_ref, akk_ref, g_ref, bonus_ref,
                q_ref, kout_ref, vout_ref,
                carry_rkv, carry_lora):
    ti = pl.program_id(1)
    tm = x_ref.shape[1]

    @pl.when(ti == 0)
    def _():
        carry_rkv[...] = jnp.zeros_like(carry_rkv)
        carry_lora[...] = jnp.zeros_like(carry_lora)

    xb = x_ref[0]
    sh_a = mod_ref[0, 0:1, :]
    sc_a = mod_ref[0, 1:2, :]
    ms = jnp.mean(xb * xb, axis=-1, keepdims=True)
    h = xb * lax.rsqrt(ms + NORM_EPS) * nmix_ref[...] * (1.0 + sc_a) + sh_a
    hb = h.astype(BF16)
    u_rkv = _dot(hb, wrkv_ref[...])
    u_lora = _dot(hb, wlora_ref[...])
    u_mla = _dot(hb, wmla_ref[...])

    row0 = lax.broadcasted_iota(I32, (tm, 1), 0) == 0

    def shift(u, carry):
        prev = jnp.where(row0, carry[...], pltpu.roll(u, 1, 0))
        carry[...] = u[tm - 1:tm, :]
        return prev

    prev_rkv = shift(u_rkv, carry_rkv)
    prev_lora = shift(u_lora, carry_lora)
    us = u_rkv + (prev_rkv - u_rkv) * mu_rkv_ref[...]
    ul = u_lora + (prev_lora - u_lora) * mu_lora_ref[...]
    r = us[:, 0:D_RWKV]
    k = us[:, D_RWKV:2 * D_RWKV]
    v = us[:, 2 * D_RWKV:3 * D_RWKV]

    lane_l = lax.broadcasted_iota(I32, ul.shape, 1)
    t_in = jnp.where(lane_l < DECAY_LORA, jnp.tanh(ul),
                     jnp.where(lane_l < DECAY_LORA + ICLR_LORA, ul, _sigmoid(ul)))
    up = _mm(t_in, wup_ref[...])
    z = w0_ref[...] + up[:, 0:D_RWKV]
    lw = (-math.exp(-0.5)) * _sigmoid(z)
    a = _sigmoid(a0_ref[...] + up[:, D_RWKV:2 * D_RWKV])
    g = up[:, 2 * D_RWKV:3 * D_RWKV]

    ones64 = _seg_ones(D_RWKV, RWKV_HEAD_DIM)
    kk = k * kk_ref[...]
    ss = _segsum(kk * kk, ones64)
    kk = kk * lax.rsqrt(jnp.maximum(ss, 1e-24))
    k2 = k * (1.0 + (a - 1.0) * ka_ref[...])
    bonus = _segsum(r * k2 * rk_ref[...], ones64) * v

    r_ref[0] = r
    lw_ref[0] = lw
    k_ref[0] = k2
    v_ref[0] = v
    kkn_ref[0] = kk
    akk_ref[0] = a * kk
    g_ref[0] = g
    bonus_ref[0] = bonus

    q_lat = u_mla[:, 0:Q_LORA_RANK]
    kv_lat = u_mla[:, Q_LORA_RANK:Q_LORA_RANK + KV_LORA_RANK]
    kpe_tile = u_mla[:, Q_LORA_RANK + KV_LORA_RANK:]
    qn = q_lat * lax.rsqrt(jnp.mean(q_lat * q_lat, axis=-1, keepdims=True) + NORM_EPS) * qan_ref[...]
    kvn = kv_lat * lax.rsqrt(jnp.mean(kv_lat * kv_lat, axis=-1, keepdims=True) + NORM_EPS) * kvan_ref[...]
    q_raw = _mm(qn, wqb_ref[...])
    kvb = kvn.astype(BF16)
    k_raw = _dot(kvb, wkb_ref[...])
    v_pad = _dot(kvb, wvb_ref[...])
    kpe_h = pltpu.roll(kpe_tile, QK_NOPE_DIM, 1)

    lane = lax.broadcasted_iota(I32, (tm, HEAD_PAD), 1)
    ang = pos_ref[0].astype(F32) * invf_ref[...]
    in_x1 = (lane >= QK_NOPE_DIM) & (lane < QK_NOPE_DIM + QK_ROPE_DIM // 2)
    in_x2 = (lane >= QK_NOPE_DIM + QK_ROPE_DIM // 2) & (lane < QK_HEAD_DIM)
    cos_t = jnp.cos(ang)
    sin_t = jnp.sin(ang)
    s1 = jnp.where(in_x1, -sin_t, 0.0)
    s2 = jnp.where(in_x2, sin_t, 0.0)
    half = QK_ROPE_DIM // 2

    def norm_rope(xh, gain):
        ssq = jnp.sum(xh * xh, axis=-1, keepdims=True) * (1.0 / QK_HEAD_DIM)
        xh = xh * lax.rsqrt(ssq + NORM_EPS) * gain
        return xh * cos_t + pltpu.roll(xh, HEAD_PAD - half, 1) * s1 + pltpu.roll(xh, half, 1) * s2

    for hh in range(MLA_HEADS):
        sl = slice(hh * HEAD_PAD, (hh + 1) * HEAD_PAD)
        q_ref[0, :, sl] = (norm_rope(q_raw[:, sl], qn_ref[...]) * ATTN_Q_SCALE).astype(BF16)
        kout_ref[0, :, sl] = norm_rope(k_raw[:, sl] + kpe_h, kn_ref[...]).astype(BF16)
    vout_ref[0] = v_pad.astype(BF16)


def _pad_heads(w, n_heads, width):
    kdim = w.shape[0]
    w = w.reshape(kdim, n_heads, width)
    w = jnp.pad(w, ((0, 0), (0, 0), (0, HEAD_PAD - width)))
    return w.reshape(kdim, n_heads * HEAD_PAD)


def _pre_call(x, mod3, positions, norm_mix, w_in, rwkv_mu, decay_w0, decay_up, iclr_a0, iclr_up,
              gate_up, k_k, k_a, r_k, q_a_norm, w_q_b, kv_a_norm, w_kv_b, q_norm, k_norm, tm):
    B, T, D = x.shape
    n_rkv = 3 * D_RWKV
    n_lora = DECAY_LORA + ICLR_LORA + GATE_LORA
    LORA_PAD = 256
    MLA_PAD = 512
    n_mla = Q_LORA_RANK + KV_LORA_RANK + QK_ROPE_DIM
    w_rkv = w_in[:, :n_rkv].astype(BF16)
    w_lora = jnp.pad(w_in[:, n_rkv:n_rkv + n_lora], ((0, 0), (0, LORA_PAD - n_lora))).astype(BF16)
    w_mla = jnp.pad(w_in[:, n_rkv + n_lora:], ((0, 0), (0, MLA_PAD - n_mla))).astype(BF16)
    mu_rkv = rwkv_mu[:n_rkv].reshape(1, n_rkv)
    mu_lora = jnp.pad(rwkv_mu[n_rkv:], (0, LORA_PAD - n_lora)).reshape(1, LORA_PAD)
    w_up = jnp.zeros((LORA_PAD, n_rkv), F32)
    w_up = w_up.at[0:DECAY_LORA, 0:D_RWKV].set(decay_up)
    w_up = w_up.at[DECAY_LORA:DECAY_LORA + ICLR_LORA, D_RWKV:2 * D_RWKV].set(iclr_up)
    w_up = w_up.at[DECAY_LORA + ICLR_LORA:n_lora, 2 * D_RWKV:].set(gate_up)
    w_up = w_up.astype(BF16)
    w_qb = _pad_heads(w_q_b, MLA_HEADS, QK_HEAD_DIM).astype(BF16)
    w_kv3 = w_kv_b.reshape(KV_LORA_RANK, MLA_HEADS, QK_NOPE_DIM + V_HEAD_DIM)
    w_kb = _pad_heads(w_kv3[:, :, :QK_NOPE_DIM].reshape(KV_LORA_RANK, -1), MLA_HEADS, QK_NOPE_DIM).astype(BF16)
    w_vb = _pad_heads(w_kv3[:, :, QK_NOPE_DIM:].reshape(KV_LORA_RANK, -1), MLA_HEADS, V_HEAD_DIM).astype(BF16)
    qn_pad = jnp.pad(q_norm, (0, HEAD_PAD - QK_HEAD_DIM)).reshape(1, HEAD_PAD)
    kn_pad = jnp.pad(k_norm, (0, HEAD_PAD - QK_HEAD_DIM)).reshape(1, HEAD_PAD)
    inv_freq = ROPE_THETA ** (-jnp.arange(0, QK_ROPE_DIM, 2, dtype=F32) / QK_ROPE_DIM)
    invf = jnp.zeros((HEAD_PAD,), F32).at[QK_NOPE_DIM:QK_HEAD_DIM].set(jnp.tile(inv_freq, 2)).reshape(1, HEAD_PAD)
    pos3 = positions.reshape(B, T, 1)
    HP = MLA_HEADS * HEAD_PAD

    row = lambda n: pl.BlockSpec((1, n), lambda b, t: (0, 0))
    full = lambda a: pl.BlockSpec(a.shape, lambda b, t: (0,) * a.ndim)
    tok = lambda n: pl.BlockSpec((1, tm, n), lambda b, t: (b, t, 0))
    outs = ([jax.ShapeDtypeStruct((B, T, D_RWKV), F32)] * 8
            + [jax.ShapeDtypeStruct((B, T, HP), BF16)] * 3)
    return pl.pallas_call(
        _pre_kernel,
        grid=(B, T // tm),
        in_specs=[tok(D),
                  pl.BlockSpec((1, 6, D), lambda b, t: (b, 0, 0)),
                  tok(1),
                  row(D), full(w_rkv), full(w_lora), full(w_mla),
                  row(n_rkv), row(LORA_PAD), full(w_up), row(D_RWKV), row(D_RWKV),
                  row(D_RWKV), row(D_RWKV), row(D_RWKV),
                  row(Q_LORA_RANK), full(w_qb), row(KV_LORA_RANK), full(w_kb), full(w_vb),
                  row(HEAD_PAD), row(HEAD_PAD), row(HEAD_PAD)],
        out_specs=[tok(D_RWKV)] * 8 + [tok(HP)] * 3,
        out_shape=outs,
        scratch_shapes=[pltpu.VMEM((1, n_rkv), F32), pltpu.VMEM((1, LORA_PAD), F32)],
        compiler_params=_cparams("arbitrary", "arbitrary"),
        name="pre",
    )(x, mod3, pos3, norm_mix.reshape(1, D), w_rkv, w_lora, w_mla, mu_rkv, mu_lora, w_up,
      decay_w0.reshape(1, -1), iclr_a0.reshape(1, -1), k_k.reshape(1, -1), k_a.reshape(1, -1),
      r_k.reshape(1, -1), q_a_norm.reshape(1, -1), w_qb, kv_a_norm.reshape(1, -1), w_kb, w_vb,
      qn_pad, kn_pad, invf)


def _scan_kernel(r_ref, lw_ref, k_ref, v_ref, kk_ref, akk_ref, y_ref, state):
    C = SCAN_CHUNK
    n_chunks = r_ref.shape[1] // C
    n_pairs = r_ref.shape[2] // LANES

    @pl.when(pl.program_id(1) == 0)
    def _():
        state[...] = jnp.zeros_like(state)

    ri = lax.broadcasted_iota(I32, (C, C), 0)
    ci = lax.broadcasted_iota(I32, (C, C), 1)
    tri_incl = jnp.where(ci <= ri, 1.0, 0.0).astype(BF16)
    r2 = lax.broadcasted_iota(I32, (2 * C, 2 * C), 0)
    c2 = lax.broadcasted_iota(I32, (2 * C, 2 * C), 1)
    same = (r2 >= C) == (c2 >= C)
    strict = same & (c2 < r2)
    incl = same & (c2 <= r2)
    eye = jnp.where(c2 == r2, 1.0, 0.0)
    head0 = lax.broadcasted_iota(I32, (C, LANES), 1) < RWKV_HEAD_DIM

    def stack2(a):
        return jnp.concatenate([jnp.where(head0, a, 0.0), jnp.where(head0, 0.0, a)], axis=0)

    C2 = 2 * C
    cat0 = lambda *a: jnp.concatenate(a, axis=0)
    cat1 = lambda *a: jnp.concatenate(a, axis=1)

    items = []
    for c in range(n_chunks):
        rows = slice(c * C, (c + 1) * C)
        lw = lw_ref[0, rows, :]
        cum = _mm_exact_rhs_left(tri_incl, lw)
        cum_end = cum[C - 1:C, :]
        w_end = jnp.exp(cum_end)
        e_pos = jnp.exp(cum)
        e_neg = jnp.exp(-cum)
        e_prev = jnp.exp(cum - lw)
        e_end = jnp.exp(cum_end - cum)
        kk = kk_ref[0, rows, :]
        k2 = k_ref[0, rows, :]
        pneg = -akk_ref[0, rows, :]
        vv = v_ref[0, rows, :]
        rt = r_ref[0, rows, :] * e_pos
        bt = kk * e_prev
        pt = pneg * e_neg
        kt = k2 * e_neg
        ph = pneg * e_end
        kh = k2 * e_end
        for pp in range(n_pairs):
            sl = slice(pp * LANES, (pp + 1) * LANES)
            items.append(dict(
                c=c, p=pp, w_end=w_end[:, sl],
                bt2=stack2(bt[:, sl]).astype(BF16), rt2=stack2(rt[:, sl]).astype(BF16),
                pk2=cat0(stack2(pt[:, sl]), stack2(kt[:, sl])).astype(BF16),
                phkh2=cat0(stack2(ph[:, sl]), stack2(kh[:, sl])).astype(BF16),
                v2=stack2(vv[:, sl])))
    for it in items:
        ab = _dot(cat0(it['bt2'], it['rt2']), it['pk2'], NT)
        it['a_ab'] = jnp.where(strict, ab[:C2, :C2], 0.0)
        it['a_ak'] = jnp.where(strict, ab[:C2, C2:], 0.0).astype(BF16)
        it['b_rpk'] = cat1(jnp.where(incl, ab[C2:, :C2], 0.0), jnp.where(incl, ab[C2:, C2:], 0.0)).astype(BF16)
        it['tinv'] = eye + it['a_ab']
    for it in items:
        it['apow'] = _mm(it['a_ab'], it['a_ab'])
    for _ in range(int(math.log2(C)) - 1):
        for it in items:
            both = _mm(cat0(it['apow'], it['tinv']), it['apow'])
            it['apow'] = both[:C2]
            it['tinv'] = it['tinv'] + both[C2:]
    for it in items:
        it['akv'] = _dot(it['a_ak'], it['v2'].astype(BF16))
    for it in items:
        tt = _dot(it['tinv'].astype(BF16), cat1(it['bt2'], it['akv'].astype(BF16)))
        it['tb_rt'] = cat0(tt[:, :LANES].astype(BF16), it['rt2'])
        it['tav'] = tt[:, LANES:]
    for it in items:
        pp = it['p']
        s0 = state[pp]
        top = _dot(it['tb_rt'], s0.astype(BF16), NT)
        u2 = top[:C2] + it['tav']
        uv = cat0(u2, it['v2']).astype(BF16)
        y2 = top[C2:] + _dot(it['b_rpk'], uv)
        state[pp] = s0 * it['w_end'] + _dot(uv, it['phkh2'], TN)
        y_ref[0, it['c'] * C:(it['c'] + 1) * C, pp * LANES:(pp + 1) * LANES] = y2[0:C] + y2[C:C2]


def _mm_exact_rhs_left(b_exact_bf16, a):
    h, m, l = _split3(a)
    return _dot(b_exact_bf16, h) + (_dot(b_exact_bf16, m) + _dot(b_exact_bf16, l))


def _scan_call(r, lw, k2, v, kk, akk):
    B, T, W = r.shape
    tb = _tile(T, SCAN_BLOCK)
    spec = pl.BlockSpec((1, tb, W), lambda b, c: (b, c, 0))
    return pl.pallas_call(
        _scan_kernel,
        grid=(B, T // tb),
        in_specs=[spec] * 6,
        out_specs=spec,
        out_shape=jax.ShapeDtypeStruct((B, T, W), F32),
        scratch_shapes=[pltpu.VMEM((W // LANES, 2 * RWKV_HEAD_DIM, LANES), F32)],
        compiler_params=_cparams("arbitrary", "arbitrary"),
        name="scan",
    )(r, lw, k2, v, kk, akk)


def _attn_kernel(q_ref, k_ref, v_ref, o_ref):
    T = q_ref.shape[1]
    tq = min(T, ATTN_TILE)
    row = lax.broadcasted_iota(I32, (tq, tq), 0)
    col = lax.broadcasted_iota(I32, (tq, tq), 1)
    causal = col <= row

    def update(q, kt, vt, carry, mask):
        m_old, l_old, acc = carry
        s = _dot(q, kt, NT)
        if mask:
            s = jnp.where(causal, s, NEG_INF)
        m_new = jnp.maximum(m_old, jnp.max(s, axis=-1, keepdims=True))
        alpha = jnp.exp2(m_old - m_new)
        p = jnp.exp2(s - m_new)
        l_new = alpha * l_old + jnp.sum(p, axis=-1, keepdims=True)
        acc = alpha * acc + _dot(p.astype(BF16), vt)
        return m_new, l_new, acc

    for qi in range(T // tq):
        q = q_ref[0, qi * tq:(qi + 1) * tq, :]
        carry = (jnp.full((tq, 1), NEG_INF, F32), jnp.zeros((tq, 1), F32),
                 jnp.zeros((tq, HEAD_PAD), F32))

        def body(ki, carry, q=q):
            rows = pl.ds(pl.multiple_of(ki * tq, tq), tq)
            return update(q, k_ref[0, rows, :], v_ref[0, rows, :], carry, False)

        carry = lax.fori_loop(0, qi, body, carry)
        diag = slice(qi * tq, (qi + 1) * tq)
        _, l_fin, acc = update(q, k_ref[0, diag, :], v_ref[0, diag, :], carry, True)
        o_ref[0, diag, :] = (acc / l_fin).astype(o_ref.dtype)


def _attn_call(q, k, v):
    B, T, HP = q.shape
    spec = pl.BlockSpec((1, T, HEAD_PAD), lambda b, h: (b, 0, h))
    return pl.pallas_call(
        _attn_kernel,
        grid=(B, MLA_HEADS),
        in_specs=[spec, spec, spec],
        out_specs=spec,
        out_shape=jax.ShapeDtypeStruct((B, T, HP), BF16),
        compiler_params=_cparams("arbitrary", "arbitrary"),
        name="attn",
    )(q, k, v)


def _post_kernel(y_ref, bonus_ref, g_ref, o_ref, x_ref, mod_ref, lnw_ref, lnb_ref,
                 wo_r_ref, wo_m_ref, nffn_ref, x1_ref, h2_ref):
    y = y_ref[0]
    ones64 = _seg_ones(D_RWKV, RWKV_HEAD_DIM)
    mean = _segsum(y, ones64) * (1.0 / RWKV_HEAD_DIM)
    yc = y - mean
    var = _segsum(yc * yc, ones64) * (1.0 / RWKV_HEAD_DIM)
    yn = yc * lax.rsqrt(var + GN_EPS) * lnw_ref[...] + lnb_ref[...]
    yr = (yn + bonus_ref[0]) * g_ref[0]
    mix = _mm(yr, wo_r_ref[...]) + _dot(o_ref[0], wo_m_ref[...])
    g_a = mod_ref[0, 2:3, :]
    sh_f = mod_ref[0, 3:4, :]
    sc_f = mod_ref[0, 4:5, :]
    x1 = x_ref[0] + g_a * mix
    x1_ref[0] = x1
    ms = jnp.mean(x1 * x1, axis=-1, keepdims=True)
    h2_ref[0] = x1 * lax.rsqrt(ms + NORM_EPS) * nffn_ref[...] * (1.0 + sc_f) + sh_f


def _post_call(y, bonus, g, o_pad, x, mod3, ln_w, ln_b, w_out, norm_ffn, tm):
    B, T, D = x.shape
    HP = MLA_HEADS * HEAD_PAD
    wo_r = w_out[:D_RWKV].astype(BF16)
    wo_m = jnp.pad(w_out[D_RWKV:].reshape(MLA_HEADS, V_HEAD_DIM, D),
                   ((0, 0), (0, HEAD_PAD - V_HEAD_DIM), (0, 0))).reshape(HP, D).astype(BF16)
    tok = lambda n: pl.BlockSpec((1, tm, n), lambda b, t: (b, t, 0))
    row = lambda n: pl.BlockSpec((1, n), lambda b, t: (0, 0))
    full = lambda a: pl.BlockSpec(a.shape, lambda b, t: (0,) * a.ndim)
    return pl.pallas_call(
        _post_kernel,
        grid=(B, T // tm),
        in_specs=[tok(D_RWKV), tok(D_RWKV), tok(D_RWKV), tok(HP), tok(D),
                  pl.BlockSpec((1, 6, D), lambda b, t: (b, 0, 0)),
                  row(D_RWKV), row(D_RWKV), full(wo_r), full(wo_m), row(D)],
        out_specs=[tok(D), tok(D)],
        out_shape=[jax.ShapeDtypeStruct((B, T, D), F32)] * 2,
        compiler_params=_cparams("arbitrary", "arbitrary"),
        name="post",
    )(y, bonus, g, o_pad, x, mod3, ln_w.reshape(1, -1), ln_b.reshape(1, -1), wo_r, wo_m,
      norm_ffn.reshape(1, D))


def _first_index(mask, iota, size, axis):
    return jnp.min(jnp.where(mask, iota, size), axis=axis, keepdims=True)


def _route_kernel(h_ref, wr_ref, bias_ref, e_ref, w_ref, rank_ref, cnt_ref, base):
    tr = h_ref.shape[0]
    E = N_EXPERTS

    @pl.when(pl.program_id(0) == 0)
    def _():
        base[...] = jnp.zeros_like(base)

    logits = _mm3(wr_ref[...], h_ref[...], NT)
    scores = _sigmoid(logits)
    sel = scores + bias_ref[...]
    iota_g = lax.broadcasted_iota(I32, (GROUP_SIZE, tr), 0)
    gs_rows = []
    for gi in range(N_GROUPS):
        blk = sel[gi * GROUP_SIZE:(gi + 1) * GROUP_SIZE, :]
        m1 = jnp.max(blk, axis=0, keepdims=True)
        i1 = _first_index(blk == m1, iota_g, GROUP_SIZE, 0)
        m2 = jnp.max(jnp.where(iota_g == i1, NEG_INF, blk), axis=0, keepdims=True)
        gs_rows.append(m1 + m2)
    gs = jnp.concatenate(gs_rows, axis=0)
    iota8 = lax.broadcasted_iota(I32, (N_GROUPS, tr), 0)
    gmask = jnp.zeros((N_GROUPS, tr), jnp.bool_)
    for _ in range(TOPK_GROUPS):
        mg = jnp.max(gs, axis=0, keepdims=True)
        ig = _first_index(gs == mg, iota8, N_GROUPS, 0)
        hit = iota8 == ig
        gmask = gmask | hit
        gs = jnp.where(hit, NEG_INF, gs)
    msel = jnp.concatenate(
        [jnp.where(gmask[gi:gi + 1, :], sel[gi * GROUP_SIZE:(gi + 1) * GROUP_SIZE, :], NEG_INF)
         for gi in range(N_GROUPS)], axis=0)
    iota_e = lax.broadcasted_iota(I32, (E, tr), 0)
    e_rows, w_rows = [], []
    onehot = jnp.zeros((E, tr), F32)
    for _ in range(TOP_K):
        mv = jnp.max(msel, axis=0, keepdims=True)
        ie = _first_index(msel == mv, iota_e, E, 0)
        hit = iota_e == ie
        e_rows.append(ie)
        w_rows.append(jnp.sum(jnp.where(hit, scores, 0.0), axis=0, keepdims=True))
        onehot = jnp.where(hit, 1.0, onehot)
        msel = jnp.where(hit, NEG_INF, msel)
    top_e = jnp.concatenate(e_rows, axis=0)
    wts = jnp.concatenate(w_rows, axis=0)
    wts = wts / jnp.sum(wts, axis=0, keepdims=True) * ROUTED_SCALE
    ti = lax.broadcasted_iota(I32, (tr, tr), 0)
    tj = lax.broadcasted_iota(I32, (tr, tr), 1)
    upper = jnp.where(ti < tj, 1.0, 0.0).astype(BF16)
    pos = _dot(onehot.astype(BF16), upper) + base[...]
    rank_rows = [jnp.sum(jnp.where(iota_e == e_rows[j], pos, 0.0), axis=0, keepdims=True)
                 for j in range(TOP_K)]
    base[...] = base[...] + jnp.sum(onehot, axis=1, keepdims=True)
    e_ref[...] = top_e
    w_ref[...] = wts
    rank_ref[...] = jnp.concatenate(rank_rows, axis=0).astype(I32)
    cnt_ref[...] = base[...].astype(I32)


def _route_call(h2, w_router, router_bias, tr):
    N, D = h2.shape
    E = N_EXPERTS
    out_kn = pl.BlockSpec((TOP_K, tr), lambda i: (0, i))
    return pl.pallas_call(
        _route_kernel,
        grid=(N // tr,),
        in_specs=[pl.BlockSpec((tr, D), lambda i: (i, 0)),
                  pl.BlockSpec((E, D), lambda i: (0, 0)),
                  pl.BlockSpec((E, 1), lambda i: (0, 0))],
        out_specs=[out_kn, out_kn, out_kn, pl.BlockSpec((E, 1), lambda i: (0, 0))],
        out_shape=[jax.ShapeDtypeStruct((TOP_K, N), I32), jax.ShapeDtypeStruct((TOP_K, N), F32),
                   jax.ShapeDtypeStruct((TOP_K, N), I32), jax.ShapeDtypeStruct((E, 1), I32)],
        scratch_shapes=[pltpu.VMEM((E, 1), F32)],
        compiler_params=_cparams("arbitrary"),
        name="route",
    )(h2, w_router.T, router_bias.reshape(E, 1))


def _dest_kernel(e_ref, rank_ref, start_ref, d_ref):
    tr = e_ref.shape[1]
    iota_e = lax.broadcasted_iota(I32, (N_EXPERTS, tr), 0)
    starts = start_ref[...]
    rows = [jnp.sum(jnp.where(iota_e == e_ref[j:j + 1, :], starts, 0), axis=0, keepdims=True)
            for j in range(TOP_K)]
    d_ref[...] = jnp.concatenate(rows, axis=0) + rank_ref[...]


def _dest_call(top_e, rank, pad_starts, tr):
    K, N = top_e.shape
    spec = pl.BlockSpec((K, tr), lambda i: (0, i))
    return pl.pallas_call(
        _dest_kernel,
        grid=(N // tr,),
        in_specs=[spec, spec, pl.BlockSpec((N_EXPERTS, 1), lambda i: (0, 0))],
        out_specs=spec,
        out_shape=jax.ShapeDtypeStruct((K, N), I32),
        compiler_params=_cparams("arbitrary"),
        name="dest",
    )(top_e, rank, pad_starts.reshape(N_EXPERTS, 1))


def _row_copy(src, s_row, dst, d_row, sem):
    return pltpu.make_async_copy(src.at[pl.ds(s_row, 1), :], dst.at[pl.ds(d_row, 1), :], sem)


def _dispatch_kernel(dest_hbm, h_ref, xs_in, xs_out, idx, sem_idx, sem_rows):
    del xs_in
    i = pl.program_id(0)
    td = h_ref.shape[0]
    cp = pltpu.make_async_copy(dest_hbm.at[i], idx, sem_idx)
    cp.start()
    cp.wait()

    def issue(tt, carry):
        base = pl.multiple_of(tt * SUBLANES, SUBLANES)
        for i in range(SUBLANES):
            for j in range(TOP_K):
                _row_copy(h_ref, base + i, xs_out, idx[j, base + i], sem_rows).start()
        return carry

    lax.fori_loop(0, td // SUBLANES, issue, 0)

    def drain(t, carry):
        for j in range(TOP_K):
            _row_copy(h_ref, 0, xs_out, 0, sem_rows).wait()
        return carry

    lax.fori_loop(0, td, drain, 0)


def _dispatch_call(dest3, h2, xs_zero, td):
    N, D = h2.shape
    P = xs_zero.shape[0]
    return pl.pallas_call(
        _dispatch_kernel,
        grid=(N // td,),
        in_specs=[pl.BlockSpec(memory_space=pl.ANY),
                  pl.BlockSpec((td, D), lambda i: (i, 0)),
                  pl.BlockSpec(memory_space=pl.ANY)],
        out_specs=pl.BlockSpec(memory_space=pl.ANY),
        out_shape=jax.ShapeDtypeStruct((P, D), h2.dtype),
        scratch_shapes=[pltpu.SMEM((TOP_K, td), I32), pltpu.SemaphoreType.DMA,
                        pltpu.SemaphoreType.DMA],
        input_output_aliases={2: 0},
        compiler_params=_cparams("arbitrary"),
        name="dispatch",
    )(dest3, h2, xs_zero)


def _moe_kernel(be_ref, nu_ref, xs_ref, wgu_ref, wdn_ref, y_ref, wgu_bf, wdn_bf):
    i = pl.program_id(0)

    @pl.when(i < nu_ref[0])
    def _():
        @pl.when((i == 0) | (be_ref[i] != be_ref[jnp.maximum(i - 1, 0)]))
        def _():
            wgu_bf[...] = wgu_ref[0].astype(BF16)
            wdn_bf[...] = wdn_ref[0].astype(BF16)

        gu = _dot(xs_ref[...].astype(BF16), wgu_bf[...])
        act = _silu(gu[:, :D_EXPERT]) * gu[:, D_EXPERT:]
        y_ref[...] = _dot(act.astype(BF16), wdn_bf[...])


def _moe_call(block_expert, n_used, xs, w_gu, w_dn):
    P, D = xs.shape
    nb = P // MOE_BLOCK
    blk = lambda i, be, nu: (jnp.minimum(i, nu[0] - 1), 0)
    return pl.pallas_call(
        _moe_kernel,
        grid_spec=pltpu.PrefetchScalarGridSpec(
            num_scalar_prefetch=2,
            grid=(nb,),
            in_specs=[pl.BlockSpec((MOE_BLOCK, D), blk),
                      pl.BlockSpec((1, D, 2 * D_EXPERT), lambda i, be, nu: (be[i], 0, 0)),
                      pl.BlockSpec((1, D_EXPERT, D), lambda i, be, nu: (be[i], 0, 0))],
            out_specs=pl.BlockSpec((MOE_BLOCK, D), blk),
            scratch_shapes=[pltpu.VMEM((D, 2 * D_EXPERT), BF16), pltpu.VMEM((D_EXPERT, D), BF16)]),
        out_shape=jax.ShapeDtypeStruct((P, D), F32),
        compiler_params=_cparams("arbitrary"),
        name="moe",
    )(block_expert, n_used, xs, w_gu, w_dn)


def _combine_kernel(dest_hbm, w_ref, y_hbm, h_ref, x1_ref, mod_ref, wsg_ref, wsd_ref, o_ref,
                    idx, rows, sem_idx, sem_rows):
    i = pl.program_id(0)
    tc = h_ref.shape[0]
    cp = pltpu.make_async_copy(dest_hbm.at[i], idx, sem_idx)
    cp.start()
    cp.wait()

    def issue(tt, carry):
        base = pl.multiple_of(tt * SUBLANES, SUBLANES)
        for i in range(SUBLANES):
            for j in range(TOP_K):
                _row_copy(y_hbm, idx[j, base + i], rows.at[j], base + i, sem_rows).start()
        return carry

    lax.fori_loop(0, tc // SUBLANES, issue, 0)
    gu = _mm(h_ref[...], wsg_ref[...])
    act = _silu(gu[:, :D_EXPERT]) * gu[:, D_EXPERT:]
    ffn = _mm(act, wsd_ref[...])

    def drain(t, carry):
        for j in range(TOP_K):
            _row_copy(y_hbm, 0, rows.at[j], 0, sem_rows).wait()
        return carry

    lax.fori_loop(0, tc, drain, 0)
    wts = w_ref[...]
    for j in range(TOP_K):
        ffn = ffn + rows[j] * wts[:, j:j + 1]
    g_f = mod_ref[0, 5:6, :]
    o_ref[...] = x1_ref[...] + g_f * ffn


def _combine_call(dest3, w_nk, y, h2, x1, mod3, w_sh_gu, w_sh_dn, tokens_per_batch, tc):
    N, D = h2.shape
    tiles_per_batch = tokens_per_batch // tc
    tok = pl.BlockSpec((tc, D), lambda i: (i, 0))
    wsg = w_sh_gu.astype(BF16)
    wsd = w_sh_dn.astype(BF16)
    return pl.pallas_call(
        _combine_kernel,
        grid=(N // tc,),
        in_specs=[pl.BlockSpec(memory_space=pl.ANY),
                  pl.BlockSpec((tc, TOP_K), lambda i: (i, 0)),
                  pl.BlockSpec(memory_space=pl.ANY),
                  tok, tok,
                  pl.BlockSpec((1, 6, D), lambda i: (i // tiles_per_batch, 0, 0)),
                  pl.BlockSpec(wsg.shape, lambda i: (0, 0)),
                  pl.BlockSpec(wsd.shape, lambda i: (0, 0))],
        out_specs=tok,
        out_shape=jax.ShapeDtypeStruct((N, D), F32),
        scratch_shapes=[pltpu.SMEM((TOP_K, tc), I32), pltpu.VMEM((TOP_K, tc, D), F32),
                        pltpu.SemaphoreType.DMA, pltpu.SemaphoreType.DMA],
        compiler_params=_cparams("arbitrary"),
        name="combine",
    )(dest3, w_nk, y, h2, x1, mod3, wsg, wsd)


def _tile(n, pref):
    t = min(n, pref)
    assert n % t == 0, (n, t)
    return t


def _layer(x, mod3, positions, norm_mix, w_in, rwkv_mu, decay_w0, decay_up, iclr_a0, iclr_up,
           gate_up, rwkv_k_k, rwkv_k_a, rwkv_r_k, ln_x_w, ln_x_b, q_a_norm, w_q_b, kv_a_norm,
           w_kv_b, q_norm, k_norm, w_out, norm_ffn, w_router, router_bias, w_e_gate_up, w_e_down,
           w_sh_gate_up, w_sh_down):
    B, T, D = x.shape
    N = B * T
    assert T % SCAN_CHUNK == 0
    (r, lw, k2, v, kk, akk, g, bonus, q_pad, k_pad, v_pad) = _pre_call(
        x, mod3, positions, norm_mix, w_in, rwkv_mu, decay_w0, decay_up, iclr_a0, iclr_up,
        gate_up, rwkv_k_k, rwkv_k_a, rwkv_r_k, q_a_norm, w_q_b, kv_a_norm, w_kv_b, q_norm, k_norm,
        tm=_tile(T, 512))
    y = _scan_call(r, lw, k2, v, kk, akk)
    o_pad = _attn_call(q_pad, k_pad, v_pad)
    x1, h2 = _post_call(y, bonus, g, o_pad, x, mod3, ln_x_w, ln_x_b, w_out, norm_ffn,
                        tm=_tile(T, 512))
    x1 = x1.reshape(N, D)
    h2 = h2.reshape(N, D)

    tr = _tile(N, 512)
    top_e, wts, rank, counts = _route_call(h2, w_router, router_bias, tr)
    counts = counts.reshape(N_EXPERTS)
    padded = (counts + MOE_BLOCK - 1) // MOE_BLOCK * MOE_BLOCK
    pad_ends = jnp.cumsum(padded)
    pad_starts = pad_ends - padded
    n_blocks = (N * TOP_K + N_EXPERTS * (MOE_BLOCK - 1)) // MOE_BLOCK
    block_expert = jnp.minimum(
        jnp.searchsorted(pad_ends, jnp.arange(n_blocks, dtype=I32) * MOE_BLOCK, side='right'),
        N_EXPERTS - 1).astype(I32)
    n_used = (pad_ends[-1:] // MOE_BLOCK).astype(I32)
    dest = _dest_call(top_e, rank, pad_starts.astype(I32), tr)

    td = _tile(T, 256)
    dest3 = dest.reshape(TOP_K, N // td, td).transpose(1, 0, 2)
    xs = _dispatch_call(dest3, h2, jnp.zeros((n_blocks * MOE_BLOCK, D), F32), td)
    y_e = _moe_call(block_expert, n_used, xs, w_e_gate_up, w_e_down)
    out = _combine_call(dest3, wts.T, y_e, h2, x1, mod3, w_sh_gate_up, w_sh_down, T, td)
    return out.reshape(B, T, D)


def kernel(x, c, positions, ada_w, ada_b, norm_mix, w_in, rwkv_mu, decay_w0, decay_up, iclr_a0, iclr_up, gate_up, rwkv_k_k, rwkv_k_a, rwkv_r_k, ln_x_w, ln_x_b, q_a_norm, w_q_b, kv_a_norm, w_kv_b, q_norm, k_norm, w_out, norm_ffn, w_router, router_bias, w_e_gate_up, w_e_down, w_sh_gate_up, w_sh_down):
    B, T, D = x.shape
    depth = ada_w.shape[0]
    for l in range(depth):
        mod3 = _mod_call(c, ada_w[l], ada_b[l]).reshape(B, 6, D)
        x = _layer(x, mod3, positions, norm_mix[l], w_in[l], rwkv_mu[l], decay_w0[l], decay_up[l],
                   iclr_a0[l], iclr_up[l], gate_up[l], rwkv_k_k[l], rwkv_k_a[l], rwkv_r_k[l],
                   ln_x_w[l], ln_x_b[l], q_a_norm[l], w_q_b[l], kv_a_norm[l], w_kv_b[l],
                   q_norm[l], k_norm[l], w_out[l], norm_ffn[l], w_router[l], router_bias[l],
                   w_e_gate_up[l], w_e_down[l], w_sh_gate_up[l], w_sh_down[l])
    return x
```

```python
import functools
import math

import jax
import jax.numpy as jnp
import numpy as np
from jax import lax
from jax.experimental import pallas as pl
from jax.experimental.pallas import tpu as pltpu

F32 = jnp.float32
BF16 = jnp.bfloat16
I32 = jnp.int32

NORM_EPS = 1e-6
GN_EPS = 64e-5
RWKV_HEADS = 8
RWKV_HEAD_DIM = 64
D_RWKV = 512
DECAY_LORA = 32
ICLR_LORA = 32
GATE_LORA = 96
MLA_HEADS = 8
QK_NOPE_DIM = 64
QK_ROPE_DIM = 32
QK_HEAD_DIM = 96
V_HEAD_DIM = 64
Q_LORA_RANK = 256
KV_LORA_RANK = 128
ROPE_THETA = 10000.0
N_EXPERTS = 256
TOP_K = 8
N_GROUPS = 8
TOPK_GROUPS = 4
GROUP_SIZE = N_EXPERTS // N_GROUPS
D_EXPERT = 256
ROUTED_SCALE = 2.5
MOE_ROWS = 512

LANES = 128
SUBLANES = 8
HEAD_PAD = 128
VMEM_LIMIT = 56 * 1024 * 1024

SCAN_CHUNK = 64
SCAN_BLOCK = 256
ATTN_TILE = 512
ATTN_Q_SCALE = QK_HEAD_DIM ** -0.5 * math.log2(math.e)
NEG_INF = float("-inf")


def _cparams(*sem):
    return pltpu.CompilerParams(dimension_semantics=sem, vmem_limit_bytes=VMEM_LIMIT)


def _split2(a):
    hi = a.astype(BF16)
    lo = (a - hi.astype(F32)).astype(BF16)
    return hi, lo


def _split3(a):
    hi = a.astype(BF16)
    r1 = a - hi.astype(F32)
    mid = r1.astype(BF16)
    lo = (r1 - mid.astype(F32)).astype(BF16)
    return hi, mid, lo


def _dot(a, b, dims=None):
    if dims is None:
        return jnp.dot(a, b, preferred_element_type=F32)
    return lax.dot_general(a, b, (dims, ((), ())), preferred_element_type=F32)


def _mm(a, b, dims=None):
    return _dot(a.astype(BF16), b.astype(BF16), dims)


def _mm3(a, b, dims=None):
    ah, al = _split2(a)
    bh, bl = _split2(b)
    return _dot(ah, bh, dims) + (_dot(ah, bl, dims) + _dot(al, bh, dims))


def _mm_exact_rhs(a, b_exact_bf16, dims=None):
    h, m, l = _split3(a)
    return _dot(h, b_exact_bf16, dims) + (_dot(m, b_exact_bf16, dims) + _dot(l, b_exact_bf16, dims))


NT = ((1,), (1,))
TN = ((0,), (0,))


def _sigmoid(z):
    return 1.0 / (1.0 + jnp.exp(-z))


def _silu(z):
    return z * _sigmoid(z)


def _seg_ones(width, seg):
    r = lax.broadcasted_iota(I32, (width, width), 0) // seg
    c = lax.broadcasted_iota(I32, (width, width), 1) // seg
    return jnp.where(r == c, 1.0, 0.0).astype(BF16)


def _segsum(a, ones_bd):
    hi, lo = _split2(a)
    return _dot(hi, ones_bd) + _dot(lo, ones_bd)


def _mod_kernel(c_ref, w_ref, b_ref, o_ref):
    ca = _silu(c_ref[...])
    o_ref[...] = _mm3(ca, w_ref[...]) + b_ref[...]


def _mod_call(c, ada_w, ada_b):
    B, D = c.shape
    n6 = ada_w.shape[1]
    tn = D
    return pl.pallas_call(
        _mod_kernel,
        grid=(n6 // tn,),
        in_specs=[pl.BlockSpec((B, D), lambda j: (0, 0)),
                  pl.BlockSpec((D, tn), lambda j: (0, j)),
                  pl.BlockSpec((1, tn), lambda j: (0, j))],
        out_specs=pl.BlockSpec((B, tn), lambda j: (0, j)),
        out_shape=jax.ShapeDtypeStruct((B, n6), F32),
        compiler_params=_cparams("arbitrary"),
        name="mod",
    )(c, ada_w, ada_b.reshape(1, n6))


def _pre_kernel(x_ref, mod_ref, pos_ref, nmix_ref, wrkv_ref, wlora_ref, wmla_ref,
                mu_rkv_ref, mu_lora_ref, wup_ref, w0_ref, a0_ref, kk_ref, ka_ref, rk_ref,
                qan_ref, wqb_ref, kvan_ref, wkb_ref, wvb_ref, qn_ref, kn_ref, invf_ref,
                r_ref, lw_ref, k_ref, v_ref, kkn_ref, akk_ref, g_ref, bonus_ref,
                q_ref, kout_ref, vout_ref,
                carry_rkv, carry_lora):
    ti = pl.program_id(1)
    tm = x_ref.shape[1]

    @pl.when(ti == 0)
    def _():
        carry_rkv[...] = jnp.zeros_like(carry_rkv)
        carry_lora[...] = jnp.zeros_like(carry_lora)

    xb = x_ref[0]
    sh_a = mod_ref[0, 0:1, :]
    sc_a = mod_ref[0, 1:2, :]
    ms = jnp.mean(xb * xb, axis=-1, keepdims=True)
    h = xb * lax.rsqrt(ms + NORM_EPS) * nmix_ref[...] * (1.0 + sc_a) + sh_a
    hb = h.astype(BF16)
    u_rkv = _dot(hb, wrkv_ref[...])
    u_lora = _dot(hb, wlora_ref[...])
    u_mla = _dot(hb, wmla_ref[...])

    row0 = lax.broadcasted_iota(I32, (tm, 1), 0) == 0

    def shift(u, carry):
        prev = jnp.where(row0, carry[...], pltpu.roll(u, 1, 0))
        carry[...] = u[tm - 1:tm, :]
        return prev

    prev_rkv = shift(u_rkv, carry_rkv)
    prev_lora = shift(u_lora, carry_lora)
    us = u_rkv + (prev_rkv - u_rkv) * mu_rkv_ref[...]
    ul = u_lora + (prev_lora - u_lora) * mu_lora_ref[...]
    r = us[:, 0:D_RWKV]
    k = us[:, D_RWKV:2 * D_RWKV]
    v = us[:, 2 * D_RWKV:3 * D_RWKV]

    lane_l = lax.broadcasted_iota(I32, ul.shape, 1)
    t_in = jnp.where(lane_l < DECAY_LORA, jnp.tanh(ul),
                     jnp.where(lane_l < DECAY_LORA + ICLR_LORA, ul, _sigmoid(ul)))
    up = _mm(t_in, wup_ref[...])
    z = w0_ref[...] + up[:, 0:D_RWKV]
    lw = (-math.exp(-0.5)) * _sigmoid(z)
    a = _sigmoid(a0_ref[...] + up[:, D_RWKV:2 * D_RWKV])
    g = up[:, 2 * D_RWKV:3 * D_RWKV]

    ones64 = _seg_ones(D_RWKV, RWKV_HEAD_DIM)
    kk = k * kk_ref[...]
    ss = _segsum(kk * kk, ones64)
    kk = kk * lax.rsqrt(jnp.maximum(ss, 1e-24))
    k2 = k * (1.0 + (a - 1.0) * ka_ref[...])
    bonus = _segsum(r * k2 * rk_ref[...], ones64) * v

    r_ref[0] = r
    lw_ref[0] = lw
    k_ref[0] = k2
    v_ref[0] = v
    kkn_ref[0] = kk
    akk_ref[0] = a * kk
    g_ref[0] = g
    bonus_ref[0] = bonus

    q_lat = u_mla[:, 0:Q_LORA_RANK]
    kv_lat = u_mla[:, Q_LORA_RANK:Q_LORA_RANK + KV_LORA_RANK]
    kpe_tile = u_mla[:, Q_LORA_RANK + KV_LORA_RANK:]
    qn = q_lat * lax.rsqrt(jnp.mean(q_lat * q_lat, axis=-1, keepdims=True) + NORM_EPS) * qan_ref[...]
    kvn = kv_lat * lax.rsqrt(jnp.mean(kv_lat * kv_lat, axis=-1, keepdims=True) + NORM_EPS) * kvan_ref[...]
    q_raw = _mm(qn, wqb_ref[...])
    kvb = kvn.astype(BF16)
    k_raw = _dot(kvb, wkb_ref[...])
    v_pad = _dot(kvb, wvb_ref[...])
    kpe_h = pltpu.roll(kpe_tile, QK_NOPE_DIM, 1)

    half = QK_ROPE_DIM // 2
    ang_t = invf_ref[...] * pos_ref[0].astype(F32)
    frow = lax.broadcasted_iota(I32, (half, HEAD_PAD), 0)
    flane = lax.broadcasted_iota(I32, (half, HEAD_PAD), 1)
    at_x1 = flane == frow + QK_NOPE_DIM
    at_x2 = flane == frow + QK_NOPE_DIM + half
    e_cos = jnp.where(at_x1 | at_x2, 1.0, 0.0).astype(BF16)
    e_sin = jnp.concatenate([jnp.where(at_x1, -1.0, 0.0), jnp.where(at_x2, 1.0, 0.0)],
                            axis=1).astype(BF16)
    lane = lax.broadcasted_iota(I32, (1, HEAD_PAD), 1)
    off_rope = jnp.where((lane >= QK_NOPE_DIM) & (lane < QK_HEAD_DIM), 0.0, 1.0)
    cos_t = _mm_exact_rhs(jnp.cos(ang_t), e_cos, TN) + off_rope
    sin2 = _mm_exact_rhs(jnp.sin(ang_t), e_sin, TN)
    s1 = sin2[:, :HEAD_PAD]
    s2 = sin2[:, HEAD_PAD:]

    def tables(gain, scale):
        g = gain * scale
        return (cos_t * g, s1 * pltpu.roll(g, HEAD_PAD - half, 1), s2 * pltpu.roll(g, half, 1))

    def norm_rope(xh, tabs):
        c_g, s1_g, s2_g = tabs
        ssq = jnp.sum(xh * xh, axis=-1, keepdims=True) * (1.0 / QK_HEAD_DIM)
        rot = xh * c_g + pltpu.roll(xh, HEAD_PAD - half, 1) * s1_g + pltpu.roll(xh, half, 1) * s2_g
        return rot * lax.rsqrt(ssq + NORM_EPS)

    q_tabs = tables(qn_ref[...], ATTN_Q_SCALE)
    k_tabs = tables(kn_ref[...], 1.0)
    for hh in range(MLA_HEADS):
        sl = slice(hh * HEAD_PAD, (hh + 1) * HEAD_PAD)
        q_ref[0, :, sl] = norm_rope(q_raw[:, sl], q_tabs).astype(BF16)
        kout_ref[0, :, sl] = norm_rope(k_raw[:, sl] + kpe_h, k_tabs).astype(BF16)
    vout_ref[0] = v_pad.astype(BF16)


def _pad_heads(w, n_heads, width):
    kdim = w.shape[0]
    w = w.reshape(kdim, n_heads, width)
    w = jnp.pad(w, ((0, 0), (0, 0), (0, HEAD_PAD - width)))
    return w.reshape(kdim, n_heads * HEAD_PAD)


def _pre_call(x, mod3, positions, norm_mix, w_in, rwkv_mu, decay_w0, decay_up, iclr_a0, iclr_up,
              gate_up, k_k, k_a, r_k, q_a_norm, w_q_b, kv_a_norm, w_kv_b, q_norm, k_norm, tm):
    B, T, D = x.shape
    n_rkv = 3 * D_RWKV
    n_lora = DECAY_LORA + ICLR_LORA + GATE_LORA
    LORA_PAD = 256
    MLA_PAD = 512
    n_mla = Q_LORA_RANK + KV_LORA_RANK + QK_ROPE_DIM
    w_rkv = w_in[:, :n_rkv].astype(BF16)
    w_lora = jnp.pad(w_in[:, n_rkv:n_rkv + n_lora], ((0, 0), (0, LORA_PAD - n_lora))).astype(BF16)
    w_mla = jnp.pad(w_in[:, n_rkv + n_lora:], ((0, 0), (0, MLA_PAD - n_mla))).astype(BF16)
    mu_rkv = rwkv_mu[:n_rkv].reshape(1, n_rkv)
    mu_lora = jnp.pad(rwkv_mu[n_rkv:], (0, LORA_PAD - n_lora)).reshape(1, LORA_PAD)
    w_up = jnp.zeros((LORA_PAD, n_rkv), F32)
    w_up = w_up.at[0:DECAY_LORA, 0:D_RWKV].set(decay_up)
    w_up = w_up.at[DECAY_LORA:DECAY_LORA + ICLR_LORA, D_RWKV:2 * D_RWKV].set(iclr_up)
    w_up = w_up.at[DECAY_LORA + ICLR_LORA:n_lora, 2 * D_RWKV:].set(gate_up)
    w_up = w_up.astype(BF16)
    w_qb = _pad_heads(w_q_b, MLA_HEADS, QK_HEAD_DIM).astype(BF16)
    w_kv3 = w_kv_b.reshape(KV_LORA_RANK, MLA_HEADS, QK_NOPE_DIM + V_HEAD_DIM)
    w_kb = _pad_heads(w_kv3[:, :, :QK_NOPE_DIM].reshape(KV_LORA_RANK, -1), MLA_HEADS, QK_NOPE_DIM).astype(BF16)
    w_vb = _pad_heads(w_kv3[:, :, QK_NOPE_DIM:].reshape(KV_LORA_RANK, -1), MLA_HEADS, V_HEAD_DIM).astype(BF16)
    qn_pad = jnp.pad(q_norm, (0, HEAD_PAD - QK_HEAD_DIM)).reshape(1, HEAD_PAD)
    kn_pad = jnp.pad(k_norm, (0, HEAD_PAD - QK_HEAD_DIM)).reshape(1, HEAD_PAD)
    inv_freq = ROPE_THETA ** (-jnp.arange(0, QK_ROPE_DIM, 2, dtype=F32) / QK_ROPE_DIM)
    invf = inv_freq.reshape(QK_ROPE_DIM // 2, 1)
    pos3 = positions.reshape(B, 1, T)
    HP = MLA_HEADS * HEAD_PAD

    row = lambda n: pl.BlockSpec((1, n), lambda b, t: (0, 0))
    full = lambda a: pl.BlockSpec(a.shape, lambda b, t: (0,) * a.ndim)
    tok = lambda n: pl.BlockSpec((1, tm, n), lambda b, t: (b, t, 0))
    outs = ([jax.ShapeDtypeStruct((B, T, D_RWKV), F32)] * 8
            + [jax.ShapeDtypeStruct((B, T, HP), BF16)] * 3)
    return pl.pallas_call(
        _pre_kernel,
        grid=(B, T // tm),
        in_specs=[tok(D),
                  pl.BlockSpec((1, 6, D), lambda b, t: (b, 0, 0)),
                  pl.BlockSpec((1, 1, tm), lambda b, t: (b, 0, t)),
                  row(D), full(w_rkv), full(w_lora), full(w_mla),
                  row(n_rkv), row(LORA_PAD), full(w_up), row(D_RWKV), row(D_RWKV),
                  row(D_RWKV), row(D_RWKV), row(D_RWKV),
                  row(Q_LORA_RANK), full(w_qb), row(KV_LORA_RANK), full(w_kb), full(w_vb),
                  row(HEAD_PAD), row(HEAD_PAD), full(invf)],
        out_specs=[tok(D_RWKV)] * 8 + [tok(HP)] * 3,
        out_shape=outs,
        scratch_shapes=[pltpu.VMEM((1, n_rkv), F32), pltpu.VMEM((1, LORA_PAD), F32)],
        compiler_params=_cparams("arbitrary", "arbitrary"),
        name="pre",
    )(x, mod3, pos3, norm_mix.reshape(1, D), w_rkv, w_lora, w_mla, mu_rkv, mu_lora, w_up,
      decay_w0.reshape(1, -1), iclr_a0.reshape(1, -1), k_k.reshape(1, -1), k_a.reshape(1, -1),
      r_k.reshape(1, -1), q_a_norm.reshape(1, -1), w_qb, kv_a_norm.reshape(1, -1), w_kb, w_vb,
      qn_pad, kn_pad, invf)


def _scan_kernel(r_ref, lw_ref, k_ref, v_ref, kk_ref, akk_ref, y_ref, state):
    C = SCAN_CHUNK
    n_chunks = r_ref.shape[1] // C
    n_pairs = r_ref.shape[2] // LANES

    @pl.when(pl.program_id(1) == 0)
    def _():
        state[...] = jnp.zeros_like(state)

    ri = lax.broadcasted_iota(I32, (C, C), 0)
    ci = lax.broadcasted_iota(I32, (C, C), 1)
    tri_incl = jnp.where(ci <= ri, 1.0, 0.0).astype(BF16)
    r2 = lax.broadcasted_iota(I32, (2 * C, 2 * C), 0)
    c2 = lax.broadcasted_iota(I32, (2 * C, 2 * C), 1)
    same = (r2 >= C) == (c2 >= C)
    strict = same & (c2 < r2)
    incl = same & (c2 <= r2)
    eye = jnp.where(c2 == r2, 1.0, 0.0)
    head0 = lax.broadcasted_iota(I32, (C, LANES), 1) < RWKV_HEAD_DIM

    def stack2(a):
        return jnp.concatenate([jnp.where(head0, a, 0.0), jnp.where(head0, 0.0, a)], axis=0)

    C2 = 2 * C
    cat0 = lambda *a: jnp.concatenate(a, axis=0)
    cat1 = lambda *a: jnp.concatenate(a, axis=1)

    items = []
    for c in range(n_chunks):
        rows = slice(c * C, (c + 1) * C)
        lw = lw_ref[0, rows, :]
        cum = _mm_exact_rhs_left(tri_incl, lw)
        cum_end = cum[C - 1:C, :]
        w_end = jnp.exp(cum_end)
        e_pos = jnp.exp(cum)
        e_neg = jnp.exp(-cum)
        e_prev = jnp.exp(cum - lw)
        e_end = jnp.exp(cum_end - cum)
        kk = kk_ref[0, rows, :]
        k2 = k_ref[0, rows, :]
        pneg = -akk_ref[0, rows, :]
        vv = v_ref[0, rows, :]
        rt = r_ref[0, rows, :] * e_pos
        bt = kk * e_prev
        pt = pneg * e_neg
        kt = k2 * e_neg
        ph = pneg * e_end
        kh = k2 * e_end
        for pp in range(n_pairs):
            sl = slice(pp * LANES, (pp + 1) * LANES)
            items.append(dict(
                c=c, p=pp, w_end=w_end[:, sl],
                bt2=stack2(bt[:, sl]).astype(BF16), rt2=stack2(rt[:, sl]).astype(BF16),
                pk2=cat0(stack2(pt[:, sl]), stack2(kt[:, sl])).astype(BF16),
                phkh2=cat0(stack2(ph[:, sl]), stack2(kh[:, sl])).astype(BF16),
                v2=stack2(vv[:, sl])))
    for it in items:
        ab = _dot(cat0(it['bt2'], it['rt2']), it['pk2'], NT)
        it['a_ab'] = jnp.where(strict, ab[:C2, :C2], 0.0)
        it['a_ak'] = jnp.where(strict, ab[:C2, C2:], 0.0).astype(BF16)
        it['b_rpk'] = cat1(jnp.where(incl, ab[C2:, :C2], 0.0), jnp.where(incl, ab[C2:, C2:], 0.0)).astype(BF16)
        it['tinv'] = eye + it['a_ab']
    for it in items:
        it['apow'] = _mm(it['a_ab'], it['a_ab'])
    for _ in range(int(math.log2(C)) - 1):
        for it in items:
            both = _mm(cat0(it['apow'], it['tinv']), it['apow'])
            it['apow'] = both[:C2]
            it['tinv'] = it['tinv'] + both[C2:]
    for it in items:
        it['akv'] = _dot(it['a_ak'], it['v2'].astype(BF16))
    for it in items:
        tt = _dot(it['tinv'].astype(BF16), cat1(it['bt2'], it['akv'].astype(BF16)))
        it['tb_rt'] = cat0(tt[:, :LANES].astype(BF16), it['rt2'])
        it['tav'] = tt[:, LANES:]
    for it in items:
        pp = it['p']
        s0 = state[pp]
        top = _dot(it['tb_rt'], s0.astype(BF16), NT)
        u2 = top[:C2] + it['tav']
        uv = cat0(u2, it['v2']).astype(BF16)
        y2 = top[C2:] + _dot(it['b_rpk'], uv)
        state[pp] = s0 * it['w_end'] + _dot(uv, it['phkh2'], TN)
        y_ref[0, it['c'] * C:(it['c'] + 1) * C, pp * LANES:(pp + 1) * LANES] = y2[0:C] + y2[C:C2]


def _mm_exact_rhs_left(b_exact_bf16, a):
    h, m, l = _split3(a)
    return _dot(b_exact_bf16, h) + (_dot(b_exact_bf16, m) + _dot(b_exact_bf16, l))


def _scan_call(r, lw, k2, v, kk, akk):
    B, T, W = r.shape
    tb = _tile(T, SCAN_BLOCK)
    spec = pl.BlockSpec((1, tb, W), lambda b, c: (b, c, 0))
    return pl.pallas_call(
        _scan_kernel,
        grid=(B, T // tb),
        in_specs=[spec] * 6,
        out_specs=spec,
        out_shape=jax.ShapeDtypeStruct((B, T, W), F32),
        scratch_shapes=[pltpu.VMEM((W // LANES, 2 * RWKV_HEAD_DIM, LANES), F32)],
        compiler_params=_cparams("arbitrary", "arbitrary"),
        name="scan",
    )(r, lw, k2, v, kk, akk)


def _attn_kernel(q_ref, k_ref, v_ref, o_ref):
    T = q_ref.shape[1]
    tq = min(T, ATTN_TILE)
    row = lax.broadcasted_iota(I32, (tq, tq), 0)
    col = lax.broadcasted_iota(I32, (tq, tq), 1)
    causal = col <= row

    def update(q, kt, vt, carry, mask):
        m_old, l_old, acc = carry
        s = _dot(q, kt, NT)
        if mask:
            s = jnp.where(causal, s, NEG_INF)
        m_new = jnp.maximum(m_old, jnp.max(s, axis=-1, keepdims=True))
        alpha = jnp.exp2(m_old - m_new)
        p = jnp.exp2(s - m_new)
        l_new = alpha * l_old + jnp.sum(p, axis=-1, keepdims=True)
        acc = alpha * acc + _dot(p.astype(BF16), vt)
        return m_new, l_new, acc

    for qi in range(T // tq):
        q = q_ref[0, qi * tq:(qi + 1) * tq, :]
        carry = (jnp.full((tq, 1), NEG_INF, F32), jnp.zeros((tq, 1), F32),
                 jnp.zeros((tq, HEAD_PAD), F32))

        def body(ki, carry, q=q):
            rows = pl.ds(pl.multiple_of(ki * tq, tq), tq)
            return update(q, k_ref[0, rows, :], v_ref[0, rows, :], carry, False)

        carry = lax.fori_loop(0, qi, body, carry)
        diag = slice(qi * tq, (qi + 1) * tq)
        _, l_fin, acc = update(q, k_ref[0, diag, :], v_ref[0, diag, :], carry, True)
        o_ref[0, diag, :] = (acc / l_fin).astype(o_ref.dtype)


def _attn_call(q, k, v):
    B, T, HP = q.shape
    spec = pl.BlockSpec((1, T, HEAD_PAD), lambda b, h: (b, 0, h))
    return pl.pallas_call(
        _attn_kernel,
        grid=(B, MLA_HEADS),
        in_specs=[spec, spec, spec],
        out_specs=spec,
        out_shape=jax.ShapeDtypeStruct((B, T, HP), BF16),
        compiler_params=_cparams("arbitrary", "arbitrary"),
        name="attn",
    )(q, k, v)


def _post_kernel(y_ref, bonus_ref, g_ref, o_ref, x_ref, mod_ref, lnw_ref, lnb_ref,
                 wo_r_ref, wo_m_ref, nffn_ref, x1_ref, h2_ref, h2t_ref):
    y = y_ref[0]
    ones64 = _seg_ones(D_RWKV, RWKV_HEAD_DIM)
    mean = _segsum(y, ones64) * (1.0 / RWKV_HEAD_DIM)
    yc = y - mean
    var = _segsum(yc * yc, ones64) * (1.0 / RWKV_HEAD_DIM)
    yn = yc * lax.rsqrt(var + GN_EPS) * lnw_ref[...] + lnb_ref[...]
    yr = (yn + bonus_ref[0]) * g_ref[0]
    mix = _mm(yr, wo_r_ref[...]) + _dot(o_ref[0], wo_m_ref[...])
    g_a = mod_ref[0, 2:3, :]
    sh_f = mod_ref[0, 3:4, :]
    sc_f = mod_ref[0, 4:5, :]
    x1 = x_ref[0] + g_a * mix
    x1_ref[0] = x1
    ms = jnp.mean(x1 * x1, axis=-1, keepdims=True)
    h2 = x1 * lax.rsqrt(ms + NORM_EPS) * nffn_ref[...] * (1.0 + sc_f) + sh_f
    h2_ref[0] = h2
    tm = h2.shape[0]
    for s in range(SUBLANES):
        h2t_ref[0, pl.ds(s, tm, stride=SUBLANES), :] = h2[:, s * LANES:(s + 1) * LANES]


def _post_call(y, bonus, g, o_pad, x, mod3, ln_w, ln_b, w_out, norm_ffn, tm):
    B, T, D = x.shape
    HP = MLA_HEADS * HEAD_PAD
    wo_r = w_out[:D_RWKV].astype(BF16)
    wo_m = jnp.pad(w_out[D_RWKV:].reshape(MLA_HEADS, V_HEAD_DIM, D),
                   ((0, 0), (0, HEAD_PAD - V_HEAD_DIM), (0, 0))).reshape(HP, D).astype(BF16)
    tok = lambda n: pl.BlockSpec((1, tm, n), lambda b, t: (b, t, 0))
    row = lambda n: pl.BlockSpec((1, n), lambda b, t: (0, 0))
    full = lambda a: pl.BlockSpec(a.shape, lambda b, t: (0,) * a.ndim)
    return pl.pallas_call(
        _post_kernel,
        grid=(B, T // tm),
        in_specs=[tok(D_RWKV), tok(D_RWKV), tok(D_RWKV), tok(HP), tok(D),
                  pl.BlockSpec((1, 6, D), lambda b, t: (b, 0, 0)),
                  row(D_RWKV), row(D_RWKV), full(wo_r), full(wo_m), row(D)],
        out_specs=[tok(D), tok(D),
                   pl.BlockSpec((1, tm * SUBLANES, LANES), lambda b, t: (b, t, 0))],
        out_shape=[jax.ShapeDtypeStruct((B, T, D), F32)] * 2
        + [jax.ShapeDtypeStruct((B, T * SUBLANES, LANES), F32)],
        compiler_params=_cparams("arbitrary", "arbitrary"),
        name="post",
    )(y, bonus, g, o_pad, x, mod3, ln_w.reshape(1, -1), ln_b.reshape(1, -1), wo_r, wo_m,
      norm_ffn.reshape(1, D))


def _first_index(mask, iota, size, axis):
    return jnp.min(jnp.where(mask, iota, size), axis=axis, keepdims=True)


def _route_kernel(h_ref, wr_ref, bias_ref, e_ref, w_ref, rank_ref, cnt_ref, base):
    tr = h_ref.shape[0]
    E = N_EXPERTS

    @pl.when(pl.program_id(0) == 0)
    def _():
        base[...] = jnp.zeros_like(base)

    logits = _mm3(wr_ref[...], h_ref[...], NT)
    scores = _sigmoid(logits)
    sel = scores + bias_ref[...]
    iota_g = lax.broadcasted_iota(I32, (GROUP_SIZE, tr), 0)
    gs_rows = []
    for gi in range(N_GROUPS):
        blk = sel[gi * GROUP_SIZE:(gi + 1) * GROUP_SIZE, :]
        m1 = jnp.max(blk, axis=0, keepdims=True)
        i1 = _first_index(blk == m1, iota_g, GROUP_SIZE, 0)
        m2 = jnp.max(jnp.where(iota_g == i1, NEG_INF, blk), axis=0, keepdims=True)
        gs_rows.append(m1 + m2)
    gs = jnp.concatenate(gs_rows, axis=0)
    iota8 = lax.broadcasted_iota(I32, (N_GROUPS, tr), 0)
    gmask = jnp.zeros((N_GROUPS, tr), jnp.bool_)
    for _ in range(TOPK_GROUPS):
        mg = jnp.max(gs, axis=0, keepdims=True)
        ig = _first_index(gs == mg, iota8, N_GROUPS, 0)
        hit = iota8 == ig
        gmask = gmask | hit
        gs = jnp.where(hit, NEG_INF, gs)
    msel = jnp.concatenate(
        [jnp.where(gmask[gi:gi + 1, :], sel[gi * GROUP_SIZE:(gi + 1) * GROUP_SIZE, :], NEG_INF)
         for gi in range(N_GROUPS)], axis=0)
    iota_e = lax.broadcasted_iota(I32, (E, tr), 0)
    e_rows, w_rows = [], []
    onehot = jnp.zeros((E, tr), F32)
    for _ in range(TOP_K):
        mv = jnp.max(msel, axis=0, keepdims=True)
        ie = _first_index(msel == mv, iota_e, E, 0)
        hit = iota_e == ie
        e_rows.append(ie)
        w_rows.append(jnp.sum(jnp.where(hit, scores, 0.0), axis=0, keepdims=True))
        onehot = jnp.where(hit, 1.0, onehot)
        msel = jnp.where(hit, NEG_INF, msel)
    top_e = jnp.concatenate(e_rows, axis=0)
    wts = jnp.concatenate(w_rows, axis=0)
    wts = wts / jnp.sum(wts, axis=0, keepdims=True) * ROUTED_SCALE
    ti = lax.broadcasted_iota(I32, (tr, tr), 0)
    tj = lax.broadcasted_iota(I32, (tr, tr), 1)
    upper = jnp.where(ti < tj, 1.0, 0.0).astype(BF16)
    pos = _dot(onehot.astype(BF16), upper) + base[...]
    rank_rows = [jnp.sum(jnp.where(iota_e == e_rows[j], pos, 0.0), axis=0, keepdims=True)
                 for j in range(TOP_K)]
    base[...] = base[...] + jnp.sum(onehot, axis=1, keepdims=True)
    e_ref[...] = top_e
    w_ref[...] = wts
    rank_ref[...] = jnp.concatenate(rank_rows, axis=0).astype(I32)
    cnt_ref[...] = base[...].astype(I32)


def _route_call(h2, w_router, router_bias, tr):
    N, D = h2.shape
    E = N_EXPERTS
    out_kn = pl.BlockSpec((TOP_K, tr), lambda i: (0, i))
    return pl.pallas_call(
        _route_kernel,
        grid=(N // tr,),
        in_specs=[pl.BlockSpec((tr, D), lambda i: (i, 0)),
                  pl.BlockSpec((E, D), lambda i: (0, 0)),
                  pl.BlockSpec((E, 1), lambda i: (0, 0))],
        out_specs=[out_kn, out_kn, out_kn, pl.BlockSpec((E, 1), lambda i: (0, 0))],
        out_shape=[jax.ShapeDtypeStruct((TOP_K, N), I32), jax.ShapeDtypeStruct((TOP_K, N), F32),
                   jax.ShapeDtypeStruct((TOP_K, N), I32), jax.ShapeDtypeStruct((E, 1), I32)],
        scratch_shapes=[pltpu.VMEM((E, 1), F32)],
        compiler_params=_cparams("arbitrary"),
        name="route",
    )(h2, w_router.T, router_bias.reshape(E, 1))


def _dest_kernel(e_ref, rank_ref, start_ref, d_ref):
    tr = e_ref.shape[1]
    iota_e = lax.broadcasted_iota(I32, (N_EXPERTS, tr), 0)
    starts = start_ref[...]
    rows = [jnp.sum(jnp.where(iota_e == e_ref[j:j + 1, :], starts, 0), axis=0, keepdims=True)
            for j in range(TOP_K)]
    d_ref[...] = jnp.concatenate(rows, axis=0) + rank_ref[...]


def _dest_call(top_e, rank, pad_starts, tr):
    K, N = top_e.shape
    spec = pl.BlockSpec((K, tr), lambda i: (0, i))
    return pl.pallas_call(
        _dest_kernel,
        grid=(N // tr,),
        in_specs=[spec, spec, pl.BlockSpec((N_EXPERTS, 1), lambda i: (0, 0))],
        out_specs=spec,
        out_shape=jax.ShapeDtypeStruct((K, N), I32),
        compiler_params=_cparams("arbitrary"),
        name="dest",
    )(top_e, rank, pad_starts.reshape(N_EXPERTS, 1))


def _row_copy(src, s_row, dst, d_row, sem):
    return pltpu.make_async_copy(src.at[s_row], dst.at[d_row], sem)


def _dispatch_kernel(last_ref, dest_hbm, h_ref, xs_out, idx, zbuf, sem_idx, sem_rows, sem_zero):
    i = pl.program_id(0)
    td = h_ref.shape[0]
    cp = pltpu.make_async_copy(dest_hbm.at[i], idx, sem_idx)
    cp.start()

    @pl.when(i == 0)
    def _():
        zbuf[...] = jnp.zeros_like(zbuf)

        def zero_copy(e):
            return pltpu.make_async_copy(zbuf, xs_out.at[pl.ds(last_ref[e], MOE_ROWS)], sem_zero)

        def z_issue(e, carry):
            @pl.when(last_ref[e] >= 0)
            def _():
                zero_copy(e).start()
            return carry

        def z_drain(e, carry):
            @pl.when(last_ref[e] >= 0)
            def _():
                zero_copy(e).wait()
            return carry

        lax.fori_loop(0, N_EXPERTS, z_issue, 0)
        lax.fori_loop(0, N_EXPERTS, z_drain, 0)

    cp.wait()

    def issue(tt, carry):
        base = pl.multiple_of(tt * SUBLANES, SUBLANES)
        for i in range(SUBLANES):
            for j in range(TOP_K):
                _row_copy(h_ref, base + i, xs_out, idx[j, base + i], sem_rows).start()
        return carry

    lax.fori_loop(0, td // SUBLANES, issue, 0)

    def drain(t, carry):
        for j in range(TOP_K):
            _row_copy(h_ref, 0, xs_out, 0, sem_rows).wait()
        return carry

    lax.fori_loop(0, td, drain, 0)


def _dispatch_call(last_block_row, dest3, h2t, n_rows, td):
    N, S, L = h2t.shape
    return pl.pallas_call(
        _dispatch_kernel,
        grid_spec=pltpu.PrefetchScalarGridSpec(
            num_scalar_prefetch=1,
            grid=(N // td,),
            in_specs=[pl.BlockSpec(memory_space=pl.ANY),
                      pl.BlockSpec((td, S, L), lambda i, last: (i, 0, 0))],
            out_specs=pl.BlockSpec(memory_space=pl.ANY),
            scratch_shapes=[pltpu.SMEM((TOP_K, td), I32), pltpu.VMEM((MOE_ROWS, S, L), F32),
                            pltpu.SemaphoreType.DMA, pltpu.SemaphoreType.DMA,
                            pltpu.SemaphoreType.DMA]),
        out_shape=jax.ShapeDtypeStruct((n_rows, S, L), h2t.dtype),
        compiler_params=_cparams("arbitrary"),
        name="dispatch",
    )(last_block_row, dest3, h2t)


def _moe_kernel(be_ref, nu_ref, xs_ref, wgu_ref, wdn_ref, y_ref, wgu_bf, wdn_bf):
    i = pl.program_id(0)

    @pl.when(i < nu_ref[0])
    def _():
        @pl.when((i == 0) | (be_ref[i] != be_ref[jnp.maximum(i - 1, 0)]))
        def _():
            wgu_bf[...] = wgu_ref[0].astype(BF16)
            wdn_bf[...] = wdn_ref[0].astype(BF16)

        xb = jnp.concatenate([xs_ref[pl.ds(s, MOE_ROWS, stride=SUBLANES), :]
                              for s in range(SUBLANES)], axis=1).astype(BF16)
        gu = _dot(xb, wgu_bf[...])
        act = _silu(gu[:, :D_EXPERT]) * gu[:, D_EXPERT:]
        y = _dot(act.astype(BF16), wdn_bf[...])
        for s in range(SUBLANES):
            y_ref[pl.ds(s, MOE_ROWS, stride=SUBLANES), :] = y[:, s * LANES:(s + 1) * LANES]


def _moe_call(block_expert, n_used, xs, w_gu, w_dn):
    P = xs.shape[0] // SUBLANES
    D = SUBLANES * LANES
    nb = P // MOE_ROWS
    blk = lambda i, be, nu: (jnp.minimum(i, nu[0] - 1), 0)
    return pl.pallas_call(
        _moe_kernel,
        grid_spec=pltpu.PrefetchScalarGridSpec(
            num_scalar_prefetch=2,
            grid=(nb,),
            in_specs=[pl.BlockSpec((MOE_ROWS * SUBLANES, LANES), blk),
                      pl.BlockSpec((1, D, 2 * D_EXPERT), lambda i, be, nu: (be[i], 0, 0)),
                      pl.BlockSpec((1, D_EXPERT, D), lambda i, be, nu: (be[i], 0, 0))],
            out_specs=pl.BlockSpec((MOE_ROWS * SUBLANES, LANES), blk),
            scratch_shapes=[pltpu.VMEM((D, 2 * D_EXPERT), BF16), pltpu.VMEM((D_EXPERT, D), BF16)]),
        out_shape=jax.ShapeDtypeStruct((P * SUBLANES, LANES), F32),
        compiler_params=_cparams("arbitrary"),
        name="moe",
    )(block_expert, n_used, xs, w_gu, w_dn)


def _combine_kernel(dest_hbm, w_hbm, y_hbm, h_ref, x1_ref, mod_ref, wsg_ref, wsd_ref, o_ref,
                    idx, wts, rows, routed, sem_idx, sem_rows):
    i = pl.program_id(0)
    tc = h_ref.shape[0]
    cp = pltpu.make_async_copy(dest_hbm.at[i], idx, sem_idx)
    cp_w = pltpu.make_async_copy(w_hbm.at[i], wts, sem_idx)
    cp.start()
    cp_w.start()
    cp.wait()
    cp_w.wait()

    def gather_copy(src_row, slot, t):
        dst_rows = pl.ds(pl.multiple_of(t * SUBLANES, SUBLANES), SUBLANES)
        return pltpu.make_async_copy(y_hbm.at[src_row], rows.at[slot, dst_rows, :], sem_rows)

    def issue(tt, carry):
        base = pl.multiple_of(tt * SUBLANES, SUBLANES)
        for i in range(SUBLANES):
            for j in range(TOP_K):
                gather_copy(idx[j, base + i], j, base + i).start()
        return carry

    lax.fori_loop(0, tc // SUBLANES, issue, 0)
    gu = _mm(h_ref[...], wsg_ref[...])
    act = _silu(gu[:, :D_EXPERT]) * gu[:, D_EXPERT:]
    ffn = _mm(act, wsd_ref[...])

    def drain(t, carry):
        for j in range(TOP_K):
            gather_copy(0, j, 0).wait()
        return carry

    lax.fori_loop(0, tc, drain, 0)
    def wsum(tt, carry):
        for u in range(SUBLANES):
            t = tt * SUBLANES + u
            tile = pl.ds(pl.multiple_of(t * SUBLANES, SUBLANES), SUBLANES)
            acc = rows[0, tile, :] * wts[0, t]
            for j in range(1, TOP_K):
                acc = acc + rows[j, tile, :] * wts[j, t]
            routed[tile, :] = acc
        return carry

    lax.fori_loop(0, tc // SUBLANES, wsum, 0)
    routed2d = jnp.concatenate([routed[pl.ds(s, tc, stride=SUBLANES), :] for s in range(SUBLANES)],
                               axis=1)
    g_f = mod_ref[0, 5:6, :]
    o_ref[...] = x1_ref[...] + g_f * (ffn + routed2d)


def _combine_call(dest3, w_nk, y, h2, x1, mod3, w_sh_gu, w_sh_dn, tokens_per_batch, tc):
    N, D = h2.shape
    tiles_per_batch = tokens_per_batch // tc
    tok = pl.BlockSpec((tc, D), lambda i: (i, 0))
    wsg = w_sh_gu.astype(BF16)
    wsd = w_sh_dn.astype(BF16)
    return pl.pallas_call(
        _combine_kernel,
        grid=(N // tc,),
        in_specs=[pl.BlockSpec(memory_space=pl.ANY),
                  pl.BlockSpec(memory_space=pl.ANY),
                  pl.BlockSpec(memory_space=pl.ANY),
                  tok, tok,
                  pl.BlockSpec((1, 6, D), lambda i: (i // tiles_per_batch, 0, 0)),
                  pl.BlockSpec(wsg.shape, lambda i: (0, 0)),
                  pl.BlockSpec(wsd.shape, lambda i: (0, 0))],
        out_specs=tok,
        out_shape=jax.ShapeDtypeStruct((N, D), F32),
        scratch_shapes=[pltpu.SMEM((TOP_K, tc), I32), pltpu.SMEM((TOP_K, tc), F32),
                        pltpu.VMEM((TOP_K, tc * SUBLANES, LANES), F32),
                        pltpu.VMEM((tc * SUBLANES, LANES), F32),
                        pltpu.SemaphoreType.DMA, pltpu.SemaphoreType.DMA],
        compiler_params=_cparams("arbitrary"),
        name="combine",
    )(dest3, w_nk, y, h2, x1, mod3, wsg, wsd)


def _tile(n, pref):
    t = min(n, pref)
    assert n % t == 0, (n, t)
    return t


def _layer(x, mod3, positions, norm_mix, w_in, rwkv_mu, decay_w0, decay_up, iclr_a0, iclr_up,
           gate_up, rwkv_k_k, rwkv_k_a, rwkv_r_k, ln_x_w, ln_x_b, q_a_norm, w_q_b, kv_a_norm,
           w_kv_b, q_norm, k_norm, w_out, norm_ffn, w_router, router_bias, w_e_gate_up, w_e_down,
           w_sh_gate_up, w_sh_down):
    B, T, D = x.shape
    N = B * T
    assert T % SCAN_CHUNK == 0
    (r, lw, k2, v, kk, akk, g, bonus, q_pad, k_pad, v_pad) = _pre_call(
        x, mod3, positions, norm_mix, w_in, rwkv_mu, decay_w0, decay_up, iclr_a0, iclr_up,
        gate_up, rwkv_k_k, rwkv_k_a, rwkv_r_k, q_a_norm, w_q_b, kv_a_norm, w_kv_b, q_norm, k_norm,
        tm=_tile(T, 512))
    y = _scan_call(r, lw, k2, v, kk, akk)
    o_pad = _attn_call(q_pad, k_pad, v_pad)
    x1, h2, h2t = _post_call(y, bonus, g, o_pad, x, mod3, ln_x_w, ln_x_b, w_out, norm_ffn,
                             tm=_tile(T, 512))
    x1 = x1.reshape(N, D)
    h2 = h2.reshape(N, D)
    h2t = h2t.reshape(N, D // LANES, LANES)

    tr = _tile(N, 512)
    top_e, wts, rank, counts = _route_call(h2, w_router, router_bias, tr)
    counts = counts.reshape(N_EXPERTS)
    padded = (counts + MOE_ROWS - 1) // MOE_ROWS * MOE_ROWS
    pad_ends = jnp.cumsum(padded)
    pad_starts = pad_ends - padded
    n_blocks = (N * TOP_K + N_EXPERTS * (MOE_ROWS - 1)) // MOE_ROWS
    block_expert = jnp.minimum(
        jnp.searchsorted(pad_ends, jnp.arange(n_blocks, dtype=I32) * MOE_ROWS, side='right'),
        N_EXPERTS - 1).astype(I32)
    n_used = (pad_ends[-1:] // MOE_ROWS).astype(I32)
    last_block_row = jnp.where(padded > 0, pad_ends - MOE_ROWS, -1).astype(I32)
    dest = _dest_call(top_e, rank, pad_starts.astype(I32), tr)

    td = _tile(T, 256)
    dest3 = dest.reshape(TOP_K, N // td, td).transpose(1, 0, 2)
    n_rows = n_blocks * MOE_ROWS
    xs = _dispatch_call(last_block_row, dest3, h2t, n_rows, td)
    y_e = _moe_call(block_expert, n_used, xs.reshape(n_rows * SUBLANES, LANES), w_e_gate_up, w_e_down)
    y_e = y_e.reshape(n_rows, SUBLANES, LANES)
    w3 = wts.reshape(TOP_K, N // td, td).transpose(1, 0, 2)
    out = _combine_call(dest3, w3, y_e, h2, x1, mod3, w_sh_gate_up, w_sh_down, T, td)
    return out.reshape(B, T, D)


def kernel(x, c, positions, ada_w, ada_b, norm_mix, w_in, rwkv_mu, decay_w0, decay_up, iclr_a0, iclr_up, gate_up, rwkv_k_k, rwkv_k_a, rwkv_r_k, ln_x_w, ln_x_b, q_a_norm, w_q_b, kv_a_norm, w_kv_b, q_norm, k_norm, w_out, norm_ffn, w_router, router_bias, w_e_gate_up, w_e_down, w_sh_gate_up, w_sh_down):
    B, T, D = x.shape
    depth = ada_w.shape[0]
    for l in range(depth):
        mod3 = _mod_call(c, ada_w[l], ada_b[l]).reshape(B, 6, D)
        x = _layer(x, mod3, positions, norm_mix[l], w_in[l], rwkv_mu[l], decay_w0[l], decay_up[l],
                   iclr_a0[l], iclr_up[l], gate_up[l], rwkv_k_k[l], rwkv_k_a[l], rwkv_r_k[l],
                   ln_x_w[l], ln_x_b[l], q_a_norm[l], w_q_b[l], kv_a_norm[l], w_kv_b[l],
                   q_norm[l], k_norm[l], w_out[l], norm_ffn[l], w_router[l], router_bias[l],
                   w_e_gate_up[l], w_e_down[l], w_sh_gate_up[l], w_sh_down[l])
    return x
```

```python
import functools
import math

import jax
import jax.numpy as jnp
import numpy as np
from jax import lax
from jax.experimental import pallas as pl
from jax.experimental.pallas import tpu as pltpu

F32 = jnp.float32
BF16 = jnp.bfloat16
I32 = jnp.int32

NORM_EPS = 1e-6
GN_EPS = 64e-5
RWKV_HEADS = 8
RWKV_HEAD_DIM = 64
D_RWKV = 512
DECAY_LORA = 32
ICLR_LORA = 32
GATE_LORA = 96
MLA_HEADS = 8
QK_NOPE_DIM = 64
QK_ROPE_DIM = 32
QK_HEAD_DIM = 96
V_HEAD_DIM = 64
Q_LORA_RANK = 256
KV_LORA_RANK = 128
ROPE_THETA = 10000.0
N_EXPERTS = 256
TOP_K = 8
N_GROUPS = 8
TOPK_GROUPS = 4
GROUP_SIZE = N_EXPERTS // N_GROUPS
D_EXPERT = 256
ROUTED_SCALE = 2.5
MOE_ROWS = 512

LANES = 128
SUBLANES = 8
HEAD_PAD = 128
VMEM_LIMIT = 56 * 1024 * 1024

SCAN_CHUNK = 64
SCAN_BLOCK = 256
ATTN_TILE = 512
ATTN_Q_SCALE = QK_HEAD_DIM ** -0.5 * math.log2(math.e)
NEG_INF = float("-inf")


def _cparams(*sem):
    return pltpu.CompilerParams(dimension_semantics=sem, vmem_limit_bytes=VMEM_LIMIT)


def _split2(a):
    hi = a.astype(BF16)
    lo = (a - hi.astype(F32)).astype(BF16)
    return hi, lo


def _split3(a):
    hi = a.astype(BF16)
    r1 = a - hi.astype(F32)
    mid = r1.astype(BF16)
    lo = (r1 - mid.astype(F32)).astype(BF16)
    return hi, mid, lo


def _dot(a, b, dims=None):
    if dims is None:
        return jnp.dot(a, b, preferred_element_type=F32)
    return lax.dot_general(a, b, (dims, ((), ())), preferred_element_type=F32)


def _mm(a, b, dims=None):
    return _dot(a.astype(BF16), b.astype(BF16), dims)


def _mm3(a, b, dims=None):
    ah, al = _split2(a)
    bh, bl = _split2(b)
    return _dot(ah, bh, dims) + (_dot(ah, bl, dims) + _dot(al, bh, dims))


def _mm_exact_rhs(a, b_exact_bf16, dims=None):
    h, m, l = _split3(a)
    return _dot(h, b_exact_bf16, dims) + (_dot(m, b_exact_bf16, dims) + _dot(l, b_exact_bf16, dims))


NT = ((1,), (1,))
TN = ((0,), (0,))


def _sigmoid(z):
    return 1.0 / (1.0 + jnp.exp(-z))


def _silu(z):
    return z * _sigmoid(z)


def _seg_ones(width, seg):
    r = lax.broadcasted_iota(I32, (width, width), 0) // seg
    c = lax.broadcasted_iota(I32, (width, width), 1) // seg
    return jnp.where(r == c, 1.0, 0.0).astype(BF16)


def _segsum(a, ones_bd):
    hi, lo = _split2(a)
    return _dot(hi, ones_bd) + _dot(lo, ones_bd)


def _mod_kernel(c_ref, w_ref, b_ref, o_ref):
    ca = _silu(c_ref[...])
    o_ref[...] = _mm3(ca, w_ref[...]) + b_ref[...]


def _mod_call(c, ada_w, ada_b):
    B, D = c.shape
    n6 = ada_w.shape[1]
    tn = D
    return pl.pallas_call(
        _mod_kernel,
        grid=(n6 // tn,),
        in_specs=[pl.BlockSpec((B, D), lambda j: (0, 0)),
                  pl.BlockSpec((D, tn), lambda j: (0, j)),
                  pl.BlockSpec((1, tn), lambda j: (0, j))],
        out_specs=pl.BlockSpec((B, tn), lambda j: (0, j)),
        out_shape=jax.ShapeDtypeStruct((B, n6), F32),
        compiler_params=_cparams("arbitrary"),
        name="mod",
    )(c, ada_w, ada_b.reshape(1, n6))


def _pre_kernel(x_ref, mod_ref, pos_ref, nmix_ref, wrkv_ref, wlora_ref, wmla_ref,
                mu_rkv_ref, mu_lora_ref, wup_ref, w0_ref, a0_ref, kk_ref, ka_ref, rk_ref,
                qan_ref, wqb_ref, kvan_ref, wkb_ref, wvb_ref, qn_ref, kn_ref, invf_ref,
                r_ref, lw_ref, k_ref, v_ref, kkn_ref, akk_ref, g_ref, bonus_ref,
                q_ref, kout_ref, vout_ref,
                carry_rkv, carry_lora):
    ti = pl.program_id(1)
    tm = x_ref.shape[1]

    @pl.when(ti == 0)
    def _():
        carry_rkv[...] = jnp.zeros_like(carry_rkv)
        carry_lora[...] = jnp.zeros_like(carry_lora)

    xb = x_ref[0]
    sh_a = mod_ref[0, 0:1, :]
    sc_a = mod_ref[0, 1:2, :]
    ms = jnp.mean(xb * xb, axis=-1, keepdims=True)
    h = xb * lax.rsqrt(ms + NORM_EPS) * nmix_ref[...] * (1.0 + sc_a) + sh_a
    hb = h.astype(BF16)
    u_rkv = _dot(hb, wrkv_ref[...])
    u_lora = _dot(hb, wlora_ref[...])
    u_mla = _dot(hb, wmla_ref[...])

    row0 = lax.broadcasted_iota(I32, (tm, 1), 0) == 0

    def shift(u, carry):
        prev = jnp.where(row0, carry[...], pltpu.roll(u, 1, 0))
        carry[...] = u[tm - 1:tm, :]
        return prev

    prev_rkv = shift(u_rkv, carry_rkv)
    prev_lora = shift(u_lora, carry_lora)
    us = u_rkv + (prev_rkv - u_rkv) * mu_rkv_ref[...]
    ul = u_lora + (prev_lora - u_lora) * mu_lora_ref[...]
    r = us[:, 0:D_RWKV]
    k = us[:, D_RWKV:2 * D_RWKV]
    v = us[:, 2 * D_RWKV:3 * D_RWKV]

    lane_l = lax.broadcasted_iota(I32, ul.shape, 1)
    t_in = jnp.where(lane_l < DECAY_LORA, jnp.tanh(ul),
                     jnp.where(lane_l < DECAY_LORA + ICLR_LORA, ul, _sigmoid(ul)))
    up = _mm(t_in, wup_ref[...])
    z = w0_ref[...] + up[:, 0:D_RWKV]
    lw = (-math.exp(-0.5)) * _sigmoid(z)
    a = _sigmoid(a0_ref[...] + up[:, D_RWKV:2 * D_RWKV])
    g = up[:, 2 * D_RWKV:3 * D_RWKV]

    ones64 = _seg_ones(D_RWKV, RWKV_HEAD_DIM)
    kk = k * kk_ref[...]
    ss = _segsum(kk * kk, ones64)
    kk = kk * lax.rsqrt(jnp.maximum(ss, 1e-24))
    k2 = k * (1.0 + (a - 1.0) * ka_ref[...])
    bonus = _segsum(r * k2 * rk_ref[...], ones64) * v

    r_ref[0] = r
    lw_ref[0] = lw
    k_ref[0] = k2
    v_ref[0] = v
    kkn_ref[0] = kk
    akk_ref[0] = a * kk
    g_ref[0] = g
    bonus_ref[0] = bonus

    q_lat = u_mla[:, 0:Q_LORA_RANK]
    kv_lat = u_mla[:, Q_LORA_RANK:Q_LORA_RANK + KV_LORA_RANK]
    kpe_tile = u_mla[:, Q_LORA_RANK + KV_LORA_RANK:]
    qn = q_lat * lax.rsqrt(jnp.mean(q_lat * q_lat, axis=-1, keepdims=True) + NORM_EPS) * qan_ref[...]
    kvn = kv_lat * lax.rsqrt(jnp.mean(kv_lat * kv_lat, axis=-1, keepdims=True) + NORM_EPS) * kvan_ref[...]
    q_raw = _mm(qn, wqb_ref[...])
    kvb = kvn.astype(BF16)
    k_raw = _dot(kvb, wkb_ref[...])
    v_pad = _dot(kvb, wvb_ref[...])
    kpe_h = pltpu.roll(kpe_tile, QK_NOPE_DIM, 1)

    half = QK_ROPE_DIM // 2
    ang_t = invf_ref[...] * pos_ref[0].astype(F32)
    frow = lax.broadcasted_iota(I32, (half, HEAD_PAD), 0)
    flane = lax.broadcasted_iota(I32, (half, HEAD_PAD), 1)
    at_x1 = flane == frow + QK_NOPE_DIM
    at_x2 = flane == frow + QK_NOPE_DIM + half
    e_cos = jnp.where(at_x1 | at_x2, 1.0, 0.0).astype(BF16)
    e_sin = jnp.concatenate([jnp.where(at_x1, -1.0, 0.0), jnp.where(at_x2, 1.0, 0.0)],
                            axis=1).astype(BF16)
    lane = lax.broadcasted_iota(I32, (1, HEAD_PAD), 1)
    off_rope = jnp.where((lane >= QK_NOPE_DIM) & (lane < QK_HEAD_DIM), 0.0, 1.0)
    cos_t = _mm_exact_rhs(jnp.cos(ang_t), e_cos, TN) + off_rope
    sin2 = _mm_exact_rhs(jnp.sin(ang_t), e_sin, TN)
    s1 = sin2[:, :HEAD_PAD]
    s2 = sin2[:, HEAD_PAD:]

    def tables(gain, scale):
        g = gain * scale
        return (cos_t * g, s1 * pltpu.roll(g, HEAD_PAD - half, 1), s2 * pltpu.roll(g, half, 1))

    def norm_rope(xh, tabs):
        c_g, s1_g, s2_g = tabs
        ssq = jnp.sum(xh * xh, axis=-1, keepdims=True) * (1.0 / QK_HEAD_DIM)
        rot = xh * c_g + pltpu.roll(xh, HEAD_PAD - half, 1) * s1_g + pltpu.roll(xh, half, 1) * s2_g
        return rot * lax.rsqrt(ssq + NORM_EPS)

    q_tabs = tables(qn_ref[...], ATTN_Q_SCALE)
    k_tabs = tables(kn_ref[...], 1.0)
    for hh in range(MLA_HEADS):
        sl = slice(hh * HEAD_PAD, (hh + 1) * HEAD_PAD)
        q_ref[0, :, sl] = norm_rope(q_raw[:, sl], q_tabs).astype(BF16)
        kout_ref[0, :, sl] = norm_rope(k_raw[:, sl] + kpe_h, k_tabs).astype(BF16)
    vout_ref[0] = v_pad.astype(BF16)


def _pad_heads(w, n_heads, width):
    kdim = w.shape[0]
    w = w.reshape(kdim, n_heads, width)
    w = jnp.pad(w, ((0, 0), (0, 0), (0, HEAD_PAD - width)))
    return w.reshape(kdim, n_heads * HEAD_PAD)


def _pre_call(x, mod3, positions, norm_mix, w_in, rwkv_mu, decay_w0, decay_up, iclr_a0, iclr_up,
              gate_up, k_k, k_a, r_k, q_a_norm, w_q_b, kv_a_norm, w_kv_b, q_norm, k_norm, tm):
    B, T, D = x.shape
    n_rkv = 3 * D_RWKV
    n_lora = DECAY_LORA + ICLR_LORA + GATE_LORA
    LORA_PAD = 256
    MLA_PAD = 512
    n_mla = Q_LORA_RANK + KV_LORA_RANK + QK_ROPE_DIM
    w_rkv = w_in[:, :n_rkv].astype(BF16)
    w_lora = jnp.pad(w_in[:, n_rkv:n_rkv + n_lora], ((0, 0), (0, LORA_PAD - n_lora))).astype(BF16)
    w_mla = jnp.pad(w_in[:, n_rkv + n_lora:], ((0, 0), (0, MLA_PAD - n_mla))).astype(BF16)
    mu_rkv = rwkv_mu[:n_rkv].reshape(1, n_rkv)
    mu_lora = jnp.pad(rwkv_mu[n_rkv:], (0, LORA_PAD - n_lora)).reshape(1, LORA_PAD)
    w_up = jnp.zeros((LORA_PAD, n_rkv), F32)
    w_up = w_up.at[0:DECAY_LORA, 0:D_RWKV].set(decay_up)
    w_up = w_up.at[DECAY_LORA:DECAY_LORA + ICLR_LORA, D_RWKV:2 * D_RWKV].set(iclr_up)
    w_up = w_up.at[DECAY_LORA + ICLR_LORA:n_lora, 2 * D_RWKV:].set(gate_up)
    w_up = w_up.astype(BF16)
    w_qb = _pad_heads(w_q_b, MLA_HEADS, QK_HEAD_DIM).astype(BF16)
    w_kv3 = w_kv_b.reshape(KV_LORA_RANK, MLA_HEADS, QK_NOPE_DIM + V_HEAD_DIM)
    w_kb = _pad_heads(w_kv3[:, :, :QK_NOPE_DIM].reshape(KV_LORA_RANK, -1), MLA_HEADS, QK_NOPE_DIM).astype(BF16)
    w_vb = _pad_heads(w_kv3[:, :, QK_NOPE_DIM:].reshape(KV_LORA_RANK, -1), MLA_HEADS, V_HEAD_DIM).astype(BF16)
    qn_pad = jnp.pad(q_norm, (0, HEAD_PAD - QK_HEAD_DIM)).reshape(1, HEAD_PAD)
    kn_pad = jnp.pad(k_norm, (0, HEAD_PAD - QK_HEAD_DIM)).reshape(1, HEAD_PAD)
    inv_freq = ROPE_THETA ** (-jnp.arange(0, QK_ROPE_DIM, 2, dtype=F32) / QK_ROPE_DIM)
    invf = inv_freq.reshape(QK_ROPE_DIM // 2, 1)
    pos3 = positions.reshape(B, 1, T)
    HP = MLA_HEADS * HEAD_PAD

    row = lambda n: pl.BlockSpec((1, n), lambda b, t: (0, 0))
    full = lambda a: pl.BlockSpec(a.shape, lambda b, t: (0,) * a.ndim)
    tok = lambda n: pl.BlockSpec((1, tm, n), lambda b, t: (b, t, 0))
    outs = ([jax.ShapeDtypeStruct((B, T, D_RWKV), F32)] * 8
            + [jax.ShapeDtypeStruct((B, T, HP), BF16)] * 3)
    return pl.pallas_call(
        _pre_kernel,
        grid=(B, T // tm),
        in_specs=[tok(D),
                  pl.BlockSpec((1, 6, D), lambda b, t: (b, 0, 0)),
                  pl.BlockSpec((1, 1, tm), lambda b, t: (b, 0, t)),
                  row(D), full(w_rkv), full(w_lora), full(w_mla),
                  row(n_rkv), row(LORA_PAD), full(w_up), row(D_RWKV), row(D_RWKV),
                  row(D_RWKV), row(D_RWKV), row(D_RWKV),
                  row(Q_LORA_RANK), full(w_qb), row(KV_LORA_RANK), full(w_kb), full(w_vb),
                  row(HEAD_PAD), row(HEAD_PAD), full(invf)],
        out_specs=[tok(D_RWKV)] * 8 + [tok(HP)] * 3,
        out_shape=outs,
        scratch_shapes=[pltpu.VMEM((1, n_rkv), F32), pltpu.VMEM((1, LORA_PAD), F32)],
        compiler_params=_cparams("arbitrary", "arbitrary"),
        name="pre",
    )(x, mod3, pos3, norm_mix.reshape(1, D), w_rkv, w_lora, w_mla, mu_rkv, mu_lora, w_up,
      decay_w0.reshape(1, -1), iclr_a0.reshape(1, -1), k_k.reshape(1, -1), k_a.reshape(1, -1),
      r_k.reshape(1, -1), q_a_norm.reshape(1, -1), w_qb, kv_a_norm.reshape(1, -1), w_kb, w_vb,
      qn_pad, kn_pad, invf)


def _scan_kernel(r_ref, lw_ref, k_ref, v_ref, kk_ref, akk_ref, y_ref, state):
    C = SCAN_CHUNK
    n_chunks = r_ref.shape[1] // C
    n_pairs = r_ref.shape[2] // LANES

    @pl.when(pl.program_id(1) == 0)
    def _():
        state[...] = jnp.zeros_like(state)

    ri = lax.broadcasted_iota(I32, (C, C), 0)
    ci = lax.broadcasted_iota(I32, (C, C), 1)
    tri_incl = jnp.where(ci <= ri, 1.0, 0.0).astype(BF16)
    r2 = lax.broadcasted_iota(I32, (2 * C, 2 * C), 0)
    c2 = lax.broadcasted_iota(I32, (2 * C, 2 * C), 1)
    same = (r2 >= C) == (c2 >= C)
    strict = same & (c2 < r2)
    incl = same & (c2 <= r2)
    eye = jnp.where(c2 == r2, 1.0, 0.0)
    head0 = lax.broadcasted_iota(I32, (C, LANES), 1) < RWKV_HEAD_DIM

    def stack2(a):
        return jnp.concatenate([jnp.where(head0, a, 0.0), jnp.where(head0, 0.0, a)], axis=0)

    C2 = 2 * C
    cat0 = lambda *a: jnp.concatenate(a, axis=0)
    cat1 = lambda *a: jnp.concatenate(a, axis=1)

    items = []
    for c in range(n_chunks):
        rows = slice(c * C, (c + 1) * C)
        lw = lw_ref[0, rows, :]
        cum = _mm_exact_rhs_left(tri_incl, lw)
        cum_end = cum[C - 1:C, :]
        w_end = jnp.exp(cum_end)
        e_pos = jnp.exp(cum)
        e_neg = jnp.exp(-cum)
        e_prev = jnp.exp(cum - lw)
        e_end = jnp.exp(cum_end - cum)
        kk = kk_ref[0, rows, :]
        k2 = k_ref[0, rows, :]
        pneg = -akk_ref[0, rows, :]
        vv = v_ref[0, rows, :]
        rt = r_ref[0, rows, :] * e_pos
        bt = kk * e_prev
        pt = pneg * e_neg
        kt = k2 * e_neg
        ph = pneg * e_end
        kh = k2 * e_end
        for pp in range(n_pairs):
            sl = slice(pp * LANES, (pp + 1) * LANES)
            items.append(dict(
                c=c, p=pp, w_end=w_end[:, sl],
                bt2=stack2(bt[:, sl]).astype(BF16), rt2=stack2(rt[:, sl]).astype(BF16),
                pk2=cat0(stack2(pt[:, sl]), stack2(kt[:, sl])).astype(BF16),
                phkh2=cat0(stack2(ph[:, sl]), stack2(kh[:, sl])).astype(BF16),
                v2=stack2(vv[:, sl])))
    for it in items:
        ab = _dot(cat0(it['bt2'], it['rt2']), it['pk2'], NT)
        it['a_ab'] = jnp.where(strict, ab[:C2, :C2], 0.0)
        it['a_ak'] = jnp.where(strict, ab[:C2, C2:], 0.0).astype(BF16)
        it['b_rpk'] = cat1(jnp.where(incl, ab[C2:, :C2], 0.0), jnp.where(incl, ab[C2:, C2:], 0.0)).astype(BF16)
        it['tinv'] = eye + it['a_ab']
    for it in items:
        it['apow'] = _mm(it['a_ab'], it['a_ab'])
    for _ in range(int(math.log2(C)) - 1):
        for it in items:
            both = _mm(cat0(it['apow'], it['tinv']), it['apow'])
            it['apow'] = both[:C2]
            it['tinv'] = it['tinv'] + both[C2:]
    for it in items:
        it['akv'] = _dot(it['a_ak'], it['v2'].astype(BF16))
    for it in items:
        tt = _dot(it['tinv'].astype(BF16), cat1(it['bt2'], it['akv'].astype(BF16)))
        it['tb_rt'] = cat0(tt[:, :LANES].astype(BF16), it['rt2'])
        it['tav'] = tt[:, LANES:]
    for it in items:
        pp = it['p']
        s0 = state[pp]
        top = _dot(it['tb_rt'], s0.astype(BF16), NT)
        u2 = top[:C2] + it['tav']
        uv = cat0(u2, it['v2']).astype(BF16)
        y2 = top[C2:] + _dot(it['b_rpk'], uv)
        state[pp] = s0 * it['w_end'] + _dot(uv, it['phkh2'], TN)
        y_ref[0, it['c'] * C:(it['c'] + 1) * C, pp * LANES:(pp + 1) * LANES] = y2[0:C] + y2[C:C2]


def _mm_exact_rhs_left(b_exact_bf16, a):
    h, m, l = _split3(a)
    return _dot(b_exact_bf16, h) + (_dot(b_exact_bf16, m) + _dot(b_exact_bf16, l))


def _scan_call(r, lw, k2, v, kk, akk):
    B, T, W = r.shape
    tb = _tile(T, SCAN_BLOCK)
    spec = pl.BlockSpec((1, tb, W), lambda b, c: (b, c, 0))
    return pl.pallas_call(
        _scan_kernel,
        grid=(B, T // tb),
        in_specs=[spec] * 6,
        out_specs=spec,
        out_shape=jax.ShapeDtypeStruct((B, T, W), F32),
        scratch_shapes=[pltpu.VMEM((W // LANES, 2 * RWKV_HEAD_DIM, LANES), F32)],
        compiler_params=_cparams("arbitrary", "arbitrary"),
        name="scan",
    )(r, lw, k2, v, kk, akk)


def _attn_kernel(q_ref, k_ref, v_ref, o_ref):
    T = q_ref.shape[1]
    tq = min(T, ATTN_TILE)
    row = lax.broadcasted_iota(I32, (tq, tq), 0)
    col = lax.broadcasted_iota(I32, (tq, tq), 1)
    causal = col <= row

    def update(q, kt, vt, carry, mask):
        m_old, l_old, acc = carry
        s = _dot(q, kt, NT)
        if mask:
            s = jnp.where(causal, s, NEG_INF)
        m_new = jnp.maximum(m_old, jnp.max(s, axis=-1, keepdims=True))
        alpha = jnp.exp2(m_old - m_new)
        p = jnp.exp2(s - m_new)
        l_new = alpha * l_old + jnp.sum(p, axis=-1, keepdims=True)
        acc = alpha * acc + _dot(p.astype(BF16), vt)
        return m_new, l_new, acc

    for qi in range(T // tq):
        q = q_ref[0, qi * tq:(qi + 1) * tq, :]
        carry = (jnp.full((tq, 1), NEG_INF, F32), jnp.zeros((tq, 1), F32),
                 jnp.zeros((tq, HEAD_PAD), F32))

        def body(ki, carry, q=q):
            rows = pl.ds(pl.multiple_of(ki * tq, tq), tq)
            return update(q, k_ref[0, rows, :], v_ref[0, rows, :], carry, False)

        carry = lax.fori_loop(0, qi, body, carry, unroll=True)
        diag = slice(qi * tq, (qi + 1) * tq)
        _, l_fin, acc = update(q, k_ref[0, diag, :], v_ref[0, diag, :], carry, True)
        o_ref[0, diag, :] = (acc / l_fin).astype(o_ref.dtype)


def _attn_call(q, k, v):
    B, T, HP = q.shape
    spec = pl.BlockSpec((1, T, HEAD_PAD), lambda b, h: (b, 0, h))
    return pl.pallas_call(
        _attn_kernel,
        grid=(B, MLA_HEADS),
        in_specs=[spec, spec, spec],
        out_specs=spec,
        out_shape=jax.ShapeDtypeStruct((B, T, HP), BF16),
        compiler_params=_cparams("arbitrary", "arbitrary"),
        name="attn",
    )(q, k, v)


def _post_kernel(y_ref, bonus_ref, g_ref, o_ref, x_ref, mod_ref, lnw_ref, lnb_ref,
                 wo_r_ref, wo_m_ref, nffn_ref, x1_ref, h2_ref, h2t_ref):
    y = y_ref[0]
    ones64 = _seg_ones(D_RWKV, RWKV_HEAD_DIM)
    mean = _segsum(y, ones64) * (1.0 / RWKV_HEAD_DIM)
    yc = y - mean
    var = _segsum(yc * yc, ones64) * (1.0 / RWKV_HEAD_DIM)
    yn = yc * lax.rsqrt(var + GN_EPS) * lnw_ref[...] + lnb_ref[...]
    yr = (yn + bonus_ref[0]) * g_ref[0]
    mix = _mm(yr, wo_r_ref[...]) + _dot(o_ref[0], wo_m_ref[...])
    g_a = mod_ref[0, 2:3, :]
    sh_f = mod_ref[0, 3:4, :]
    sc_f = mod_ref[0, 4:5, :]
    x1 = x_ref[0] + g_a * mix
    x1_ref[0] = x1
    ms = jnp.mean(x1 * x1, axis=-1, keepdims=True)
    h2 = x1 * lax.rsqrt(ms + NORM_EPS) * nffn_ref[...] * (1.0 + sc_f) + sh_f
    h2_ref[0] = h2
    tm = h2.shape[0]
    for s in range(SUBLANES):
        h2t_ref[0, pl.ds(s, tm, stride=SUBLANES), :] = h2[:, s * LANES:(s + 1) * LANES]


def _post_call(y, bonus, g, o_pad, x, mod3, ln_w, ln_b, w_out, norm_ffn, tm):
    B, T, D = x.shape
    HP = MLA_HEADS * HEAD_PAD
    wo_r = w_out[:D_RWKV].astype(BF16)
    wo_m = jnp.pad(w_out[D_RWKV:].reshape(MLA_HEADS, V_HEAD_DIM, D),
                   ((0, 0), (0, HEAD_PAD - V_HEAD_DIM), (0, 0))).reshape(HP, D).astype(BF16)
    tok = lambda n: pl.BlockSpec((1, tm, n), lambda b, t: (b, t, 0))
    row = lambda n: pl.BlockSpec((1, n), lambda b, t: (0, 0))
    full = lambda a: pl.BlockSpec(a.shape, lambda b, t: (0,) * a.ndim)
    return pl.pallas_call(
        _post_kernel,
        grid=(B, T // tm),
        in_specs=[tok(D_RWKV), tok(D_RWKV), tok(D_RWKV), tok(HP), tok(D),
                  pl.BlockSpec((1, 6, D), lambda b, t: (b, 0, 0)),
                  row(D_RWKV), row(D_RWKV), full(wo_r), full(wo_m), row(D)],
        out_specs=[tok(D), tok(D),
                   pl.BlockSpec((1, tm * SUBLANES, LANES), lambda b, t: (b, t, 0))],
        out_shape=[jax.ShapeDtypeStruct((B, T, D), F32)] * 2
        + [jax.ShapeDtypeStruct((B, T * SUBLANES, LANES), F32)],
        compiler_params=_cparams("arbitrary", "arbitrary"),
        name="post",
    )(y, bonus, g, o_pad, x, mod3, ln_w.reshape(1, -1), ln_b.reshape(1, -1), wo_r, wo_m,
      norm_ffn.reshape(1, D))


def _first_index(mask, iota, size, axis):
    return jnp.min(jnp.where(mask, iota, size), axis=axis, keepdims=True)


def _route_kernel(h_ref, wr_ref, bias_ref, e_ref, w_ref, rank_ref, cnt_ref, base):
    tr = h_ref.shape[0]
    E = N_EXPERTS

    @pl.when(pl.program_id(0) == 0)
    def _():
        base[...] = jnp.zeros_like(base)

    logits = _mm3(wr_ref[...], h_ref[...], NT)
    scores = _sigmoid(logits)
    sel = scores + bias_ref[...]
    iota_g = lax.broadcasted_iota(I32, (GROUP_SIZE, tr), 0)
    gs_rows = []
    for gi in range(N_GROUPS):
        blk = sel[gi * GROUP_SIZE:(gi + 1) * GROUP_SIZE, :]
        m1 = jnp.max(blk, axis=0, keepdims=True)
        i1 = _first_index(blk == m1, iota_g, GROUP_SIZE, 0)
        m2 = jnp.max(jnp.where(iota_g == i1, NEG_INF, blk), axis=0, keepdims=True)
        gs_rows.append(m1 + m2)
    gs = jnp.concatenate(gs_rows, axis=0)
    iota8 = lax.broadcasted_iota(I32, (N_GROUPS, tr), 0)
    gmask = jnp.zeros((N_GROUPS, tr), jnp.bool_)
    for _ in range(TOPK_GROUPS):
        mg = jnp.max(gs, axis=0, keepdims=True)
        ig = _first_index(gs == mg, iota8, N_GROUPS, 0)
        hit = iota8 == ig
        gmask = gmask | hit
        gs = jnp.where(hit, NEG_INF, gs)
    msel = jnp.concatenate(
        [jnp.where(gmask[gi:gi + 1, :], sel[gi * GROUP_SIZE:(gi + 1) * GROUP_SIZE, :], NEG_INF)
         for gi in range(N_GROUPS)], axis=0)
    iota_e = lax.broadcasted_iota(I32, (E, tr), 0)
    e_rows, w_rows = [], []
    onehot = jnp.zeros((E, tr), F32)
    for _ in range(TOP_K):
        mv = jnp.max(msel, axis=0, keepdims=True)
        ie = _first_index(msel == mv, iota_e, E, 0)
        hit = iota_e == ie
        e_rows.append(ie)
        w_rows.append(jnp.sum(jnp.where(hit, scores, 0.0), axis=0, keepdims=True))
        onehot = jnp.where(hit, 1.0, onehot)
        msel = jnp.where(hit, NEG_INF, msel)
    top_e = jnp.concatenate(e_rows, axis=0)
    wts = jnp.concatenate(w_rows, axis=0)
    wts = wts / jnp.sum(wts, axis=0, keepdims=True) * ROUTED_SCALE
    ti = lax.broadcasted_iota(I32, (tr, tr), 0)
    tj = lax.broadcasted_iota(I32, (tr, tr), 1)
    upper = jnp.where(ti < tj, 1.0, 0.0).astype(BF16)
    pos = _dot(onehot.astype(BF16), upper) + base[...]
    rank_rows = [jnp.sum(jnp.where(iota_e == e_rows[j], pos, 0.0), axis=0, keepdims=True)
                 for j in range(TOP_K)]
    base[...] = base[...] + jnp.sum(onehot, axis=1, keepdims=True)
    e_ref[...] = top_e
    w_ref[...] = wts
    rank_ref[...] = jnp.concatenate(rank_rows, axis=0).astype(I32)
    cnt_ref[...] = base[...].astype(I32)


def _route_call(h2, w_router, router_bias, tr):
    N, D = h2.shape
    E = N_EXPERTS
    out_kn = pl.BlockSpec((TOP_K, tr), lambda i: (0, i))
    return pl.pallas_call(
        _route_kernel,
        grid=(N // tr,),
        in_specs=[pl.BlockSpec((tr, D), lambda i: (i, 0)),
                  pl.BlockSpec((E, D), lambda i: (0, 0)),
                  pl.BlockSpec((E, 1), lambda i: (0, 0))],
        out_specs=[out_kn, out_kn, out_kn, pl.BlockSpec((E, 1), lambda i: (0, 0))],
        out_shape=[jax.ShapeDtypeStruct((TOP_K, N), I32), jax.ShapeDtypeStruct((TOP_K, N), F32),
                   jax.ShapeDtypeStruct((TOP_K, N), I32), jax.ShapeDtypeStruct((E, 1), I32)],
        scratch_shapes=[pltpu.VMEM((E, 1), F32)],
        compiler_params=_cparams("arbitrary"),
        name="route",
    )(h2, w_router.T, router_bias.reshape(E, 1))


def _dest_kernel(e_ref, rank_ref, start_ref, d_ref):
    tr = e_ref.shape[1]
    iota_e = lax.broadcasted_iota(I32, (N_EXPERTS, tr), 0)
    starts = start_ref[...]
    rows = [jnp.sum(jnp.where(iota_e == e_ref[j:j + 1, :], starts, 0), axis=0, keepdims=True)
            for j in range(TOP_K)]
    d_ref[...] = jnp.concatenate(rows, axis=0) + rank_ref[...]


def _dest_call(top_e, rank, pad_starts, tr):
    K, N = top_e.shape
    spec = pl.BlockSpec((K, tr), lambda i: (0, i))
    return pl.pallas_call(
        _dest_kernel,
        grid=(N // tr,),
        in_specs=[spec, spec, pl.BlockSpec((N_EXPERTS, 1), lambda i: (0, 0))],
        out_specs=spec,
        out_shape=jax.ShapeDtypeStruct((K, N), I32),
        compiler_params=_cparams("arbitrary"),
        name="dest",
    )(top_e, rank, pad_starts.reshape(N_EXPERTS, 1))


def _row_copy(src, s_row, dst, d_row, sem):
    return pltpu.make_async_copy(src.at[s_row], dst.at[d_row], sem)


def _dispatch_kernel(last_ref, dest_hbm, h_ref, xs_out, idx, zbuf, sem_idx, sem_rows, sem_zero):
    i = pl.program_id(0)
    td = h_ref.shape[0]
    cp = pltpu.make_async_copy(dest_hbm.at[i], idx, sem_idx)
    cp.start()

    @pl.when(i == 0)
    def _():
        zbuf[...] = jnp.zeros_like(zbuf)

        def zero_copy(e):
            return pltpu.make_async_copy(zbuf, xs_out.at[pl.ds(last_ref[e], MOE_ROWS)], sem_zero)

        def z_issue(e, carry):
            @pl.when(last_ref[e] >= 0)
            def _():
                zero_copy(e).start()
            return carry

        def z_drain(e, carry):
            @pl.when(last_ref[e] >= 0)
            def _():
                zero_copy(e).wait()
            return carry

        lax.fori_loop(0, N_EXPERTS, z_issue, 0)
        lax.fori_loop(0, N_EXPERTS, z_drain, 0)

    cp.wait()

    def issue(tt, carry):
        base = pl.multiple_of(tt * SUBLANES, SUBLANES)
        for i in range(SUBLANES):
            for j in range(TOP_K):
                _row_copy(h_ref, base + i, xs_out, idx[j, base + i], sem_rows).start()
        return carry

    lax.fori_loop(0, td // SUBLANES, issue, 0)

    def drain(t, carry):
        for j in range(TOP_K):
            _row_copy(h_ref, 0, xs_out, 0, sem_rows).wait()
        return carry

    lax.fori_loop(0, td, drain, 0)


def _dispatch_call(last_block_row, dest3, h2t, n_rows, td):
    N, S, L = h2t.shape
    return pl.pallas_call(
        _dispatch_kernel,
        grid_spec=pltpu.PrefetchScalarGridSpec(
            num_scalar_prefetch=1,
            grid=(N // td,),
            in_specs=[pl.BlockSpec(memory_space=pl.ANY),
                      pl.BlockSpec((td, S, L), lambda i, last: (i, 0, 0))],
            out_specs=pl.BlockSpec(memory_space=pl.ANY),
            scratch_shapes=[pltpu.SMEM((TOP_K, td), I32), pltpu.VMEM((MOE_ROWS, S, L), F32),
                            pltpu.SemaphoreType.DMA, pltpu.SemaphoreType.DMA,
                            pltpu.SemaphoreType.DMA]),
        out_shape=jax.ShapeDtypeStruct((n_rows, S, L), h2t.dtype),
        compiler_params=_cparams("arbitrary"),
        name="dispatch",
    )(last_block_row, dest3, h2t)


def _moe_kernel(be_ref, nu_ref, xs_ref, wgu_ref, wdn_ref, y_ref, wgu_bf, wdn_bf):
    i = pl.program_id(0)

    @pl.when(i < nu_ref[0])
    def _():
        @pl.when((i == 0) | (be_ref[i] != be_ref[jnp.maximum(i - 1, 0)]))
        def _():
            wgu_bf[...] = wgu_ref[0].astype(BF16)
            wdn_bf[...] = wdn_ref[0].astype(BF16)

        xb = jnp.concatenate([xs_ref[pl.ds(s, MOE_ROWS, stride=SUBLANES), :]
                              for s in range(SUBLANES)], axis=1).astype(BF16)
        gu = _dot(xb, wgu_bf[...])
        act = _silu(gu[:, :D_EXPERT]) * gu[:, D_EXPERT:]
        y = _dot(act.astype(BF16), wdn_bf[...])
        for s in range(SUBLANES):
            y_ref[pl.ds(s, MOE_ROWS, stride=SUBLANES), :] = y[:, s * LANES:(s + 1) * LANES]


def _moe_call(block_expert, n_used, xs, w_gu, w_dn):
    P = xs.shape[0] // SUBLANES
    D = SUBLANES * LANES
    nb = P // MOE_ROWS
    blk = lambda i, be, nu: (jnp.minimum(i, nu[0] - 1), 0)
    return pl.pallas_call(
        _moe_kernel,
        grid_spec=pltpu.PrefetchScalarGridSpec(
            num_scalar_prefetch=2,
            grid=(nb,),
            in_specs=[pl.BlockSpec((MOE_ROWS * SUBLANES, LANES), blk),
                      pl.BlockSpec((1, D, 2 * D_EXPERT), lambda i, be, nu: (be[i], 0, 0)),
                      pl.BlockSpec((1, D_EXPERT, D), lambda i, be, nu: (be[i], 0, 0))],
            out_specs=pl.BlockSpec((MOE_ROWS * SUBLANES, LANES), blk),
            scratch_shapes=[pltpu.VMEM((D, 2 * D_EXPERT), BF16), pltpu.VMEM((D_EXPERT, D), BF16)]),
        out_shape=jax.ShapeDtypeStruct((P * SUBLANES, LANES), F32),
        compiler_params=_cparams("arbitrary"),
        name="moe",
    )(block_expert, n_used, xs, w_gu, w_dn)


def _combine_kernel(dest_hbm, w_hbm, y_hbm, h_ref, x1_ref, mod_ref, wsg_ref, wsd_ref, o_ref,
                    idx, wts, rows, routed, sem_idx, sem_rows):
    i = pl.program_id(0)
    n_tiles = pl.num_programs(0)
    tc = h_ref.shape[0]
    cur = i % 2
    nxt = 1 - cur

    def gather_copy(buf, src_row, slot, t):
        dst_rows = pl.ds(pl.multiple_of(t * SUBLANES, SUBLANES), SUBLANES)
        return pltpu.make_async_copy(y_hbm.at[src_row], rows.at[buf, slot, dst_rows, :],
                                     sem_rows.at[buf])

    def start_gather(tile, buf):
        cp = pltpu.make_async_copy(dest_hbm.at[tile], idx.at[buf], sem_idx)
        cp_w = pltpu.make_async_copy(w_hbm.at[tile], wts.at[buf], sem_idx)
        cp.start()
        cp_w.start()
        cp.wait()
        cp_w.wait()

        def issue(tt, carry):
            base = pl.multiple_of(tt * SUBLANES, SUBLANES)
            for u in range(SUBLANES):
                for j in range(TOP_K):
                    gather_copy(buf, idx[buf, j, base + u], j, base + u).start()
            return carry

        lax.fori_loop(0, tc // SUBLANES, issue, 0)

    @pl.when(i == 0)
    def _():
        start_gather(0, 0)

    @pl.when(i + 1 < n_tiles)
    def _():
        start_gather(i + 1, nxt)

    gu = _mm(h_ref[...], wsg_ref[...])
    act = _silu(gu[:, :D_EXPERT]) * gu[:, D_EXPERT:]
    ffn = _mm(act, wsd_ref[...])

    def drain(t, carry):
        for j in range(TOP_K):
            gather_copy(cur, 0, j, 0).wait()
        return carry

    lax.fori_loop(0, tc, drain, 0)
    def wsum(tt, carry):
        for u in range(SUBLANES):
            t = tt * SUBLANES + u
            tile = pl.ds(pl.multiple_of(t * SUBLANES, SUBLANES), SUBLANES)
            acc = rows[cur, 0, tile, :] * wts[cur, 0, t]
            for j in range(1, TOP_K):
                acc = acc + rows[cur, j, tile, :] * wts[cur, j, t]
            routed[tile, :] = acc
        return carry

    lax.fori_loop(0, tc // SUBLANES, wsum, 0)
    routed2d = jnp.concatenate([routed[pl.ds(s, tc, stride=SUBLANES), :] for s in range(SUBLANES)],
                               axis=1)
    g_f = mod_ref[0, 5:6, :]
    o_ref[...] = x1_ref[...] + g_f * (ffn + routed2d)


def _combine_call(dest3, w_nk, y, h2, x1, mod3, w_sh_gu, w_sh_dn, tokens_per_batch, tc):
    N, D = h2.shape
    tiles_per_batch = tokens_per_batch // tc
    tok = pl.BlockSpec((tc, D), lambda i: (i, 0))
    wsg = w_sh_gu.astype(BF16)
    wsd = w_sh_dn.astype(BF16)
    return pl.pallas_call(
        _combine_kernel,
        grid=(N // tc,),
        in_specs=[pl.BlockSpec(memory_space=pl.ANY),
                  pl.BlockSpec(memory_space=pl.ANY),
                  pl.BlockSpec(memory_space=pl.ANY),
                  tok, tok,
                  pl.BlockSpec((1, 6, D), lambda i: (i // tiles_per_batch, 0, 0)),
                  pl.BlockSpec(wsg.shape, lambda i: (0, 0)),
                  pl.BlockSpec(wsd.shape, lambda i: (0, 0))],
        out_specs=tok,
        out_shape=jax.ShapeDtypeStruct((N, D), F32),
        scratch_shapes=[pltpu.SMEM((2, TOP_K, tc), I32), pltpu.SMEM((2, TOP_K, tc), F32),
                        pltpu.VMEM((2, TOP_K, tc * SUBLANES, LANES), F32),
                        pltpu.VMEM((tc * SUBLANES, LANES), F32),
                        pltpu.SemaphoreType.DMA, pltpu.SemaphoreType.DMA((2,))],
        compiler_params=_cparams("arbitrary"),
        name="combine",
    )(dest3, w_nk, y, h2, x1, mod3, wsg, wsd)


def _tile(n, pref):
    t = min(n, pref)
    assert n % t == 0, (n, t)
    return t


def _layer(x, mod3, positions, norm_mix, w_in, rwkv_mu, decay_w0, decay_up, iclr_a0, iclr_up,
           gate_up, rwkv_k_k, rwkv_k_a, rwkv_r_k, ln_x_w, ln_x_b, q_a_norm, w_q_b, kv_a_norm,
           w_kv_b, q_norm, k_norm, w_out, norm_ffn, w_router, router_bias, w_e_gate_up, w_e_down,
           w_sh_gate_up, w_sh_down):
    B, T, D = x.shape
    N = B * T
    assert T % SCAN_CHUNK == 0
    (r, lw, k2, v, kk, akk, g, bonus, q_pad, k_pad, v_pad) = _pre_call(
        x, mod3, positions, norm_mix, w_in, rwkv_mu, decay_w0, decay_up, iclr_a0, iclr_up,
        gate_up, rwkv_k_k, rwkv_k_a, rwkv_r_k, q_a_norm, w_q_b, kv_a_norm, w_kv_b, q_norm, k_norm,
        tm=_tile(T, 512))
    y = _scan_call(r, lw, k2, v, kk, akk)
    o_pad = _attn_call(q_pad, k_pad, v_pad)
    x1, h2, h2t = _post_call(y, bonus, g, o_pad, x, mod3, ln_x_w, ln_x_b, w_out, norm_ffn,
                             tm=_tile(T, 512))
    x1 = x1.reshape(N, D)
    h2 = h2.reshape(N, D)
    h2t = h2t.reshape(N, D // LANES, LANES)

    tr = _tile(N, 512)
    top_e, wts, rank, counts = _route_call(h2, w_router, router_bias, tr)
    counts = counts.reshape(N_EXPERTS)
    padded = (counts + MOE_ROWS - 1) // MOE_ROWS * MOE_ROWS
    pad_ends = jnp.cumsum(padded)
    pad_starts = pad_ends - padded
    n_blocks = (N * TOP_K + N_EXPERTS * (MOE_ROWS - 1)) // MOE_ROWS
    block_expert = jnp.minimum(
        jnp.searchsorted(pad_ends, jnp.arange(n_blocks, dtype=I32) * MOE_ROWS, side='right'),
        N_EXPERTS - 1).astype(I32)
    n_used = (pad_ends[-1:] // MOE_ROWS).astype(I32)
    last_block_row = jnp.where(padded > 0, pad_ends - MOE_ROWS, -1).astype(I32)
    dest = _dest_call(top_e, rank, pad_starts.astype(I32), tr)

    td = _tile(T, 256)
    dest3 = dest.reshape(TOP_K, N // td, td).transpose(1, 0, 2)
    n_rows = n_blocks * MOE_ROWS
    xs = _dispatch_call(last_block_row, dest3, h2t, n_rows, td)
    y_e = _moe_call(block_expert, n_used, xs.reshape(n_rows * SUBLANES, LANES), w_e_gate_up, w_e_down)
    y_e = y_e.reshape(n_rows, SUBLANES, LANES)
    w3 = wts.reshape(TOP_K, N // td, td).transpose(1, 0, 2)
    out = _combine_call(dest3, w3, y_e, h2, x1, mod3, w_sh_gate_up, w_sh_down, T, td)
    return out.reshape(B, T, D)


def kernel(x, c, positions, ada_w, ada_b, norm_mix, w_in, rwkv_mu, decay_w0, decay_up, iclr_a0, iclr_up, gate_up, rwkv_k_k, rwkv_k_a, rwkv_r_k, ln_x_w, ln_x_b, q_a_norm, w_q_b, kv_a_norm, w_kv_b, q_norm, k_norm, w_out, norm_ffn, w_router, router_bias, w_e_gate_up, w_e_down, w_sh_gate_up, w_sh_down):
    B, T, D = x.shape
    depth = ada_w.shape[0]
    for l in range(depth):
        mod3 = _mod_call(c, ada_w[l], ada_b[l]).reshape(B, 6, D)
        x = _layer(x, mod3, positions, norm_mix[l], w_in[l], rwkv_mu[l], decay_w0[l], decay_up[l],
                   iclr_a0[l], iclr_up[l], gate_up[l], rwkv_k_k[l], rwkv_k_a[l], rwkv_r_k[l],
                   ln_x_w[l], ln_x_b[l], q_a_norm[l], w_q_b[l], kv_a_norm[l], w_kv_b[l],
                   q_norm[l], k_norm[l], w_out[l], norm_ffn[l], w_router[l], router_bias[l],
                   w_e_gate_up[l], w_e_down[l], w_sh_gate_up[l], w_sh_down[l])
    return x
```

```python
import functools
import math

import jax
import jax.numpy as jnp
import numpy as np
from jax import lax
from jax.experimental import pallas as pl
from jax.experimental.pallas import tpu as pltpu

F32 = jnp.float32
BF16 = jnp.bfloat16
I32 = jnp.int32

NORM_EPS = 1e-6
GN_EPS = 64e-5
RWKV_HEADS = 8
RWKV_HEAD_DIM = 64
D_RWKV = 512
DECAY_LORA = 32
ICLR_LORA = 32
GATE_LORA = 96
MLA_HEADS = 8
QK_NOPE_DIM = 64
QK_ROPE_DIM = 32
QK_HEAD_DIM = 96
V_HEAD_DIM = 64
Q_LORA_RANK = 256
KV_LORA_RANK = 128
ROPE_THETA = 10000.0
N_EXPERTS = 256
TOP_K = 8
N_GROUPS = 8
TOPK_GROUPS = 4
GROUP_SIZE = N_EXPERTS // N_GROUPS
D_EXPERT = 256
ROUTED_SCALE = 2.5
MOE_ROWS = 512
DISPATCH_TILE = 128
COMBINE_TILE = 256

LANES = 128
SUBLANES = 8
HEAD_PAD = 128
VMEM_LIMIT = 56 * 1024 * 1024

SCAN_CHUNK = 64
SCAN_BLOCK = 256
ATTN_TILE = 512
ATTN_Q_SCALE = QK_HEAD_DIM ** -0.5 * math.log2(math.e)
NEG_INF = float("-inf")


def _cparams(*sem):
    return pltpu.CompilerParams(dimension_semantics=sem, vmem_limit_bytes=VMEM_LIMIT)


def _split2(a):
    hi = a.astype(BF16)
    lo = (a - hi.astype(F32)).astype(BF16)
    return hi, lo


def _split3(a):
    hi = a.astype(BF16)
    r1 = a - hi.astype(F32)
    mid = r1.astype(BF16)
    lo = (r1 - mid.astype(F32)).astype(BF16)
    return hi, mid, lo


def _dot(a, b, dims=None):
    if dims is None:
        return jnp.dot(a, b, preferred_element_type=F32)
    return lax.dot_general(a, b, (dims, ((), ())), preferred_element_type=F32)


def _mm(a, b, dims=None):
    return _dot(a.astype(BF16), b.astype(BF16), dims)


def _mm3(a, b, dims=None):
    ah, al = _split2(a)
    bh, bl = _split2(b)
    return _dot(ah, bh, dims) + (_dot(ah, bl, dims) + _dot(al, bh, dims))


def _mm_exact_rhs(a, b_exact_bf16, dims=None):
    h, m, l = _split3(a)
    return _dot(h, b_exact_bf16, dims) + (_dot(m, b_exact_bf16, dims) + _dot(l, b_exact_bf16, dims))


NT = ((1,), (1,))
TN = ((0,), (0,))


def _sigmoid(z):
    return 1.0 / (1.0 + jnp.exp(-z))


def _silu(z):
    return z * _sigmoid(z)


def _seg_ones(width, seg):
    r = lax.broadcasted_iota(I32, (width, width), 0) // seg
    c = lax.broadcasted_iota(I32, (width, width), 1) // seg
    return jnp.where(r == c, 1.0, 0.0).astype(BF16)


def _segsum(a, ones_bd):
    hi, lo = _split2(a)
    return _dot(hi, ones_bd) + _dot(lo, ones_bd)


def _mod_kernel(c_ref, w_ref, b_ref, o_ref):
    ca = _silu(c_ref[...])
    o_ref[...] = _mm3(ca, w_ref[...]) + b_ref[...]


def _mod_call(c, ada_w, ada_b):
    B, D = c.shape
    n6 = ada_w.shape[1]
    tn = D
    return pl.pallas_call(
        _mod_kernel,
        grid=(n6 // tn,),
        in_specs=[pl.BlockSpec((B, D), lambda j: (0, 0)),
                  pl.BlockSpec((D, tn), lambda j: (0, j)),
                  pl.BlockSpec((1, tn), lambda j: (0, j))],
        out_specs=pl.BlockSpec((B, tn), lambda j: (0, j)),
        out_shape=jax.ShapeDtypeStruct((B, n6), F32),
        compiler_params=_cparams("arbitrary"),
        name="mod",
    )(c, ada_w, ada_b.reshape(1, n6))


def _pre_kernel(x_ref, mod_ref, pos_ref, nmix_ref, wrkv_ref, wlora_ref, wmla_ref,
                mu_rkv_ref, mu_lora_ref, wup_ref, w0_ref, a0_ref, kk_ref, ka_ref, rk_ref,
                qan_ref, wqb_ref, kvan_ref, wkb_ref, wvb_ref, qn_ref, kn_ref, invf_ref,
                r_ref, lw_ref, k_ref, v_ref, kkn_ref, akk_ref, g_ref, bonus_ref,
                q_ref, kout_ref, vout_ref,
                carry_rkv, carry_lora):
    ti = pl.program_id(1)
    tm = x_ref.shape[1]

    @pl.when(ti == 0)
    def _():
        carry_rkv[...] = jnp.zeros_like(carry_rkv)
        carry_lora[...] = jnp.zeros_like(carry_lora)

    xb = x_ref[0]
    sh_a = mod_ref[0, 0:1, :]
    sc_a = mod_ref[0, 1:2, :]
    ms = jnp.mean(xb * xb, axis=-1, keepdims=True)
    h = xb * lax.rsqrt(ms + NORM_EPS) * nmix_ref[...] * (1.0 + sc_a) + sh_a
    hb = h.astype(BF16)
    u_rkv = _dot(hb, wrkv_ref[...])
    u_lora = _dot(hb, wlora_ref[...])
    u_mla = _dot(hb, wmla_ref[...])

    row0 = lax.broadcasted_iota(I32, (tm, 1), 0) == 0

    def shift(u, carry):
        prev = jnp.where(row0, carry[...], pltpu.roll(u, 1, 0))
        carry[...] = u[tm - 1:tm, :]
        return prev

    prev_rkv = shift(u_rkv, carry_rkv)
    prev_lora = shift(u_lora, carry_lora)
    us = u_rkv + (prev_rkv - u_rkv) * mu_rkv_ref[...]
    ul = u_lora + (prev_lora - u_lora) * mu_lora_ref[...]
    r = us[:, 0:D_RWKV]
    k = us[:, D_RWKV:2 * D_RWKV]
    v = us[:, 2 * D_RWKV:3 * D_RWKV]

    lane_l = lax.broadcasted_iota(I32, ul.shape, 1)
    t_in = jnp.where(lane_l < DECAY_LORA, jnp.tanh(ul),
                     jnp.where(lane_l < DECAY_LORA + ICLR_LORA, ul, _sigmoid(ul)))
    up = _mm(t_in, wup_ref[...])
    z = w0_ref[...] + up[:, 0:D_RWKV]
    lw = (-math.exp(-0.5)) * _sigmoid(z)
    a = _sigmoid(a0_ref[...] + up[:, D_RWKV:2 * D_RWKV])
    g = up[:, 2 * D_RWKV:3 * D_RWKV]

    ones64 = _seg_ones(D_RWKV, RWKV_HEAD_DIM)
    kk = k * kk_ref[...]
    ss = _segsum(kk * kk, ones64)
    kk = kk * lax.rsqrt(jnp.maximum(ss, 1e-24))
    k2 = k * (1.0 + (a - 1.0) * ka_ref[...])
    bonus = _segsum(r * k2 * rk_ref[...], ones64) * v

    r_ref[0] = r
    lw_ref[0] = lw
    k_ref[0] = k2
    v_ref[0] = v
    kkn_ref[0] = kk
    akk_ref[0] = a * kk
    g_ref[0] = g
    bonus_ref[0] = bonus

    q_lat = u_mla[:, 0:Q_LORA_RANK]
    kv_lat = u_mla[:, Q_LORA_RANK:Q_LORA_RANK + KV_LORA_RANK]
    kpe_tile = u_mla[:, Q_LORA_RANK + KV_LORA_RANK:]
    qn = q_lat * lax.rsqrt(jnp.mean(q_lat * q_lat, axis=-1, keepdims=True) + NORM_EPS) * qan_ref[...]
    kvn = kv_lat * lax.rsqrt(jnp.mean(kv_lat * kv_lat, axis=-1, keepdims=True) + NORM_EPS) * kvan_ref[...]
    q_raw = _mm(qn, wqb_ref[...])
    kvb = kvn.astype(BF16)
    k_raw = _dot(kvb, wkb_ref[...])
    v_pad = _dot(kvb, wvb_ref[...])
    kpe_h = pltpu.roll(kpe_tile, QK_NOPE_DIM, 1)

    half = QK_ROPE_DIM // 2
    ang_t = invf_ref[...] * pos_ref[0].astype(F32)
    frow = lax.broadcasted_iota(I32, (half, HEAD_PAD), 0)
    flane = lax.broadcasted_iota(I32, (half, HEAD_PAD), 1)
    at_x1 = flane == frow + QK_NOPE_DIM
    at_x2 = flane == frow + QK_NOPE_DIM + half
    e_cos = jnp.where(at_x1 | at_x2, 1.0, 0.0).astype(BF16)
    e_sin = jnp.concatenate([jnp.where(at_x1, -1.0, 0.0), jnp.where(at_x2, 1.0, 0.0)],
                            axis=1).astype(BF16)
    lane = lax.broadcasted_iota(I32, (1, HEAD_PAD), 1)
    off_rope = jnp.where((lane >= QK_NOPE_DIM) & (lane < QK_HEAD_DIM), 0.0, 1.0)
    cos_t = _mm_exact_rhs(jnp.cos(ang_t), e_cos, TN) + off_rope
    sin2 = _mm_exact_rhs(jnp.sin(ang_t), e_sin, TN)
    s1 = sin2[:, :HEAD_PAD]
    s2 = sin2[:, HEAD_PAD:]

    def tables(gain, scale):
        g = gain * scale
        return (cos_t * g, s1 * pltpu.roll(g, HEAD_PAD - half, 1), s2 * pltpu.roll(g, half, 1))

    def norm_rope(xh, tabs):
        c_g, s1_g, s2_g = tabs
        ssq = jnp.sum(xh * xh, axis=-1, keepdims=True) * (1.0 / QK_HEAD_DIM)
        rot = xh * c_g + pltpu.roll(xh, HEAD_PAD - half, 1) * s1_g + pltpu.roll(xh, half, 1) * s2_g
        return rot * lax.rsqrt(ssq + NORM_EPS)

    q_tabs = tables(qn_ref[...], ATTN_Q_SCALE)
    k_tabs = tables(kn_ref[...], 1.0)
    for hh in range(MLA_HEADS):
        sl = slice(hh * HEAD_PAD, (hh + 1) * HEAD_PAD)
        q_ref[0, :, sl] = norm_rope(q_raw[:, sl], q_tabs).astype(BF16)
        kout_ref[0, :, sl] = norm_rope(k_raw[:, sl] + kpe_h, k_tabs).astype(BF16)
    vout_ref[0] = v_pad.astype(BF16)


def _pad_heads(w, n_heads, width):
    kdim = w.shape[0]
    w = w.reshape(kdim, n_heads, width)
    w = jnp.pad(w, ((0, 0), (0, 0), (0, HEAD_PAD - width)))
    return w.reshape(kdim, n_heads * HEAD_PAD)


def _pre_call(x, mod3, positions, norm_mix, w_in, rwkv_mu, decay_w0, decay_up, iclr_a0, iclr_up,
              gate_up, k_k, k_a, r_k, q_a_norm, w_q_b, kv_a_norm, w_kv_b, q_norm, k_norm, tm):
    B, T, D = x.shape
    n_rkv = 3 * D_RWKV
    n_lora = DECAY_LORA + ICLR_LORA + GATE_LORA
    LORA_PAD = 256
    MLA_PAD = 512
    n_mla = Q_LORA_RANK + KV_LORA_RANK + QK_ROPE_DIM
    w_rkv = w_in[:, :n_rkv].astype(BF16)
    w_lora = jnp.pad(w_in[:, n_rkv:n_rkv + n_lora], ((0, 0), (0, LORA_PAD - n_lora))).astype(BF16)
    w_mla = jnp.pad(w_in[:, n_rkv + n_lora:], ((0, 0), (0, MLA_PAD - n_mla))).astype(BF16)
    mu_rkv = rwkv_mu[:n_rkv].reshape(1, n_rkv)
    mu_lora = jnp.pad(rwkv_mu[n_rkv:], (0, LORA_PAD - n_lora)).reshape(1, LORA_PAD)
    w_up = jnp.zeros((LORA_PAD, n_rkv), F32)
    w_up = w_up.at[0:DECAY_LORA, 0:D_RWKV].set(decay_up)
    w_up = w_up.at[DECAY_LORA:DECAY_LORA + ICLR_LORA, D_RWKV:2 * D_RWKV].set(iclr_up)
    w_up = w_up.at[DECAY_LORA + ICLR_LORA:n_lora, 2 * D_RWKV:].set(gate_up)
    w_up = w_up.astype(BF16)
    w_qb = _pad_heads(w_q_b, MLA_HEADS, QK_HEAD_DIM).astype(BF16)
    w_kv3 = w_kv_b.reshape(KV_LORA_RANK, MLA_HEADS, QK_NOPE_DIM + V_HEAD_DIM)
    w_kb = _pad_heads(w_kv3[:, :, :QK_NOPE_DIM].reshape(KV_LORA_RANK, -1), MLA_HEADS, QK_NOPE_DIM).astype(BF16)
    w_vb = _pad_heads(w_kv3[:, :, QK_NOPE_DIM:].reshape(KV_LORA_RANK, -1), MLA_HEADS, V_HEAD_DIM).astype(BF16)
    qn_pad = jnp.pad(q_norm, (0, HEAD_PAD - QK_HEAD_DIM)).reshape(1, HEAD_PAD)
    kn_pad = jnp.pad(k_norm, (0, HEAD_PAD - QK_HEAD_DIM)).reshape(1, HEAD_PAD)
    inv_freq = ROPE_THETA ** (-jnp.arange(0, QK_ROPE_DIM, 2, dtype=F32) / QK_ROPE_DIM)
    invf = inv_freq.reshape(QK_ROPE_DIM // 2, 1)
    pos3 = positions.reshape(B, 1, T)
    HP = MLA_HEADS * HEAD_PAD

    row = lambda n: pl.BlockSpec((1, n), lambda b, t: (0, 0))
    full = lambda a: pl.BlockSpec(a.shape, lambda b, t: (0,) * a.ndim)
    tok = lambda n: pl.BlockSpec((1, tm, n), lambda b, t: (b, t, 0))
    outs = ([jax.ShapeDtypeStruct((B, T, D_RWKV), F32)] * 8
            + [jax.ShapeDtypeStruct((B, T, HP), BF16)] * 3)
    return pl.pallas_call(
        _pre_kernel,
        grid=(B, T // tm),
        in_specs=[tok(D),
                  pl.BlockSpec((1, 6, D), lambda b, t: (b, 0, 0)),
                  pl.BlockSpec((1, 1, tm), lambda b, t: (b, 0, t)),
                  row(D), full(w_rkv), full(w_lora), full(w_mla),
                  row(n_rkv), row(LORA_PAD), full(w_up), row(D_RWKV), row(D_RWKV),
                  row(D_RWKV), row(D_RWKV), row(D_RWKV),
                  row(Q_LORA_RANK), full(w_qb), row(KV_LORA_RANK), full(w_kb), full(w_vb),
                  row(HEAD_PAD), row(HEAD_PAD), full(invf)],
        out_specs=[tok(D_RWKV)] * 8 + [tok(HP)] * 3,
        out_shape=outs,
        scratch_shapes=[pltpu.VMEM((1, n_rkv), F32), pltpu.VMEM((1, LORA_PAD), F32)],
        compiler_params=_cparams("arbitrary", "arbitrary"),
        name="pre",
    )(x, mod3, pos3, norm_mix.reshape(1, D), w_rkv, w_lora, w_mla, mu_rkv, mu_lora, w_up,
      decay_w0.reshape(1, -1), iclr_a0.reshape(1, -1), k_k.reshape(1, -1), k_a.reshape(1, -1),
      r_k.reshape(1, -1), q_a_norm.reshape(1, -1), w_qb, kv_a_norm.reshape(1, -1), w_kb, w_vb,
      qn_pad, kn_pad, invf)


def _scan_kernel(r_ref, lw_ref, k_ref, v_ref, kk_ref, akk_ref, y_ref, state):
    C = SCAN_CHUNK
    n_chunks = r_ref.shape[1] // C
    n_pairs = r_ref.shape[2] // LANES

    @pl.when(pl.program_id(1) == 0)
    def _():
        state[...] = jnp.zeros_like(state)

    ri = lax.broadcasted_iota(I32, (C, C), 0)
    ci = lax.broadcasted_iota(I32, (C, C), 1)
    tri_incl = jnp.where(ci <= ri, 1.0, 0.0).astype(BF16)
    r2 = lax.broadcasted_iota(I32, (2 * C, 2 * C), 0)
    c2 = lax.broadcasted_iota(I32, (2 * C, 2 * C), 1)
    same = (r2 >= C) == (c2 >= C)
    strict = same & (c2 < r2)
    incl = same & (c2 <= r2)
    eye = jnp.where(c2 == r2, 1.0, 0.0)
    head0 = lax.broadcasted_iota(I32, (C, LANES), 1) < RWKV_HEAD_DIM

    def stack2(a):
        return jnp.concatenate([jnp.where(head0, a, 0.0), jnp.where(head0, 0.0, a)], axis=0)

    C2 = 2 * C
    cat0 = lambda *a: jnp.concatenate(a, axis=0)
    cat1 = lambda *a: jnp.concatenate(a, axis=1)

    items = []
    for c in range(n_chunks):
        rows = slice(c * C, (c + 1) * C)
        lw = lw_ref[0, rows, :]
        cum = _mm_exact_rhs_left(tri_incl, lw)
        cum_end = cum[C - 1:C, :]
        w_end = jnp.exp(cum_end)
        e_pos = jnp.exp(cum)
        e_neg = jnp.exp(-cum)
        e_prev = jnp.exp(cum - lw)
        e_end = jnp.exp(cum_end - cum)
        kk = kk_ref[0, rows, :]
        k2 = k_ref[0, rows, :]
        pneg = -akk_ref[0, rows, :]
        vv = v_ref[0, rows, :]
        rt = r_ref[0, rows, :] * e_pos
        bt = kk * e_prev
        pt = pneg * e_neg
        kt = k2 * e_neg
        ph = pneg * e_end
        kh = k2 * e_end
        for pp in range(n_pairs):
            sl = slice(pp * LANES, (pp + 1) * LANES)
            items.append(dict(
                c=c, p=pp, w_end=w_end[:, sl],
                bt2=stack2(bt[:, sl]).astype(BF16), rt2=stack2(rt[:, sl]).astype(BF16),
                pk2=cat0(stack2(pt[:, sl]), stack2(kt[:, sl])).astype(BF16),
                phkh2=cat0(stack2(ph[:, sl]), stack2(kh[:, sl])).astype(BF16),
                v2=stack2(vv[:, sl])))
    for it in items:
        ab = _dot(cat0(it['bt2'], it['rt2']), it['pk2'], NT)
        it['a_ab'] = jnp.where(strict, ab[:C2, :C2], 0.0)
        it['a_ak'] = jnp.where(strict, ab[:C2, C2:], 0.0).astype(BF16)
        it['b_rpk'] = cat1(jnp.where(incl, ab[C2:, :C2], 0.0), jnp.where(incl, ab[C2:, C2:], 0.0)).astype(BF16)
        it['tinv'] = eye + it['a_ab']
    for it in items:
        it['apow'] = _mm(it['a_ab'], it['a_ab'])
    for _ in range(int(math.log2(C)) - 1):
        for it in items:
            both = _mm(cat0(it['apow'], it['tinv']), it['apow'])
            it['apow'] = both[:C2]
            it['tinv'] = it['tinv'] + both[C2:]
    for it in items:
        it['akv'] = _dot(it['a_ak'], it['v2'].astype(BF16))
    for it in items:
        tt = _dot(it['tinv'].astype(BF16), cat1(it['bt2'], it['akv'].astype(BF16)))
        it['tb_rt'] = cat0(tt[:, :LANES].astype(BF16), it['rt2'])
        it['tav'] = tt[:, LANES:]
    for it in items:
        pp = it['p']
        s0 = state[pp]
        top = _dot(it['tb_rt'], s0.astype(BF16), NT)
        u2 = top[:C2] + it['tav']
        uv = cat0(u2, it['v2']).astype(BF16)
        y2 = top[C2:] + _dot(it['b_rpk'], uv)
        state[pp] = s0 * it['w_end'] + _dot(uv, it['phkh2'], TN)
        y_ref[0, it['c'] * C:(it['c'] + 1) * C, pp * LANES:(pp + 1) * LANES] = y2[0:C] + y2[C:C2]


def _mm_exact_rhs_left(b_exact_bf16, a):
    h, m, l = _split3(a)
    return _dot(b_exact_bf16, h) + (_dot(b_exact_bf16, m) + _dot(b_exact_bf16, l))


def _scan_call(r, lw, k2, v, kk, akk):
    B, T, W = r.shape
    tb = _tile(T, SCAN_BLOCK)
    spec = pl.BlockSpec((1, tb, W), lambda b, c: (b, c, 0))
    return pl.pallas_call(
        _scan_kernel,
        grid=(B, T // tb),
        in_specs=[spec] * 6,
        out_specs=spec,
        out_shape=jax.ShapeDtypeStruct((B, T, W), F32),
        scratch_shapes=[pltpu.VMEM((W // LANES, 2 * RWKV_HEAD_DIM, LANES), F32)],
        compiler_params=_cparams("arbitrary", "arbitrary"),
        name="scan",
    )(r, lw, k2, v, kk, akk)


def _attn_kernel(q_ref, k_ref, v_ref, o_ref):
    T = q_ref.shape[1]
    tq = min(T, ATTN_TILE)
    row = lax.broadcasted_iota(I32, (tq, tq), 0)
    col = lax.broadcasted_iota(I32, (tq, tq), 1)
    causal = col <= row

    def update(q, kt, vt, carry, mask):
        m_old, l_old, acc = carry
        s = _dot(q, kt, NT)
        if mask:
            s = jnp.where(causal, s, NEG_INF)
        m_new = jnp.maximum(m_old, jnp.max(s, axis=-1, keepdims=True))
        alpha = jnp.exp2(m_old - m_new)
        p = jnp.exp2(s - m_new)
        l_new = alpha * l_old + jnp.sum(p, axis=-1, keepdims=True)
        acc = alpha * acc + _dot(p.astype(BF16), vt)
        return m_new, l_new, acc

    for qi in range(T // tq):
        q = q_ref[0, qi * tq:(qi + 1) * tq, :]
        carry = (jnp.full((tq, 1), NEG_INF, F32), jnp.zeros((tq, 1), F32),
                 jnp.zeros((tq, HEAD_PAD), F32))

        def body(ki, carry, q=q):
            rows = pl.ds(pl.multiple_of(ki * tq, tq), tq)
            return update(q, k_ref[0, rows, :], v_ref[0, rows, :], carry, False)

        carry = lax.fori_loop(0, qi, body, carry, unroll=True)
        diag = slice(qi * tq, (qi + 1) * tq)
        _, l_fin, acc = update(q, k_ref[0, diag, :], v_ref[0, diag, :], carry, True)
        o_ref[0, diag, :] = (acc / l_fin).astype(o_ref.dtype)


def _attn_call(q, k, v):
    B, T, HP = q.shape
    spec = pl.BlockSpec((1, T, HEAD_PAD), lambda b, h: (b, 0, h))
    return pl.pallas_call(
        _attn_kernel,
        grid=(B, MLA_HEADS),
        in_specs=[spec, spec, spec],
        out_specs=spec,
        out_shape=jax.ShapeDtypeStruct((B, T, HP), BF16),
        compiler_params=_cparams("arbitrary", "arbitrary"),
        name="attn",
    )(q, k, v)


def _post_kernel(y_ref, bonus_ref, g_ref, o_ref, x_ref, mod_ref, lnw_ref, lnb_ref,
                 wo_r_ref, wo_m_ref, nffn_ref, x1_ref, h2_ref, h2t_ref):
    y = y_ref[0]
    ones64 = _seg_ones(D_RWKV, RWKV_HEAD_DIM)
    mean = _segsum(y, ones64) * (1.0 / RWKV_HEAD_DIM)
    yc = y - mean
    var = _segsum(yc * yc, ones64) * (1.0 / RWKV_HEAD_DIM)
    yn = yc * lax.rsqrt(var + GN_EPS) * lnw_ref[...] + lnb_ref[...]
    yr = (yn + bonus_ref[0]) * g_ref[0]
    mix = _mm(yr, wo_r_ref[...]) + _dot(o_ref[0], wo_m_ref[...])
    g_a = mod_ref[0, 2:3, :]
    sh_f = mod_ref[0, 3:4, :]
    sc_f = mod_ref[0, 4:5, :]
    x1 = x_ref[0] + g_a * mix
    x1_ref[0] = x1
    ms = jnp.mean(x1 * x1, axis=-1, keepdims=True)
    h2 = x1 * lax.rsqrt(ms + NORM_EPS) * nffn_ref[...] * (1.0 + sc_f) + sh_f
    h2_ref[0] = h2
    tm = h2.shape[0]
    for s in range(SUBLANES):
        h2t_ref[0, pl.ds(s, tm, stride=SUBLANES), :] = h2[:, s * LANES:(s + 1) * LANES]


def _post_call(y, bonus, g, o_pad, x, mod3, ln_w, ln_b, w_out, norm_ffn, tm):
    B, T, D = x.shape
    HP = MLA_HEADS * HEAD_PAD
    wo_r = w_out[:D_RWKV].astype(BF16)
    wo_m = jnp.pad(w_out[D_RWKV:].reshape(MLA_HEADS, V_HEAD_DIM, D),
                   ((0, 0), (0, HEAD_PAD - V_HEAD_DIM), (0, 0))).reshape(HP, D).astype(BF16)
    tok = lambda n: pl.BlockSpec((1, tm, n), lambda b, t: (b, t, 0))
    row = lambda n: pl.BlockSpec((1, n), lambda b, t: (0, 0))
    full = lambda a: pl.BlockSpec(a.shape, lambda b, t: (0,) * a.ndim)
    return pl.pallas_call(
        _post_kernel,
        grid=(B, T // tm),
        in_specs=[tok(D_RWKV), tok(D_RWKV), tok(D_RWKV), tok(HP), tok(D),
                  pl.BlockSpec((1, 6, D), lambda b, t: (b, 0, 0)),
                  row(D_RWKV), row(D_RWKV), full(wo_r), full(wo_m), row(D)],
        out_specs=[tok(D), tok(D),
                   pl.BlockSpec((1, tm * SUBLANES, LANES), lambda b, t: (b, t, 0))],
        out_shape=[jax.ShapeDtypeStruct((B, T, D), F32)] * 2
        + [jax.ShapeDtypeStruct((B, T * SUBLANES, LANES), F32)],
        compiler_params=_cparams("arbitrary", "arbitrary"),
        name="post",
    )(y, bonus, g, o_pad, x, mod3, ln_w.reshape(1, -1), ln_b.reshape(1, -1), wo_r, wo_m,
      norm_ffn.reshape(1, D))


def _first_index(mask, iota, size, axis):
    return jnp.min(jnp.where(mask, iota, size), axis=axis, keepdims=True)


def _route_kernel(h_ref, wr_ref, bias_ref, e_ref, w_ref, rank_ref, cnt_ref, base):
    tr = h_ref.shape[0]
    E = N_EXPERTS

    @pl.when(pl.program_id(0) == 0)
    def _():
        base[...] = jnp.zeros_like(base)

    logits = _mm3(wr_ref[...], h_ref[...], NT)
    scores = _sigmoid(logits)
    sel = scores + bias_ref[...]
    iota_g = lax.broadcasted_iota(I32, (GROUP_SIZE, tr), 0)
    gs_rows = []
    for gi in range(N_GROUPS):
        blk = sel[gi * GROUP_SIZE:(gi + 1) * GROUP_SIZE, :]
        m1 = jnp.max(blk, axis=0, keepdims=True)
        i1 = _first_index(blk == m1, iota_g, GROUP_SIZE, 0)
        m2 = jnp.max(jnp.where(iota_g == i1, NEG_INF, blk), axis=0, keepdims=True)
        gs_rows.append(m1 + m2)
    gs = jnp.concatenate(gs_rows, axis=0)
    iota8 = lax.broadcasted_iota(I32, (N_GROUPS, tr), 0)
    gmask = jnp.zeros((N_GROUPS, tr), jnp.bool_)
    for _ in range(TOPK_GROUPS):
        mg = jnp.max(gs, axis=0, keepdims=True)
        ig = _first_index(gs == mg, iota8, N_GROUPS, 0)
        hit = iota8 == ig
        gmask = gmask | hit
        gs = jnp.where(hit, NEG_INF, gs)
    msel = jnp.concatenate(
        [jnp.where(gmask[gi:gi + 1, :], sel[gi * GROUP_SIZE:(gi + 1) * GROUP_SIZE, :], NEG_INF)
         for gi in range(N_GROUPS)], axis=0)
    iota_e = lax.broadcasted_iota(I32, (E, tr), 0)
    e_rows, w_rows = [], []
    onehot = jnp.zeros((E, tr), F32)
    for _ in range(TOP_K):
        mv = jnp.max(msel, axis=0, keepdims=True)
        ie = _first_index(msel == mv, iota_e, E, 0)
        hit = iota_e == ie
        e_rows.append(ie)
        w_rows.append(jnp.sum(jnp.where(hit, scores, 0.0), axis=0, keepdims=True))
        onehot = jnp.where(hit, 1.0, onehot)
        msel = jnp.where(hit, NEG_INF, msel)
    top_e = jnp.concatenate(e_rows, axis=0)
    wts = jnp.concatenate(w_rows, axis=0)
    wts = wts / jnp.sum(wts, axis=0, keepdims=True) * ROUTED_SCALE
    ti = lax.broadcasted_iota(I32, (tr, tr), 0)
    tj = lax.broadcasted_iota(I32, (tr, tr), 1)
    upper = jnp.where(ti < tj, 1.0, 0.0).astype(BF16)
    pos = _dot(onehot.astype(BF16), upper) + base[...]
    rank_rows = [jnp.sum(jnp.where(iota_e == e_rows[j], pos, 0.0), axis=0, keepdims=True)
                 for j in range(TOP_K)]
    base[...] = base[...] + jnp.sum(onehot, axis=1, keepdims=True)
    e_ref[...] = top_e
    w_ref[...] = wts
    rank_ref[...] = jnp.concatenate(rank_rows, axis=0).astype(I32)
    cnt_ref[...] = base[...].astype(I32)


def _route_call(h2, w_router, router_bias, tr):
    N, D = h2.shape
    E = N_EXPERTS
    out_kn = pl.BlockSpec((TOP_K, tr), lambda i: (0, i))
    return pl.pallas_call(
        _route_kernel,
        grid=(N // tr,),
        in_specs=[pl.BlockSpec((tr, D), lambda i: (i, 0)),
                  pl.BlockSpec((E, D), lambda i: (0, 0)),
                  pl.BlockSpec((E, 1), lambda i: (0, 0))],
        out_specs=[out_kn, out_kn, out_kn, pl.BlockSpec((E, 1), lambda i: (0, 0))],
        out_shape=[jax.ShapeDtypeStruct((TOP_K, N), I32), jax.ShapeDtypeStruct((TOP_K, N), F32),
                   jax.ShapeDtypeStruct((TOP_K, N), I32), jax.ShapeDtypeStruct((E, 1), I32)],
        scratch_shapes=[pltpu.VMEM((E, 1), F32)],
        compiler_params=_cparams("arbitrary"),
        name="route",
    )(h2, w_router.T, router_bias.reshape(E, 1))


def _dest_kernel(e_ref, rank_ref, start_ref, d_ref):
    tr = e_ref.shape[1]
    iota_e = lax.broadcasted_iota(I32, (N_EXPERTS, tr), 0)
    starts = start_ref[...]
    rows = [jnp.sum(jnp.where(iota_e == e_ref[j:j + 1, :], starts, 0), axis=0, keepdims=True)
            for j in range(TOP_K)]
    d_ref[...] = jnp.concatenate(rows, axis=0) + rank_ref[...]


def _dest_call(top_e, rank, pad_starts, tr):
    K, N = top_e.shape
    spec = pl.BlockSpec((K, tr), lambda i: (0, i))
    return pl.pallas_call(
        _dest_kernel,
        grid=(N // tr,),
        in_specs=[spec, spec, pl.BlockSpec((N_EXPERTS, 1), lambda i: (0, 0))],
        out_specs=spec,
        out_shape=jax.ShapeDtypeStruct((K, N), I32),
        compiler_params=_cparams("arbitrary"),
        name="dest",
    )(top_e, rank, pad_starts.reshape(N_EXPERTS, 1))


def _dispatch_kernel(last_ref, dest_hbm, h_ref, xs_out, idx, src, zbuf, sem_idx, sem_rows, sem_zero):
    i = pl.program_id(0)
    n_tiles = pl.num_programs(0)
    td = h_ref.shape[0]
    cur = i % 2
    cp = pltpu.make_async_copy(dest_hbm.at[i], idx, sem_idx)
    cp.start()
    sub = lax.broadcasted_iota(I32, (1, SUBLANES, LANES), 1)
    lane = lax.broadcasted_iota(I32, (1, SUBLANES, LANES), 2)
    tag_here = (sub == 0) & (lane == 0)

    @pl.when(i == 0)
    def _():
        zbuf[:, 0:SUBLANES, :] = jnp.zeros((MOE_ROWS, SUBLANES, LANES), F32)
        zbuf[:, SUBLANES:, :] = jnp.broadcast_to(jnp.where(tag_here, -1.0, 0.0),
                                                 (MOE_ROWS, SUBLANES, LANES))

        def zero_copy(e):
            return pltpu.make_async_copy(zbuf, xs_out.at[pl.ds(last_ref[e], MOE_ROWS)], sem_zero)

        def z_issue(e, carry):
            @pl.when(last_ref[e] >= 0)
            def _():
                zero_copy(e).start()
            return carry

        def z_drain(e, carry):
            @pl.when(last_ref[e] >= 0)
            def _():
                zero_copy(e).wait()
            return carry

        lax.fori_loop(0, N_EXPERTS, z_issue, 0)
        lax.fori_loop(0, N_EXPERTS, z_drain, 0)

    tok = (i * td + lax.broadcasted_iota(I32, (td, SUBLANES, LANES), 0)) * TOP_K
    payload = h_ref[...]
    for j in range(TOP_K):
        src[cur, j, :, 0:SUBLANES, :] = payload
        src[cur, j, :, SUBLANES:, :] = jnp.where(tag_here, (tok + j).astype(F32), 0.0)
    cp.wait()

    def row_copy(buf, slot, t, dst_row):
        return pltpu.make_async_copy(src.at[buf, slot, t], xs_out.at[dst_row], sem_rows.at[buf])

    def issue(tt, carry):
        base = pl.multiple_of(tt * SUBLANES, SUBLANES)
        for u in range(SUBLANES):
            for j in range(TOP_K):
                row_copy(cur, j, base + u, idx[j, base + u]).start()
        return carry

    lax.fori_loop(0, td // SUBLANES, issue, 0)

    def drain(buf):
        def body(t, carry):
            for j in range(TOP_K):
                row_copy(buf, j, 0, 0).wait()
            return carry
        lax.fori_loop(0, td, body, 0)

    @pl.when(i > 0)
    def _():
        drain(1 - cur)

    @pl.when(i == n_tiles - 1)
    def _():
        drain(cur)


def _dispatch_call(last_block_row, dest3, h2t, n_rows, td):
    N, S, L = h2t.shape
    return pl.pallas_call(
        _dispatch_kernel,
        grid_spec=pltpu.PrefetchScalarGridSpec(
            num_scalar_prefetch=1,
            grid=(N // td,),
            in_specs=[pl.BlockSpec(memory_space=pl.ANY),
                      pl.BlockSpec((td, S, L), lambda i, last: (i, 0, 0))],
            out_specs=pl.BlockSpec(memory_space=pl.ANY),
            scratch_shapes=[pltpu.SMEM((TOP_K, td), I32),
                            pltpu.VMEM((2, TOP_K, td, 2 * S, L), F32),
                            pltpu.VMEM((MOE_ROWS, 2 * S, L), F32),
                            pltpu.SemaphoreType.DMA, pltpu.SemaphoreType.DMA((2,)),
                            pltpu.SemaphoreType.DMA]),
        out_shape=jax.ShapeDtypeStruct((n_rows, 2 * S, L), h2t.dtype),
        compiler_params=_cparams("arbitrary"),
        name="dispatch",
    )(last_block_row, dest3, h2t)


def _moe_kernel(be_ref, nu_ref, xs_ref, wgu_ref, wdn_ref, yt_hbm, wgu_bf, wdn_bf, ybuf, tag_v, tag_s,
                n_sent, sem_tag, sem_rows):
    i = pl.program_id(0)
    n_steps = pl.num_programs(0)
    cur = i % 2
    tile_rows = 2 * SUBLANES
    n_real = yt_hbm.shape[0] - 2 * MOE_ROWS

    def row_copy(buf, r, tag):
        rows = pl.ds(pl.multiple_of(r * SUBLANES, SUBLANES), SUBLANES)
        return pltpu.make_async_copy(ybuf.at[buf, rows, :], yt_hbm.at[tag], sem_rows.at[buf])

    def drain(buf):
        @pl.when(n_sent[buf] > 0)
        def _():
            pltpu.make_async_copy(ybuf.at[buf], ybuf.at[buf], sem_rows.at[buf]).wait()
            n_sent[buf] = 0

    @pl.when(i == 0)
    def _():
        n_sent[0] = 0
        n_sent[1] = 0

    @pl.when(i < nu_ref[0])
    def _():
        @pl.when((i == 0) | (be_ref[i] != be_ref[jnp.maximum(i - 1, 0)]))
        def _():
            wgu_bf[...] = wgu_ref[0].astype(BF16)
            wdn_bf[...] = wdn_ref[0].astype(BF16)

        tags = xs_ref[pl.ds(SUBLANES, MOE_ROWS, stride=tile_rows), :]
        spare = (n_real + cur * MOE_ROWS
                 + lax.broadcasted_iota(I32, (MOE_ROWS, LANES), 0)).astype(F32)
        tags = jnp.where(tags < 0.0, spare, tags)
        tag_v[...] = tags.T[0:1, :].astype(I32)
        cp = pltpu.make_async_copy(tag_v, tag_s, sem_tag)
        cp.start()
        xb = jnp.concatenate([xs_ref[pl.ds(s, MOE_ROWS, stride=tile_rows), :]
                              for s in range(SUBLANES)], axis=1).astype(BF16)
        gu = _dot(xb, wgu_bf[...])
        act = _silu(gu[:, :D_EXPERT]) * gu[:, D_EXPERT:]
        y = _dot(act.astype(BF16), wdn_bf[...])
        for s in range(SUBLANES):
            ybuf[cur, pl.ds(s, MOE_ROWS, stride=SUBLANES), :] = y[:, s * LANES:(s + 1) * LANES]
        cp.wait()

        def issue(tt, carry):
            for u in range(SUBLANES):
                r = tt * SUBLANES + u
                row_copy(cur, r, tag_s[0, r]).start()
            return carry

        lax.fori_loop(0, MOE_ROWS // SUBLANES, issue, 0)
        n_sent[cur] = MOE_ROWS

    @pl.when(i > 0)
    def _():
        drain(1 - cur)

    @pl.when(i == n_steps - 1)
    def _():
        drain(cur)


def _moe_call(block_expert, n_used, xs, w_gu, w_dn, n_out_rows):
    tile_rows = 2 * SUBLANES
    P = xs.shape[0] // tile_rows
    D = SUBLANES * LANES
    nb = P // MOE_ROWS
    blk = lambda i, be, nu: (jnp.minimum(i, nu[0] - 1), 0)
    return pl.pallas_call(
        _moe_kernel,
        grid_spec=pltpu.PrefetchScalarGridSpec(
            num_scalar_prefetch=2,
            grid=(nb,),
            in_specs=[pl.BlockSpec((MOE_ROWS * tile_rows, LANES), blk),
                      pl.BlockSpec((1, D, 2 * D_EXPERT), lambda i, be, nu: (be[i], 0, 0)),
                      pl.BlockSpec((1, D_EXPERT, D), lambda i, be, nu: (be[i], 0, 0))],
            out_specs=pl.BlockSpec(memory_space=pl.ANY),
            scratch_shapes=[pltpu.VMEM((D, 2 * D_EXPERT), BF16), pltpu.VMEM((D_EXPERT, D), BF16),
                            pltpu.VMEM((2, MOE_ROWS * SUBLANES, LANES), F32),
                            pltpu.VMEM((1, MOE_ROWS), I32), pltpu.SMEM((1, MOE_ROWS), I32),
                            pltpu.SMEM((2,), I32),
                            pltpu.SemaphoreType.DMA, pltpu.SemaphoreType.DMA((2,))]),
        out_shape=jax.ShapeDtypeStruct((n_out_rows, SUBLANES, LANES), F32),
        compiler_params=_cparams("arbitrary"),
        name="moe",
    )(block_expert, n_used, xs, w_gu, w_dn)


def _combine_kernel(w_hbm, yt_ref, h_ref, x1_ref, mod_ref, wsg_ref, wsd_ref, o_ref,
                    wts, routed, sem_w):
    i = pl.program_id(0)
    tc = h_ref.shape[0]
    cp_w = pltpu.make_async_copy(w_hbm.at[i], wts, sem_w)
    cp_w.start()
    gu = _mm(h_ref[...], wsg_ref[...])
    act = _silu(gu[:, :D_EXPERT]) * gu[:, D_EXPERT:]
    ffn = _mm(act, wsd_ref[...])
    cp_w.wait()

    def wsum(tt, carry):
        for u in range(SUBLANES):
            t = tt * SUBLANES + u
            first = pl.multiple_of(t * (TOP_K * SUBLANES), SUBLANES)
            acc = yt_ref[pl.ds(first, SUBLANES), :] * wts[0, t]
            for j in range(1, TOP_K):
                acc = acc + yt_ref[pl.ds(first + j * SUBLANES, SUBLANES), :] * wts[j, t]
            routed[pl.ds(pl.multiple_of(t * SUBLANES, SUBLANES), SUBLANES), :] = acc
        return carry

    lax.fori_loop(0, tc // SUBLANES, wsum, 0)
    routed2d = jnp.concatenate([routed[pl.ds(s, tc, stride=SUBLANES), :] for s in range(SUBLANES)],
                               axis=1)
    g_f = mod_ref[0, 5:6, :]
    o_ref[...] = x1_ref[...] + g_f * (ffn + routed2d)


def _combine_call(w3, yt, h2, x1, mod3, w_sh_gu, w_sh_dn, tokens_per_batch, tc):
    N, D = h2.shape
    tiles_per_batch = tokens_per_batch // tc
    tok = pl.BlockSpec((tc, D), lambda i: (i, 0))
    wsg = w_sh_gu.astype(BF16)
    wsd = w_sh_dn.astype(BF16)
    return pl.pallas_call(
        _combine_kernel,
        grid=(N // tc,),
        in_specs=[pl.BlockSpec(memory_space=pl.ANY),
                  pl.BlockSpec((tc * TOP_K * SUBLANES, LANES), lambda i: (i, 0)),
                  tok, tok,
                  pl.BlockSpec((1, 6, D), lambda i: (i // tiles_per_batch, 0, 0)),
                  pl.BlockSpec(wsg.shape, lambda i: (0, 0)),
                  pl.BlockSpec(wsd.shape, lambda i: (0, 0))],
        out_specs=tok,
        out_shape=jax.ShapeDtypeStruct((N, D), F32),
        scratch_shapes=[pltpu.SMEM((TOP_K, tc), F32), pltpu.VMEM((tc * SUBLANES, LANES), F32),
                        pltpu.SemaphoreType.DMA],
        compiler_params=_cparams("arbitrary"),
        name="combine",
    )(w3, yt, h2, x1, mod3, wsg, wsd)


def _tile(n, pref):
    t = min(n, pref)
    assert n % t == 0, (n, t)
    return t


def _layer(x, mod3, positions, norm_mix, w_in, rwkv_mu, decay_w0, decay_up, iclr_a0, iclr_up,
           gate_up, rwkv_k_k, rwkv_k_a, rwkv_r_k, ln_x_w, ln_x_b, q_a_norm, w_q_b, kv_a_norm,
           w_kv_b, q_norm, k_norm, w_out, norm_ffn, w_router, router_bias, w_e_gate_up, w_e_down,
           w_sh_gate_up, w_sh_down):
    B, T, D = x.shape
    N = B * T
    assert T % SCAN_CHUNK == 0
    (r, lw, k2, v, kk, akk, g, bonus, q_pad, k_pad, v_pad) = _pre_call(
        x, mod3, positions, norm_mix, w_in, rwkv_mu, decay_w0, decay_up, iclr_a0, iclr_up,
        gate_up, rwkv_k_k, rwkv_k_a, rwkv_r_k, q_a_norm, w_q_b, kv_a_norm, w_kv_b, q_norm, k_norm,
        tm=_tile(T, 512))
    y = _scan_call(r, lw, k2, v, kk, akk)
    o_pad = _attn_call(q_pad, k_pad, v_pad)
    x1, h2, h2t = _post_call(y, bonus, g, o_pad, x, mod3, ln_x_w, ln_x_b, w_out, norm_ffn,
                             tm=_tile(T, 512))
    x1 = x1.reshape(N, D)
    h2 = h2.reshape(N, D)
    h2t = h2t.reshape(N, D // LANES, LANES)

    tr = _tile(N, 512)
    top_e, wts, rank, counts = _route_call(h2, w_router, router_bias, tr)
    counts = counts.reshape(N_EXPERTS)
    padded = (counts + MOE_ROWS - 1) // MOE_ROWS * MOE_ROWS
    pad_ends = jnp.cumsum(padded)
    pad_starts = pad_ends - padded
    n_blocks = (N * TOP_K + N_EXPERTS * (MOE_ROWS - 1)) // MOE_ROWS
    block_expert = jnp.minimum(
        jnp.searchsorted(pad_ends, jnp.arange(n_blocks, dtype=I32) * MOE_ROWS, side='right'),
        N_EXPERTS - 1).astype(I32)
    n_used = (pad_ends[-1:] // MOE_ROWS).astype(I32)
    last_block_row = jnp.where(padded > 0, pad_ends - MOE_ROWS, -1).astype(I32)
    dest = _dest_call(top_e, rank, pad_starts.astype(I32), tr)

    td = _tile(T, DISPATCH_TILE)
    dest3 = dest.reshape(TOP_K, N // td, td).transpose(1, 0, 2)
    n_rows = n_blocks * MOE_ROWS
    xs = _dispatch_call(last_block_row, dest3, h2t, n_rows, td)
    yt = _moe_call(block_expert, n_used, xs.reshape(n_rows * 2 * SUBLANES, LANES), w_e_gate_up,
                   w_e_down, N * TOP_K + 2 * MOE_ROWS)
    tc = _tile(T, COMBINE_TILE)
    w3 = wts.reshape(TOP_K, N // tc, tc).transpose(1, 0, 2)
    out = _combine_call(w3, yt.reshape(-1, LANES), h2, x1, mod3, w_sh_gate_up,
                        w_sh_down, T, tc)
    return out.reshape(B, T, D)


def kernel(x, c, positions, ada_w, ada_b, norm_mix, w_in, rwkv_mu, decay_w0, decay_up, iclr_a0, iclr_up, gate_up, rwkv_k_k, rwkv_k_a, rwkv_r_k, ln_x_w, ln_x_b, q_a_norm, w_q_b, kv_a_norm, w_kv_b, q_norm, k_norm, w_out, norm_ffn, w_router, router_bias, w_e_gate_up, w_e_down, w_sh_gate_up, w_sh_down):
    B, T, D = x.shape
    depth = ada_w.shape[0]
    for l in range(depth):
        mod3 = _mod_call(c, ada_w[l], ada_b[l]).reshape(B, 6, D)
        x = _layer(x, mod3, positions, norm_mix[l], w_in[l], rwkv_mu[l], decay_w0[l], decay_up[l],
                   iclr_a0[l], iclr_up[l], gate_up[l], rwkv_k_k[l], rwkv_k_a[l], rwkv_r_k[l],
                   ln_x_w[l], ln_x_b[l], q_a_norm[l], w_q_b[l], kv_a_norm[l], w_kv_b[l],
                   q_norm[l], k_norm[l], w_out[l], norm_ffn[l], w_router[l], router_bias[l],
                   w_e_gate_up[l], w_e_down[l], w_sh_gate_up[l], w_sh_down[l])
    return x
```

```python
import functools
import math

import jax
import jax.numpy as jnp
import numpy as np
from jax import lax
from jax.experimental import pallas as pl
from jax.experimental.pallas import tpu as pltpu

F32 = jnp.float32
BF16 = jnp.bfloat16
I32 = jnp.int32

NORM_EPS = 1e-6
GN_EPS = 64e-5
RWKV_HEADS = 8
RWKV_HEAD_DIM = 64
D_RWKV = 512
DECAY_LORA = 32
ICLR_LORA = 32
GATE_LORA = 96
MLA_HEADS = 8
QK_NOPE_DIM = 64
QK_ROPE_DIM = 32
QK_HEAD_DIM = 96
V_HEAD_DIM = 64
Q_LORA_RANK = 256
KV_LORA_RANK = 128
ROPE_THETA = 10000.0
N_EXPERTS = 256
TOP_K = 8
N_GROUPS = 8
TOPK_GROUPS = 4
GROUP_SIZE = N_EXPERTS // N_GROUPS
D_EXPERT = 256
ROUTED_SCALE = 2.5
MOE_ROWS = 512
DISPATCH_TILE = 128
COMBINE_TILE = 256

LANES = 128
SUBLANES = 8
HEAD_PAD = 128
VMEM_LIMIT = 56 * 1024 * 1024

SCAN_CHUNK = 64
SCAN_BLOCK = 256
ATTN_TILE = 512
ATTN_Q_SCALE = QK_HEAD_DIM ** -0.5 * math.log2(math.e)
NEG_INF = float("-inf")


def _cparams(*sem):
    return pltpu.CompilerParams(dimension_semantics=sem, vmem_limit_bytes=VMEM_LIMIT)


def _split2(a):
    hi = a.astype(BF16)
    lo = (a - hi.astype(F32)).astype(BF16)
    return hi, lo


def _split3(a):
    hi = a.astype(BF16)
    r1 = a - hi.astype(F32)
    mid = r1.astype(BF16)
    lo = (r1 - mid.astype(F32)).astype(BF16)
    return hi, mid, lo


def _dot(a, b, dims=None):
    if dims is None:
        return jnp.dot(a, b, preferred_element_type=F32)
    return lax.dot_general(a, b, (dims, ((), ())), preferred_element_type=F32)


def _mm(a, b, dims=None):
    return _dot(a.astype(BF16), b.astype(BF16), dims)


def _mm3(a, b, dims=None):
    ah, al = _split2(a)
    bh, bl = _split2(b)
    return _dot(ah, bh, dims) + (_dot(ah, bl, dims) + _dot(al, bh, dims))


def _mm_exact_rhs(a, b_exact_bf16, dims=None):
    h, m, l = _split3(a)
    return _dot(h, b_exact_bf16, dims) + (_dot(m, b_exact_bf16, dims) + _dot(l, b_exact_bf16, dims))


NT = ((1,), (1,))
TN = ((0,), (0,))


def _sigmoid(z):
    return 1.0 / (1.0 + jnp.exp(-z))


def _silu(z):
    return z * _sigmoid(z)


def _seg_ones(width, seg):
    r = lax.broadcasted_iota(I32, (width, width), 0) // seg
    c = lax.broadcasted_iota(I32, (width, width), 1) // seg
    return jnp.where(r == c, 1.0, 0.0).astype(BF16)


def _segsum(a, ones_bd):
    hi, lo = _split2(a)
    return _dot(hi, ones_bd) + _dot(lo, ones_bd)


def _mod_kernel(c_ref, w_ref, b_ref, o_ref):
    ca = _silu(c_ref[...])
    o_ref[...] = _mm3(ca, w_ref[...]) + b_ref[...]


def _mod_call(c, ada_w, ada_b):
    B, D = c.shape
    n6 = ada_w.shape[1]
    tn = D
    return pl.pallas_call(
        _mod_kernel,
        grid=(n6 // tn,),
        in_specs=[pl.BlockSpec((B, D), lambda j: (0, 0)),
                  pl.BlockSpec((D, tn), lambda j: (0, j)),
                  pl.BlockSpec((1, tn), lambda j: (0, j))],
        out_specs=pl.BlockSpec((B, tn), lambda j: (0, j)),
        out_shape=jax.ShapeDtypeStruct((B, n6), F32),
        compiler_params=_cparams("arbitrary"),
        name="mod",
    )(c, ada_w, ada_b.reshape(1, n6))


def _pre_kernel(x_ref, mod_ref, pos_ref, nmix_ref, wrkv_ref, wlora_ref, wmla_ref,
                mu_rkv_ref, mu_lora_ref, wup_ref, w0_ref, a0_ref, kk_ref, ka_ref, rk_ref,
                qan_ref, wqb_ref, kvan_ref, wkb_ref, wvb_ref, qn_ref, kn_ref, invf_ref,
                r_ref, lw_ref, k_ref, v_ref, kkn_ref, akk_ref, g_ref, bonus_ref,
                q_ref, kout_ref, vout_ref,
                carry_rkv, carry_lora):
    ti = pl.program_id(1)
    tm = x_ref.shape[1]

    @pl.when(ti == 0)
    def _():
        carry_rkv[...] = jnp.zeros_like(carry_rkv)
        carry_lora[...] = jnp.zeros_like(carry_lora)

    xb = x_ref[0]
    sh_a = mod_ref[0, 0:1, :]
    sc_a = mod_ref[0, 1:2, :]
    ms = jnp.mean(xb * xb, axis=-1, keepdims=True)
    h = xb * lax.rsqrt(ms + NORM_EPS) * nmix_ref[...] * (1.0 + sc_a) + sh_a
    hb = h.astype(BF16)
    u_rkv = _dot(hb, wrkv_ref[...])
    u_lora = _dot(hb, wlora_ref[...])
    u_mla = _dot(hb, wmla_ref[...])

    row0 = lax.broadcasted_iota(I32, (tm, 1), 0) == 0

    def shift(u, carry):
        prev = jnp.where(row0, carry[...], pltpu.roll(u, 1, 0))
        carry[...] = u[tm - 1:tm, :]
        return prev

    prev_rkv = shift(u_rkv, carry_rkv)
    prev_lora = shift(u_lora, carry_lora)
    us = u_rkv + (prev_rkv - u_rkv) * mu_rkv_ref[...]
    ul = u_lora + (prev_lora - u_lora) * mu_lora_ref[...]
    r = us[:, 0:D_RWKV]
    k = us[:, D_RWKV:2 * D_RWKV]
    v = us[:, 2 * D_RWKV:3 * D_RWKV]

    lane_l = lax.broadcasted_iota(I32, ul.shape, 1)
    t_in = jnp.where(lane_l < DECAY_LORA, jnp.tanh(ul),
                     jnp.where(lane_l < DECAY_LORA + ICLR_LORA, ul, _sigmoid(ul)))
    up = _mm(t_in, wup_ref[...])
    z = w0_ref[...] + up[:, 0:D_RWKV]
    lw = (-math.exp(-0.5)) * _sigmoid(z)
    a = _sigmoid(a0_ref[...] + up[:, D_RWKV:2 * D_RWKV])
    g = up[:, 2 * D_RWKV:3 * D_RWKV]

    ones64 = _seg_ones(D_RWKV, RWKV_HEAD_DIM)
    kk = k * kk_ref[...]
    ss = _segsum(kk * kk, ones64)
    kk = kk * lax.rsqrt(jnp.maximum(ss, 1e-24))
    k2 = k * (1.0 + (a - 1.0) * ka_ref[...])
    bonus = _segsum(r * k2 * rk_ref[...], ones64) * v

    r_ref[0] = r
    lw_ref[0] = lw
    k_ref[0] = k2
    v_ref[0] = v
    kkn_ref[0] = kk
    akk_ref[0] = a * kk
    g_ref[0] = g
    bonus_ref[0] = bonus

    q_lat = u_mla[:, 0:Q_LORA_RANK]
    kv_lat = u_mla[:, Q_LORA_RANK:Q_LORA_RANK + KV_LORA_RANK]
    kpe_tile = u_mla[:, Q_LORA_RANK + KV_LORA_RANK:]
    qn = q_lat * lax.rsqrt(jnp.mean(q_lat * q_lat, axis=-1, keepdims=True) + NORM_EPS) * qan_ref[...]
    kvn = kv_lat * lax.rsqrt(jnp.mean(kv_lat * kv_lat, axis=-1, keepdims=True) + NORM_EPS) * kvan_ref[...]
    q_raw = _mm(qn, wqb_ref[...])
    kvb = kvn.astype(BF16)
    k_raw = _dot(kvb, wkb_ref[...])
    v_pad = _dot(kvb, wvb_ref[...])
    kpe_h = pltpu.roll(kpe_tile, QK_NOPE_DIM, 1)

    half = QK_ROPE_DIM // 2
    ang_t = invf_ref[...] * pos_ref[0].astype(F32)
    frow = lax.broadcasted_iota(I32, (half, HEAD_PAD), 0)
    flane = lax.broadcasted_iota(I32, (half, HEAD_PAD), 1)
    at_x1 = flane == frow + QK_NOPE_DIM
    at_x2 = flane == frow + QK_NOPE_DIM + half
    e_cos = jnp.where(at_x1 | at_x2, 1.0, 0.0).astype(BF16)
    e_sin = jnp.concatenate([jnp.where(at_x1, -1.0, 0.0), jnp.where(at_x2, 1.0, 0.0)],
                            axis=1).astype(BF16)
    lane = lax.broadcasted_iota(I32, (1, HEAD_PAD), 1)
    off_rope = jnp.where((lane >= QK_NOPE_DIM) & (lane < QK_HEAD_DIM), 0.0, 1.0)
    cos_t = _mm_exact_rhs(jnp.cos(ang_t), e_cos, TN) + off_rope
    sin2 = _mm_exact_rhs(jnp.sin(ang_t), e_sin, TN)
    s1 = sin2[:, :HEAD_PAD]
    s2 = sin2[:, HEAD_PAD:]

    def tables(gain, scale):
        g = gain * scale
        return (cos_t * g, s1 * pltpu.roll(g, HEAD_PAD - half, 1), s2 * pltpu.roll(g, half, 1))

    def norm_rope(xh, tabs):
        c_g, s1_g, s2_g = tabs
        ssq = jnp.sum(xh * xh, axis=-1, keepdims=True) * (1.0 / QK_HEAD_DIM)
        rot = xh * c_g + pltpu.roll(xh, HEAD_PAD - half, 1) * s1_g + pltpu.roll(xh, half, 1) * s2_g
        return rot * lax.rsqrt(ssq + NORM_EPS)

    q_tabs = tables(qn_ref[...], ATTN_Q_SCALE)
    k_tabs = tables(kn_ref[...], 1.0)
    for hh in range(MLA_HEADS):
        sl = slice(hh * HEAD_PAD, (hh + 1) * HEAD_PAD)
        q_ref[0, :, sl] = norm_rope(q_raw[:, sl], q_tabs).astype(BF16)
        kout_ref[0, :, sl] = norm_rope(k_raw[:, sl] + kpe_h, k_tabs).astype(BF16)
    vout_ref[0] = v_pad.astype(BF16)


def _pad_heads(w, n_heads, width):
    kdim = w.shape[0]
    w = w.reshape(kdim, n_heads, width)
    w = jnp.pad(w, ((0, 0), (0, 0), (0, HEAD_PAD - width)))
    return w.reshape(kdim, n_heads * HEAD_PAD)


def _pre_call(x, mod3, positions, norm_mix, w_in, rwkv_mu, decay_w0, decay_up, iclr_a0, iclr_up,
              gate_up, k_k, k_a, r_k, q_a_norm, w_q_b, kv_a_norm, w_kv_b, q_norm, k_norm, tm):
    B, T, D = x.shape
    n_rkv = 3 * D_RWKV
    n_lora = DECAY_LORA + ICLR_LORA + GATE_LORA
    LORA_PAD = 256
    MLA_PAD = 512
    n_mla = Q_LORA_RANK + KV_LORA_RANK + QK_ROPE_DIM
    w_rkv = w_in[:, :n_rkv].astype(BF16)
    w_lora = jnp.pad(w_in[:, n_rkv:n_rkv + n_lora], ((0, 0), (0, LORA_PAD - n_lora))).astype(BF16)
    w_mla = jnp.pad(w_in[:, n_rkv + n_lora:], ((0, 0), (0, MLA_PAD - n_mla))).astype(BF16)
    mu_rkv = rwkv_mu[:n_rkv].reshape(1, n_rkv)
    mu_lora = jnp.pad(rwkv_mu[n_rkv:], (0, LORA_PAD - n_lora)).reshape(1, LORA_PAD)
    w_up = jnp.zeros((LORA_PAD, n_rkv), F32)
    w_up = w_up.at[0:DECAY_LORA, 0:D_RWKV].set(decay_up)
    w_up = w_up.at[DECAY_LORA:DECAY_LORA + ICLR_LORA, D_RWKV:2 * D_RWKV].set(iclr_up)
    w_up = w_up.at[DECAY_LORA + ICLR_LORA:n_lora, 2 * D_RWKV:].set(gate_up)
    w_up = w_up.astype(BF16)
    w_qb = _pad_heads(w_q_b, MLA_HEADS, QK_HEAD_DIM).astype(BF16)
    w_kv3 = w_kv_b.reshape(KV_LORA_RANK, MLA_HEADS, QK_NOPE_DIM + V_HEAD_DIM)
    w_kb = _pad_heads(w_kv3[:, :, :QK_NOPE_DIM].reshape(KV_LORA_RANK, -1), MLA_HEADS, QK_NOPE_DIM).astype(BF16)
    w_vb = _pad_heads(w_kv3[:, :, QK_NOPE_DIM:].reshape(KV_LORA_RANK, -1), MLA_HEADS, V_HEAD_DIM).astype(BF16)
    qn_pad = jnp.pad(q_norm, (0, HEAD_PAD - QK_HEAD_DIM)).reshape(1, HEAD_PAD)
    kn_pad = jnp.pad(k_norm, (0, HEAD_PAD - QK_HEAD_DIM)).reshape(1, HEAD_PAD)
    inv_freq = ROPE_THETA ** (-jnp.arange(0, QK_ROPE_DIM, 2, dtype=F32) / QK_ROPE_DIM)
    invf = inv_freq.reshape(QK_ROPE_DIM // 2, 1)
    pos3 = positions.reshape(B, 1, T)
    HP = MLA_HEADS * HEAD_PAD

    row = lambda n: pl.BlockSpec((1, n), lambda b, t: (0, 0))
    full = lambda a: pl.BlockSpec(a.shape, lambda b, t: (0,) * a.ndim)
    tok = lambda n: pl.BlockSpec((1, tm, n), lambda b, t: (b, t, 0))
    outs = ([jax.ShapeDtypeStruct((B, T, D_RWKV), F32)] * 8
            + [jax.ShapeDtypeStruct((B, T, HP), BF16)] * 3)
    return pl.pallas_call(
        _pre_kernel,
        grid=(B, T // tm),
        in_specs=[tok(D),
                  pl.BlockSpec((1, 6, D), lambda b, t: (b, 0, 0)),
                  pl.BlockSpec((1, 1, tm), lambda b, t: (b, 0, t)),
                  row(D), full(w_rkv), full(w_lora), full(w_mla),
                  row(n_rkv), row(LORA_PAD), full(w_up), row(D_RWKV), row(D_RWKV),
                  row(D_RWKV), row(D_RWKV), row(D_RWKV),
                  row(Q_LORA_RANK), full(w_qb), row(KV_LORA_RANK), full(w_kb), full(w_vb),
                  row(HEAD_PAD), row(HEAD_PAD), full(invf)],
        out_specs=[tok(D_RWKV)] * 8 + [tok(HP)] * 3,
        out_shape=outs,
        scratch_shapes=[pltpu.VMEM((1, n_rkv), F32), pltpu.VMEM((1, LORA_PAD), F32)],
        compiler_params=_cparams("arbitrary", "arbitrary"),
        name="pre",
    )(x, mod3, pos3, norm_mix.reshape(1, D), w_rkv, w_lora, w_mla, mu_rkv, mu_lora, w_up,
      decay_w0.reshape(1, -1), iclr_a0.reshape(1, -1), k_k.reshape(1, -1), k_a.reshape(1, -1),
      r_k.reshape(1, -1), q_a_norm.reshape(1, -1), w_qb, kv_a_norm.reshape(1, -1), w_kb, w_vb,
      qn_pad, kn_pad, invf)


def _scan_kernel(r_ref, lw_ref, k_ref, v_ref, kk_ref, akk_ref, y_ref, state):
    C = SCAN_CHUNK
    n_chunks = r_ref.shape[1] // C
    n_pairs = r_ref.shape[2] // LANES

    @pl.when(pl.program_id(1) == 0)
    def _():
        state[...] = jnp.zeros_like(state)

    ri = lax.broadcasted_iota(I32, (C, C), 0)
    ci = lax.broadcasted_iota(I32, (C, C), 1)
    tri_incl = jnp.where(ci <= ri, 1.0, 0.0).astype(BF16)
    r2 = lax.broadcasted_iota(I32, (2 * C, 2 * C), 0)
    c2 = lax.broadcasted_iota(I32, (2 * C, 2 * C), 1)
    same = (r2 >= C) == (c2 >= C)
    strict = same & (c2 < r2)
    incl = same & (c2 <= r2)
    eye = jnp.where(c2 == r2, 1.0, 0.0)
    head0 = lax.broadcasted_iota(I32, (C, LANES), 1) < RWKV_HEAD_DIM

    def stack2(a):
        return jnp.concatenate([jnp.where(head0, a, 0.0), jnp.where(head0, 0.0, a)], axis=0)

    C2 = 2 * C
    cat0 = lambda *a: jnp.concatenate(a, axis=0)
    cat1 = lambda *a: jnp.concatenate(a, axis=1)

    items = []
    for c in range(n_chunks):
        rows = slice(c * C, (c + 1) * C)
        lw = lw_ref[0, rows, :]
        cum = _mm_exact_rhs_left(tri_incl, lw)
        cum_end = cum[C - 1:C, :]
        w_end = jnp.exp(cum_end)
        e_pos = jnp.exp(cum)
        e_neg = jnp.exp(-cum)
        e_prev = jnp.exp(cum - lw)
        e_end = jnp.exp(cum_end - cum)
        kk = kk_ref[0, rows, :]
        k2 = k_ref[0, rows, :]
        pneg = -akk_ref[0, rows, :]
        vv = v_ref[0, rows, :]
        rt = r_ref[0, rows, :] * e_pos
        bt = kk * e_prev
        pt = pneg * e_neg
        kt = k2 * e_neg
        ph = pneg * e_end
        kh = k2 * e_end
        for pp in range(n_pairs):
            sl = slice(pp * LANES, (pp + 1) * LANES)
            items.append(dict(
                c=c, p=pp, w_end=w_end[:, sl],
                bt2=stack2(bt[:, sl]).astype(BF16), rt2=stack2(rt[:, sl]).astype(BF16),
                pk2=cat0(stack2(pt[:, sl]), stack2(kt[:, sl])).astype(BF16),
                phkh2=cat0(stack2(ph[:, sl]), stack2(kh[:, sl])).astype(BF16),
                v2=stack2(vv[:, sl])))
    for it in items:
        ab = _dot(cat0(it['bt2'], it['rt2']), it['pk2'], NT)
        it['a_ab'] = jnp.where(strict, ab[:C2, :C2], 0.0)
        it['a_ak'] = jnp.where(strict, ab[:C2, C2:], 0.0).astype(BF16)
        it['b_rpk'] = cat1(jnp.where(incl, ab[C2:, :C2], 0.0), jnp.where(incl, ab[C2:, C2:], 0.0)).astype(BF16)
        it['tinv'] = eye + it['a_ab']
    for it in items:
        it['apow'] = _mm(it['a_ab'], it['a_ab'])
    for _ in range(int(math.log2(C)) - 1):
        for it in items:
            both = _mm(cat0(it['apow'], it['tinv']), it['apow'])
            it['apow'] = both[:C2]
            it['tinv'] = it['tinv'] + both[C2:]
    for it in items:
        it['akv'] = _dot(it['a_ak'], it['v2'].astype(BF16))
    for it in items:
        tt = _dot(it['tinv'].astype(BF16), cat1(it['bt2'], it['akv'].astype(BF16)))
        it['tb_rt'] = cat0(tt[:, :LANES].astype(BF16), it['rt2'])
        it['tav'] = tt[:, LANES:]
    for it in items:
        pp = it['p']
        s0 = state[pp]
        top = _dot(it['tb_rt'], s0.astype(BF16), NT)
        u2 = top[:C2] + it['tav']
        uv = cat0(u2, it['v2']).astype(BF16)
        y2 = top[C2:] + _dot(it['b_rpk'], uv)
        state[pp] = s0 * it['w_end'] + _dot(uv, it['phkh2'], TN)
        y_ref[0, it['c'] * C:(it['c'] + 1) * C, pp * LANES:(pp + 1) * LANES] = y2[0:C] + y2[C:C2]


def _mm_exact_rhs_left(b_exact_bf16, a):
    h, m, l = _split3(a)
    return _dot(b_exact_bf16, h) + (_dot(b_exact_bf16, m) + _dot(b_exact_bf16, l))


def _scan_call(r, lw, k2, v, kk, akk):
    B, T, W = r.shape
    tb = _tile(T, SCAN_BLOCK)
    spec = pl.BlockSpec((1, tb, W), lambda b, c: (b, c, 0))
    return pl.pallas_call(
        _scan_kernel,
        grid=(B, T // tb),
        in_specs=[spec] * 6,
        out_specs=spec,
        out_shape=jax.ShapeDtypeStruct((B, T, W), F32),
        scratch_shapes=[pltpu.VMEM((W // LANES, 2 * RWKV_HEAD_DIM, LANES), F32)],
        compiler_params=_cparams("arbitrary", "arbitrary"),
        name="scan",
    )(r, lw, k2, v, kk, akk)


def _attn_kernel(q_ref, k_ref, v_ref, o_ref):
    T = q_ref.shape[1]
    tq = min(T, ATTN_TILE)
    row = lax.broadcasted_iota(I32, (tq, tq), 0)
    col = lax.broadcasted_iota(I32, (tq, tq), 1)
    causal = col <= row

    def update(q, kt, vt, carry, mask):
        m_old, l_old, acc = carry
        s = _dot(q, kt, NT)
        if mask:
            s = jnp.where(causal, s, NEG_INF)
        m_new = jnp.maximum(m_old, jnp.max(s, axis=-1, keepdims=True))
        alpha = jnp.exp2(m_old - m_new)
        p = jnp.exp2(s - m_new)
        l_new = alpha * l_old + jnp.sum(p, axis=-1, keepdims=True)
        acc = alpha * acc + _dot(p.astype(BF16), vt)
        return m_new, l_new, acc

    for qi in range(T // tq):
        q = q_ref[0, qi * tq:(qi + 1) * tq, :]
        carry = (jnp.full((tq, 1), NEG_INF, F32), jnp.zeros((tq, 1), F32),
                 jnp.zeros((tq, HEAD_PAD), F32))

        def body(ki, carry, q=q):
            rows = pl.ds(pl.multiple_of(ki * tq, tq), tq)
            return update(q, k_ref[0, rows, :], v_ref[0, rows, :], carry, False)

        carry = lax.fori_loop(0, qi, body, carry, unroll=True)
        diag = slice(qi * tq, (qi + 1) * tq)
        _, l_fin, acc = update(q, k_ref[0, diag, :], v_ref[0, diag, :], carry, True)
        o_ref[0, diag, :] = (acc / l_fin).astype(o_ref.dtype)


def _attn_call(q, k, v):
    B, T, HP = q.shape
    spec = pl.BlockSpec((1, T, HEAD_PAD), lambda b, h: (b, 0, h))
    return pl.pallas_call(
        _attn_kernel,
        grid=(B, MLA_HEADS),
        in_specs=[spec, spec, spec],
        out_specs=spec,
        out_shape=jax.ShapeDtypeStruct((B, T, HP), BF16),
        compiler_params=_cparams("arbitrary", "arbitrary"),
        name="attn",
    )(q, k, v)


def _post_kernel(y_ref, bonus_ref, g_ref, o_ref, x_ref, mod_ref, lnw_ref, lnb_ref,
                 wo_r_ref, wo_m_ref, nffn_ref, x1_ref, h2_ref, h2t_ref):
    y = y_ref[0]
    ones64 = _seg_ones(D_RWKV, RWKV_HEAD_DIM)
    mean = _segsum(y, ones64) * (1.0 / RWKV_HEAD_DIM)
    yc = y - mean
    var = _segsum(yc * yc, ones64) * (1.0 / RWKV_HEAD_DIM)
    yn = yc * lax.rsqrt(var + GN_EPS) * lnw_ref[...] + lnb_ref[...]
    yr = (yn + bonus_ref[0]) * g_ref[0]
    mix = _mm(yr, wo_r_ref[...]) + _dot(o_ref[0], wo_m_ref[...])
    g_a = mod_ref[0, 2:3, :]
    sh_f = mod_ref[0, 3:4, :]
    sc_f = mod_ref[0, 4:5, :]
    x1 = x_ref[0] + g_a * mix
    x1_ref[0] = x1
    ms = jnp.mean(x1 * x1, axis=-1, keepdims=True)
    h2 = x1 * lax.rsqrt(ms + NORM_EPS) * nffn_ref[...] * (1.0 + sc_f) + sh_f
    h2_ref[0] = h2
    tm = h2.shape[0]
    for s in range(SUBLANES):
        h2t_ref[0, pl.ds(s, tm, stride=SUBLANES), :] = h2[:, s * LANES:(s + 1) * LANES]


def _post_call(y, bonus, g, o_pad, x, mod3, ln_w, ln_b, w_out, norm_ffn, tm):
    B, T, D = x.shape
    HP = MLA_HEADS * HEAD_PAD
    wo_r = w_out[:D_RWKV].astype(BF16)
    wo_m = jnp.pad(w_out[D_RWKV:].reshape(MLA_HEADS, V_HEAD_DIM, D),
                   ((0, 0), (0, HEAD_PAD - V_HEAD_DIM), (0, 0))).reshape(HP, D).astype(BF16)
    tok = lambda n: pl.BlockSpec((1, tm, n), lambda b, t: (b, t, 0))
    row = lambda n: pl.BlockSpec((1, n), lambda b, t: (0, 0))
    full = lambda a: pl.BlockSpec(a.shape, lambda b, t: (0,) * a.ndim)
    return pl.pallas_call(
        _post_kernel,
        grid=(B, T // tm),
        in_specs=[tok(D_RWKV), tok(D_RWKV), tok(D_RWKV), tok(HP), tok(D),
                  pl.BlockSpec((1, 6, D), lambda b, t: (b, 0, 0)),
                  row(D_RWKV), row(D_RWKV), full(wo_r), full(wo_m), row(D)],
        out_specs=[tok(D), tok(D),
                   pl.BlockSpec((1, tm * SUBLANES, LANES), lambda b, t: (b, t, 0))],
        out_shape=[jax.ShapeDtypeStruct((B, T, D), F32)] * 2
        + [jax.ShapeDtypeStruct((B, T * SUBLANES, LANES), F32)],
        compiler_params=_cparams("arbitrary", "arbitrary"),
        name="post",
    )(y, bonus, g, o_pad, x, mod3, ln_w.reshape(1, -1), ln_b.reshape(1, -1), wo_r, wo_m,
      norm_ffn.reshape(1, D))


def _first_index(mask, iota, size, axis):
    return jnp.min(jnp.where(mask, iota, size), axis=axis, keepdims=True)


def _route_kernel(h_ref, wr_ref, bias_ref, e_ref, w_ref, rank_ref, cnt_ref, base):
    tr = h_ref.shape[0]
    E = N_EXPERTS

    @pl.when(pl.program_id(0) == 0)
    def _():
        base[...] = jnp.zeros_like(base)

    logits = _mm3(wr_ref[...], h_ref[...], NT)
    scores = _sigmoid(logits)
    sel = scores + bias_ref[...]
    iota_g = lax.broadcasted_iota(I32, (GROUP_SIZE, tr), 0)
    gs_rows = []
    for gi in range(N_GROUPS):
        blk = sel[gi * GROUP_SIZE:(gi + 1) * GROUP_SIZE, :]
        m1 = jnp.max(blk, axis=0, keepdims=True)
        i1 = _first_index(blk == m1, iota_g, GROUP_SIZE, 0)
        m2 = jnp.max(jnp.where(iota_g == i1, NEG_INF, blk), axis=0, keepdims=True)
        gs_rows.append(m1 + m2)
    gs = jnp.concatenate(gs_rows, axis=0)
    iota8 = lax.broadcasted_iota(I32, (N_GROUPS, tr), 0)
    gmask = jnp.zeros((N_GROUPS, tr), jnp.bool_)
    for _ in range(TOPK_GROUPS):
        mg = jnp.max(gs, axis=0, keepdims=True)
        ig = _first_index(gs == mg, iota8, N_GROUPS, 0)
        hit = iota8 == ig
        gmask = gmask | hit
        gs = jnp.where(hit, NEG_INF, gs)
    msel = jnp.concatenate(
        [jnp.where(gmask[gi:gi + 1, :], sel[gi * GROUP_SIZE:(gi + 1) * GROUP_SIZE, :], NEG_INF)
         for gi in range(N_GROUPS)], axis=0)
    iota_e = lax.broadcasted_iota(I32, (E, tr), 0)
    e_rows, w_rows = [], []
    onehot = jnp.zeros((E, tr), F32)
    for _ in range(TOP_K):
        mv = jnp.max(msel, axis=0, keepdims=True)
        ie = _first_index(msel == mv, iota_e, E, 0)
        hit = iota_e == ie
        e_rows.append(ie)
        w_rows.append(jnp.sum(jnp.where(hit, scores, 0.0), axis=0, keepdims=True))
        onehot = jnp.where(hit, 1.0, onehot)
        msel = jnp.where(hit, NEG_INF, msel)
    top_e = jnp.concatenate(e_rows, axis=0)
    wts = jnp.concatenate(w_rows, axis=0)
    wts = wts / jnp.sum(wts, axis=0, keepdims=True) * ROUTED_SCALE
    ti = lax.broadcasted_iota(I32, (tr, tr), 0)
    tj = lax.broadcasted_iota(I32, (tr, tr), 1)
    upper = jnp.where(ti < tj, 1.0, 0.0).astype(BF16)
    pos = _dot(onehot.astype(BF16), upper) + base[...]
    rank_rows = [jnp.sum(jnp.where(iota_e == e_rows[j], pos, 0.0), axis=0, keepdims=True)
                 for j in range(TOP_K)]
    base[...] = base[...] + jnp.sum(onehot, axis=1, keepdims=True)
    e_ref[...] = top_e
    w_ref[...] = wts
    rank_ref[...] = jnp.concatenate(rank_rows, axis=0).astype(I32)
    cnt_ref[...] = base[...].astype(I32)


def _route_call(h2, w_router, router_bias, tr):
    N, D = h2.shape
    E = N_EXPERTS
    out_kn = pl.BlockSpec((TOP_K, tr), lambda i: (0, i))
    return pl.pallas_call(
        _route_kernel,
        grid=(N // tr,),
        in_specs=[pl.BlockSpec((tr, D), lambda i: (i, 0)),
                  pl.BlockSpec((E, D), lambda i: (0, 0)),
                  pl.BlockSpec((E, 1), lambda i: (0, 0))],
        out_specs=[out_kn, out_kn, out_kn, pl.BlockSpec((E, 1), lambda i: (0, 0))],
        out_shape=[jax.ShapeDtypeStruct((TOP_K, N), I32), jax.ShapeDtypeStruct((TOP_K, N), F32),
                   jax.ShapeDtypeStruct((TOP_K, N), I32), jax.ShapeDtypeStruct((E, 1), I32)],
        scratch_shapes=[pltpu.VMEM((E, 1), F32)],
        compiler_params=_cparams("arbitrary"),
        name="route",
    )(h2, w_router.T, router_bias.reshape(E, 1))


def _dest_kernel(e_ref, rank_ref, start_ref, d_ref):
    tr = e_ref.shape[1]
    iota_e = lax.broadcasted_iota(I32, (N_EXPERTS, tr), 0)
    starts = start_ref[...]
    rows = [jnp.sum(jnp.where(iota_e == e_ref[j:j + 1, :], starts, 0), axis=0, keepdims=True)
            for j in range(TOP_K)]
    d_ref[...] = jnp.concatenate(rows, axis=0) + rank_ref[...]


def _dest_call(top_e, rank, pad_starts, tr):
    K, N = top_e.shape
    spec = pl.BlockSpec((K, tr), lambda i: (0, i))
    return pl.pallas_call(
        _dest_kernel,
        grid=(N // tr,),
        in_specs=[spec, spec, pl.BlockSpec((N_EXPERTS, 1), lambda i: (0, 0))],
        out_specs=spec,
        out_shape=jax.ShapeDtypeStruct((K, N), I32),
        compiler_params=_cparams("arbitrary"),
        name="dest",
    )(top_e, rank, pad_starts.reshape(N_EXPERTS, 1))


def _dispatch_kernel(last_ref, dest_hbm, h_ref, xs_out, idx, src, zbuf, sem_idx, sem_rows, sem_zero):
    i = pl.program_id(0)
    n_tiles = pl.num_programs(0)
    td = h_ref.shape[0]
    cur = i % 2
    cp = pltpu.make_async_copy(dest_hbm.at[i], idx, sem_idx)
    cp.start()
    sub = lax.broadcasted_iota(I32, (1, SUBLANES, LANES), 1)
    lane = lax.broadcasted_iota(I32, (1, SUBLANES, LANES), 2)
    tag_here = (sub == 0) & (lane == 0)

    @pl.when(i == 0)
    def _():
        zbuf[:, 0:SUBLANES, :] = jnp.zeros((MOE_ROWS, SUBLANES, LANES), F32)
        zbuf[:, SUBLANES:, :] = jnp.broadcast_to(jnp.where(tag_here, -1.0, 0.0),
                                                 (MOE_ROWS, SUBLANES, LANES))

        def zero_copy(e):
            return pltpu.make_async_copy(zbuf, xs_out.at[pl.ds(last_ref[e], MOE_ROWS)], sem_zero)

        def z_issue(e, carry):
            @pl.when(last_ref[e] >= 0)
            def _():
                zero_copy(e).start()
            return carry

        def z_drain(e, carry):
            @pl.when(last_ref[e] >= 0)
            def _():
                zero_copy(e).wait()
            return carry

        lax.fori_loop(0, N_EXPERTS, z_issue, 0)
        lax.fori_loop(0, N_EXPERTS, z_drain, 0)

    tok = (i * td + lax.broadcasted_iota(I32, (td, SUBLANES, LANES), 0)) * TOP_K
    payload = h_ref[...]
    for j in range(TOP_K):
        src[cur, j, :, 0:SUBLANES, :] = payload
        src[cur, j, :, SUBLANES:, :] = jnp.where(tag_here, (tok + j).astype(F32), 0.0)
    cp.wait()

    def row_copy(buf, slot, t, dst_row):
        return pltpu.make_async_copy(src.at[buf, slot, t], xs_out.at[dst_row], sem_rows.at[buf])

    def issue(tt, carry):
        base = pl.multiple_of(tt * SUBLANES, SUBLANES)
        for u in range(SUBLANES):
            for j in range(TOP_K):
                row_copy(cur, j, base + u, idx[j, base + u]).start()
        return carry

    lax.fori_loop(0, td // SUBLANES, issue, 0)

    def drain(buf):
        def body(t, carry):
            for j in range(TOP_K):
                row_copy(buf, j, 0, 0).wait()
            return carry
        lax.fori_loop(0, td, body, 0)

    @pl.when(i > 0)
    def _():
        drain(1 - cur)

    @pl.when(i == n_tiles - 1)
    def _():
        drain(cur)


def _dispatch_call(last_block_row, dest3, h2t, n_rows, td):
    N, S, L = h2t.shape
    return pl.pallas_call(
        _dispatch_kernel,
        grid_spec=pltpu.PrefetchScalarGridSpec(
            num_scalar_prefetch=1,
            grid=(N // td,),
            in_specs=[pl.BlockSpec(memory_space=pl.ANY),
                      pl.BlockSpec((td, S, L), lambda i, last: (i, 0, 0))],
            out_specs=pl.BlockSpec(memory_space=pl.ANY),
            scratch_shapes=[pltpu.SMEM((TOP_K, td), I32),
                            pltpu.VMEM((2, TOP_K, td, 2 * S, L), F32),
                            pltpu.VMEM((MOE_ROWS, 2 * S, L), F32),
                            pltpu.SemaphoreType.DMA, pltpu.SemaphoreType.DMA((2,)),
                            pltpu.SemaphoreType.DMA]),
        out_shape=jax.ShapeDtypeStruct((n_rows, 2 * S, L), h2t.dtype),
        compiler_params=_cparams("arbitrary"),
        name="dispatch",
    )(last_block_row, dest3, h2t)


def _moe_kernel(be_ref, nu_ref, xs_ref, wgu_ref, wdn_ref, yt_hbm, wgu_bf, wdn_bf, ybuf, tag_v, tag_s,
                sem_tag, sem_rows):
    i = pl.program_id(0)
    n_steps = pl.num_programs(0)
    cur = i % 2
    tile_rows = 2 * SUBLANES
    n_real = yt_hbm.shape[0] - 2 * MOE_ROWS

    def row_copy(buf, r, tag):
        rows = pl.ds(pl.multiple_of(r * SUBLANES, SUBLANES), SUBLANES)
        return pltpu.make_async_copy(ybuf.at[buf, rows, :], yt_hbm.at[tag], sem_rows.at[buf])

    n_used = nu_ref[0]
    prv = 1 - cur

    def wait_sent(buf):
        pltpu.make_async_copy(ybuf.at[buf], ybuf.at[buf], sem_rows.at[buf]).wait()

    def send_prev():
        for r in range(MOE_ROWS):
            row_copy(prv, r, tag_s[prv, r]).start()

    @pl.when((i >= 2) & (i - 2 < n_used))
    def _():
        wait_sent(cur)

    i_blk = jnp.minimum(i, n_steps - 2)
    @pl.when((i < n_used) & ((i == 0) | (be_ref[i_blk] != be_ref[jnp.maximum(i_blk - 1, 0)])))
    def _():
        wgu_bf[...] = wgu_ref[0].astype(BF16)
        wdn_bf[...] = wdn_ref[0].astype(BF16)

    def compute():
        tags = xs_ref[pl.ds(SUBLANES, MOE_ROWS, stride=tile_rows), :]
        spare = (n_real + cur * MOE_ROWS
                 + lax.broadcasted_iota(I32, (MOE_ROWS, LANES), 0)).astype(F32)
        tags = jnp.where(tags < 0.0, spare, tags)
        tag_v[...] = tags.T[0:1, :].astype(I32)
        cp = pltpu.make_async_copy(tag_v, tag_s.at[pl.ds(cur, 1)], sem_tag)
        cp.start()
        xb = jnp.concatenate([xs_ref[pl.ds(s, MOE_ROWS, stride=tile_rows), :]
                              for s in range(SUBLANES)], axis=1).astype(BF16)
        gu = _dot(xb, wgu_bf[...])
        act = _silu(gu[:, :D_EXPERT]) * gu[:, D_EXPERT:]
        y = _dot(act.astype(BF16), wdn_bf[...])
        for s in range(SUBLANES):
            ybuf[cur, pl.ds(s, MOE_ROWS, stride=SUBLANES), :] = y[:, s * LANES:(s + 1) * LANES]
        cp.wait()

    @pl.when((i == 0) & (i < n_used))
    def _():
        compute()

    @pl.when((i > 0) & (i < n_used))
    def _():
        send_prev()
        compute()

    @pl.when((i > 0) & (i == n_used))
    def _():
        send_prev()

    @pl.when((i == n_steps - 1) & (i - 1 < n_used))
    def _():
        wait_sent(prv)


def _moe_call(block_expert, n_used, xs, w_gu, w_dn, n_out_rows):
    tile_rows = 2 * SUBLANES
    P = xs.shape[0] // tile_rows
    D = SUBLANES * LANES
    nb = P // MOE_ROWS
    blk = lambda i, be, nu: (jnp.minimum(i, nu[0] - 1), 0)
    wblk = lambda i, be, nu: (be[jnp.minimum(i, nu[0] - 1)], 0, 0)
    return pl.pallas_call(
        _moe_kernel,
        grid_spec=pltpu.PrefetchScalarGridSpec(
            num_scalar_prefetch=2,
            grid=(nb + 1,),
            in_specs=[pl.BlockSpec((MOE_ROWS * tile_rows, LANES), blk),
                      pl.BlockSpec((1, D, 2 * D_EXPERT), wblk),
                      pl.BlockSpec((1, D_EXPERT, D), wblk)],
            out_specs=pl.BlockSpec(memory_space=pl.ANY),
            scratch_shapes=[pltpu.VMEM((D, 2 * D_EXPERT), BF16), pltpu.VMEM((D_EXPERT, D), BF16),
                            pltpu.VMEM((2, MOE_ROWS * SUBLANES, LANES), F32),
                            pltpu.VMEM((1, MOE_ROWS), I32), pltpu.SMEM((2, MOE_ROWS), I32),
                            pltpu.SemaphoreType.DMA, pltpu.SemaphoreType.DMA((2,))]),
        out_shape=jax.ShapeDtypeStruct((n_out_rows, SUBLANES, LANES), F32),
        compiler_params=_cparams("arbitrary"),
        name="moe",
    )(block_expert, n_used, xs, w_gu, w_dn)


def _combine_kernel(w_hbm, yt_ref, h_ref, x1_ref, mod_ref, wsg_ref, wsd_ref, o_ref,
                    wts, routed, sem_w):
    i = pl.program_id(0)
    tc = h_ref.shape[0]
    cp_w = pltpu.make_async_copy(w_hbm.at[i], wts, sem_w)
    cp_w.start()
    gu = _mm(h_ref[...], wsg_ref[...])
    act = _silu(gu[:, :D_EXPERT]) * gu[:, D_EXPERT:]
    ffn = _mm(act, wsd_ref[...])
    cp_w.wait()

    def wsum(tt, carry):
        for u in range(SUBLANES):
            t = tt * SUBLANES + u
            first = pl.multiple_of(t * (TOP_K * SUBLANES), SUBLANES)
            acc = yt_ref[pl.ds(first, SUBLANES), :] * wts[0, t]
            for j in range(1, TOP_K):
                acc = acc + yt_ref[pl.ds(first + j * SUBLANES, SUBLANES), :] * wts[j, t]
            routed[pl.ds(pl.multiple_of(t * SUBLANES, SUBLANES), SUBLANES), :] = acc
        return carry

    lax.fori_loop(0, tc // SUBLANES, wsum, 0)
    routed2d = jnp.concatenate([routed[pl.ds(s, tc, stride=SUBLANES), :] for s in range(SUBLANES)],
                               axis=1)
    g_f = mod_ref[0, 5:6, :]
    o_ref[...] = x1_ref[...] + g_f * (ffn + routed2d)


def _combine_call(w3, yt, h2, x1, mod3, w_sh_gu, w_sh_dn, tokens_per_batch, tc):
    N, D = h2.shape
    tiles_per_batch = tokens_per_batch // tc
    tok = pl.BlockSpec((tc, D), lambda i: (i, 0))
    wsg = w_sh_gu.astype(BF16)
    wsd = w_sh_dn.astype(BF16)
    return pl.pallas_call(
        _combine_kernel,
        grid=(N // tc,),
        in_specs=[pl.BlockSpec(memory_space=pl.ANY),
                  pl.BlockSpec((tc * TOP_K * SUBLANES, LANES), lambda i: (i, 0)),
                  tok, tok,
                  pl.BlockSpec((1, 6, D), lambda i: (i // tiles_per_batch, 0, 0)),
                  pl.BlockSpec(wsg.shape, lambda i: (0, 0)),
                  pl.BlockSpec(wsd.shape, lambda i: (0, 0))],
        out_specs=tok,
        out_shape=jax.ShapeDtypeStruct((N, D), F32),
        scratch_shapes=[pltpu.SMEM((TOP_K, tc), F32), pltpu.VMEM((tc * SUBLANES, LANES), F32),
                        pltpu.SemaphoreType.DMA],
        compiler_params=_cparams("arbitrary"),
        name="combine",
    )(w3, yt, h2, x1, mod3, wsg, wsd)


def _tile(n, pref):
    t = min(n, pref)
    assert n % t == 0, (n, t)
    return t


def _layer(x, mod3, positions, norm_mix, w_in, rwkv_mu, decay_w0, decay_up, iclr_a0, iclr_up,
           gate_up, rwkv_k_k, rwkv_k_a, rwkv_r_k, ln_x_w, ln_x_b, q_a_norm, w_q_b, kv_a_norm,
           w_kv_b, q_norm, k_norm, w_out, norm_ffn, w_router, router_bias, w_e_gate_up, w_e_down,
           w_sh_gate_up, w_sh_down):
    B, T, D = x.shape
    N = B * T
    assert T % SCAN_CHUNK == 0
    (r, lw, k2, v, kk, akk, g, bonus, q_pad, k_pad, v_pad) = _pre_call(
        x, mod3, positions, norm_mix, w_in, rwkv_mu, decay_w0, decay_up, iclr_a0, iclr_up,
        gate_up, rwkv_k_k, rwkv_k_a, rwkv_r_k, q_a_norm, w_q_b, kv_a_norm, w_kv_b, q_norm, k_norm,
        tm=_tile(T, 512))
    y = _scan_call(r, lw, k2, v, kk, akk)
    o_pad = _attn_call(q_pad, k_pad, v_pad)
    x1, h2, h2t = _post_call(y, bonus, g, o_pad, x, mod3, ln_x_w, ln_x_b, w_out, norm_ffn,
                             tm=_tile(T, 512))
    x1 = x1.reshape(N, D)
    h2 = h2.reshape(N, D)
    h2t = h2t.reshape(N, D // LANES, LANES)

    tr = _tile(N, 512)
    top_e, wts, rank, counts = _route_call(h2, w_router, router_bias, tr)
    counts = counts.reshape(N_EXPERTS)
    padded = (counts + MOE_ROWS - 1) // MOE_ROWS * MOE_ROWS
    pad_ends = jnp.cumsum(padded)
    pad_starts = pad_ends - padded
    n_blocks = (N * TOP_K + N_EXPERTS * (MOE_ROWS - 1)) // MOE_ROWS
    block_expert = jnp.minimum(
        jnp.searchsorted(pad_ends, jnp.arange(n_blocks, dtype=I32) * MOE_ROWS, side='right'),
        N_EXPERTS - 1).astype(I32)
    n_used = (pad_ends[-1:] // MOE_ROWS).astype(I32)
    last_block_row = jnp.where(padded > 0, pad_ends - MOE_ROWS, -1).astype(I32)
    dest = _dest_call(top_e, rank, pad_starts.astype(I32), tr)

    td = _tile(T, DISPATCH_TILE)
    dest3 = dest.reshape(TOP_K, N // td, td).transpose(1, 0, 2)
    n_rows = n_blocks * MOE_ROWS
    xs = _dispatch_call(last_block_row, dest3, h2t, n_rows, td)
    yt = _moe_call(block_expert, n_used, xs.reshape(n_rows * 2 * SUBLANES, LANES), w_e_gate_up,
                   w_e_down, N * TOP_K + 2 * MOE_ROWS)
    tc = _tile(T, COMBINE_TILE)
    w3 = wts.reshape(TOP_K, N // tc, tc).transpose(1, 0, 2)
    out = _combine_call(w3, yt.reshape(-1, LANES), h2, x1, mod3, w_sh_gate_up,
                        w_sh_down, T, tc)
    return out.reshape(B, T, D)


def kernel(x, c, positions, ada_w, ada_b, norm_mix, w_in, rwkv_mu, decay_w0, decay_up, iclr_a0, iclr_up, gate_up, rwkv_k_k, rwkv_k_a, rwkv_r_k, ln_x_w, ln_x_b, q_a_norm, w_q_b, kv_a_norm, w_kv_b, q_norm, k_norm, w_out, norm_ffn, w_router, router_bias, w_e_gate_up, w_e_down, w_sh_gate_up, w_sh_down):
    B, T, D = x.shape
    depth = ada_w.shape[0]
    for l in range(depth):
        mod3 = _mod_call(c, ada_w[l], ada_b[l]).reshape(B, 6, D)
        x = _layer(x, mod3, positions, norm_mix[l], w_in[l], rwkv_mu[l], decay_w0[l], decay_up[l],
                   iclr_a0[l], iclr_up[l], gate_up[l], rwkv_k_k[l], rwkv_k_a[l], rwkv_r_k[l],
                   ln_x_w[l], ln_x_b[l], q_a_norm[l], w_q_b[l], kv_a_norm[l], w_kv_b[l],
                   q_norm[l], k_norm[l], w_out[l], norm_ffn[l], w_router[l], router_bias[l],
                   w_e_gate_up[l], w_e_down[l], w_sh_gate_up[l], w_sh_down[l])
    return x
```

```python
import functools
import math

import jax
import jax.numpy as jnp
import numpy as np
from jax import lax
from jax.experimental import pallas as pl
from jax.experimental.pallas import tpu as pltpu
from jax.experimental.pallas import tpu_sc as plsc

F32 = jnp.float32
BF16 = jnp.bfloat16
I32 = jnp.int32

NORM_EPS = 1e-6
GN_EPS = 64e-5
RWKV_HEADS = 8
RWKV_HEAD_DIM = 64
D_RWKV = 512
DECAY_LORA = 32
ICLR_LORA = 32
GATE_LORA = 96
MLA_HEADS = 8
QK_NOPE_DIM = 64
QK_ROPE_DIM = 32
QK_HEAD_DIM = 96
V_HEAD_DIM = 64
Q_LORA_RANK = 256
KV_LORA_RANK = 128
ROPE_THETA = 10000.0
N_EXPERTS = 256
TOP_K = 8
N_GROUPS = 8
TOPK_GROUPS = 4
GROUP_SIZE = N_EXPERTS // N_GROUPS
D_EXPERT = 256
ROUTED_SCALE = 2.5
MOE_ROWS = 512
SC_CHUNK = 64
COMBINE_TILE = 256

LANES = 128
SUBLANES = 8
HEAD_PAD = 128
VMEM_LIMIT = 56 * 1024 * 1024

PRE_SUBTILES = 1
SCAN_CHUNK = 64
SCAN_BLOCK = 512
ATTN_TILE = 512
ATTN_Q_SCALE = QK_HEAD_DIM ** -0.5 * math.log2(math.e)
NEG_INF = float("-inf")


def _cparams(*sem):
    return pltpu.CompilerParams(dimension_semantics=sem, vmem_limit_bytes=VMEM_LIMIT)


def _split2(a):
    hi = a.astype(BF16)
    lo = (a - hi.astype(F32)).astype(BF16)
    return hi, lo


def _split3(a):
    hi = a.astype(BF16)
    r1 = a - hi.astype(F32)
    mid = r1.astype(BF16)
    lo = (r1 - mid.astype(F32)).astype(BF16)
    return hi, mid, lo


def _dot(a, b, dims=None):
    if dims is None:
        return jnp.dot(a, b, preferred_element_type=F32)
    return lax.dot_general(a, b, (dims, ((), ())), preferred_element_type=F32)


def _mm(a, b, dims=None):
    return _dot(a.astype(BF16), b.astype(BF16), dims)


def _mm3(a, b, dims=None):
    ah, al = _split2(a)
    bh, bl = _split2(b)
    return _dot(ah, bh, dims) + (_dot(ah, bl, dims) + _dot(al, bh, dims))


def _mm_exact_rhs(a, b_exact_bf16, dims=None):
    h, m, l = _split3(a)
    return _dot(h, b_exact_bf16, dims) + (_dot(m, b_exact_bf16, dims) + _dot(l, b_exact_bf16, dims))


NT = ((1,), (1,))
TN = ((0,), (0,))


def _sigmoid(z):
    return 1.0 / (1.0 + jnp.exp(-z))


def _silu(z):
    return z * _sigmoid(z)


def _seg_ones(width, seg):
    r = lax.broadcasted_iota(I32, (width, width), 0) // seg
    c = lax.broadcasted_iota(I32, (width, width), 1) // seg
    return jnp.where(r == c, 1.0, 0.0).astype(BF16)


def _segsum(a, ones_bd):
    hi, lo = _split2(a)
    return _dot(hi, ones_bd) + _dot(lo, ones_bd)


def _mod_kernel(c_ref, w_ref, b_ref, o_ref):
    ca = _silu(c_ref[...])
    o_ref[...] = _mm3(ca, w_ref[...]) + b_ref[...]


def _mod_call(c, ada_w, ada_b):
    B, D = c.shape
    n6 = ada_w.shape[1]
    tn = D
    return pl.pallas_call(
        _mod_kernel,
        grid=(n6 // tn,),
        in_specs=[pl.BlockSpec((B, D), lambda j: (0, 0)),
                  pl.BlockSpec((D, tn), lambda j: (0, j)),
                  pl.BlockSpec((1, tn), lambda j: (0, j))],
        out_specs=pl.BlockSpec((B, tn), lambda j: (0, j)),
        out_shape=jax.ShapeDtypeStruct((B, n6), F32),
        compiler_params=_cparams("arbitrary"),
        name="mod",
    )(c, ada_w, ada_b.reshape(1, n6))


def _pre_kernel(x_ref, mod_ref, pos_ref, nmix_ref, wrkv_ref, wlora_ref, wmla_ref,
                mu_rkv_ref, mu_lora_ref, wup_ref, w0_ref, a0_ref, kk_ref, ka_ref, rk_ref,
                qan_ref, wqb_ref, kvan_ref, wkb_ref, wvb_ref, qn_ref, kn_ref, invf_ref,
                r_ref, lw_ref, k_ref, v_ref, kkn_ref, akk_ref, g_ref, bonus_ref,
                q_ref, kout_ref, vout_ref,
                carry_rkv, carry_lora):
    ti = pl.program_id(1)
    tm = x_ref.shape[1]

    @pl.when(ti == 0)
    def _():
        carry_rkv[...] = jnp.zeros_like(carry_rkv)
        carry_lora[...] = jnp.zeros_like(carry_lora)

    ts = tm // PRE_SUBTILES
    sh_a = mod_ref[0, 0:1, :]
    sc_a = mod_ref[0, 1:2, :]
    ones64 = _seg_ones(D_RWKV, RWKV_HEAD_DIM)
    row0 = lax.broadcasted_iota(I32, (ts, 1), 0) == 0
    half = QK_ROPE_DIM // 2
    last_rows = {}

    def sub_tile(s):
        rows = slice(s * ts, (s + 1) * ts)
        xb = x_ref[0, rows, :]
        ms = jnp.mean(xb * xb, axis=-1, keepdims=True)
        h = xb * lax.rsqrt(ms + NORM_EPS) * nmix_ref[...] * (1.0 + sc_a) + sh_a
        hb = h.astype(BF16)
        yield
        u_rkv = _dot(hb, wrkv_ref[...])
        u_lora = _dot(hb, wlora_ref[...])
        u_mla = _dot(hb, wmla_ref[...])
        last_rows[s] = (u_rkv[ts - 1:ts, :], u_lora[ts - 1:ts, :])
        yield

        before_rkv, before_lora = (carry_rkv[...], carry_lora[...]) if s == 0 else last_rows[s - 1]
        prev_rkv = jnp.where(row0, before_rkv, pltpu.roll(u_rkv, 1, 0))
        prev_lora = jnp.where(row0, before_lora, pltpu.roll(u_lora, 1, 0))
        if s == PRE_SUBTILES - 1:
            carry_rkv[...] = u_rkv[ts - 1:ts, :]
            carry_lora[...] = u_lora[ts - 1:ts, :]
        us = u_rkv + (prev_rkv - u_rkv) * mu_rkv_ref[...]
        ul = u_lora + (prev_lora - u_lora) * mu_lora_ref[...]
        r = us[:, 0:D_RWKV]
        k = us[:, D_RWKV:2 * D_RWKV]
        v = us[:, 2 * D_RWKV:3 * D_RWKV]
        lane_l = lax.broadcasted_iota(I32, ul.shape, 1)
        t_in = jnp.where(lane_l < DECAY_LORA, jnp.tanh(ul),
                         jnp.where(lane_l < DECAY_LORA + ICLR_LORA, ul, _sigmoid(ul)))
        yield
        up = _mm(t_in, wup_ref[...])
        yield
        z = w0_ref[...] + up[:, 0:D_RWKV]
        lw = (-math.exp(-0.5)) * _sigmoid(z)
        a = _sigmoid(a0_ref[...] + up[:, D_RWKV:2 * D_RWKV])
        g = up[:, 2 * D_RWKV:3 * D_RWKV]
        kk = k * kk_ref[...]
        k2 = k * (1.0 + (a - 1.0) * ka_ref[...])
        yield
        ss = _segsum(kk * kk, ones64)
        bonus_sum = _segsum(r * k2 * rk_ref[...], ones64)
        yield
        kk = kk * lax.rsqrt(jnp.maximum(ss, 1e-24))
        r_ref[0, rows, :] = r
        lw_ref[0, rows, :] = lw
        k_ref[0, rows, :] = k2
        v_ref[0, rows, :] = v
        kkn_ref[0, rows, :] = kk
        akk_ref[0, rows, :] = a * kk
        g_ref[0, rows, :] = g
        bonus_ref[0, rows, :] = bonus_sum * v
        yield

        q_lat = u_mla[:, 0:Q_LORA_RANK]
        kv_lat = u_mla[:, Q_LORA_RANK:Q_LORA_RANK + KV_LORA_RANK]
        kpe_tile = u_mla[:, Q_LORA_RANK + KV_LORA_RANK:]
        qn = q_lat * lax.rsqrt(jnp.mean(q_lat * q_lat, axis=-1, keepdims=True) + NORM_EPS) * qan_ref[...]
        kvn = kv_lat * lax.rsqrt(jnp.mean(kv_lat * kv_lat, axis=-1, keepdims=True) + NORM_EPS) * kvan_ref[...]
        kvb = kvn.astype(BF16)
        yield
        q_raw = _mm(qn, wqb_ref[...])
        k_raw = _dot(kvb, wkb_ref[...])
        v_pad = _dot(kvb, wvb_ref[...])
        yield
        kpe_h = pltpu.roll(kpe_tile, QK_NOPE_DIM, 1)
        cos_t, s1, s2 = rope_tables(s)

        def tables(gain, scale):
            g_s = gain * scale
            return (cos_t * g_s, s1 * pltpu.roll(g_s, HEAD_PAD - half, 1),
                    s2 * pltpu.roll(g_s, half, 1))

        def norm_rope(xh, tabs):
            c_g, s1_g, s2_g = tabs
            ssq = jnp.sum(xh * xh, axis=-1, keepdims=True) * (1.0 / QK_HEAD_DIM)
            rot = xh * c_g + pltpu.roll(xh, HEAD_PAD - half, 1) * s1_g + pltpu.roll(xh, half, 1) * s2_g
            return rot * lax.rsqrt(ssq + NORM_EPS)

        q_tabs = tables(qn_ref[...], ATTN_Q_SCALE)
        k_tabs = tables(kn_ref[...], 1.0)
        vout_ref[0, rows, :] = v_pad.astype(BF16)
        yield
        for hh in range(MLA_HEADS):
            sl = slice(hh * HEAD_PAD, (hh + 1) * HEAD_PAD)
            q_ref[0, rows, sl] = norm_rope(q_raw[:, sl], q_tabs).astype(BF16)
            kout_ref[0, rows, sl] = norm_rope(k_raw[:, sl] + kpe_h, k_tabs).astype(BF16)
            yield

    def rope_tables(s):
        ang_t = invf_ref[...] * pos_ref[0, :, s * ts:(s + 1) * ts].astype(F32)
        frow = lax.broadcasted_iota(I32, (half, HEAD_PAD), 0)
        flane = lax.broadcasted_iota(I32, (half, HEAD_PAD), 1)
        at_x1 = flane == frow + QK_NOPE_DIM
        at_x2 = flane == frow + QK_NOPE_DIM + half
        e_cos = jnp.where(at_x1 | at_x2, 1.0, 0.0).astype(BF16)
        e_sin = jnp.concatenate([jnp.where(at_x1, -1.0, 0.0), jnp.where(at_x2, 1.0, 0.0)],
                                axis=1).astype(BF16)
        lane = lax.broadcasted_iota(I32, (1, HEAD_PAD), 1)
        off_rope = jnp.where((lane >= QK_NOPE_DIM) & (lane < QK_HEAD_DIM), 0.0, 1.0)
        cos_t = _mm_exact_rhs(jnp.cos(ang_t), e_cos, TN) + off_rope
        sin2 = _mm_exact_rhs(jnp.sin(ang_t), e_sin, TN)
        return cos_t, sin2[:, :HEAD_PAD], sin2[:, HEAD_PAD:]

    live = [sub_tile(s) for s in range(PRE_SUBTILES)]
    while live:
        live = [gen for gen in live if next(gen, "done") != "done"]


def _pad_heads(w, n_heads, width):
    kdim = w.shape[0]
    w = w.reshape(kdim, n_heads, width)
    w = jnp.pad(w, ((0, 0), (0, 0), (0, HEAD_PAD - width)))
    return w.reshape(kdim, n_heads * HEAD_PAD)


def _pre_call(x, mod3, positions, norm_mix, w_in, rwkv_mu, decay_w0, decay_up, iclr_a0, iclr_up,
              gate_up, k_k, k_a, r_k, q_a_norm, w_q_b, kv_a_norm, w_kv_b, q_norm, k_norm, tm):
    B, T, D = x.shape
    n_rkv = 3 * D_RWKV
    n_lora = DECAY_LORA + ICLR_LORA + GATE_LORA
    LORA_PAD = 256
    MLA_PAD = 512
    n_mla = Q_LORA_RANK + KV_LORA_RANK + QK_ROPE_DIM
    w_rkv = w_in[:, :n_rkv].astype(BF16)
    w_lora = jnp.pad(w_in[:, n_rkv:n_rkv + n_lora], ((0, 0), (0, LORA_PAD - n_lora))).astype(BF16)
    w_mla = jnp.pad(w_in[:, n_rkv + n_lora:], ((0, 0), (0, MLA_PAD - n_mla))).astype(BF16)
    mu_rkv = rwkv_mu[:n_rkv].reshape(1, n_rkv)
    mu_lora = jnp.pad(rwkv_mu[n_rkv:], (0, LORA_PAD - n_lora)).reshape(1, LORA_PAD)
    w_up = jnp.zeros((LORA_PAD, n_rkv), F32)
    w_up = w_up.at[0:DECAY_LORA, 0:D_RWKV].set(decay_up)
    w_up = w_up.at[DECAY_LORA:DECAY_LORA + ICLR_LORA, D_RWKV:2 * D_RWKV].set(iclr_up)
    w_up = w_up.at[DECAY_LORA + ICLR_LORA:n_lora, 2 * D_RWKV:].set(gate_up)
    w_up = w_up.astype(BF16)
    w_qb = _pad_heads(w_q_b, MLA_HEADS, QK_HEAD_DIM).astype(BF16)
    w_kv3 = w_kv_b.reshape(KV_LORA_RANK, MLA_HEADS, QK_NOPE_DIM + V_HEAD_DIM)
    w_kb = _pad_heads(w_kv3[:, :, :QK_NOPE_DIM].reshape(KV_LORA_RANK, -1), MLA_HEADS, QK_NOPE_DIM).astype(BF16)
    w_vb = _pad_heads(w_kv3[:, :, QK_NOPE_DIM:].reshape(KV_LORA_RANK, -1), MLA_HEADS, V_HEAD_DIM).astype(BF16)
    qn_pad = jnp.pad(q_norm, (0, HEAD_PAD - QK_HEAD_DIM)).reshape(1, HEAD_PAD)
    kn_pad = jnp.pad(k_norm, (0, HEAD_PAD - QK_HEAD_DIM)).reshape(1, HEAD_PAD)
    inv_freq = ROPE_THETA ** (-jnp.arange(0, QK_ROPE_DIM, 2, dtype=F32) / QK_ROPE_DIM)
    invf = inv_freq.reshape(QK_ROPE_DIM // 2, 1)
    pos3 = positions.reshape(B, 1, T)
    HP = MLA_HEADS * HEAD_PAD

    row = lambda n: pl.BlockSpec((1, n), lambda b, t: (0, 0))
    full = lambda a: pl.BlockSpec(a.shape, lambda b, t: (0,) * a.ndim)
    tok = lambda n: pl.BlockSpec((1, tm, n), lambda b, t: (b, t, 0))
    outs = ([jax.ShapeDtypeStruct((B, T, D_RWKV), F32)] * 8
            + [jax.ShapeDtypeStruct((B, T, HP), BF16)] * 3)
    return pl.pallas_call(
        _pre_kernel,
        grid=(B, T // tm),
        in_specs=[tok(D),
                  pl.BlockSpec((1, 6, D), lambda b, t: (b, 0, 0)),
                  pl.BlockSpec((1, 1, tm), lambda b, t: (b, 0, t)),
                  row(D), full(w_rkv), full(w_lora), full(w_mla),
                  row(n_rkv), row(LORA_PAD), full(w_up), row(D_RWKV), row(D_RWKV),
                  row(D_RWKV), row(D_RWKV), row(D_RWKV),
                  row(Q_LORA_RANK), full(w_qb), row(KV_LORA_RANK), full(w_kb), full(w_vb),
                  row(HEAD_PAD), row(HEAD_PAD), full(invf)],
        out_specs=[tok(D_RWKV)] * 8 + [tok(HP)] * 3,
        out_shape=outs,
        scratch_shapes=[pltpu.VMEM((1, n_rkv), F32), pltpu.VMEM((1, LORA_PAD), F32)],
        compiler_params=_cparams("arbitrary", "arbitrary"),
        name="pre",
    )(x, mod3, pos3, norm_mix.reshape(1, D), w_rkv, w_lora, w_mla, mu_rkv, mu_lora, w_up,
      decay_w0.reshape(1, -1), iclr_a0.reshape(1, -1), k_k.reshape(1, -1), k_a.reshape(1, -1),
      r_k.reshape(1, -1), q_a_norm.reshape(1, -1), w_qb, kv_a_norm.reshape(1, -1), w_kb, w_vb,
      qn_pad, kn_pad, invf)


def _scan_kernel(r_ref, lw_ref, k_ref, v_ref, kk_ref, akk_ref, y_ref, state):
    C = SCAN_CHUNK
    n_chunks = r_ref.shape[1] // C
    n_pairs = r_ref.shape[2] // LANES

    @pl.when(pl.program_id(1) == 0)
    def _():
        state[...] = jnp.zeros_like(state)

    ri = lax.broadcasted_iota(I32, (C, C), 0)
    ci = lax.broadcasted_iota(I32, (C, C), 1)
    tri_incl = jnp.where(ci <= ri, 1.0, 0.0).astype(BF16)
    r2 = lax.broadcasted_iota(I32, (2 * C, 2 * C), 0)
    c2 = lax.broadcasted_iota(I32, (2 * C, 2 * C), 1)
    same = (r2 >= C) == (c2 >= C)
    strict = same & (c2 < r2)
    incl = same & (c2 <= r2)
    eye = jnp.where(c2 == r2, 1.0, 0.0)
    head0 = lax.broadcasted_iota(I32, (C, LANES), 1) < RWKV_HEAD_DIM

    def stack2(a):
        return jnp.concatenate([jnp.where(head0, a, 0.0), jnp.where(head0, 0.0, a)], axis=0)

    C2 = 2 * C
    cat0 = lambda *a: jnp.concatenate(a, axis=0)
    cat1 = lambda *a: jnp.concatenate(a, axis=1)

    items = []
    for c in range(n_chunks):
        rows = slice(c * C, (c + 1) * C)
        lw = lw_ref[0, rows, :]
        cum = _mm_exact_rhs_left(tri_incl, lw)
        cum_end = cum[C - 1:C, :]
        w_end = jnp.exp(cum_end)
        e_pos = jnp.exp(cum)
        e_neg = jnp.exp(-cum)
        e_prev = jnp.exp(cum - lw)
        e_end = jnp.exp(cum_end - cum)
        kk = kk_ref[0, rows, :]
        k2 = k_ref[0, rows, :]
        pneg = -akk_ref[0, rows, :]
        vv = v_ref[0, rows, :]
        rt = r_ref[0, rows, :] * e_pos
        bt = kk * e_prev
        pt = pneg * e_neg
        kt = k2 * e_neg
        ph = pneg * e_end
        kh = k2 * e_end
        for pp in range(n_pairs):
            sl = slice(pp * LANES, (pp + 1) * LANES)
            items.append(dict(
                c=c, p=pp, w_end=w_end[:, sl],
                bt2=stack2(bt[:, sl]).astype(BF16), rt2=stack2(rt[:, sl]).astype(BF16),
                pk2=cat0(stack2(pt[:, sl]), stack2(kt[:, sl])).astype(BF16),
                phkh2=cat0(stack2(ph[:, sl]), stack2(kh[:, sl])).astype(BF16),
                v2=stack2(vv[:, sl])))
    for it in items:
        ab = _dot(cat0(it['bt2'], it['rt2']), it['pk2'], NT)
        it['a_ab'] = jnp.where(strict, ab[:C2, :C2], 0.0)
        it['a_ak'] = jnp.where(strict, ab[:C2, C2:], 0.0).astype(BF16)
        it['b_rpk'] = cat1(jnp.where(incl, ab[C2:, :C2], 0.0), jnp.where(incl, ab[C2:, C2:], 0.0)).astype(BF16)
        it['tinv'] = eye + it['a_ab']
    for it in items:
        it['apow'] = _mm(it['a_ab'], it['a_ab'])
    for _ in range(int(math.log2(C)) - 1):
        for it in items:
            both = _mm(cat0(it['apow'], it['tinv']), it['apow'])
            it['apow'] = both[:C2]
            it['tinv'] = it['tinv'] + both[C2:]
    for it in items:
        it['akv'] = _dot(it['a_ak'], it['v2'].astype(BF16))
    for it in items:
        tt = _dot(it['tinv'].astype(BF16), cat1(it['bt2'], it['akv'].astype(BF16)))
        it['tb_rt'] = cat0(tt[:, :LANES].astype(BF16), it['rt2'])
        it['tav'] = tt[:, LANES:]
    for it in items:
        pp = it['p']
        s0 = state[pp]
        top = _dot(it['tb_rt'], s0.astype(BF16), NT)
        u2 = top[:C2] + it['tav']
        uv = cat0(u2, it['v2']).astype(BF16)
        y2 = top[C2:] + _dot(it['b_rpk'], uv)
        state[pp] = s0 * it['w_end'] + _dot(uv, it['phkh2'], TN)
        y_ref[0, it['c'] * C:(it['c'] + 1) * C, pp * LANES:(pp + 1) * LANES] = y2[0:C] + y2[C:C2]


def _mm_exact_rhs_left(b_exact_bf16, a):
    h, m, l = _split3(a)
    return _dot(b_exact_bf16, h) + (_dot(b_exact_bf16, m) + _dot(b_exact_bf16, l))


def _scan_call(r, lw, k2, v, kk, akk):
    B, T, W = r.shape
    tb = _tile(T, SCAN_BLOCK)
    spec = pl.BlockSpec((1, tb, W), lambda b, c: (b, c, 0))
    return pl.pallas_call(
        _scan_kernel,
        grid=(B, T // tb),
        in_specs=[spec] * 6,
        out_specs=spec,
        out_shape=jax.ShapeDtypeStruct((B, T, W), F32),
        scratch_shapes=[pltpu.VMEM((W // LANES, 2 * RWKV_HEAD_DIM, LANES), F32)],
        compiler_params=_cparams("arbitrary", "arbitrary"),
        name="scan",
    )(r, lw, k2, v, kk, akk)


def _attn_kernel(q_ref, k_ref, v_ref, o_ref):
    T = q_ref.shape[1]
    tq = min(T, ATTN_TILE)
    row = lax.broadcasted_iota(I32, (tq, tq), 0)
    col = lax.broadcasted_iota(I32, (tq, tq), 1)
    causal = col <= row

    def update(q, kt, vt, carry, mask):
        m_old, l_old, acc = carry
        s = _dot(q, kt, NT)
        if mask:
            s = jnp.where(causal, s, NEG_INF)
        m_new = jnp.maximum(m_old, jnp.max(s, axis=-1, keepdims=True))
        alpha = jnp.exp2(m_old - m_new)
        p = jnp.exp2(s - m_new)
        l_new = alpha * l_old + jnp.sum(p, axis=-1, keepdims=True)
        acc = alpha * acc + _dot(p.astype(BF16), vt)
        return m_new, l_new, acc

    for qi in range(T // tq):
        q = q_ref[0, qi * tq:(qi + 1) * tq, :]
        carry = (jnp.full((tq, 1), NEG_INF, F32), jnp.zeros((tq, 1), F32),
                 jnp.zeros((tq, HEAD_PAD), F32))

        def body(ki, carry, q=q):
            rows = pl.ds(pl.multiple_of(ki * tq, tq), tq)
            return update(q, k_ref[0, rows, :], v_ref[0, rows, :], carry, False)

        carry = lax.fori_loop(0, qi, body, carry, unroll=True)
        diag = slice(qi * tq, (qi + 1) * tq)
        _, l_fin, acc = update(q, k_ref[0, diag, :], v_ref[0, diag, :], carry, True)
        o_ref[0, diag, :] = (acc / l_fin).astype(o_ref.dtype)


def _attn_call(q, k, v):
    B, T, HP = q.shape
    spec = pl.BlockSpec((1, T, HEAD_PAD), lambda b, h: (b, 0, h))
    return pl.pallas_call(
        _attn_kernel,
        grid=(B, MLA_HEADS),
        in_specs=[spec, spec, spec],
        out_specs=spec,
        out_shape=jax.ShapeDtypeStruct((B, T, HP), BF16),
        compiler_params=_cparams("arbitrary", "arbitrary"),
        name="attn",
    )(q, k, v)


def _post_kernel(y_ref, bonus_ref, g_ref, o_ref, x_ref, mod_ref, lnw_ref, lnb_ref,
                 wo_r_ref, wo_m_ref, nffn_ref, x1_ref, h2_ref, h2t_ref):
    y = y_ref[0]
    ones64 = _seg_ones(D_RWKV, RWKV_HEAD_DIM)
    mean = _segsum(y, ones64) * (1.0 / RWKV_HEAD_DIM)
    yc = y - mean
    var = _segsum(yc * yc, ones64) * (1.0 / RWKV_HEAD_DIM)
    yn = yc * lax.rsqrt(var + GN_EPS) * lnw_ref[...] + lnb_ref[...]
    yr = (yn + bonus_ref[0]) * g_ref[0]
    mix = _mm(yr, wo_r_ref[...]) + _dot(o_ref[0], wo_m_ref[...])
    g_a = mod_ref[0, 2:3, :]
    sh_f = mod_ref[0, 3:4, :]
    sc_f = mod_ref[0, 4:5, :]
    x1 = x_ref[0] + g_a * mix
    x1_ref[0] = x1
    ms = jnp.mean(x1 * x1, axis=-1, keepdims=True)
    h2 = x1 * lax.rsqrt(ms + NORM_EPS) * nffn_ref[...] * (1.0 + sc_f) + sh_f
    h2_ref[0] = h2
    tm = h2.shape[0]
    for s in range(SUBLANES):
        h2t_ref[0, pl.ds(s, tm, stride=SUBLANES), :] = h2[:, s * LANES:(s + 1) * LANES]


def _post_call(y, bonus, g, o_pad, x, mod3, ln_w, ln_b, w_out, norm_ffn, tm):
    B, T, D = x.shape
    HP = MLA_HEADS * HEAD_PAD
    wo_r = w_out[:D_RWKV].astype(BF16)
    wo_m = jnp.pad(w_out[D_RWKV:].reshape(MLA_HEADS, V_HEAD_DIM, D),
                   ((0, 0), (0, HEAD_PAD - V_HEAD_DIM), (0, 0))).reshape(HP, D).astype(BF16)
    tok = lambda n: pl.BlockSpec((1, tm, n), lambda b, t: (b, t, 0))
    row = lambda n: pl.BlockSpec((1, n), lambda b, t: (0, 0))
    full = lambda a: pl.BlockSpec(a.shape, lambda b, t: (0,) * a.ndim)
    return pl.pallas_call(
        _post_kernel,
        grid=(B, T // tm),
        in_specs=[tok(D_RWKV), tok(D_RWKV), tok(D_RWKV), tok(HP), tok(D),
                  pl.BlockSpec((1, 6, D), lambda b, t: (b, 0, 0)),
                  row(D_RWKV), row(D_RWKV), full(wo_r), full(wo_m), row(D)],
        out_specs=[tok(D), tok(D),
                   pl.BlockSpec((1, tm * SUBLANES, LANES), lambda b, t: (b, t, 0))],
        out_shape=[jax.ShapeDtypeStruct((B, T, D), F32)] * 2
        + [jax.ShapeDtypeStruct((B, T * SUBLANES, LANES), F32)],
        compiler_params=_cparams("arbitrary", "arbitrary"),
        name="post",
    )(y, bonus, g, o_pad, x, mod3, ln_w.reshape(1, -1), ln_b.reshape(1, -1), wo_r, wo_m,
      norm_ffn.reshape(1, D))


def _first_index(mask, iota, size, axis):
    return jnp.min(jnp.where(mask, iota, size), axis=axis, keepdims=True)


def _route_kernel(h_ref, wr_ref, bias_ref, e_ref, w_ref, rank_ref, cnt_ref, base):
    tr = h_ref.shape[0]
    E = N_EXPERTS

    @pl.when(pl.program_id(0) == 0)
    def _():
        base[...] = jnp.zeros_like(base)

    logits = _mm3(wr_ref[...], h_ref[...], NT)
    scores = _sigmoid(logits)
    sel = scores + bias_ref[...]
    iota_g = lax.broadcasted_iota(I32, (GROUP_SIZE, tr), 0)
    gs_rows = []
    for gi in range(N_GROUPS):
        blk = sel[gi * GROUP_SIZE:(gi + 1) * GROUP_SIZE, :]
        m1 = jnp.max(blk, axis=0, keepdims=True)
        i1 = _first_index(blk == m1, iota_g, GROUP_SIZE, 0)
        m2 = jnp.max(jnp.where(iota_g == i1, NEG_INF, blk), axis=0, keepdims=True)
        gs_rows.append(m1 + m2)
    gs = jnp.concatenate(gs_rows, axis=0)
    iota8 = lax.broadcasted_iota(I32, (N_GROUPS, tr), 0)
    gmask = jnp.zeros((N_GROUPS, tr), jnp.bool_)
    for _ in range(TOPK_GROUPS):
        mg = jnp.max(gs, axis=0, keepdims=True)
        ig = _first_index(gs == mg, iota8, N_GROUPS, 0)
        hit = iota8 == ig
        gmask = gmask | hit
        gs = jnp.where(hit, NEG_INF, gs)
    msel = jnp.concatenate(
        [jnp.where(gmask[gi:gi + 1, :], sel[gi * GROUP_SIZE:(gi + 1) * GROUP_SIZE, :], NEG_INF)
         for gi in range(N_GROUPS)], axis=0)
    iota_e = lax.broadcasted_iota(I32, (E, tr), 0)
    e_rows, w_rows = [], []
    onehot = jnp.zeros((E, tr), F32)
    for _ in range(TOP_K):
        mv = jnp.max(msel, axis=0, keepdims=True)
        ie = _first_index(msel == mv, iota_e, E, 0)
        hit = iota_e == ie
        e_rows.append(ie)
        w_rows.append(jnp.sum(jnp.where(hit, scores, 0.0), axis=0, keepdims=True))
        onehot = jnp.where(hit, 1.0, onehot)
        msel = jnp.where(hit, NEG_INF, msel)
    top_e = jnp.concatenate(e_rows, axis=0)
    wts = jnp.concatenate(w_rows, axis=0)
    wts = wts / jnp.sum(wts, axis=0, keepdims=True) * ROUTED_SCALE
    ti = lax.broadcasted_iota(I32, (tr, tr), 0)
    tj = lax.broadcasted_iota(I32, (tr, tr), 1)
    upper = jnp.where(ti < tj, 1.0, 0.0).astype(BF16)
    pos = _dot(onehot.astype(BF16), upper) + base[...]
    rank_rows = [jnp.sum(jnp.where(iota_e == e_rows[j], pos, 0.0), axis=0, keepdims=True)
                 for j in range(TOP_K)]
    base[...] = base[...] + jnp.sum(onehot, axis=1, keepdims=True)
    e_ref[...] = top_e
    w_ref[...] = wts
    rank_ref[...] = jnp.concatenate(rank_rows, axis=0).astype(I32)
    cnt_ref[...] = base[...].astype(I32)


def _route_call(h2, w_router, router_bias, tr):
    N, D = h2.shape
    E = N_EXPERTS
    out_kn = pl.BlockSpec((TOP_K, tr), lambda i: (0, i))
    return pl.pallas_call(
        _route_kernel,
        grid=(N // tr,),
        in_specs=[pl.BlockSpec((tr, D), lambda i: (i, 0)),
                  pl.BlockSpec((E, D), lambda i: (0, 0)),
                  pl.BlockSpec((E, 1), lambda i: (0, 0))],
        out_specs=[out_kn, out_kn, out_kn, pl.BlockSpec((E, 1), lambda i: (0, 0))],
        out_shape=[jax.ShapeDtypeStruct((TOP_K, N), I32), jax.ShapeDtypeStruct((TOP_K, N), F32),
                   jax.ShapeDtypeStruct((TOP_K, N), I32), jax.ShapeDtypeStruct((E, 1), I32)],
        scratch_shapes=[pltpu.VMEM((E, 1), F32)],
        compiler_params=_cparams("arbitrary"),
        name="route",
    )(h2, w_router.T, router_bias.reshape(E, 1))


def _dest_kernel(e_ref, rank_ref, start_ref, d_ref):
    tr = e_ref.shape[1]
    iota_e = lax.broadcasted_iota(I32, (N_EXPERTS, tr), 0)
    starts = start_ref[...]
    rows = [jnp.sum(jnp.where(iota_e == e_ref[j:j + 1, :], starts, 0), axis=0, keepdims=True)
            for j in range(TOP_K)]
    d_ref[...] = jnp.concatenate(rows, axis=0) + rank_ref[...]


def _dest_call(top_e, rank, pad_starts, tr):
    K, N = top_e.shape
    spec = pl.BlockSpec((K, tr), lambda i: (0, i))
    return pl.pallas_call(
        _dest_kernel,
        grid=(N // tr,),
        in_specs=[spec, spec, pl.BlockSpec((N_EXPERTS, 1), lambda i: (0, 0))],
        out_specs=spec,
        out_shape=jax.ShapeDtypeStruct((K, N), I32),
        compiler_params=_cparams("arbitrary"),
        name="dest",
    )(top_e, rank, pad_starts.reshape(N_EXPERTS, 1))


def _dispatch_kernel(last_ref, dest_hbm, h_ref, xs_out, idx, src, zbuf, sem_idx, sem_rows, sem_zero):
    i = pl.program_id(0)
    n_tiles = pl.num_programs(0)
    td = h_ref.shape[0]
    cur = i % 2
    cp = pltpu.make_async_copy(dest_hbm.at[i], idx, sem_idx)
    cp.start()
    sub = lax.broadcasted_iota(I32, (1, SUBLANES, LANES), 1)
    lane = lax.broadcasted_iota(I32, (1, SUBLANES, LANES), 2)
    tag_here = (sub == 0) & (lane == 0)

    @pl.when(i == 0)
    def _():
        zbuf[:, 0:SUBLANES, :] = jnp.zeros((MOE_ROWS, SUBLANES, LANES), F32)
        zbuf[:, SUBLANES:, :] = jnp.broadcast_to(jnp.where(tag_here, -1.0, 0.0),
                                                 (MOE_ROWS, SUBLANES, LANES))

        def zero_copy(e):
            return pltpu.make_async_copy(zbuf, xs_out.at[pl.ds(last_ref[e], MOE_ROWS)], sem_zero)

        def z_issue(e, carry):
            @pl.when(last_ref[e] >= 0)
            def _():
                zero_copy(e).start()
            return carry

        def z_drain(e, carry):
            @pl.when(last_ref[e] >= 0)
            def _():
                zero_copy(e).wait()
            return carry

        lax.fori_loop(0, N_EXPERTS, z_issue, 0)
        lax.fori_loop(0, N_EXPERTS, z_drain, 0)

    tok = (i * td + lax.broadcasted_iota(I32, (td, SUBLANES, LANES), 0)) * TOP_K
    payload = h_ref[...]
    for j in range(TOP_K):
        src[cur, j, :, 0:SUBLANES, :] = payload
        src[cur, j, :, SUBLANES:, :] = jnp.where(tag_here, (tok + j).astype(F32), 0.0)
    cp.wait()

    def row_copy(buf, slot, t, dst_row):
        return pltpu.make_async_copy(src.at[buf, slot, t], xs_out.at[dst_row], sem_rows.at[buf])

    def issue(tt, carry):
        base = pl.multiple_of(tt * SUBLANES, SUBLANES)
        for u in range(SUBLANES):
            for j in range(TOP_K):
                row_copy(cur, j, base + u, idx[j, base + u]).start()
        return carry

    lax.fori_loop(0, td // SUBLANES, issue, 0)

    def drain(buf):
        def body(t, carry):
            for j in range(TOP_K):
                row_copy(buf, j, 0, 0).wait()
            return carry
        lax.fori_loop(0, td, body, 0)

    @pl.when(i > 0)
    def _():
        drain(1 - cur)

    @pl.when(i == n_tiles - 1)
    def _():
        drain(cur)


def _dispatch_call(last_block_row, dest3, h2t, n_rows, td):
    N, S, L = h2t.shape
    return pl.pallas_call(
        _dispatch_kernel,
        grid_spec=pltpu.PrefetchScalarGridSpec(
            num_scalar_prefetch=1,
            grid=(N // td,),
            in_specs=[pl.BlockSpec(memory_space=pl.ANY),
                      pl.BlockSpec((td, S, L), lambda i, last: (i, 0, 0))],
            out_specs=pl.BlockSpec(memory_space=pl.ANY),
            scratch_shapes=[pltpu.SMEM((TOP_K, td), I32),
                            pltpu.VMEM((2, TOP_K, td, 2 * S, L), F32),
                            pltpu.VMEM((MOE_ROWS, 2 * S, L), F32),
                            pltpu.SemaphoreType.DMA, pltpu.SemaphoreType.DMA((2,)),
                            pltpu.SemaphoreType.DMA]),
        out_shape=jax.ShapeDtypeStruct((n_rows, 2 * S, L), h2t.dtype),
        compiler_params=_cparams("arbitrary"),
        name="dispatch",
    )(last_block_row, dest3, h2t)


def _sc_dispatch_call(h2t, dest3, n_rows):
    N = h2t.shape[0]
    info = plsc.get_sparse_core_info()
    n_workers = info.num_cores * info.num_subcores
    n_chunks = N // (SC_CHUNK * n_workers)
    assert n_chunks * SC_CHUNK * n_workers == N
    mesh = plsc.VectorSubcoreMesh(core_axis_name="c", subcore_axis_name="s")

    @functools.partial(
        pl.kernel, mesh=mesh,
        out_type=(jax.ShapeDtypeStruct((n_rows,) + h2t.shape[1:], h2t.dtype),
                  jax.ShapeDtypeStruct((n_rows, LANES), I32)),
        scratch_types=[pltpu.VMEM((TOP_K, SC_CHUNK), I32),
                       pltpu.VMEM((SC_CHUNK,) + h2t.shape[1:], h2t.dtype),
                       pltpu.VMEM((SC_CHUNK, LANES), I32)],
        name="sc_dispatch",
    )
    def scatter_rows(h_hbm, dest_hbm, xs_hbm, tag_hbm, idx_v, rows_v, tag_v):
        wid = lax.axis_index("s") * info.num_cores + lax.axis_index("c")
        zeros = jnp.zeros((info.num_lanes,), I32)

        @pl.loop(0, SC_CHUNK)
        def _(r):
            for l0 in range(0, LANES, info.num_lanes):
                tag_v[r, pl.ds(l0, info.num_lanes)] = zeros

        @pl.loop(0, n_chunks)
        def _(c):
            chunk = wid * n_chunks + c
            base = chunk * SC_CHUNK
            pltpu.sync_copy(dest_hbm.at[chunk], idx_v)
            pltpu.sync_copy(h_hbm.at[pl.ds(base, SC_CHUNK)], rows_v)
            for j in range(TOP_K):
                pltpu.sync_copy(rows_v, xs_hbm.at[idx_v.at[j]])

                @pl.loop(0, SC_CHUNK)
                def _(r):
                    tag_v[r, pl.ds(0, info.num_lanes)] = zeros + ((base + r) * TOP_K + j)

                pltpu.sync_copy(tag_v, tag_hbm.at[idx_v.at[j]])

    return scatter_rows(h2t, dest3)


def _moe_kernel(be_ref, nu_ref, nv_ref, xs_ref, xtag_ref, wgu_ref, wdn_ref, yt_hbm, wgu_bf, wdn_bf,
                ybuf, tag_v, tag_s, sem_tag, sem_rows):
    i = pl.program_id(0)
    n_steps = pl.num_programs(0)
    cur = i % 2
    tile_rows = SUBLANES
    n_real = yt_hbm.shape[0] - 2 * MOE_ROWS

    def row_copy(buf, r, tag):
        rows = pl.ds(pl.multiple_of(r * SUBLANES, SUBLANES), SUBLANES)
        return pltpu.make_async_copy(ybuf.at[buf, rows, :], yt_hbm.at[tag], sem_rows.at[buf])

    n_used = nu_ref[0]
    prv = 1 - cur

    def wait_sent(buf):
        pltpu.make_async_copy(ybuf.at[buf], ybuf.at[buf], sem_rows.at[buf]).wait()

    def send_prev():
        for r in range(MOE_ROWS):
            row_copy(prv, r, tag_s[prv, r]).start()

    @pl.when((i >= 2) & (i - 2 < n_used))
    def _():
        wait_sent(cur)

    i_blk = jnp.minimum(i, n_steps - 2)
    @pl.when((i < n_used) & ((i == 0) | (be_ref[i_blk] != be_ref[jnp.maximum(i_blk - 1, 0)])))
    def _():
        wgu_bf[...] = wgu_ref[0].astype(BF16)
        wdn_bf[...] = wdn_ref[0].astype(BF16)

    def compute():
        row = lax.broadcasted_iota(I32, (MOE_ROWS, LANES), 0)
        tags = jnp.where(row < nv_ref[i_blk], xtag_ref[...], n_real + cur * MOE_ROWS + row)
        tag_v[...] = tags.astype(F32).T[0:1, :].astype(I32)
        cp = pltpu.make_async_copy(tag_v, tag_s.at[pl.ds(cur, 1)], sem_tag)
        cp.start()
        xb = jnp.concatenate([xs_ref[pl.ds(s, MOE_ROWS, stride=tile_rows), :]
                              for s in range(SUBLANES)], axis=1).astype(BF16)
        gu = _dot(xb, wgu_bf[...])
        act = _silu(gu[:, :D_EXPERT]) * gu[:, D_EXPERT:]
        y = _dot(act.astype(BF16), wdn_bf[...])
        for s in range(SUBLANES):
            ybuf[cur, pl.ds(s, MOE_ROWS, stride=SUBLANES), :] = y[:, s * LANES:(s + 1) * LANES]
        cp.wait()

    @pl.when((i == 0) & (i < n_used))
    def _():
        compute()

    @pl.when((i > 0) & (i < n_used))
    def _():
        send_prev()
        compute()

    @pl.when((i > 0) & (i == n_used))
    def _():
        send_prev()

    @pl.when((i == n_steps - 1) & (i - 1 < n_used))
    def _():
        wait_sent(prv)


def _moe_call(block_expert, n_used, n_valid, xs, xtag, w_gu, w_dn, n_out_rows):
    tile_rows = SUBLANES
    P = xs.shape[0] // tile_rows
    D = SUBLANES * LANES
    nb = P // MOE_ROWS
    blk = lambda i, be, nu, nv: (jnp.minimum(i, nu[0] - 1), 0)
    wblk = lambda i, be, nu, nv: (be[jnp.minimum(i, nu[0] - 1)], 0, 0)
    return pl.pallas_call(
        _moe_kernel,
        grid_spec=pltpu.PrefetchScalarGridSpec(
            num_scalar_prefetch=3,
            grid=(nb + 1,),
            in_specs=[pl.BlockSpec((MOE_ROWS * tile_rows, LANES), blk),
                      pl.BlockSpec((MOE_ROWS, LANES), blk),
                      pl.BlockSpec((1, D, 2 * D_EXPERT), wblk),
                      pl.BlockSpec((1, D_EXPERT, D), wblk)],
            out_specs=pl.BlockSpec(memory_space=pl.ANY),
            scratch_shapes=[pltpu.VMEM((D, 2 * D_EXPERT), BF16), pltpu.VMEM((D_EXPERT, D), BF16),
                            pltpu.VMEM((2, MOE_ROWS * SUBLANES, LANES), F32),
                            pltpu.VMEM((1, MOE_ROWS), I32), pltpu.SMEM((2, MOE_ROWS), I32),
                            pltpu.SemaphoreType.DMA, pltpu.SemaphoreType.DMA((2,))]),
        out_shape=jax.ShapeDtypeStruct((n_out_rows, SUBLANES, LANES), F32),
        compiler_params=_cparams("arbitrary"),
        name="moe",
    )(block_expert, n_used, n_valid, xs, xtag, w_gu, w_dn)


def _combine_kernel(w_hbm, yt_ref, h_ref, x1_ref, mod_ref, wsg_ref, wsd_ref, o_ref,
                    wts, routed, sem_w):
    i = pl.program_id(0)
    tc = h_ref.shape[0]
    cp_w = pltpu.make_async_copy(w_hbm.at[i], wts, sem_w)
    cp_w.start()
    gu = _mm(h_ref[...], wsg_ref[...])
    act = _silu(gu[:, :D_EXPERT]) * gu[:, D_EXPERT:]
    ffn = _mm(act, wsd_ref[...])
    cp_w.wait()

    def wsum(tt, carry):
        for u in range(SUBLANES):
            t = tt * SUBLANES + u
            first = pl.multiple_of(t * (TOP_K * SUBLANES), SUBLANES)
            acc = yt_ref[pl.ds(first, SUBLANES), :] * wts[0, t]
            for j in range(1, TOP_K):
                acc = acc + yt_ref[pl.ds(first + j * SUBLANES, SUBLANES), :] * wts[j, t]
            routed[pl.ds(pl.multiple_of(t * SUBLANES, SUBLANES), SUBLANES), :] = acc
        return carry

    lax.fori_loop(0, tc // SUBLANES, wsum, 0)
    routed2d = jnp.concatenate([routed[pl.ds(s, tc, stride=SUBLANES), :] for s in range(SUBLANES)],
                               axis=1)
    g_f = mod_ref[0, 5:6, :]
    o_ref[...] = x1_ref[...] + g_f * (ffn + routed2d)


def _combine_call(w3, yt, h2, x1, mod3, w_sh_gu, w_sh_dn, tokens_per_batch, tc):
    N, D = h2.shape
    tiles_per_batch = tokens_per_batch // tc
    tok = pl.BlockSpec((tc, D), lambda i: (i, 0))
    wsg = w_sh_gu.astype(BF16)
    wsd = w_sh_dn.astype(BF16)
    return pl.pallas_call(
        _combine_kernel,
        grid=(N // tc,),
        in_specs=[pl.BlockSpec(memory_space=pl.ANY),
                  pl.BlockSpec((tc * TOP_K * SUBLANES, LANES), lambda i: (i, 0)),
                  tok, tok,
                  pl.BlockSpec((1, 6, D), lambda i: (i // tiles_per_batch, 0, 0)),
                  pl.BlockSpec(wsg.shape, lambda i: (0, 0)),
                  pl.BlockSpec(wsd.shape, lambda i: (0, 0))],
        out_specs=tok,
        out_shape=jax.ShapeDtypeStruct((N, D), F32),
        scratch_shapes=[pltpu.SMEM((TOP_K, tc), F32), pltpu.VMEM((tc * SUBLANES, LANES), F32),
                        pltpu.SemaphoreType.DMA],
        compiler_params=_cparams("arbitrary"),
        name="combine",
    )(w3, yt, h2, x1, mod3, wsg, wsd)


def _tile(n, pref):
    t = min(n, pref)
    assert n % t == 0, (n, t)
    return t


def _layer(x, mod3, positions, norm_mix, w_in, rwkv_mu, decay_w0, decay_up, iclr_a0, iclr_up,
           gate_up, rwkv_k_k, rwkv_k_a, rwkv_r_k, ln_x_w, ln_x_b, q_a_norm, w_q_b, kv_a_norm,
           w_kv_b, q_norm, k_norm, w_out, norm_ffn, w_router, router_bias, w_e_gate_up, w_e_down,
           w_sh_gate_up, w_sh_down):
    B, T, D = x.shape
    N = B * T
    assert T % SCAN_CHUNK == 0
    (r, lw, k2, v, kk, akk, g, bonus, q_pad, k_pad, v_pad) = _pre_call(
        x, mod3, positions, norm_mix, w_in, rwkv_mu, decay_w0, decay_up, iclr_a0, iclr_up,
        gate_up, rwkv_k_k, rwkv_k_a, rwkv_r_k, q_a_norm, w_q_b, kv_a_norm, w_kv_b, q_norm, k_norm,
        tm=_tile(T, 512))
    y = _scan_call(r, lw, k2, v, kk, akk)
    o_pad = _attn_call(q_pad, k_pad, v_pad)
    x1, h2, h2t = _post_call(y, bonus, g, o_pad, x, mod3, ln_x_w, ln_x_b, w_out, norm_ffn,
                             tm=_tile(T, 512))
    x1 = x1.reshape(N, D)
    h2 = h2.reshape(N, D)
    h2t = h2t.reshape(N, D // LANES, LANES)

    tr = _tile(N, 512)
    top_e, wts, rank, counts = _route_call(h2, w_router, router_bias, tr)
    counts = counts.reshape(N_EXPERTS)
    padded = (counts + MOE_ROWS - 1) // MOE_ROWS * MOE_ROWS
    pad_ends = jnp.cumsum(padded)
    pad_starts = pad_ends - padded
    n_blocks = (N * TOP_K + N_EXPERTS * (MOE_ROWS - 1)) // MOE_ROWS
    block_expert = jnp.minimum(
        jnp.searchsorted(pad_ends, jnp.arange(n_blocks, dtype=I32) * MOE_ROWS, side='right'),
        N_EXPERTS - 1).astype(I32)
    n_used = (pad_ends[-1:] // MOE_ROWS).astype(I32)
    block_row0 = jnp.arange(n_blocks + 1, dtype=I32) * MOE_ROWS
    be_pad = jnp.concatenate([block_expert, block_expert[-1:]])
    n_valid = jnp.clip((pad_starts + counts)[be_pad] - block_row0, 0, MOE_ROWS).astype(I32)
    dest = _dest_call(top_e, rank, pad_starts.astype(I32), tr)

    dest3 = dest.reshape(TOP_K, N // SC_CHUNK, SC_CHUNK).transpose(1, 0, 2)
    n_rows = n_blocks * MOE_ROWS
    xs, xtag = _sc_dispatch_call(h2t, dest3, n_rows)
    yt = _moe_call(block_expert, n_used, n_valid, xs.reshape(n_rows * SUBLANES, LANES), xtag,
                   w_e_gate_up, w_e_down, N * TOP_K + 2 * MOE_ROWS)
    tc = _tile(T, COMBINE_TILE)
    w3 = wts.reshape(TOP_K, N // tc, tc).transpose(1, 0, 2)
    out = _combine_call(w3, yt.reshape(-1, LANES), h2, x1, mod3, w_sh_gate_up,
                        w_sh_down, T, tc)
    return out.reshape(B, T, D)


def kernel(x, c, positions, ada_w, ada_b, norm_mix, w_in, rwkv_mu, decay_w0, decay_up, iclr_a0, iclr_up, gate_up, rwkv_k_k, rwkv_k_a, rwkv_r_k, ln_x_w, ln_x_b, q_a_norm, w_q_b, kv_a_norm, w_kv_b, q_norm, k_norm, w_out, norm_ffn, w_router, router_bias, w_e_gate_up, w_e_down, w_sh_gate_up, w_sh_down):
    B, T, D = x.shape
    depth = ada_w.shape[0]
    for l in range(depth):
        mod3 = _mod_call(c, ada_w[l], ada_b[l]).reshape(B, 6, D)
        x = _layer(x, mod3, positions, norm_mix[l], w_in[l], rwkv_mu[l], decay_w0[l], decay_up[l],
                   iclr_a0[l], iclr_up[l], gate_up[l], rwkv_k_k[l], rwkv_k_a[l], rwkv_r_k[l],
                   ln_x_w[l], ln_x_b[l], q_a_norm[l], w_q_b[l], kv_a_norm[l], w_kv_b[l],
                   q_norm[l], k_norm[l], w_out[l], norm_ffn[l], w_router[l], router_bias[l],
                   w_e_gate_up[l], w_e_down[l], w_sh_gate_up[l], w_sh_down[l])
    return x
```

```python
import functools
import math

import jax
import jax.numpy as jnp
import numpy as np
from jax import lax
from jax.experimental import pallas as pl
from jax.experimental.pallas import tpu as pltpu
from jax.experimental.pallas import tpu_sc as plsc

F32 = jnp.float32
BF16 = jnp.bfloat16
I32 = jnp.int32

NORM_EPS = 1e-6
GN_EPS = 64e-5
RWKV_HEADS = 8
RWKV_HEAD_DIM = 64
D_RWKV = 512
DECAY_LORA = 32
ICLR_LORA = 32
GATE_LORA = 96
MLA_HEADS = 8
QK_NOPE_DIM = 64
QK_ROPE_DIM = 32
QK_HEAD_DIM = 96
V_HEAD_DIM = 64
Q_LORA_RANK = 256
KV_LORA_RANK = 128
ROPE_THETA = 10000.0
N_EXPERTS = 256
TOP_K = 8
N_GROUPS = 8
TOPK_GROUPS = 4
GROUP_SIZE = N_EXPERTS // N_GROUPS
D_EXPERT = 256
ROUTED_SCALE = 2.5
MOE_ROWS = 512
SC_CHUNK = 64
COMBINE_TILE = 256

LANES = 128
SUBLANES = 8
HEAD_PAD = 128
VMEM_LIMIT = 56 * 1024 * 1024

PRE_SUBTILES = 1
SCAN_CHUNK = 64
SCAN_BLOCK = 512
ATTN_TILE = 512
ATTN_Q_SCALE = QK_HEAD_DIM ** -0.5 * math.log2(math.e)
NEG_INF = float("-inf")


def _cparams(*sem):
    return pltpu.CompilerParams(dimension_semantics=sem, vmem_limit_bytes=VMEM_LIMIT)


def _split2(a):
    hi = a.astype(BF16)
    lo = (a - hi.astype(F32)).astype(BF16)
    return hi, lo


def _split3(a):
    hi = a.astype(BF16)
    r1 = a - hi.astype(F32)
    mid = r1.astype(BF16)
    lo = (r1 - mid.astype(F32)).astype(BF16)
    return hi, mid, lo


def _dot(a, b, dims=None):
    if dims is None:
        return jnp.dot(a, b, preferred_element_type=F32)
    return lax.dot_general(a, b, (dims, ((), ())), preferred_element_type=F32)


def _mm(a, b, dims=None):
    return _dot(a.astype(BF16), b.astype(BF16), dims)


def _mm3(a, b, dims=None):
    ah, al = _split2(a)
    bh, bl = _split2(b)
    return _dot(ah, bh, dims) + (_dot(ah, bl, dims) + _dot(al, bh, dims))


def _mm_exact_rhs(a, b_exact_bf16, dims=None):
    h, m, l = _split3(a)
    return _dot(h, b_exact_bf16, dims) + (_dot(m, b_exact_bf16, dims) + _dot(l, b_exact_bf16, dims))


NT = ((1,), (1,))
TN = ((0,), (0,))


def _sigmoid(z):
    return 1.0 / (1.0 + jnp.exp(-z))


def _silu(z):
    return z * _sigmoid(z)


def _seg_ones(width, seg):
    r = lax.broadcasted_iota(I32, (width, width), 0) // seg
    c = lax.broadcasted_iota(I32, (width, width), 1) // seg
    return jnp.where(r == c, 1.0, 0.0).astype(BF16)


def _segsum(a, ones_bd):
    hi, lo = _split2(a)
    return _dot(hi, ones_bd) + _dot(lo, ones_bd)


def _mod_kernel(c_ref, w_ref, b_ref, o_ref):
    ca = _silu(c_ref[...])
    o_ref[...] = _mm3(ca, w_ref[...]) + b_ref[...]


def _mod_call(c, ada_w, ada_b):
    B, D = c.shape
    n6 = ada_w.shape[1]
    tn = D
    return pl.pallas_call(
        _mod_kernel,
        grid=(n6 // tn,),
        in_specs=[pl.BlockSpec((B, D), lambda j: (0, 0)),
                  pl.BlockSpec((D, tn), lambda j: (0, j)),
                  pl.BlockSpec((1, tn), lambda j: (0, j))],
        out_specs=pl.BlockSpec((B, tn), lambda j: (0, j)),
        out_shape=jax.ShapeDtypeStruct((B, n6), F32),
        compiler_params=_cparams("arbitrary"),
        name="mod",
    )(c, ada_w, ada_b.reshape(1, n6))


def _pre_kernel(x_ref, mod_ref, pos_ref, nmix_ref, wrkv_ref, wlora_ref, wmla_ref,
                mu_rkv_ref, mu_lora_ref, wup_ref, w0_ref, a0_ref, kk_ref, ka_ref, rk_ref,
                qan_ref, wqb_ref, kvan_ref, wkb_ref, wvb_ref, qn_ref, kn_ref, invf_ref,
                r_ref, lw_ref, k_ref, v_ref, kkn_ref, akk_ref, g_ref, bonus_ref,
                q_ref, kout_ref, vout_ref,
                carry_rkv, carry_lora):
    ti = pl.program_id(1)
    tm = x_ref.shape[1]

    @pl.when(ti == 0)
    def _():
        carry_rkv[...] = jnp.zeros_like(carry_rkv)
        carry_lora[...] = jnp.zeros_like(carry_lora)

    ts = tm // PRE_SUBTILES
    sh_a = mod_ref[0, 0:1, :]
    sc_a = mod_ref[0, 1:2, :]
    ones64 = _seg_ones(D_RWKV, RWKV_HEAD_DIM)
    row0 = lax.broadcasted_iota(I32, (ts, 1), 0) == 0
    half = QK_ROPE_DIM // 2
    last_rows = {}

    def sub_tile(s):
        rows = slice(s * ts, (s + 1) * ts)
        xb = x_ref[0, rows, :]
        ms = jnp.mean(xb * xb, axis=-1, keepdims=True)
        h = xb * lax.rsqrt(ms + NORM_EPS) * nmix_ref[...] * (1.0 + sc_a) + sh_a
        hb = h.astype(BF16)
        yield
        u_rkv = _dot(hb, wrkv_ref[...])
        u_lora = _dot(hb, wlora_ref[...])
        u_mla = _dot(hb, wmla_ref[...])
        last_rows[s] = (u_rkv[ts - 1:ts, :], u_lora[ts - 1:ts, :])
        yield

        before_rkv, before_lora = (carry_rkv[...], carry_lora[...]) if s == 0 else last_rows[s - 1]
        prev_rkv = jnp.where(row0, before_rkv, pltpu.roll(u_rkv, 1, 0))
        prev_lora = jnp.where(row0, before_lora, pltpu.roll(u_lora, 1, 0))
        if s == PRE_SUBTILES - 1:
            carry_rkv[...] = u_rkv[ts - 1:ts, :]
            carry_lora[...] = u_lora[ts - 1:ts, :]
        us = u_rkv + (prev_rkv - u_rkv) * mu_rkv_ref[...]
        ul = u_lora + (prev_lora - u_lora) * mu_lora_ref[...]
        r = us[:, 0:D_RWKV]
        k = us[:, D_RWKV:2 * D_RWKV]
        v = us[:, 2 * D_RWKV:3 * D_RWKV]
        lane_l = lax.broadcasted_iota(I32, ul.shape, 1)
        t_in = jnp.where(lane_l < DECAY_LORA, jnp.tanh(ul),
                         jnp.where(lane_l < DECAY_LORA + ICLR_LORA, ul, _sigmoid(ul)))
        yield
        up = _mm(t_in, wup_ref[...])
        yield
        z = w0_ref[...] + up[:, 0:D_RWKV]
        lw = (-math.exp(-0.5)) * _sigmoid(z)
        a = _sigmoid(a0_ref[...] + up[:, D_RWKV:2 * D_RWKV])
        g = up[:, 2 * D_RWKV:3 * D_RWKV]
        kk = k * kk_ref[...]
        k2 = k * (1.0 + (a - 1.0) * ka_ref[...])
        yield
        ss = _segsum(kk * kk, ones64)
        bonus_sum = _segsum(r * k2 * rk_ref[...], ones64)
        yield
        kk = kk * lax.rsqrt(jnp.maximum(ss, 1e-24))
        r_ref[0, rows, :] = r
        lw_ref[0, rows, :] = lw
        k_ref[0, rows, :] = k2
        v_ref[0, rows, :] = v
        kkn_ref[0, rows, :] = kk
        akk_ref[0, rows, :] = a * kk
        g_ref[0, rows, :] = g
        bonus_ref[0, rows, :] = bonus_sum * v
        yield

        q_lat = u_mla[:, 0:Q_LORA_RANK]
        kv_lat = u_mla[:, Q_LORA_RANK:Q_LORA_RANK + KV_LORA_RANK]
        kpe_tile = u_mla[:, Q_LORA_RANK + KV_LORA_RANK:]
        qn = q_lat * lax.rsqrt(jnp.mean(q_lat * q_lat, axis=-1, keepdims=True) + NORM_EPS) * qan_ref[...]
        kvn = kv_lat * lax.rsqrt(jnp.mean(kv_lat * kv_lat, axis=-1, keepdims=True) + NORM_EPS) * kvan_ref[...]
        kvb = kvn.astype(BF16)
        yield
        q_raw = _mm(qn, wqb_ref[...])
        k_raw = _dot(kvb, wkb_ref[...])
        v_pad = _dot(kvb, wvb_ref[...])
        yield
        kpe_h = pltpu.roll(kpe_tile, QK_NOPE_DIM, 1)
        cos_t, s1, s2 = rope_tables(s)

        def tables(gain, scale):
            g_s = gain * scale
            return (cos_t * g_s, s1 * pltpu.roll(g_s, HEAD_PAD - half, 1),
                    s2 * pltpu.roll(g_s, half, 1))

        def norm_rope(xh, tabs):
            c_g, s1_g, s2_g = tabs
            ssq = jnp.sum(xh * xh, axis=-1, keepdims=True) * (1.0 / QK_HEAD_DIM)
            rot = xh * c_g + pltpu.roll(xh, HEAD_PAD - half, 1) * s1_g + pltpu.roll(xh, half, 1) * s2_g
            return rot * lax.rsqrt(ssq + NORM_EPS)

        q_tabs = tables(qn_ref[...], ATTN_Q_SCALE)
        k_tabs = tables(kn_ref[...], 1.0)
        vout_ref[0, rows, :] = v_pad.astype(BF16)
        yield
        for hh in range(MLA_HEADS):
            sl = slice(hh * HEAD_PAD, (hh + 1) * HEAD_PAD)
            q_ref[0, rows, sl] = norm_rope(q_raw[:, sl], q_tabs).astype(BF16)
            kout_ref[0, rows, sl] = norm_rope(k_raw[:, sl] + kpe_h, k_tabs).astype(BF16)
            yield

    def rope_tables(s):
        ang_t = invf_ref[...] * pos_ref[0, :, s * ts:(s + 1) * ts].astype(F32)
        frow = lax.broadcasted_iota(I32, (half, HEAD_PAD), 0)
        flane = lax.broadcasted_iota(I32, (half, HEAD_PAD), 1)
        at_x1 = flane == frow + QK_NOPE_DIM
        at_x2 = flane == frow + QK_NOPE_DIM + half
        e_cos = jnp.where(at_x1 | at_x2, 1.0, 0.0).astype(BF16)
        e_sin = jnp.concatenate([jnp.where(at_x1, -1.0, 0.0), jnp.where(at_x2, 1.0, 0.0)],
                                axis=1).astype(BF16)
        lane = lax.broadcasted_iota(I32, (1, HEAD_PAD), 1)
        off_rope = jnp.where((lane >= QK_NOPE_DIM) & (lane < QK_HEAD_DIM), 0.0, 1.0)
        cos_t = _mm_exact_rhs(jnp.cos(ang_t), e_cos, TN) + off_rope
        sin2 = _mm_exact_rhs(jnp.sin(ang_t), e_sin, TN)
        return cos_t, sin2[:, :HEAD_PAD], sin2[:, HEAD_PAD:]

    live = [sub_tile(s) for s in range(PRE_SUBTILES)]
    while live:
        live = [gen for gen in live if next(gen, "done") != "done"]


def _pad_heads(w, n_heads, width):
    kdim = w.shape[0]
    w = w.reshape(kdim, n_heads, width)
    w = jnp.pad(w, ((0, 0), (0, 0), (0, HEAD_PAD - width)))
    return w.reshape(kdim, n_heads * HEAD_PAD)


def _pre_call(x, mod3, positions, norm_mix, w_in, rwkv_mu, decay_w0, decay_up, iclr_a0, iclr_up,
              gate_up, k_k, k_a, r_k, q_a_norm, w_q_b, kv_a_norm, w_kv_b, q_norm, k_norm, tm):
    B, T, D = x.shape
    n_rkv = 3 * D_RWKV
    n_lora = DECAY_LORA + ICLR_LORA + GATE_LORA
    LORA_PAD = 256
    MLA_PAD = 512
    n_mla = Q_LORA_RANK + KV_LORA_RANK + QK_ROPE_DIM
    w_rkv = w_in[:, :n_rkv].astype(BF16)
    w_lora = jnp.pad(w_in[:, n_rkv:n_rkv + n_lora], ((0, 0), (0, LORA_PAD - n_lora))).astype(BF16)
    w_mla = jnp.pad(w_in[:, n_rkv + n_lora:], ((0, 0), (0, MLA_PAD - n_mla))).astype(BF16)
    mu_rkv = rwkv_mu[:n_rkv].reshape(1, n_rkv)
    mu_lora = jnp.pad(rwkv_mu[n_rkv:], (0, LORA_PAD - n_lora)).reshape(1, LORA_PAD)
    w_up = jnp.zeros((LORA_PAD, n_rkv), F32)
    w_up = w_up.at[0:DECAY_LORA, 0:D_RWKV].set(decay_up)
    w_up = w_up.at[DECAY_LORA:DECAY_LORA + ICLR_LORA, D_RWKV:2 * D_RWKV].set(iclr_up)
    w_up = w_up.at[DECAY_LORA + ICLR_LORA:n_lora, 2 * D_RWKV:].set(gate_up)
    w_up = w_up.astype(BF16)
    w_qb = _pad_heads(w_q_b, MLA_HEADS, QK_HEAD_DIM).astype(BF16)
    w_kv3 = w_kv_b.reshape(KV_LORA_RANK, MLA_HEADS, QK_NOPE_DIM + V_HEAD_DIM)
    w_kb = _pad_heads(w_kv3[:, :, :QK_NOPE_DIM].reshape(KV_LORA_RANK, -1), MLA_HEADS, QK_NOPE_DIM).astype(BF16)
    w_vb = _pad_heads(w_kv3[:, :, QK_NOPE_DIM:].reshape(KV_LORA_RANK, -1), MLA_HEADS, V_HEAD_DIM).astype(BF16)
    qn_pad = jnp.pad(q_norm, (0, HEAD_PAD - QK_HEAD_DIM)).reshape(1, HEAD_PAD)
    kn_pad = jnp.pad(k_norm, (0, HEAD_PAD - QK_HEAD_DIM)).reshape(1, HEAD_PAD)
    inv_freq = ROPE_THETA ** (-jnp.arange(0, QK_ROPE_DIM, 2, dtype=F32) / QK_ROPE_DIM)
    invf = inv_freq.reshape(QK_ROPE_DIM // 2, 1)
    pos3 = positions.reshape(B, 1, T)
    HP = MLA_HEADS * HEAD_PAD

    row = lambda n: pl.BlockSpec((1, n), lambda b, t: (0, 0))
    full = lambda a: pl.BlockSpec(a.shape, lambda b, t: (0,) * a.ndim)
    tok = lambda n: pl.BlockSpec((1, tm, n), lambda b, t: (b, t, 0))
    outs = ([jax.ShapeDtypeStruct((B, T, D_RWKV), F32)] * 8
            + [jax.ShapeDtypeStruct((B, T, HP), BF16)] * 3)
    return pl.pallas_call(
        _pre_kernel,
        grid=(B, T // tm),
        in_specs=[tok(D),
                  pl.BlockSpec((1, 6, D), lambda b, t: (b, 0, 0)),
                  pl.BlockSpec((1, 1, tm), lambda b, t: (b, 0, t)),
                  row(D), full(w_rkv), full(w_lora), full(w_mla),
                  row(n_rkv), row(LORA_PAD), full(w_up), row(D_RWKV), row(D_RWKV),
                  row(D_RWKV), row(D_RWKV), row(D_RWKV),
                  row(Q_LORA_RANK), full(w_qb), row(KV_LORA_RANK), full(w_kb), full(w_vb),
                  row(HEAD_PAD), row(HEAD_PAD), full(invf)],
        out_specs=[tok(D_RWKV)] * 8 + [tok(HP)] * 3,
        out_shape=outs,
        scratch_shapes=[pltpu.VMEM((1, n_rkv), F32), pltpu.VMEM((1, LORA_PAD), F32)],
        compiler_params=_cparams("arbitrary", "arbitrary"),
        name="pre",
    )(x, mod3, pos3, norm_mix.reshape(1, D), w_rkv, w_lora, w_mla, mu_rkv, mu_lora, w_up,
      decay_w0.reshape(1, -1), iclr_a0.reshape(1, -1), k_k.reshape(1, -1), k_a.reshape(1, -1),
      r_k.reshape(1, -1), q_a_norm.reshape(1, -1), w_qb, kv_a_norm.reshape(1, -1), w_kb, w_vb,
      qn_pad, kn_pad, invf)


def _scan_kernel(r_ref, lw_ref, k_ref, v_ref, kk_ref, akk_ref, y_ref, state):
    C = SCAN_CHUNK
    n_chunks = r_ref.shape[1] // C
    n_pairs = r_ref.shape[2] // LANES

    @pl.when(pl.program_id(1) == 0)
    def _():
        state[...] = jnp.zeros_like(state)

    ri = lax.broadcasted_iota(I32, (C, C), 0)
    ci = lax.broadcasted_iota(I32, (C, C), 1)
    tri_incl = jnp.where(ci <= ri, 1.0, 0.0).astype(BF16)
    r2 = lax.broadcasted_iota(I32, (2 * C, 2 * C), 0)
    c2 = lax.broadcasted_iota(I32, (2 * C, 2 * C), 1)
    same = (r2 >= C) == (c2 >= C)
    strict = same & (c2 < r2)
    incl = same & (c2 <= r2)
    eye = jnp.where(c2 == r2, 1.0, 0.0)
    head0 = lax.broadcasted_iota(I32, (C, LANES), 1) < RWKV_HEAD_DIM

    def stack2(a):
        return jnp.concatenate([jnp.where(head0, a, 0.0), jnp.where(head0, 0.0, a)], axis=0)

    C2 = 2 * C
    cat0 = lambda *a: jnp.concatenate(a, axis=0)
    cat1 = lambda *a: jnp.concatenate(a, axis=1)

    items = []
    for c in range(n_chunks):
        rows = slice(c * C, (c + 1) * C)
        lw = lw_ref[0, rows, :]
        cum = _mm_exact_rhs_left(tri_incl, lw)
        cum_end = cum[C - 1:C, :]
        w_end = jnp.exp(cum_end)
        e_pos = jnp.exp(cum)
        e_neg = jnp.exp(-cum)
        e_prev = jnp.exp(cum - lw)
        e_end = jnp.exp(cum_end - cum)
        kk = kk_ref[0, rows, :]
        k2 = k_ref[0, rows, :]
        pneg = -akk_ref[0, rows, :]
        vv = v_ref[0, rows, :]
        rt = r_ref[0, rows, :] * e_pos
        bt = kk * e_prev
        pt = pneg * e_neg
        kt = k2 * e_neg
        ph = pneg * e_end
        kh = k2 * e_end
        for pp in range(n_pairs):
            sl = slice(pp * LANES, (pp + 1) * LANES)
            items.append(dict(
                c=c, p=pp, w_end=w_end[:, sl],
                bt2=stack2(bt[:, sl]).astype(BF16), rt2=stack2(rt[:, sl]).astype(BF16),
                pk2=cat0(stack2(pt[:, sl]), stack2(kt[:, sl])).astype(BF16),
                phkh2=cat0(stack2(ph[:, sl]), stack2(kh[:, sl])).astype(BF16),
                v2=stack2(vv[:, sl])))
    for it in items:
        ab = _dot(cat0(it['bt2'], it['rt2']), it['pk2'], NT)
        it['a_ab'] = jnp.where(strict, ab[:C2, :C2], 0.0)
        it['a_ak'] = jnp.where(strict, ab[:C2, C2:], 0.0).astype(BF16)
        it['b_rpk'] = cat1(jnp.where(incl, ab[C2:, :C2], 0.0), jnp.where(incl, ab[C2:, C2:], 0.0)).astype(BF16)
        it['tinv'] = eye + it['a_ab']
    for it in items:
        it['apow'] = _mm(it['a_ab'], it['a_ab'])
    for _ in range(int(math.log2(C)) - 1):
        for it in items:
            both = _mm(cat0(it['apow'], it['tinv']), it['apow'])
            it['apow'] = both[:C2]
            it['tinv'] = it['tinv'] + both[C2:]
    for it in items:
        it['akv'] = _dot(it['a_ak'], it['v2'].astype(BF16))
    for it in items:
        tt = _dot(it['tinv'].astype(BF16), cat1(it['bt2'], it['akv'].astype(BF16)))
        it['tb_rt'] = cat0(tt[:, :LANES].astype(BF16), it['rt2'])
        it['tav'] = tt[:, LANES:]
    for it in items:
        pp = it['p']
        s0 = state[pp]
        top = _dot(it['tb_rt'], s0.astype(BF16), NT)
        u2 = top[:C2] + it['tav']
        uv = cat0(u2, it['v2']).astype(BF16)
        y2 = top[C2:] + _dot(it['b_rpk'], uv)
        state[pp] = s0 * it['w_end'] + _dot(uv, it['phkh2'], TN)
        y_ref[0, it['c'] * C:(it['c'] + 1) * C, pp * LANES:(pp + 1) * LANES] = y2[0:C] + y2[C:C2]


def _mm_exact_rhs_left(b_exact_bf16, a):
    h, m, l = _split3(a)
    return _dot(b_exact_bf16, h) + (_dot(b_exact_bf16, m) + _dot(b_exact_bf16, l))


def _scan_call(r, lw, k2, v, kk, akk):
    B, T, W = r.shape
    tb = _tile(T, SCAN_BLOCK)
    spec = pl.BlockSpec((1, tb, W), lambda b, c: (b, c, 0))
    return pl.pallas_call(
        _scan_kernel,
        grid=(B, T // tb),
        in_specs=[spec] * 6,
        out_specs=spec,
        out_shape=jax.ShapeDtypeStruct((B, T, W), F32),
        scratch_shapes=[pltpu.VMEM((W // LANES, 2 * RWKV_HEAD_DIM, LANES), F32)],
        compiler_params=_cparams("arbitrary", "arbitrary"),
        name="scan",
    )(r, lw, k2, v, kk, akk)


def _attn_kernel(q_ref, k_ref, v_ref, o_ref):
    T = q_ref.shape[1]
    tq = min(T, ATTN_TILE)
    row = lax.broadcasted_iota(I32, (tq, tq), 0)
    col = lax.broadcasted_iota(I32, (tq, tq), 1)
    causal = col <= row

    def update(q, kt, vt, carry, mask):
        m_old, l_old, acc = carry
        s = _dot(q, kt, NT)
        if mask:
            s = jnp.where(causal, s, NEG_INF)
        m_new = jnp.maximum(m_old, jnp.max(s, axis=-1, keepdims=True))
        alpha = jnp.exp2(m_old - m_new)
        p = jnp.exp2(s - m_new)
        l_new = alpha * l_old + jnp.sum(p, axis=-1, keepdims=True)
        acc = alpha * acc + _dot(p.astype(BF16), vt)
        return m_new, l_new, acc

    for qi in range(T // tq):
        q = q_ref[0, qi * tq:(qi + 1) * tq, :]
        carry = (jnp.full((tq, 1), NEG_INF, F32), jnp.zeros((tq, 1), F32),
                 jnp.zeros((tq, HEAD_PAD), F32))

        def body(ki, carry, q=q):
            rows = pl.ds(pl.multiple_of(ki * tq, tq), tq)
            return update(q, k_ref[0, rows, :], v_ref[0, rows, :], carry, False)

        carry = lax.fori_loop(0, qi, body, carry, unroll=True)
        diag = slice(qi * tq, (qi + 1) * tq)
        _, l_fin, acc = update(q, k_ref[0, diag, :], v_ref[0, diag, :], carry, True)
        o_ref[0, diag, :] = (acc / l_fin).astype(o_ref.dtype)


def _attn_call(q, k, v):
    B, T, HP = q.shape
    spec = pl.BlockSpec((1, T, HEAD_PAD), lambda b, h: (b, 0, h))
    return pl.pallas_call(
        _attn_kernel,
        grid=(B, MLA_HEADS),
        in_specs=[spec, spec, spec],
        out_specs=spec,
        out_shape=jax.ShapeDtypeStruct((B, T, HP), BF16),
        compiler_params=_cparams("arbitrary", "arbitrary"),
        name="attn",
    )(q, k, v)


def _post_kernel(y_ref, bonus_ref, g_ref, o_ref, x_ref, mod_ref, lnw_ref, lnb_ref,
                 wo_r_ref, wo_m_ref, nffn_ref, x1_ref, h2_ref, h2t_ref):
    y = y_ref[0]
    ones64 = _seg_ones(D_RWKV, RWKV_HEAD_DIM)
    mean = _segsum(y, ones64) * (1.0 / RWKV_HEAD_DIM)
    yc = y - mean
    var = _segsum(yc * yc, ones64) * (1.0 / RWKV_HEAD_DIM)
    yn = yc * lax.rsqrt(var + GN_EPS) * lnw_ref[...] + lnb_ref[...]
    yr = (yn + bonus_ref[0]) * g_ref[0]
    mix = _mm(yr, wo_r_ref[...]) + _dot(o_ref[0], wo_m_ref[...])
    g_a = mod_ref[0, 2:3, :]
    sh_f = mod_ref[0, 3:4, :]
    sc_f = mod_ref[0, 4:5, :]
    x1 = x_ref[0] + g_a * mix
    x1_ref[0] = x1
    ms = jnp.mean(x1 * x1, axis=-1, keepdims=True)
    h2 = x1 * lax.rsqrt(ms + NORM_EPS) * nffn_ref[...] * (1.0 + sc_f) + sh_f
    h2_ref[0] = h2
    tm = h2.shape[0]
    for s in range(SUBLANES):
        h2t_ref[0, pl.ds(s, tm, stride=SUBLANES), :] = h2[:, s * LANES:(s + 1) * LANES]


def _post_call(y, bonus, g, o_pad, x, mod3, ln_w, ln_b, w_out, norm_ffn, tm):
    B, T, D = x.shape
    HP = MLA_HEADS * HEAD_PAD
    wo_r = w_out[:D_RWKV].astype(BF16)
    wo_m = jnp.pad(w_out[D_RWKV:].reshape(MLA_HEADS, V_HEAD_DIM, D),
                   ((0, 0), (0, HEAD_PAD - V_HEAD_DIM), (0, 0))).reshape(HP, D).astype(BF16)
    tok = lambda n: pl.BlockSpec((1, tm, n), lambda b, t: (b, t, 0))
    row = lambda n: pl.BlockSpec((1, n), lambda b, t: (0, 0))
    full = lambda a: pl.BlockSpec(a.shape, lambda b, t: (0,) * a.ndim)
    return pl.pallas_call(
        _post_kernel,
        grid=(B, T // tm),
        in_specs=[tok(D_RWKV), tok(D_RWKV), tok(D_RWKV), tok(HP), tok(D),
                  pl.BlockSpec((1, 6, D), lambda b, t: (b, 0, 0)),
                  row(D_RWKV), row(D_RWKV), full(wo_r), full(wo_m), row(D)],
        out_specs=[tok(D), tok(D),
                   pl.BlockSpec((1, tm * SUBLANES, LANES), lambda b, t: (b, t, 0))],
        out_shape=[jax.ShapeDtypeStruct((B, T, D), F32)] * 2
        + [jax.ShapeDtypeStruct((B, T * SUBLANES, LANES), F32)],
        compiler_params=_cparams("arbitrary", "arbitrary"),
        name="post",
    )(y, bonus, g, o_pad, x, mod3, ln_w.reshape(1, -1), ln_b.reshape(1, -1), wo_r, wo_m,
      norm_ffn.reshape(1, D))


def _first_index(mask, iota, size, axis):
    return jnp.min(jnp.where(mask, iota, size), axis=axis, keepdims=True)


def _route_kernel(h_ref, wr_ref, bias_ref, e_ref, w_ref, rank_ref, cnt_ref, base):
    tr = h_ref.shape[0]
    E = N_EXPERTS

    @pl.when(pl.program_id(0) == 0)
    def _():
        base[...] = jnp.zeros_like(base)

    logits = _mm3(wr_ref[...], h_ref[...], NT)
    scores = _sigmoid(logits)
    sel = scores + bias_ref[...]
    iota_g = lax.broadcasted_iota(I32, (GROUP_SIZE, tr), 0)
    gs_rows = []
    for gi in range(N_GROUPS):
        blk = sel[gi * GROUP_SIZE:(gi + 1) * GROUP_SIZE, :]
        m1 = jnp.max(blk, axis=0, keepdims=True)
        i1 = _first_index(blk == m1, iota_g, GROUP_SIZE, 0)
        m2 = jnp.max(jnp.where(iota_g == i1, NEG_INF, blk), axis=0, keepdims=True)
        gs_rows.append(m1 + m2)
    gs = jnp.concatenate(gs_rows, axis=0)
    iota8 = lax.broadcasted_iota(I32, (N_GROUPS, tr), 0)
    gmask = jnp.zeros((N_GROUPS, tr), jnp.bool_)
    for _ in range(TOPK_GROUPS):
        mg = jnp.max(gs, axis=0, keepdims=True)
        ig = _first_index(gs == mg, iota8, N_GROUPS, 0)
        hit = iota8 == ig
        gmask = gmask | hit
        gs = jnp.where(hit, NEG_INF, gs)
    msel = jnp.concatenate(
        [jnp.where(gmask[gi:gi + 1, :], sel[gi * GROUP_SIZE:(gi + 1) * GROUP_SIZE, :], NEG_INF)
         for gi in range(N_GROUPS)], axis=0)
    iota_e = lax.broadcasted_iota(I32, (E, tr), 0)
    e_rows, w_rows = [], []
    onehot = jnp.zeros((E, tr), F32)
    for _ in range(TOP_K):
        mv = jnp.max(msel, axis=0, keepdims=True)
        ie = _first_index(msel == mv, iota_e, E, 0)
        hit = iota_e == ie
        e_rows.append(ie)
        w_rows.append(jnp.sum(jnp.where(hit, scores, 0.0), axis=0, keepdims=True))
        onehot = jnp.where(hit, 1.0, onehot)
        msel = jnp.where(hit, NEG_INF, msel)
    top_e = jnp.concatenate(e_rows, axis=0)
    wts = jnp.concatenate(w_rows, axis=0)
    wts = wts / jnp.sum(wts, axis=0, keepdims=True) * ROUTED_SCALE
    ti = lax.broadcasted_iota(I32, (tr, tr), 0)
    tj = lax.broadcasted_iota(I32, (tr, tr), 1)
    upper = jnp.where(ti < tj, 1.0, 0.0).astype(BF16)
    pos = _dot(onehot.astype(BF16), upper) + base[...]
    rank_rows = [jnp.sum(jnp.where(iota_e == e_rows[j], pos, 0.0), axis=0, keepdims=True)
                 for j in range(TOP_K)]
    base[...] = base[...] + jnp.sum(onehot, axis=1, keepdims=True)
    e_ref[...] = top_e
    w_ref[...] = wts
    rank_ref[...] = jnp.concatenate(rank_rows, axis=0).astype(I32)
    cnt_ref[...] = base[...].astype(I32)


def _route_call(h2, w_router, router_bias, tr):
    N, D = h2.shape
    E = N_EXPERTS
    out_kn = pl.BlockSpec((TOP_K, tr), lambda i: (0, i))
    return pl.pallas_call(
        _route_kernel,
        grid=(N // tr,),
        in_specs=[pl.BlockSpec((tr, D), lambda i: (i, 0)),
                  pl.BlockSpec((E, D), lambda i: (0, 0)),
                  pl.BlockSpec((E, 1), lambda i: (0, 0))],
        out_specs=[out_kn, out_kn, out_kn, pl.BlockSpec((E, 1), lambda i: (0, 0))],
        out_shape=[jax.ShapeDtypeStruct((TOP_K, N), I32), jax.ShapeDtypeStruct((TOP_K, N), F32),
                   jax.ShapeDtypeStruct((TOP_K, N), I32), jax.ShapeDtypeStruct((E, 1), I32)],
        scratch_shapes=[pltpu.VMEM((E, 1), F32)],
        compiler_params=_cparams("arbitrary"),
        name="route",
    )(h2, w_router.T, router_bias.reshape(E, 1))


def _dest_kernel(e_ref, rank_ref, start_ref, d_ref):
    tr = e_ref.shape[1]
    iota_e = lax.broadcasted_iota(I32, (N_EXPERTS, tr), 0)
    starts = start_ref[...]
    rows = [jnp.sum(jnp.where(iota_e == e_ref[j:j + 1, :], starts, 0), axis=0, keepdims=True)
            for j in range(TOP_K)]
    d_ref[...] = jnp.concatenate(rows, axis=0) + rank_ref[...]


def _dest_call(top_e, rank, pad_starts, tr):
    K, N = top_e.shape
    spec = pl.BlockSpec((K, tr), lambda i: (0, i))
    return pl.pallas_call(
        _dest_kernel,
        grid=(N // tr,),
        in_specs=[spec, spec, pl.BlockSpec((N_EXPERTS, 1), lambda i: (0, 0))],
        out_specs=spec,
        out_shape=jax.ShapeDtypeStruct((K, N), I32),
        compiler_params=_cparams("arbitrary"),
        name="dest",
    )(top_e, rank, pad_starts.reshape(N_EXPERTS, 1))


def _sc_dispatch_call(h2t, dest3, n_rows):
    N = h2t.shape[0]
    info = plsc.get_sparse_core_info()
    n_workers = info.num_cores * info.num_subcores
    n_chunks = N // (SC_CHUNK * n_workers)
    assert n_chunks * SC_CHUNK * n_workers == N
    mesh = plsc.VectorSubcoreMesh(core_axis_name="c", subcore_axis_name="s")

    @functools.partial(
        pl.kernel, mesh=mesh,
        out_type=(jax.ShapeDtypeStruct((n_rows,) + h2t.shape[1:], h2t.dtype),
                  jax.ShapeDtypeStruct((n_rows, LANES), I32)),
        scratch_types=[pltpu.VMEM((TOP_K, SC_CHUNK), I32),
                       pltpu.VMEM((SC_CHUNK,) + h2t.shape[1:], h2t.dtype),
                       pltpu.VMEM((SC_CHUNK, LANES), I32)],
        name="sc_dispatch",
    )
    def scatter_rows(h_hbm, dest_hbm, xs_hbm, tag_hbm, idx_v, rows_v, tag_v):
        wid = lax.axis_index("s") * info.num_cores + lax.axis_index("c")
        zeros = jnp.zeros((info.num_lanes,), I32)

        @pl.loop(0, SC_CHUNK)
        def _(r):
            for l0 in range(0, LANES, info.num_lanes):
                tag_v[r, pl.ds(l0, info.num_lanes)] = zeros

        @pl.loop(0, n_chunks)
        def _(c):
            chunk = wid * n_chunks + c
            base = chunk * SC_CHUNK
            pltpu.sync_copy(dest_hbm.at[chunk], idx_v)
            pltpu.sync_copy(h_hbm.at[pl.ds(base, SC_CHUNK)], rows_v)
            for j in range(TOP_K):
                pltpu.sync_copy(rows_v, xs_hbm.at[idx_v.at[j]])

                @pl.loop(0, SC_CHUNK)
                def _(r):
                    tag_v[r, pl.ds(0, info.num_lanes)] = zeros + ((base + r) * TOP_K + j)

                pltpu.sync_copy(tag_v, tag_hbm.at[idx_v.at[j]])

    return scatter_rows(h2t, dest3)


def _sc_return_call(y, tags3, n_out_rows):
    rows = y.shape[0]
    info = plsc.get_sparse_core_info()
    n_workers = info.num_cores * info.num_subcores
    n_chunks = rows // (SC_CHUNK * n_workers)
    assert n_chunks * SC_CHUNK * n_workers == rows
    mesh = plsc.VectorSubcoreMesh(core_axis_name="c", subcore_axis_name="s")

    @functools.partial(
        pl.kernel, mesh=mesh,
        out_type=jax.ShapeDtypeStruct((n_out_rows,) + y.shape[1:], y.dtype),
        scratch_types=[pltpu.VMEM((1, SC_CHUNK), I32),
                       pltpu.VMEM((SC_CHUNK,) + y.shape[1:], y.dtype)],
        name="sc_return",
    )
    def scatter_rows(y_hbm, tag_hbm, yt_hbm, idx_v, rows_v):
        wid = lax.axis_index("s") * info.num_cores + lax.axis_index("c")

        @pl.loop(0, n_chunks)
        def _(c):
            chunk = wid * n_chunks + c
            pltpu.sync_copy(tag_hbm.at[chunk], idx_v)
            pltpu.sync_copy(y_hbm.at[pl.ds(chunk * SC_CHUNK, SC_CHUNK)], rows_v)
            pltpu.sync_copy(rows_v, yt_hbm.at[idx_v.at[0]])

    return scatter_rows(y, tags3)


def _moe_kernel(n_real, be_ref, nu_ref, nv_ref, xs_ref, xtag_ref, wgu_ref, wdn_ref, y_ref, tags_ref,
                wgu_bf, wdn_bf):
    i = pl.program_id(0)
    n_used = nu_ref[0]
    row = lax.broadcasted_iota(I32, (MOE_ROWS, LANES), 0)
    spare = n_real + (i % 2) * MOE_ROWS + row

    @pl.when(i >= n_used)
    def _():
        tags_ref[0] = spare.astype(F32).T[0:1, :].astype(I32)

    @pl.when((i < n_used) & ((i == 0) | (be_ref[i] != be_ref[jnp.maximum(i - 1, 0)])))
    def _():
        wgu_bf[...] = wgu_ref[0].astype(BF16)
        wdn_bf[...] = wdn_ref[0].astype(BF16)

    @pl.when(i < n_used)
    def _():
        tags = jnp.where(row < nv_ref[i], xtag_ref[...], spare)
        tags_ref[0] = tags.astype(F32).T[0:1, :].astype(I32)
        xb = jnp.concatenate([xs_ref[pl.ds(s, MOE_ROWS, stride=SUBLANES), :]
                              for s in range(SUBLANES)], axis=1).astype(BF16)
        gu = _dot(xb, wgu_bf[...])
        act = _silu(gu[:, :D_EXPERT]) * gu[:, D_EXPERT:]
        y = _dot(act.astype(BF16), wdn_bf[...])
        for s in range(SUBLANES):
            y_ref[pl.ds(s, MOE_ROWS, stride=SUBLANES), :] = y[:, s * LANES:(s + 1) * LANES]


def _moe_call(block_expert, n_used, n_valid, xs, xtag, w_gu, w_dn, n_real):
    P = xs.shape[0] // SUBLANES
    D = SUBLANES * LANES
    nb = P // MOE_ROWS
    blk = lambda i, be, nu, nv: (jnp.minimum(i, nu[0] - 1), 0)
    wblk = lambda i, be, nu, nv: (be[jnp.minimum(i, nu[0] - 1)], 0, 0)
    return pl.pallas_call(
        functools.partial(_moe_kernel, n_real),
        grid_spec=pltpu.PrefetchScalarGridSpec(
            num_scalar_prefetch=3,
            grid=(nb,),
            in_specs=[pl.BlockSpec((MOE_ROWS * SUBLANES, LANES), blk),
                      pl.BlockSpec((MOE_ROWS, LANES), blk),
                      pl.BlockSpec((1, D, 2 * D_EXPERT), wblk),
                      pl.BlockSpec((1, D_EXPERT, D), wblk)],
            out_specs=[pl.BlockSpec((MOE_ROWS * SUBLANES, LANES), blk),
                       pl.BlockSpec((1, 1, MOE_ROWS), lambda i, be, nu, nv: (i, 0, 0))],
            scratch_shapes=[pltpu.VMEM((D, 2 * D_EXPERT), BF16), pltpu.VMEM((D_EXPERT, D), BF16)]),
        out_shape=[jax.ShapeDtypeStruct((P * SUBLANES, LANES), F32),
                   jax.ShapeDtypeStruct((nb, 1, MOE_ROWS), I32)],
        compiler_params=_cparams("arbitrary"),
        name="moe",
    )(block_expert, n_used, n_valid, xs, xtag, w_gu, w_dn)


def _shared_kernel(h_ref, x1_ref, mod_ref, wsg_ref, wsd_ref, o_ref):
    gu = _mm(h_ref[...], wsg_ref[...])
    act = _silu(gu[:, :D_EXPERT]) * gu[:, D_EXPERT:]
    o_ref[...] = x1_ref[...] + mod_ref[0, 5:6, :] * _mm(act, wsd_ref[...])


def _shared_call(h2, x1, mod3, w_sh_gu, w_sh_dn, tokens_per_batch, tm):
    N, D = h2.shape
    tiles_per_batch = tokens_per_batch // tm
    tok = pl.BlockSpec((tm, D), lambda i: (i, 0))
    wsg = w_sh_gu.astype(BF16)
    wsd = w_sh_dn.astype(BF16)
    return pl.pallas_call(
        _shared_kernel,
        grid=(N // tm,),
        in_specs=[tok, tok,
                  pl.BlockSpec((1, 6, D), lambda i: (i // tiles_per_batch, 0, 0)),
                  pl.BlockSpec(wsg.shape, lambda i: (0, 0)),
                  pl.BlockSpec(wsd.shape, lambda i: (0, 0))],
        out_specs=tok,
        out_shape=jax.ShapeDtypeStruct((N, D), F32),
        compiler_params=_cparams("arbitrary"),
        name="shared",
    )(h2, x1, mod3, wsg, wsd)


def _combine_kernel(w_hbm, yt_ref, xs_ref, mod_ref, o_ref, wts, routed, sem_w):
    i = pl.program_id(0)
    tc = xs_ref.shape[0]
    cp_w = pltpu.make_async_copy(w_hbm.at[i], wts, sem_w)
    cp_w.start()
    cp_w.wait()

    def wsum(tt, carry):
        for u in range(SUBLANES):
            t = tt * SUBLANES + u
            first = pl.multiple_of(t * (TOP_K * SUBLANES), SUBLANES)
            acc = yt_ref[pl.ds(first, SUBLANES), :] * wts[0, t]
            for j in range(1, TOP_K):
                acc = acc + yt_ref[pl.ds(first + j * SUBLANES, SUBLANES), :] * wts[j, t]
            routed[pl.ds(pl.multiple_of(t * SUBLANES, SUBLANES), SUBLANES), :] = acc
        return carry

    lax.fori_loop(0, tc // SUBLANES, wsum, 0)
    routed2d = jnp.concatenate([routed[pl.ds(s, tc, stride=SUBLANES), :] for s in range(SUBLANES)],
                               axis=1)
    g_f = mod_ref[0, 5:6, :]
    o_ref[...] = xs_ref[...] + g_f * routed2d


def _combine_call(w3, yt, x_shared, mod3, tokens_per_batch, tc):
    N, D = x_shared.shape
    tiles_per_batch = tokens_per_batch // tc
    tok = pl.BlockSpec((tc, D), lambda i: (i, 0))
    return pl.pallas_call(
        _combine_kernel,
        grid=(N // tc,),
        in_specs=[pl.BlockSpec(memory_space=pl.ANY),
                  pl.BlockSpec((tc * TOP_K * SUBLANES, LANES), lambda i: (i, 0)),
                  tok,
                  pl.BlockSpec((1, 6, D), lambda i: (i // tiles_per_batch, 0, 0))],
        out_specs=tok,
        out_shape=jax.ShapeDtypeStruct((N, D), F32),
        scratch_shapes=[pltpu.SMEM((TOP_K, tc), F32), pltpu.VMEM((tc * SUBLANES, LANES), F32),
                        pltpu.SemaphoreType.DMA],
        compiler_params=_cparams("arbitrary"),
        name="combine",
    )(w3, yt, x_shared, mod3)


def _tile(n, pref):
    t = min(n, pref)
    assert n % t == 0, (n, t)
    return t


def _layer(x, mod3, positions, norm_mix, w_in, rwkv_mu, decay_w0, decay_up, iclr_a0, iclr_up,
           gate_up, rwkv_k_k, rwkv_k_a, rwkv_r_k, ln_x_w, ln_x_b, q_a_norm, w_q_b, kv_a_norm,
           w_kv_b, q_norm, k_norm, w_out, norm_ffn, w_router, router_bias, w_e_gate_up, w_e_down,
           w_sh_gate_up, w_sh_down):
    B, T, D = x.shape
    N = B * T
    assert T % SCAN_CHUNK == 0
    (r, lw, k2, v, kk, akk, g, bonus, q_pad, k_pad, v_pad) = _pre_call(
        x, mod3, positions, norm_mix, w_in, rwkv_mu, decay_w0, decay_up, iclr_a0, iclr_up,
        gate_up, rwkv_k_k, rwkv_k_a, rwkv_r_k, q_a_norm, w_q_b, kv_a_norm, w_kv_b, q_norm, k_norm,
        tm=_tile(T, 512))
    y = _scan_call(r, lw, k2, v, kk, akk)
    o_pad = _attn_call(q_pad, k_pad, v_pad)
    x1, h2, h2t = _post_call(y, bonus, g, o_pad, x, mod3, ln_x_w, ln_x_b, w_out, norm_ffn,
                             tm=_tile(T, 512))
    x1 = x1.reshape(N, D)
    h2 = h2.reshape(N, D)
    h2t = h2t.reshape(N, D // LANES, LANES)

    tr = _tile(N, 512)
    top_e, wts, rank, counts = _route_call(h2, w_router, router_bias, tr)
    counts = counts.reshape(N_EXPERTS)
    padded = (counts + MOE_ROWS - 1) // MOE_ROWS * MOE_ROWS
    pad_ends = jnp.cumsum(padded)
    pad_starts = pad_ends - padded
    n_blocks = (N * TOP_K + N_EXPERTS * (MOE_ROWS - 1)) // MOE_ROWS
    sc_rows = SC_CHUNK * plsc.get_sparse_core_info().num_cores * plsc.get_sparse_core_info().num_subcores
    n_blocks = -(-n_blocks // (sc_rows // MOE_ROWS)) * (sc_rows // MOE_ROWS)
    block_expert = jnp.minimum(
        jnp.searchsorted(pad_ends, jnp.arange(n_blocks, dtype=I32) * MOE_ROWS, side='right'),
        N_EXPERTS - 1).astype(I32)
    n_used = (pad_ends[-1:] // MOE_ROWS).astype(I32)
    block_row0 = jnp.arange(n_blocks, dtype=I32) * MOE_ROWS
    n_valid = jnp.clip((pad_starts + counts)[block_expert] - block_row0, 0, MOE_ROWS).astype(I32)
    dest = _dest_call(top_e, rank, pad_starts.astype(I32), tr)

    dest3 = dest.reshape(TOP_K, N // SC_CHUNK, SC_CHUNK).transpose(1, 0, 2)
    n_rows = n_blocks * MOE_ROWS
    xs, xtag = _sc_dispatch_call(h2t, dest3, n_rows)
    y_e, tags = _moe_call(block_expert, n_used, n_valid, xs.reshape(n_rows * SUBLANES, LANES), xtag,
                          w_e_gate_up, w_e_down, N * TOP_K)
    yt = _sc_return_call(y_e.reshape(n_rows, SUBLANES, LANES),
                         tags.reshape(n_rows // SC_CHUNK, 1, SC_CHUNK), N * TOP_K + 2 * MOE_ROWS)
    tc = _tile(T, COMBINE_TILE)
    w3 = wts.reshape(TOP_K, N // tc, tc).transpose(1, 0, 2)
    x_shared = _shared_call(h2, x1, mod3, w_sh_gate_up, w_sh_down, T, _tile(T, 512))
    out = _combine_call(w3, yt.reshape(-1, LANES), x_shared, mod3, T, tc)
    return out.reshape(B, T, D)


def kernel(x, c, positions, ada_w, ada_b, norm_mix, w_in, rwkv_mu, decay_w0, decay_up, iclr_a0, iclr_up, gate_up, rwkv_k_k, rwkv_k_a, rwkv_r_k, ln_x_w, ln_x_b, q_a_norm, w_q_b, kv_a_norm, w_kv_b, q_norm, k_norm, w_out, norm_ffn, w_router, router_bias, w_e_gate_up, w_e_down, w_sh_gate_up, w_sh_down):
    B, T, D = x.shape
    depth = ada_w.shape[0]
    for l in range(depth):
        mod3 = _mod_call(c, ada_w[l], ada_b[l]).reshape(B, 6, D)
        x = _layer(x, mod3, positions, norm_mix[l], w_in[l], rwkv_mu[l], decay_w0[l], decay_up[l],
                   iclr_a0[l], iclr_up[l], gate_up[l], rwkv_k_k[l], rwkv_k_a[l], rwkv_r_k[l],
                   ln_x_w[l], ln_x_b[l], q_a_norm[l], w_q_b[l], kv_a_norm[l], w_kv_b[l],
                   q_norm[l], k_norm[l], w_out[l], norm_ffn[l], w_router[l], router_bias[l],
                   w_e_gate_up[l], w_e_down[l], w_sh_gate_up[l], w_sh_down[l])
    return x
```

```python
import functools
import math

import jax
import jax.numpy as jnp
import numpy as np
from jax import lax
from jax.experimental import pallas as pl
from jax.experimental.pallas import tpu as pltpu
from jax.experimental.pallas import tpu_sc as plsc

F32 = jnp.float32
BF16 = jnp.bfloat16
I32 = jnp.int32

NORM_EPS = 1e-6
GN_EPS = 64e-5
RWKV_HEADS = 8
RWKV_HEAD_DIM = 64
D_RWKV = 512
DECAY_LORA = 32
ICLR_LORA = 32
GATE_LORA = 96
MLA_HEADS = 8
QK_NOPE_DIM = 64
QK_ROPE_DIM = 32
QK_HEAD_DIM = 96
V_HEAD_DIM = 64
Q_LORA_RANK = 256
KV_LORA_RANK = 128
ROPE_THETA = 10000.0
N_EXPERTS = 256
TOP_K = 8
N_GROUPS = 8
TOPK_GROUPS = 4
GROUP_SIZE = N_EXPERTS // N_GROUPS
D_EXPERT = 256
ROUTED_SCALE = 2.5
MOE_ROWS = 512
SC_CHUNK = 64
COMBINE_TILE = 256
SHARED_TILE = 512
SHARED_EVERY = 8

LANES = 128
SUBLANES = 8
HEAD_PAD = 128
VMEM_LIMIT = 56 * 1024 * 1024

PRE_SUBTILES = 1
SCAN_CHUNK = 64
SCAN_BLOCK = 512
ATTN_TILE = 512
ATTN_Q_SCALE = QK_HEAD_DIM ** -0.5 * math.log2(math.e)
NEG_INF = float("-inf")


def _cparams(*sem):
    return pltpu.CompilerParams(dimension_semantics=sem, vmem_limit_bytes=VMEM_LIMIT)


def _split2(a):
    hi = a.astype(BF16)
    lo = (a - hi.astype(F32)).astype(BF16)
    return hi, lo


def _split3(a):
    hi = a.astype(BF16)
    r1 = a - hi.astype(F32)
    mid = r1.astype(BF16)
    lo = (r1 - mid.astype(F32)).astype(BF16)
    return hi, mid, lo


def _dot(a, b, dims=None):
    if dims is None:
        return jnp.dot(a, b, preferred_element_type=F32)
    return lax.dot_general(a, b, (dims, ((), ())), preferred_element_type=F32)


def _mm(a, b, dims=None):
    return _dot(a.astype(BF16), b.astype(BF16), dims)


def _mm3(a, b, dims=None):
    ah, al = _split2(a)
    bh, bl = _split2(b)
    return _dot(ah, bh, dims) + (_dot(ah, bl, dims) + _dot(al, bh, dims))


def _mm_exact_rhs(a, b_exact_bf16, dims=None):
    h, m, l = _split3(a)
    return _dot(h, b_exact_bf16, dims) + (_dot(m, b_exact_bf16, dims) + _dot(l, b_exact_bf16, dims))


NT = ((1,), (1,))
TN = ((0,), (0,))


def _sigmoid(z):
    return 1.0 / (1.0 + jnp.exp(-z))


def _silu(z):
    return z * _sigmoid(z)


def _seg_ones(width, seg):
    r = lax.broadcasted_iota(I32, (width, width), 0) // seg
    c = lax.broadcasted_iota(I32, (width, width), 1) // seg
    return jnp.where(r == c, 1.0, 0.0).astype(BF16)


def _segsum(a, ones_bd):
    hi, lo = _split2(a)
    return _dot(hi, ones_bd) + _dot(lo, ones_bd)


def _mod_kernel(c_ref, w_ref, b_ref, o_ref):
    ca = _silu(c_ref[...])
    o_ref[...] = _mm3(ca, w_ref[...]) + b_ref[...]


def _mod_call(c, ada_w, ada_b):
    B, D = c.shape
    n6 = ada_w.shape[1]
    tn = D
    return pl.pallas_call(
        _mod_kernel,
        grid=(n6 // tn,),
        in_specs=[pl.BlockSpec((B, D), lambda j: (0, 0)),
                  pl.BlockSpec((D, tn), lambda j: (0, j)),
                  pl.BlockSpec((1, tn), lambda j: (0, j))],
        out_specs=pl.BlockSpec((B, tn), lambda j: (0, j)),
        out_shape=jax.ShapeDtypeStruct((B, n6), F32),
        compiler_params=_cparams("arbitrary"),
        name="mod",
    )(c, ada_w, ada_b.reshape(1, n6))


def _pre_kernel(x_ref, mod_ref, pos_ref, nmix_ref, wrkv_ref, wlora_ref, wmla_ref,
                mu_rkv_ref, mu_lora_ref, wup_ref, w0_ref, a0_ref, kk_ref, ka_ref, rk_ref,
                qan_ref, wqb_ref, kvan_ref, wkb_ref, wvb_ref, qn_ref, kn_ref, invf_ref,
                r_ref, lw_ref, k_ref, v_ref, kkn_ref, akk_ref, g_ref, bonus_ref,
                q_ref, kout_ref, vout_ref,
                carry_rkv, carry_lora):
    ti = pl.program_id(1)
    tm = x_ref.shape[1]

    @pl.when(ti == 0)
    def _():
        carry_rkv[...] = jnp.zeros_like(carry_rkv)
        carry_lora[...] = jnp.zeros_like(carry_lora)

    ts = tm // PRE_SUBTILES
    sh_a = mod_ref[0, 0:1, :]
    sc_a = mod_ref[0, 1:2, :]
    ones64 = _seg_ones(D_RWKV, RWKV_HEAD_DIM)
    row0 = lax.broadcasted_iota(I32, (ts, 1), 0) == 0
    half = QK_ROPE_DIM // 2
    last_rows = {}

    def sub_tile(s):
        rows = slice(s * ts, (s + 1) * ts)
        xb = x_ref[0, rows, :]
        ms = jnp.mean(xb * xb, axis=-1, keepdims=True)
        h = xb * lax.rsqrt(ms + NORM_EPS) * nmix_ref[...] * (1.0 + sc_a) + sh_a
        hb = h.astype(BF16)
        yield
        u_rkv = _dot(hb, wrkv_ref[...])
        u_lora = _dot(hb, wlora_ref[...])
        u_mla = _dot(hb, wmla_ref[...])
        last_rows[s] = (u_rkv[ts - 1:ts, :], u_lora[ts - 1:ts, :])
        yield

        before_rkv, before_lora = (carry_rkv[...], carry_lora[...]) if s == 0 else last_rows[s - 1]
        prev_rkv = jnp.where(row0, before_rkv, pltpu.roll(u_rkv, 1, 0))
        prev_lora = jnp.where(row0, before_lora, pltpu.roll(u_lora, 1, 0))
        if s == PRE_SUBTILES - 1:
            carry_rkv[...] = u_rkv[ts - 1:ts, :]
            carry_lora[...] = u_lora[ts - 1:ts, :]
        us = u_rkv + (prev_rkv - u_rkv) * mu_rkv_ref[...]
        ul = u_lora + (prev_lora - u_lora) * mu_lora_ref[...]
        r = us[:, 0:D_RWKV]
        k = us[:, D_RWKV:2 * D_RWKV]
        v = us[:, 2 * D_RWKV:3 * D_RWKV]
        lane_l = lax.broadcasted_iota(I32, ul.shape, 1)
        t_in = jnp.where(lane_l < DECAY_LORA, jnp.tanh(ul),
                         jnp.where(lane_l < DECAY_LORA + ICLR_LORA, ul, _sigmoid(ul)))
        yield
        up = _mm(t_in, wup_ref[...])
        yield
        z = w0_ref[...] + up[:, 0:D_RWKV]
        lw = (-math.exp(-0.5)) * _sigmoid(z)
        a = _sigmoid(a0_ref[...] + up[:, D_RWKV:2 * D_RWKV])
        g = up[:, 2 * D_RWKV:3 * D_RWKV]
        kk = k * kk_ref[...]
        k2 = k * (1.0 + (a - 1.0) * ka_ref[...])
        yield
        ss = _segsum(kk * kk, ones64)
        bonus_sum = _segsum(r * k2 * rk_ref[...], ones64)
        yield
        kk = kk * lax.rsqrt(jnp.maximum(ss, 1e-24))
        r_ref[0, rows, :] = r
        lw_ref[0, rows, :] = lw
        k_ref[0, rows, :] = k2
        v_ref[0, rows, :] = v
        kkn_ref[0, rows, :] = kk
        akk_ref[0, rows, :] = a * kk
        g_ref[0, rows, :] = g
        bonus_ref[0, rows, :] = bonus_sum * v
        yield

        q_lat = u_mla[:, 0:Q_LORA_RANK]
        kv_lat = u_mla[:, Q_LORA_RANK:Q_LORA_RANK + KV_LORA_RANK]
        kpe_tile = u_mla[:, Q_LORA_RANK + KV_LORA_RANK:]
        qn = q_lat * lax.rsqrt(jnp.mean(q_lat * q_lat, axis=-1, keepdims=True) + NORM_EPS) * qan_ref[...]
        kvn = kv_lat * lax.rsqrt(jnp.mean(kv_lat * kv_lat, axis=-1, keepdims=True) + NORM_EPS) * kvan_ref[...]
        kvb = kvn.astype(BF16)
        yield
        q_raw = _mm(qn, wqb_ref[...])
        k_raw = _dot(kvb, wkb_ref[...])
        v_pad = _dot(kvb, wvb_ref[...])
        yield
        kpe_h = pltpu.roll(kpe_tile, QK_NOPE_DIM, 1)
        cos_t, s1, s2 = rope_tables(s)

        def tables(gain, scale):
            g_s = gain * scale
            return (cos_t * g_s, s1 * pltpu.roll(g_s, HEAD_PAD - half, 1),
                    s2 * pltpu.roll(g_s, half, 1))

        def norm_rope(xh, tabs):
            c_g, s1_g, s2_g = tabs
            ssq = jnp.sum(xh * xh, axis=-1, keepdims=True) * (1.0 / QK_HEAD_DIM)
            rot = xh * c_g + pltpu.roll(xh, HEAD_PAD - half, 1) * s1_g + pltpu.roll(xh, half, 1) * s2_g
            return rot * lax.rsqrt(ssq + NORM_EPS)

        q_tabs = tables(qn_ref[...], ATTN_Q_SCALE)
        k_tabs = tables(kn_ref[...], 1.0)
        vout_ref[0, rows, :] = v_pad.astype(BF16)
        yield
        for hh in range(MLA_HEADS):
            sl = slice(hh * HEAD_PAD, (hh + 1) * HEAD_PAD)
            q_ref[0, rows, sl] = norm_rope(q_raw[:, sl], q_tabs).astype(BF16)
            kout_ref[0, rows, sl] = norm_rope(k_raw[:, sl] + kpe_h, k_tabs).astype(BF16)
            yield

    def rope_tables(s):
        ang_t = invf_ref[...] * pos_ref[0, :, s * ts:(s + 1) * ts].astype(F32)
        frow = lax.broadcasted_iota(I32, (half, HEAD_PAD), 0)
        flane = lax.broadcasted_iota(I32, (half, HEAD_PAD), 1)
        at_x1 = flane == frow + QK_NOPE_DIM
        at_x2 = flane == frow + QK_NOPE_DIM + half
        e_cos = jnp.where(at_x1 | at_x2, 1.0, 0.0).astype(BF16)
        e_sin = jnp.concatenate([jnp.where(at_x1, -1.0, 0.0), jnp.where(at_x2, 1.0, 0.0)],
                                axis=1).astype(BF16)
        lane = lax.broadcasted_iota(I32, (1, HEAD_PAD), 1)
        off_rope = jnp.where((lane >= QK_NOPE_DIM) & (lane < QK_HEAD_DIM), 0.0, 1.0)
        cos_t = _mm_exact_rhs(jnp.cos(ang_t), e_cos, TN) + off_rope
        sin2 = _mm_exact_rhs(jnp.sin(ang_t), e_sin, TN)
        return cos_t, sin2[:, :HEAD_PAD], sin2[:, HEAD_PAD:]

    live = [sub_tile(s) for s in range(PRE_SUBTILES)]
    while live:
        live = [gen for gen in live if next(gen, "done") != "done"]


def _pad_heads(w, n_heads, width):
    kdim = w.shape[0]
    w = w.reshape(kdim, n_heads, width)
    w = jnp.pad(w, ((0, 0), (0, 0), (0, HEAD_PAD - width)))
    return w.reshape(kdim, n_heads * HEAD_PAD)


def _pre_call(x, mod3, positions, norm_mix, w_in, rwkv_mu, decay_w0, decay_up, iclr_a0, iclr_up,
              gate_up, k_k, k_a, r_k, q_a_norm, w_q_b, kv_a_norm, w_kv_b, q_norm, k_norm, tm):
    B, T, D = x.shape
    n_rkv = 3 * D_RWKV
    n_lora = DECAY_LORA + ICLR_LORA + GATE_LORA
    LORA_PAD = 256
    MLA_PAD = 512
    n_mla = Q_LORA_RANK + KV_LORA_RANK + QK_ROPE_DIM
    w_rkv = w_in[:, :n_rkv].astype(BF16)
    w_lora = jnp.pad(w_in[:, n_rkv:n_rkv + n_lora], ((0, 0), (0, LORA_PAD - n_lora))).astype(BF16)
    w_mla = jnp.pad(w_in[:, n_rkv + n_lora:], ((0, 0), (0, MLA_PAD - n_mla))).astype(BF16)
    mu_rkv = rwkv_mu[:n_rkv].reshape(1, n_rkv)
    mu_lora = jnp.pad(rwkv_mu[n_rkv:], (0, LORA_PAD - n_lora)).reshape(1, LORA_PAD)
    w_up = jnp.zeros((LORA_PAD, n_rkv), F32)
    w_up = w_up.at[0:DECAY_LORA, 0:D_RWKV].set(decay_up)
    w_up = w_up.at[DECAY_LORA:DECAY_LORA + ICLR_LORA, D_RWKV:2 * D_RWKV].set(iclr_up)
    w_up = w_up.at[DECAY_LORA + ICLR_LORA:n_lora, 2 * D_RWKV:].set(gate_up)
    w_up = w_up.astype(BF16)
    w_qb = _pad_heads(w_q_b, MLA_HEADS, QK_HEAD_DIM).astype(BF16)
    w_kv3 = w_kv_b.reshape(KV_LORA_RANK, MLA_HEADS, QK_NOPE_DIM + V_HEAD_DIM)
    w_kb = _pad_heads(w_kv3[:, :, :QK_NOPE_DIM].reshape(KV_LORA_RANK, -1), MLA_HEADS, QK_NOPE_DIM).astype(BF16)
    w_vb = _pad_heads(w_kv3[:, :, QK_NOPE_DIM:].reshape(KV_LORA_RANK, -1), MLA_HEADS, V_HEAD_DIM).astype(BF16)
    qn_pad = jnp.pad(q_norm, (0, HEAD_PAD - QK_HEAD_DIM)).reshape(1, HEAD_PAD)
    kn_pad = jnp.pad(k_norm, (0, HEAD_PAD - QK_HEAD_DIM)).reshape(1, HEAD_PAD)
    inv_freq = ROPE_THETA ** (-jnp.arange(0, QK_ROPE_DIM, 2, dtype=F32) / QK_ROPE_DIM)
    invf = inv_freq.reshape(QK_ROPE_DIM // 2, 1)
    pos3 = positions.reshape(B, 1, T)
    HP = MLA_HEADS * HEAD_PAD

    row = lambda n: pl.BlockSpec((1, n), lambda b, t: (0, 0))
    full = lambda a: pl.BlockSpec(a.shape, lambda b, t: (0,) * a.ndim)
    tok = lambda n: pl.BlockSpec((1, tm, n), lambda b, t: (b, t, 0))
    outs = ([jax.ShapeDtypeStruct((B, T, D_RWKV), F32)] * 8
            + [jax.ShapeDtypeStruct((B, T, HP), BF16)] * 3)
    return pl.pallas_call(
        _pre_kernel,
        grid=(B, T // tm),
        in_specs=[tok(D),
                  pl.BlockSpec((1, 6, D), lambda b, t: (b, 0, 0)),
                  pl.BlockSpec((1, 1, tm), lambda b, t: (b, 0, t)),
                  row(D), full(w_rkv), full(w_lora), full(w_mla),
                  row(n_rkv), row(LORA_PAD), full(w_up), row(D_RWKV), row(D_RWKV),
                  row(D_RWKV), row(D_RWKV), row(D_RWKV),
                  row(Q_LORA_RANK), full(w_qb), row(KV_LORA_RANK), full(w_kb), full(w_vb),
                  row(HEAD_PAD), row(HEAD_PAD), full(invf)],
        out_specs=[tok(D_RWKV)] * 8 + [tok(HP)] * 3,
        out_shape=outs,
        scratch_shapes=[pltpu.VMEM((1, n_rkv), F32), pltpu.VMEM((1, LORA_PAD), F32)],
        compiler_params=_cparams("arbitrary", "arbitrary"),
        name="pre",
    )(x, mod3, pos3, norm_mix.reshape(1, D), w_rkv, w_lora, w_mla, mu_rkv, mu_lora, w_up,
      decay_w0.reshape(1, -1), iclr_a0.reshape(1, -1), k_k.reshape(1, -1), k_a.reshape(1, -1),
      r_k.reshape(1, -1), q_a_norm.reshape(1, -1), w_qb, kv_a_norm.reshape(1, -1), w_kb, w_vb,
      qn_pad, kn_pad, invf)


def _scan_kernel(r_ref, lw_ref, k_ref, v_ref, kk_ref, akk_ref, y_ref, state):
    C = SCAN_CHUNK
    n_chunks = r_ref.shape[1] // C
    n_pairs = r_ref.shape[2] // LANES

    @pl.when(pl.program_id(1) == 0)
    def _():
        state[...] = jnp.zeros_like(state)

    ri = lax.broadcasted_iota(I32, (C, C), 0)
    ci = lax.broadcasted_iota(I32, (C, C), 1)
    tri_incl = jnp.where(ci <= ri, 1.0, 0.0).astype(BF16)
    r2 = lax.broadcasted_iota(I32, (2 * C, 2 * C), 0)
    c2 = lax.broadcasted_iota(I32, (2 * C, 2 * C), 1)
    same = (r2 >= C) == (c2 >= C)
    strict = same & (c2 < r2)
    incl = same & (c2 <= r2)
    eye = jnp.where(c2 == r2, 1.0, 0.0)
    head0 = lax.broadcasted_iota(I32, (C, LANES), 1) < RWKV_HEAD_DIM

    def stack2(a):
        return jnp.concatenate([jnp.where(head0, a, 0.0), jnp.where(head0, 0.0, a)], axis=0)

    C2 = 2 * C
    cat0 = lambda *a: jnp.concatenate(a, axis=0)
    cat1 = lambda *a: jnp.concatenate(a, axis=1)

    items = []
    for c in range(n_chunks):
        rows = slice(c * C, (c + 1) * C)
        lw = lw_ref[0, rows, :]
        cum = _mm_exact_rhs_left(tri_incl, lw)
        cum_end = cum[C - 1:C, :]
        w_end = jnp.exp(cum_end)
        e_pos = jnp.exp(cum)
        e_neg = jnp.exp(-cum)
        e_prev = jnp.exp(cum - lw)
        e_end = jnp.exp(cum_end - cum)
        kk = kk_ref[0, rows, :]
        k2 = k_ref[0, rows, :]
        pneg = -akk_ref[0, rows, :]
        vv = v_ref[0, rows, :]
        rt = r_ref[0, rows, :] * e_pos
        bt = kk * e_prev
        pt = pneg * e_neg
        kt = k2 * e_neg
        ph = pneg * e_end
        kh = k2 * e_end
        for pp in range(n_pairs):
            sl = slice(pp * LANES, (pp + 1) * LANES)
            items.append(dict(
                c=c, p=pp, w_end=w_end[:, sl],
                bt2=stack2(bt[:, sl]).astype(BF16), rt2=stack2(rt[:, sl]).astype(BF16),
                pk2=cat0(stack2(pt[:, sl]), stack2(kt[:, sl])).astype(BF16),
                phkh2=cat0(stack2(ph[:, sl]), stack2(kh[:, sl])).astype(BF16),
                v2=stack2(vv[:, sl])))
    for it in items:
        ab = _dot(cat0(it['bt2'], it['rt2']), it['pk2'], NT)
        it['a_ab'] = jnp.where(strict, ab[:C2, :C2], 0.0)
        it['a_ak'] = jnp.where(strict, ab[:C2, C2:], 0.0).astype(BF16)
        it['b_rpk'] = cat1(jnp.where(incl, ab[C2:, :C2], 0.0), jnp.where(incl, ab[C2:, C2:], 0.0)).astype(BF16)
        it['tinv'] = eye + it['a_ab']
    for it in items:
        it['apow'] = _mm(it['a_ab'], it['a_ab'])
    for _ in range(int(math.log2(C)) - 1):
        for it in items:
            both = _mm(cat0(it['apow'], it['tinv']), it['apow'])
            it['apow'] = both[:C2]
            it['tinv'] = it['tinv'] + both[C2:]
    for it in items:
        it['akv'] = _dot(it['a_ak'], it['v2'].astype(BF16))
    for it in items:
        tt = _dot(it['tinv'].astype(BF16), cat1(it['bt2'], it['akv'].astype(BF16)))
        it['tb_rt'] = cat0(tt[:, :LANES].astype(BF16), it['rt2'])
        it['tav'] = tt[:, LANES:]
    for it in items:
        pp = it['p']
        s0 = state[pp]
        top = _dot(it['tb_rt'], s0.astype(BF16), NT)
        u2 = top[:C2] + it['tav']
        uv = cat0(u2, it['v2']).astype(BF16)
        y2 = top[C2:] + _dot(it['b_rpk'], uv)
        state[pp] = s0 * it['w_end'] + _dot(uv, it['phkh2'], TN)
        y_ref[0, it['c'] * C:(it['c'] + 1) * C, pp * LANES:(pp + 1) * LANES] = y2[0:C] + y2[C:C2]


def _mm_exact_rhs_left(b_exact_bf16, a):
    h, m, l = _split3(a)
    return _dot(b_exact_bf16, h) + (_dot(b_exact_bf16, m) + _dot(b_exact_bf16, l))


def _scan_call(r, lw, k2, v, kk, akk):
    B, T, W = r.shape
    tb = _tile(T, SCAN_BLOCK)
    spec = pl.BlockSpec((1, tb, W), lambda b, c: (b, c, 0))
    return pl.pallas_call(
        _scan_kernel,
        grid=(B, T // tb),
        in_specs=[spec] * 6,
        out_specs=spec,
        out_shape=jax.ShapeDtypeStruct((B, T, W), F32),
        scratch_shapes=[pltpu.VMEM((W // LANES, 2 * RWKV_HEAD_DIM, LANES), F32)],
        compiler_params=_cparams("arbitrary", "arbitrary"),
        name="scan",
    )(r, lw, k2, v, kk, akk)


def _attn_kernel(q_ref, k_ref, v_ref, o_ref):
    T = q_ref.shape[1]
    tq = min(T, ATTN_TILE)
    row = lax.broadcasted_iota(I32, (tq, tq), 0)
    col = lax.broadcasted_iota(I32, (tq, tq), 1)
    causal = col <= row

    def update(q, kt, vt, carry, mask):
        m_old, l_old, acc = carry
        s = _dot(q, kt, NT)
        if mask:
            s = jnp.where(causal, s, NEG_INF)
        m_new = jnp.maximum(m_old, jnp.max(s, axis=-1, keepdims=True))
        alpha = jnp.exp2(m_old - m_new)
        p = jnp.exp2(s - m_new)
        l_new = alpha * l_old + jnp.sum(p, axis=-1, keepdims=True)
        acc = alpha * acc + _dot(p.astype(BF16), vt)
        return m_new, l_new, acc

    for qi in range(T // tq):
        q = q_ref[0, qi * tq:(qi + 1) * tq, :]
        carry = (jnp.full((tq, 1), NEG_INF, F32), jnp.zeros((tq, 1), F32),
                 jnp.zeros((tq, HEAD_PAD), F32))

        def body(ki, carry, q=q):
            rows = pl.ds(pl.multiple_of(ki * tq, tq), tq)
            return update(q, k_ref[0, rows, :], v_ref[0, rows, :], carry, False)

        carry = lax.fori_loop(0, qi, body, carry, unroll=True)
        diag = slice(qi * tq, (qi + 1) * tq)
        _, l_fin, acc = update(q, k_ref[0, diag, :], v_ref[0, diag, :], carry, True)
        o_ref[0, diag, :] = (acc / l_fin).astype(o_ref.dtype)


def _attn_call(q, k, v):
    B, T, HP = q.shape
    spec = pl.BlockSpec((1, T, HEAD_PAD), lambda b, h: (b, 0, h))
    return pl.pallas_call(
        _attn_kernel,
        grid=(B, MLA_HEADS),
        in_specs=[spec, spec, spec],
        out_specs=spec,
        out_shape=jax.ShapeDtypeStruct((B, T, HP), BF16),
        compiler_params=_cparams("arbitrary", "arbitrary"),
        name="attn",
    )(q, k, v)


def _post_kernel(y_ref, bonus_ref, g_ref, o_ref, x_ref, mod_ref, lnw_ref, lnb_ref,
                 wo_r_ref, wo_m_ref, nffn_ref, x1_ref, h2_ref, h2t_ref):
    y = y_ref[0]
    ones64 = _seg_ones(D_RWKV, RWKV_HEAD_DIM)
    mean = _segsum(y, ones64) * (1.0 / RWKV_HEAD_DIM)
    yc = y - mean
    var = _segsum(yc * yc, ones64) * (1.0 / RWKV_HEAD_DIM)
    yn = yc * lax.rsqrt(var + GN_EPS) * lnw_ref[...] + lnb_ref[...]
    yr = (yn + bonus_ref[0]) * g_ref[0]
    mix = _mm(yr, wo_r_ref[...]) + _dot(o_ref[0], wo_m_ref[...])
    g_a = mod_ref[0, 2:3, :]
    sh_f = mod_ref[0, 3:4, :]
    sc_f = mod_ref[0, 4:5, :]
    x1 = x_ref[0] + g_a * mix
    x1_ref[0] = x1
    ms = jnp.mean(x1 * x1, axis=-1, keepdims=True)
    h2 = x1 * lax.rsqrt(ms + NORM_EPS) * nffn_ref[...] * (1.0 + sc_f) + sh_f
    h2_ref[0] = h2
    tm = h2.shape[0]
    for s in range(SUBLANES):
        h2t_ref[0, pl.ds(s, tm, stride=SUBLANES), :] = h2[:, s * LANES:(s + 1) * LANES]


def _post_call(y, bonus, g, o_pad, x, mod3, ln_w, ln_b, w_out, norm_ffn, tm):
    B, T, D = x.shape
    HP = MLA_HEADS * HEAD_PAD
    wo_r = w_out[:D_RWKV].astype(BF16)
    wo_m = jnp.pad(w_out[D_RWKV:].reshape(MLA_HEADS, V_HEAD_DIM, D),
                   ((0, 0), (0, HEAD_PAD - V_HEAD_DIM), (0, 0))).reshape(HP, D).astype(BF16)
    tok = lambda n: pl.BlockSpec((1, tm, n), lambda b, t: (b, t, 0))
    row = lambda n: pl.BlockSpec((1, n), lambda b, t: (0, 0))
    full = lambda a: pl.BlockSpec(a.shape, lambda b, t: (0,) * a.ndim)
    return pl.pallas_call(
        _post_kernel,
        grid=(B, T // tm),
        in_specs=[tok(D_RWKV), tok(D_RWKV), tok(D_RWKV), tok(HP), tok(D),
                  pl.BlockSpec((1, 6, D), lambda b, t: (b, 0, 0)),
                  row(D_RWKV), row(D_RWKV), full(wo_r), full(wo_m), row(D)],
        out_specs=[tok(D), tok(D),
                   pl.BlockSpec((1, tm * SUBLANES, LANES), lambda b, t: (b, t, 0))],
        out_shape=[jax.ShapeDtypeStruct((B, T, D), F32)] * 2
        + [jax.ShapeDtypeStruct((B, T * SUBLANES, LANES), F32)],
        compiler_params=_cparams("arbitrary", "arbitrary"),
        name="post",
    )(y, bonus, g, o_pad, x, mod3, ln_w.reshape(1, -1), ln_b.reshape(1, -1), wo_r, wo_m,
      norm_ffn.reshape(1, D))


def _first_index(mask, iota, size, axis):
    return jnp.min(jnp.where(mask, iota, size), axis=axis, keepdims=True)


def _route_kernel(h_ref, wr_ref, bias_ref, e_ref, w_ref, rank_ref, cnt_ref, base):
    tr = h_ref.shape[0]
    E = N_EXPERTS

    @pl.when(pl.program_id(0) == 0)
    def _():
        base[...] = jnp.zeros_like(base)

    logits = _mm3(wr_ref[...], h_ref[...], NT)
    scores = _sigmoid(logits)
    sel = scores + bias_ref[...]
    iota_g = lax.broadcasted_iota(I32, (GROUP_SIZE, tr), 0)
    gs_rows = []
    for gi in range(N_GROUPS):
        blk = sel[gi * GROUP_SIZE:(gi + 1) * GROUP_SIZE, :]
        m1 = jnp.max(blk, axis=0, keepdims=True)
        i1 = _first_index(blk == m1, iota_g, GROUP_SIZE, 0)
        m2 = jnp.max(jnp.where(iota_g == i1, NEG_INF, blk), axis=0, keepdims=True)
        gs_rows.append(m1 + m2)
    gs = jnp.concatenate(gs_rows, axis=0)
    iota8 = lax.broadcasted_iota(I32, (N_GROUPS, tr), 0)
    gmask = jnp.zeros((N_GROUPS, tr), jnp.bool_)
    for _ in range(TOPK_GROUPS):
        mg = jnp.max(gs, axis=0, keepdims=True)
        ig = _first_index(gs == mg, iota8, N_GROUPS, 0)
        hit = iota8 == ig
        gmask = gmask | hit
        gs = jnp.where(hit, NEG_INF, gs)
    msel = jnp.concatenate(
        [jnp.where(gmask[gi:gi + 1, :], sel[gi * GROUP_SIZE:(gi + 1) * GROUP_SIZE, :], NEG_INF)
         for gi in range(N_GROUPS)], axis=0)
    iota_e = lax.broadcasted_iota(I32, (E, tr), 0)
    e_rows, w_rows = [], []
    onehot = jnp.zeros((E, tr), F32)
    for _ in range(TOP_K):
        mv = jnp.max(msel, axis=0, keepdims=True)
        ie = _first_index(msel == mv, iota_e, E, 0)
        hit = iota_e == ie
        e_rows.append(ie)
        w_rows.append(jnp.sum(jnp.where(hit, scores, 0.0), axis=0, keepdims=True))
        onehot = jnp.where(hit, 1.0, onehot)
        msel = jnp.where(hit, NEG_INF, msel)
    top_e = jnp.concatenate(e_rows, axis=0)
    wts = jnp.concatenate(w_rows, axis=0)
    wts = wts / jnp.sum(wts, axis=0, keepdims=True) * ROUTED_SCALE
    ti = lax.broadcasted_iota(I32, (tr, tr), 0)
    tj = lax.broadcasted_iota(I32, (tr, tr), 1)
    upper = jnp.where(ti < tj, 1.0, 0.0).astype(BF16)
    pos = _dot(onehot.astype(BF16), upper) + base[...]
    rank_rows = [jnp.sum(jnp.where(iota_e == e_rows[j], pos, 0.0), axis=0, keepdims=True)
                 for j in range(TOP_K)]
    base[...] = base[...] + jnp.sum(onehot, axis=1, keepdims=True)
    e_ref[...] = top_e
    w_ref[...] = wts
    rank_ref[...] = jnp.concatenate(rank_rows, axis=0).astype(I32)
    cnt_ref[...] = base[...].astype(I32)


def _route_call(h2, w_router, router_bias, tr):
    N, D = h2.shape
    E = N_EXPERTS
    out_kn = pl.BlockSpec((TOP_K, tr), lambda i: (0, i))
    return pl.pallas_call(
        _route_kernel,
        grid=(N // tr,),
        in_specs=[pl.BlockSpec((tr, D), lambda i: (i, 0)),
                  pl.BlockSpec((E, D), lambda i: (0, 0)),
                  pl.BlockSpec((E, 1), lambda i: (0, 0))],
        out_specs=[out_kn, out_kn, out_kn, pl.BlockSpec((E, 1), lambda i: (0, 0))],
        out_shape=[jax.ShapeDtypeStruct((TOP_K, N), I32), jax.ShapeDtypeStruct((TOP_K, N), F32),
                   jax.ShapeDtypeStruct((TOP_K, N), I32), jax.ShapeDtypeStruct((E, 1), I32)],
        scratch_shapes=[pltpu.VMEM((E, 1), F32)],
        compiler_params=_cparams("arbitrary"),
        name="route",
    )(h2, w_router.T, router_bias.reshape(E, 1))


def _dest_kernel(e_ref, rank_ref, start_ref, d_ref):
    tr = e_ref.shape[1]
    iota_e = lax.broadcasted_iota(I32, (N_EXPERTS, tr), 0)
    starts = start_ref[...]
    rows = [jnp.sum(jnp.where(iota_e == e_ref[j:j + 1, :], starts, 0), axis=0, keepdims=True)
            for j in range(TOP_K)]
    d_ref[...] = jnp.concatenate(rows, axis=0) + rank_ref[...]


def _dest_call(top_e, rank, pad_starts, tr):
    K, N = top_e.shape
    spec = pl.BlockSpec((K, tr), lambda i: (0, i))
    return pl.pallas_call(
        _dest_kernel,
        grid=(N // tr,),
        in_specs=[spec, spec, pl.BlockSpec((N_EXPERTS, 1), lambda i: (0, 0))],
        out_specs=spec,
        out_shape=jax.ShapeDtypeStruct((K, N), I32),
        compiler_params=_cparams("arbitrary"),
        name="dest",
    )(top_e, rank, pad_starts.reshape(N_EXPERTS, 1))


def _sc_dispatch_call(h2t, dest3, n_rows):
    N = h2t.shape[0]
    info = plsc.get_sparse_core_info()
    n_workers = info.num_cores * info.num_subcores
    n_chunks = N // (SC_CHUNK * n_workers)
    assert n_chunks * SC_CHUNK * n_workers == N
    mesh = plsc.VectorSubcoreMesh(core_axis_name="c", subcore_axis_name="s")

    @functools.partial(
        pl.kernel, mesh=mesh,
        out_type=(jax.ShapeDtypeStruct((n_rows,) + h2t.shape[1:], h2t.dtype),
                  jax.ShapeDtypeStruct((n_rows, LANES), I32)),
        scratch_types=[pltpu.VMEM((TOP_K, SC_CHUNK), I32),
                       pltpu.VMEM((SC_CHUNK,) + h2t.shape[1:], h2t.dtype),
                       pltpu.VMEM((SC_CHUNK, LANES), I32)],
        name="sc_dispatch",
    )
    def scatter_rows(h_hbm, dest_hbm, xs_hbm, tag_hbm, idx_v, rows_v, tag_v):
        wid = lax.axis_index("s") * info.num_cores + lax.axis_index("c")
        zeros = jnp.zeros((info.num_lanes,), I32)

        @pl.loop(0, SC_CHUNK)
        def _(r):
            for l0 in range(0, LANES, info.num_lanes):
                tag_v[r, pl.ds(l0, info.num_lanes)] = zeros

        @pl.loop(0, n_chunks)
        def _(c):
            chunk = wid * n_chunks + c
            base = chunk * SC_CHUNK
            pltpu.sync_copy(dest_hbm.at[chunk], idx_v)
            pltpu.sync_copy(h_hbm.at[pl.ds(base, SC_CHUNK)], rows_v)
            for j in range(TOP_K):
                pltpu.sync_copy(rows_v, xs_hbm.at[idx_v.at[j]])

                @pl.loop(0, SC_CHUNK)
                def _(r):
                    tag_v[r, pl.ds(0, info.num_lanes)] = zeros + ((base + r) * TOP_K + j)

                pltpu.sync_copy(tag_v, tag_hbm.at[idx_v.at[j]])

    return scatter_rows(h2t, dest3)


def _moe_kernel(n_tokens, be_ref, nu_ref, nv_ref, xs_ref, xtag_ref, wgu_ref, wdn_ref, h_ref, x1_ref,
                mod_ref, wsg_ref, wsd_ref, yt_hbm, xsh_ref, wgu_bf, wdn_bf, ybuf, tag_v, tag_s,
                sem_tag, sem_rows):
    i = pl.program_id(0)
    n_steps = pl.num_programs(0)
    cur = i % 2
    n_real = yt_hbm.shape[0] - 2 * MOE_ROWS
    n_shared_tiles = n_tokens // h_ref.shape[0]

    def row_copy(buf, r, tag):
        rows = pl.ds(pl.multiple_of(r * SUBLANES, SUBLANES), SUBLANES)
        return pltpu.make_async_copy(ybuf.at[buf, rows, :], yt_hbm.at[tag], sem_rows.at[buf])

    n_used = nu_ref[0]
    prv = 1 - cur

    def wait_sent(buf):
        pltpu.make_async_copy(ybuf.at[buf], ybuf.at[buf], sem_rows.at[buf]).wait()

    def send_prev():
        for r in range(MOE_ROWS):
            row_copy(prv, r, tag_s[prv, r]).start()

    @pl.when((i >= 2) & (i - 2 < n_used))
    def _():
        wait_sent(cur)

    i_blk = jnp.minimum(i, n_steps - 2)
    @pl.when((i < n_used) & ((i == 0) | (be_ref[i_blk] != be_ref[jnp.maximum(i_blk - 1, 0)])))
    def _():
        wgu_bf[...] = wgu_ref[0].astype(BF16)
        wdn_bf[...] = wdn_ref[0].astype(BF16)

    def compute():
        row = lax.broadcasted_iota(I32, (MOE_ROWS, LANES), 0)
        tags = jnp.where(row < nv_ref[i_blk], xtag_ref[...], n_real + cur * MOE_ROWS + row)
        tag_v[...] = tags.astype(F32).T[0:1, :].astype(I32)
        cp = pltpu.make_async_copy(tag_v, tag_s.at[pl.ds(cur, 1)], sem_tag)
        cp.start()
        xb = jnp.concatenate([xs_ref[pl.ds(s, MOE_ROWS, stride=SUBLANES), :]
                              for s in range(SUBLANES)], axis=1).astype(BF16)
        gu = _dot(xb, wgu_bf[...])
        act = _silu(gu[:, :D_EXPERT]) * gu[:, D_EXPERT:]
        y = _dot(act.astype(BF16), wdn_bf[...])
        for s in range(SUBLANES):
            ybuf[cur, pl.ds(s, MOE_ROWS, stride=SUBLANES), :] = y[:, s * LANES:(s + 1) * LANES]
        cp.wait()

    @pl.when((i == 0) & (i < n_used))
    def _():
        compute()

    @pl.when((i > 0) & (i < n_used))
    def _():
        send_prev()
        compute()

    @pl.when((i > 0) & (i == n_used))
    def _():
        send_prev()

    @pl.when((i == n_steps - 1) & (i - 1 < n_used))
    def _():
        wait_sent(prv)

    @pl.when((i % SHARED_EVERY == 0) & (i // SHARED_EVERY < n_shared_tiles))
    def _():
        gu = _mm(h_ref[...], wsg_ref[...])
        act = _silu(gu[:, :D_EXPERT]) * gu[:, D_EXPERT:]
        xsh_ref[...] = x1_ref[...] + mod_ref[0, 5:6, :] * _mm(act, wsd_ref[...])


def _moe_call(block_expert, n_used, n_valid, xs, xtag, w_gu, w_dn, h2, x1, mod3, w_sh_gu, w_sh_dn,
              tokens_per_batch, n_out_rows):
    P = xs.shape[0] // SUBLANES
    N, D = h2.shape
    nb = P // MOE_ROWS
    ts = _tile(tokens_per_batch, SHARED_TILE)
    n_shared = N // ts
    assert n_shared * SHARED_EVERY <= N * TOP_K // MOE_ROWS
    tiles_per_batch = tokens_per_batch // ts
    wsg = w_sh_gu.astype(BF16)
    wsd = w_sh_dn.astype(BF16)
    blk = lambda i, be, nu, nv: (jnp.minimum(i, nu[0] - 1), 0)
    wblk = lambda i, be, nu, nv: (be[jnp.minimum(i, nu[0] - 1)], 0, 0)
    stile = lambda i: jnp.minimum(i // SHARED_EVERY, n_shared - 1)
    sblk = lambda i, be, nu, nv: (stile(i), 0)
    return pl.pallas_call(
        functools.partial(_moe_kernel, N),
        grid_spec=pltpu.PrefetchScalarGridSpec(
            num_scalar_prefetch=3,
            grid=(nb + 1,),
            in_specs=[pl.BlockSpec((MOE_ROWS * SUBLANES, LANES), blk),
                      pl.BlockSpec((MOE_ROWS, LANES), blk),
                      pl.BlockSpec((1, D, 2 * D_EXPERT), wblk),
                      pl.BlockSpec((1, D_EXPERT, D), wblk),
                      pl.BlockSpec((ts, D), sblk),
                      pl.BlockSpec((ts, D), sblk),
                      pl.BlockSpec((1, 6, D), lambda i, be, nu, nv: (stile(i) // tiles_per_batch, 0, 0)),
                      pl.BlockSpec(wsg.shape, lambda i, be, nu, nv: (0, 0)),
                      pl.BlockSpec(wsd.shape, lambda i, be, nu, nv: (0, 0))],
            out_specs=[pl.BlockSpec(memory_space=pl.ANY),
                       pl.BlockSpec((ts, D), sblk)],
            scratch_shapes=[pltpu.VMEM((D, 2 * D_EXPERT), BF16), pltpu.VMEM((D_EXPERT, D), BF16),
                            pltpu.VMEM((2, MOE_ROWS * SUBLANES, LANES), F32),
                            pltpu.VMEM((1, MOE_ROWS), I32), pltpu.SMEM((2, MOE_ROWS), I32),
                            pltpu.SemaphoreType.DMA, pltpu.SemaphoreType.DMA((2,))]),
        out_shape=[jax.ShapeDtypeStruct((n_out_rows, SUBLANES, LANES), F32),
                   jax.ShapeDtypeStruct((N, D), F32)],
        compiler_params=_cparams("arbitrary"),
        name="moe",
    )(block_expert, n_used, n_valid, xs, xtag, w_gu, w_dn, h2, x1, mod3, wsg, wsd)


def _combine_kernel(w_hbm, yt_ref, xs_ref, mod_ref, o_ref, wts, routed, sem_w):
    i = pl.program_id(0)
    tc = xs_ref.shape[0]
    cp_w = pltpu.make_async_copy(w_hbm.at[i], wts, sem_w)
    cp_w.start()
    cp_w.wait()

    def wsum(tt, carry):
        for u in range(SUBLANES):
            t = tt * SUBLANES + u
            first = pl.multiple_of(t * (TOP_K * SUBLANES), SUBLANES)
            acc = yt_ref[pl.ds(first, SUBLANES), :] * wts[0, t]
            for j in range(1, TOP_K):
                acc = acc + yt_ref[pl.ds(first + j * SUBLANES, SUBLANES), :] * wts[j, t]
            routed[pl.ds(pl.multiple_of(t * SUBLANES, SUBLANES), SUBLANES), :] = acc
        return carry

    lax.fori_loop(0, tc // SUBLANES, wsum, 0)
    routed2d = jnp.concatenate([routed[pl.ds(s, tc, stride=SUBLANES), :] for s in range(SUBLANES)],
                               axis=1)
    o_ref[...] = xs_ref[...] + mod_ref[0, 5:6, :] * routed2d


def _combine_call(w3, yt, x_shared, mod3, tokens_per_batch, tc):
    N, D = x_shared.shape
    tiles_per_batch = tokens_per_batch // tc
    tok = pl.BlockSpec((tc, D), lambda i: (i, 0))
    return pl.pallas_call(
        _combine_kernel,
        grid=(N // tc,),
        in_specs=[pl.BlockSpec(memory_space=pl.ANY),
                  pl.BlockSpec((tc * TOP_K * SUBLANES, LANES), lambda i: (i, 0)),
                  tok,
                  pl.BlockSpec((1, 6, D), lambda i: (i // tiles_per_batch, 0, 0))],
        out_specs=tok,
        out_shape=jax.ShapeDtypeStruct((N, D), F32),
        scratch_shapes=[pltpu.SMEM((TOP_K, tc), F32), pltpu.VMEM((tc * SUBLANES, LANES), F32),
                        pltpu.SemaphoreType.DMA],
        compiler_params=_cparams("arbitrary"),
        name="combine",
    )(w3, yt, x_shared, mod3)


def _tile(n, pref):
    t = min(n, pref)
    assert n % t == 0, (n, t)
    return t


def _layer(x, mod3, positions, norm_mix, w_in, rwkv_mu, decay_w0, decay_up, iclr_a0, iclr_up,
           gate_up, rwkv_k_k, rwkv_k_a, rwkv_r_k, ln_x_w, ln_x_b, q_a_norm, w_q_b, kv_a_norm,
           w_kv_b, q_norm, k_norm, w_out, norm_ffn, w_router, router_bias, w_e_gate_up, w_e_down,
           w_sh_gate_up, w_sh_down):
    B, T, D = x.shape
    N = B * T
    assert T % SCAN_CHUNK == 0
    (r, lw, k2, v, kk, akk, g, bonus, q_pad, k_pad, v_pad) = _pre_call(
        x, mod3, positions, norm_mix, w_in, rwkv_mu, decay_w0, decay_up, iclr_a0, iclr_up,
        gate_up, rwkv_k_k, rwkv_k_a, rwkv_r_k, q_a_norm, w_q_b, kv_a_norm, w_kv_b, q_norm, k_norm,
        tm=_tile(T, 512))
    y = _scan_call(r, lw, k2, v, kk, akk)
    o_pad = _attn_call(q_pad, k_pad, v_pad)
    x1, h2, h2t = _post_call(y, bonus, g, o_pad, x, mod3, ln_x_w, ln_x_b, w_out, norm_ffn,
                             tm=_tile(T, 512))
    x1 = x1.reshape(N, D)
    h2 = h2.reshape(N, D)
    h2t = h2t.reshape(N, D // LANES, LANES)

    tr = _tile(N, 512)
    top_e, wts, rank, counts = _route_call(h2, w_router, router_bias, tr)
    counts = counts.reshape(N_EXPERTS)
    padded = (counts + MOE_ROWS - 1) // MOE_ROWS * MOE_ROWS
    pad_ends = jnp.cumsum(padded)
    pad_starts = pad_ends - padded
    n_blocks = (N * TOP_K + N_EXPERTS * (MOE_ROWS - 1)) // MOE_ROWS
    block_expert = jnp.minimum(
        jnp.searchsorted(pad_ends, jnp.arange(n_blocks, dtype=I32) * MOE_ROWS, side='right'),
        N_EXPERTS - 1).astype(I32)
    n_used = (pad_ends[-1:] // MOE_ROWS).astype(I32)
    block_row0 = jnp.arange(n_blocks + 1, dtype=I32) * MOE_ROWS
    be_pad = jnp.concatenate([block_expert, block_expert[-1:]])
    n_valid = jnp.clip((pad_starts + counts)[be_pad] - block_row0, 0, MOE_ROWS).astype(I32)
    dest = _dest_call(top_e, rank, pad_starts.astype(I32), tr)

    dest3 = dest.reshape(TOP_K, N // SC_CHUNK, SC_CHUNK).transpose(1, 0, 2)
    n_rows = n_blocks * MOE_ROWS
    xs, xtag = _sc_dispatch_call(h2t, dest3, n_rows)
    yt, x_shared = _moe_call(block_expert, n_used, n_valid, xs.reshape(n_rows * SUBLANES, LANES), xtag,
                             w_e_gate_up, w_e_down, h2, x1, mod3, w_sh_gate_up, w_sh_down, T,
                             N * TOP_K + 2 * MOE_ROWS)
    tc = _tile(T, COMBINE_TILE)
    w3 = wts.reshape(TOP_K, N // tc, tc).transpose(1, 0, 2)
    out = _combine_call(w3, yt.reshape(-1, LANES), x_shared, mod3, T, tc)
    return out.reshape(B, T, D)


def kernel(x, c, positions, ada_w, ada_b, norm_mix, w_in, rwkv_mu, decay_w0, decay_up, iclr_a0, iclr_up, gate_up, rwkv_k_k, rwkv_k_a, rwkv_r_k, ln_x_w, ln_x_b, q_a_norm, w_q_b, kv_a_norm, w_kv_b, q_norm, k_norm, w_out, norm_ffn, w_router, router_bias, w_e_gate_up, w_e_down, w_sh_gate_up, w_sh_down):
    B, T, D = x.shape
    depth = ada_w.shape[0]
    for l in range(depth):
        mod3 = _mod_call(c, ada_w[l], ada_b[l]).reshape(B, 6, D)
        x = _layer(x, mod3, positions, norm_mix[l], w_in[l], rwkv_mu[l], decay_w0[l], decay_up[l],
                   iclr_a0[l], iclr_up[l], gate_up[l], rwkv_k_k[l], rwkv_k_a[l], rwkv_r_k[l],
                   ln_x_w[l], ln_x_b[l], q_a_norm[l], w_q_b[l], kv_a_norm[l], w_kv_b[l],
                   q_norm[l], k_norm[l], w_out[l], norm_ffn[l], w_router[l], router_bias[l],
                   w_e_gate_up[l], w_e_down[l], w_sh_gate_up[l], w_sh_down[l])
    return x
```

```python
import functools
import math

import jax
import jax.numpy as jnp
import numpy as np
from jax import lax
from jax.experimental import pallas as pl
from jax.experimental.pallas import tpu as pltpu
from jax.experimental.pallas import tpu_sc as plsc

F32 = jnp.float32
BF16 = jnp.bfloat16
I32 = jnp.int32

NORM_EPS = 1e-6
GN_EPS = 64e-5
RWKV_HEADS = 8
RWKV_HEAD_DIM = 64
D_RWKV = 512
DECAY_LORA = 32
ICLR_LORA = 32
GATE_LORA = 96
MLA_HEADS = 8
QK_NOPE_DIM = 64
QK_ROPE_DIM = 32
QK_HEAD_DIM = 96
V_HEAD_DIM = 64
Q_LORA_RANK = 256
KV_LORA_RANK = 128
ROPE_THETA = 10000.0
N_EXPERTS = 256
TOP_K = 8
N_GROUPS = 8
TOPK_GROUPS = 4
GROUP_SIZE = N_EXPERTS // N_GROUPS
D_EXPERT = 256
ROUTED_SCALE = 2.5
MOE_ROWS = 512
SC_CHUNK = 64
COMBINE_TILE = 256

LANES = 128
SUBLANES = 8
HEAD_PAD = 128
VMEM_LIMIT = 56 * 1024 * 1024

PRE_SUBTILES = 1
SCAN_CHUNK = 64
SCAN_BLOCK = 512
ATTN_TILE = 512
ATTN_Q_SCALE = QK_HEAD_DIM ** -0.5 * math.log2(math.e)
NEG_INF = float("-inf")


def _cparams(*sem):
    return pltpu.CompilerParams(dimension_semantics=sem, vmem_limit_bytes=VMEM_LIMIT)


def _split2(a):
    hi = a.astype(BF16)
    lo = (a - hi.astype(F32)).astype(BF16)
    return hi, lo


def _split3(a):
    hi = a.astype(BF16)
    r1 = a - hi.astype(F32)
    mid = r1.astype(BF16)
    lo = (r1 - mid.astype(F32)).astype(BF16)
    return hi, mid, lo


def _dot(a, b, dims=None):
    if dims is None:
        return jnp.dot(a, b, preferred_element_type=F32)
    return lax.dot_general(a, b, (dims, ((), ())), preferred_element_type=F32)


def _mm(a, b, dims=None):
    return _dot(a.astype(BF16), b.astype(BF16), dims)


def _mm3(a, b, dims=None):
    ah, al = _split2(a)
    bh, bl = _split2(b)
    return _dot(ah, bh, dims) + (_dot(ah, bl, dims) + _dot(al, bh, dims))


def _mm_exact_rhs(a, b_exact_bf16, dims=None):
    h, m, l = _split3(a)
    return _dot(h, b_exact_bf16, dims) + (_dot(m, b_exact_bf16, dims) + _dot(l, b_exact_bf16, dims))


NT = ((1,), (1,))
TN = ((0,), (0,))


def _sigmoid(z):
    return 1.0 / (1.0 + jnp.exp(-z))


def _silu(z):
    return z * _sigmoid(z)


def _seg_ones(width, seg):
    r = lax.broadcasted_iota(I32, (width, width), 0) // seg
    c = lax.broadcasted_iota(I32, (width, width), 1) // seg
    return jnp.where(r == c, 1.0, 0.0).astype(BF16)


def _segsum(a, ones_bd):
    hi, lo = _split2(a)
    return _dot(hi, ones_bd) + _dot(lo, ones_bd)


def _mod_kernel(c_ref, w_ref, b_ref, o_ref):
    ca = _silu(c_ref[...])
    o_ref[...] = _mm3(ca, w_ref[...]) + b_ref[...]


def _mod_call(c, ada_w, ada_b):
    B, D = c.shape
    n6 = ada_w.shape[1]
    tn = D
    return pl.pallas_call(
        _mod_kernel,
        grid=(n6 // tn,),
        in_specs=[pl.BlockSpec((B, D), lambda j: (0, 0)),
                  pl.BlockSpec((D, tn), lambda j: (0, j)),
                  pl.BlockSpec((1, tn), lambda j: (0, j))],
        out_specs=pl.BlockSpec((B, tn), lambda j: (0, j)),
        out_shape=jax.ShapeDtypeStruct((B, n6), F32),
        compiler_params=_cparams("arbitrary"),
        name="mod",
    )(c, ada_w, ada_b.reshape(1, n6))


def _pre_kernel(x_ref, mod_ref, pos_ref, nmix_ref, wrkv_ref, wlora_ref, wmla_ref,
                mu_rkv_ref, mu_lora_ref, wup_ref, w0_ref, a0_ref, kk_ref, ka_ref, rk_ref,
                qan_ref, wqb_ref, kvan_ref, wkb_ref, wvb_ref, qn_ref, kn_ref, invf_ref,
                r_ref, lw_ref, k_ref, v_ref, kkn_ref, akk_ref, g_ref, bonus_ref,
                q_ref, kout_ref, vout_ref,
                carry_rkv, carry_lora):
    ti = pl.program_id(1)
    tm = x_ref.shape[1]

    @pl.when(ti == 0)
    def _():
        carry_rkv[...] = jnp.zeros_like(carry_rkv)
        carry_lora[...] = jnp.zeros_like(carry_lora)

    ts = tm // PRE_SUBTILES
    sh_a = mod_ref[0, 0:1, :]
    sc_a = mod_ref[0, 1:2, :]
    ones64 = _seg_ones(D_RWKV, RWKV_HEAD_DIM)
    row0 = lax.broadcasted_iota(I32, (ts, 1), 0) == 0
    half = QK_ROPE_DIM // 2
    last_rows = {}

    def sub_tile(s):
        rows = slice(s * ts, (s + 1) * ts)
        xb = x_ref[0, rows, :]
        ms = jnp.mean(xb * xb, axis=-1, keepdims=True)
        h = xb * lax.rsqrt(ms + NORM_EPS) * nmix_ref[...] * (1.0 + sc_a) + sh_a
        hb = h.astype(BF16)
        yield
        u_rkv = _dot(hb, wrkv_ref[...])
        u_lora = _dot(hb, wlora_ref[...])
        u_mla = _dot(hb, wmla_ref[...])
        last_rows[s] = (u_rkv[ts - 1:ts, :], u_lora[ts - 1:ts, :])
        yield

        before_rkv, before_lora = (carry_rkv[...], carry_lora[...]) if s == 0 else last_rows[s - 1]
        prev_rkv = jnp.where(row0, before_rkv, pltpu.roll(u_rkv, 1, 0))
        prev_lora = jnp.where(row0, before_lora, pltpu.roll(u_lora, 1, 0))
        if s == PRE_SUBTILES - 1:
            carry_rkv[...] = u_rkv[ts - 1:ts, :]
            carry_lora[...] = u_lora[ts - 1:ts, :]
        us = u_rkv + (prev_rkv - u_rkv) * mu_rkv_ref[...]
        ul = u_lora + (prev_lora - u_lora) * mu_lora_ref[...]
        r = us[:, 0:D_RWKV]
        k = us[:, D_RWKV:2 * D_RWKV]
        v = us[:, 2 * D_RWKV:3 * D_RWKV]
        lane_l = lax.broadcasted_iota(I32, ul.shape, 1)
        t_in = jnp.where(lane_l < DECAY_LORA, jnp.tanh(ul),
                         jnp.where(lane_l < DECAY_LORA + ICLR_LORA, ul, _sigmoid(ul)))
        yield
        up = _mm(t_in, wup_ref[...])
        yield
        z = w0_ref[...] + up[:, 0:D_RWKV]
        lw = (-math.exp(-0.5)) * _sigmoid(z)
        a = _sigmoid(a0_ref[...] + up[:, D_RWKV:2 * D_RWKV])
        g = up[:, 2 * D_RWKV:3 * D_RWKV]
        kk = k * kk_ref[...]
        k2 = k * (1.0 + (a - 1.0) * ka_ref[...])
        yield
        ss = _segsum(kk * kk, ones64)
        bonus_sum = _segsum(r * k2 * rk_ref[...], ones64)
        yield
        kk = kk * lax.rsqrt(jnp.maximum(ss, 1e-24))
        r_ref[0, rows, :] = r
        lw_ref[0, rows, :] = lw
        k_ref[0, rows, :] = k2
        v_ref[0, rows, :] = v
        kkn_ref[0, rows, :] = kk
        akk_ref[0, rows, :] = a * kk
        g_ref[0, rows, :] = g
        bonus_ref[0, rows, :] = bonus_sum * v
        yield

        q_lat = u_mla[:, 0:Q_LORA_RANK]
        kv_lat = u_mla[:, Q_LORA_RANK:Q_LORA_RANK + KV_LORA_RANK]
        kpe_tile = u_mla[:, Q_LORA_RANK + KV_LORA_RANK:]
        qn = q_lat * lax.rsqrt(jnp.mean(q_lat * q_lat, axis=-1, keepdims=True) + NORM_EPS) * qan_ref[...]
        kvn = kv_lat * lax.rsqrt(jnp.mean(kv_lat * kv_lat, axis=-1, keepdims=True) + NORM_EPS) * kvan_ref[...]
        kvb = kvn.astype(BF16)
        yield
        q_raw = _mm(qn, wqb_ref[...])
        k_raw = _dot(kvb, wkb_ref[...])
        v_pad = _dot(kvb, wvb_ref[...])
        yield
        kpe_h = pltpu.roll(kpe_tile, QK_NOPE_DIM, 1)
        cos_t, s1, s2 = rope_tables(s)

        def tables(gain, scale):
            g_s = gain * scale
            return (cos_t * g_s, s1 * pltpu.roll(g_s, HEAD_PAD - half, 1),
                    s2 * pltpu.roll(g_s, half, 1))

        def norm_rope(xh, tabs):
            c_g, s1_g, s2_g = tabs
            ssq = jnp.sum(xh * xh, axis=-1, keepdims=True) * (1.0 / QK_HEAD_DIM)
            rot = xh * c_g + pltpu.roll(xh, HEAD_PAD - half, 1) * s1_g + pltpu.roll(xh, half, 1) * s2_g
            return rot * lax.rsqrt(ssq + NORM_EPS)

        q_tabs = tables(qn_ref[...], ATTN_Q_SCALE)
        k_tabs = tables(kn_ref[...], 1.0)
        vout_ref[0, rows, :] = v_pad.astype(BF16)
        yield
        for hh in range(MLA_HEADS):
            sl = slice(hh * HEAD_PAD, (hh + 1) * HEAD_PAD)
            q_ref[0, rows, sl] = norm_rope(q_raw[:, sl], q_tabs).astype(BF16)
            kout_ref[0, rows, sl] = norm_rope(k_raw[:, sl] + kpe_h, k_tabs).astype(BF16)
            yield

    def rope_tables(s):
        ang_t = invf_ref[...] * pos_ref[0, :, s * ts:(s + 1) * ts].astype(F32)
        frow = lax.broadcasted_iota(I32, (half, HEAD_PAD), 0)
        flane = lax.broadcasted_iota(I32, (half, HEAD_PAD), 1)
        at_x1 = flane == frow + QK_NOPE_DIM
        at_x2 = flane == frow + QK_NOPE_DIM + half
        e_cos = jnp.where(at_x1 | at_x2, 1.0, 0.0).astype(BF16)
        e_sin = jnp.concatenate([jnp.where(at_x1, -1.0, 0.0), jnp.where(at_x2, 1.0, 0.0)],
                                axis=1).astype(BF16)
        lane = lax.broadcasted_iota(I32, (1, HEAD_PAD), 1)
        off_rope = jnp.where((lane >= QK_NOPE_DIM) & (lane < QK_HEAD_DIM), 0.0, 1.0)
        cos_t = _mm_exact_rhs(jnp.cos(ang_t), e_cos, TN) + off_rope
        sin2 = _mm_exact_rhs(jnp.sin(ang_t), e_sin, TN)
        return cos_t, sin2[:, :HEAD_PAD], sin2[:, HEAD_PAD:]

    live = [sub_tile(s) for s in range(PRE_SUBTILES)]
    while live:
        live = [gen for gen in live if next(gen, "done") != "done"]


def _pad_heads(w, n_heads, width):
    kdim = w.shape[0]
    w = w.reshape(kdim, n_heads, width)
    w = jnp.pad(w, ((0, 0), (0, 0), (0, HEAD_PAD - width)))
    return w.reshape(kdim, n_heads * HEAD_PAD)


def _pre_call(x, mod3, positions, norm_mix, w_in, rwkv_mu, decay_w0, decay_up, iclr_a0, iclr_up,
              gate_up, k_k, k_a, r_k, q_a_norm, w_q_b, kv_a_norm, w_kv_b, q_norm, k_norm, tm):
    B, T, D = x.shape
    n_rkv = 3 * D_RWKV
    n_lora = DECAY_LORA + ICLR_LORA + GATE_LORA
    LORA_PAD = 256
    MLA_PAD = 512
    n_mla = Q_LORA_RANK + KV_LORA_RANK + QK_ROPE_DIM
    w_rkv = w_in[:, :n_rkv].astype(BF16)
    w_lora = jnp.pad(w_in[:, n_rkv:n_rkv + n_lora], ((0, 0), (0, LORA_PAD - n_lora))).astype(BF16)
    w_mla = jnp.pad(w_in[:, n_rkv + n_lora:], ((0, 0), (0, MLA_PAD - n_mla))).astype(BF16)
    mu_rkv = rwkv_mu[:n_rkv].reshape(1, n_rkv)
    mu_lora = jnp.pad(rwkv_mu[n_rkv:], (0, LORA_PAD - n_lora)).reshape(1, LORA_PAD)
    w_up = jnp.zeros((LORA_PAD, n_rkv), F32)
    w_up = w_up.at[0:DECAY_LORA, 0:D_RWKV].set(decay_up)
    w_up = w_up.at[DECAY_LORA:DECAY_LORA + ICLR_LORA, D_RWKV:2 * D_RWKV].set(iclr_up)
    w_up = w_up.at[DECAY_LORA + ICLR_LORA:n_lora, 2 * D_RWKV:].set(gate_up)
    w_up = w_up.astype(BF16)
    w_qb = _pad_heads(w_q_b, MLA_HEADS, QK_HEAD_DIM).astype(BF16)
    w_kv3 = w_kv_b.reshape(KV_LORA_RANK, MLA_HEADS, QK_NOPE_DIM + V_HEAD_DIM)
    w_kb = _pad_heads(w_kv3[:, :, :QK_NOPE_DIM].reshape(KV_LORA_RANK, -1), MLA_HEADS, QK_NOPE_DIM).astype(BF16)
    w_vb = _pad_heads(w_kv3[:, :, QK_NOPE_DIM:].reshape(KV_LORA_RANK, -1), MLA_HEADS, V_HEAD_DIM).astype(BF16)
    qn_pad = jnp.pad(q_norm, (0, HEAD_PAD - QK_HEAD_DIM)).reshape(1, HEAD_PAD)
    kn_pad = jnp.pad(k_norm, (0, HEAD_PAD - QK_HEAD_DIM)).reshape(1, HEAD_PAD)
    inv_freq = ROPE_THETA ** (-jnp.arange(0, QK_ROPE_DIM, 2, dtype=F32) / QK_ROPE_DIM)
    invf = inv_freq.reshape(QK_ROPE_DIM // 2, 1)
    pos3 = positions.reshape(B, 1, T)
    HP = MLA_HEADS * HEAD_PAD

    row = lambda n: pl.BlockSpec((1, n), lambda b, t: (0, 0))
    full = lambda a: pl.BlockSpec(a.shape, lambda b, t: (0,) * a.ndim)
    tok = lambda n: pl.BlockSpec((1, tm, n), lambda b, t: (b, t, 0))
    outs = ([jax.ShapeDtypeStruct((B, T, D_RWKV), F32)] * 8
            + [jax.ShapeDtypeStruct((B, T, HP), BF16)] * 3)
    return pl.pallas_call(
        _pre_kernel,
        grid=(B, T // tm),
        in_specs=[tok(D),
                  pl.BlockSpec((1, 6, D), lambda b, t: (b, 0, 0)),
                  pl.BlockSpec((1, 1, tm), lambda b, t: (b, 0, t)),
                  row(D), full(w_rkv), full(w_lora), full(w_mla),
                  row(n_rkv), row(LORA_PAD), full(w_up), row(D_RWKV), row(D_RWKV),
                  row(D_RWKV), row(D_RWKV), row(D_RWKV),
                  row(Q_LORA_RANK), full(w_qb), row(KV_LORA_RANK), full(w_kb), full(w_vb),
                  row(HEAD_PAD), row(HEAD_PAD), full(invf)],
        out_specs=[tok(D_RWKV)] * 8 + [tok(HP)] * 3,
        out_shape=outs,
        scratch_shapes=[pltpu.VMEM((1, n_rkv), F32), pltpu.VMEM((1, LORA_PAD), F32)],
        compiler_params=_cparams("arbitrary", "arbitrary"),
        name="pre",
    )(x, mod3, pos3, norm_mix.reshape(1, D), w_rkv, w_lora, w_mla, mu_rkv, mu_lora, w_up,
      decay_w0.reshape(1, -1), iclr_a0.reshape(1, -1), k_k.reshape(1, -1), k_a.reshape(1, -1),
      r_k.reshape(1, -1), q_a_norm.reshape(1, -1), w_qb, kv_a_norm.reshape(1, -1), w_kb, w_vb,
      qn_pad, kn_pad, invf)


def _scan_kernel(r_ref, lw_ref, k_ref, v_ref, kk_ref, akk_ref, y_ref, state):
    C = SCAN_CHUNK
    n_chunks = r_ref.shape[1] // C
    n_pairs = r_ref.shape[2] // LANES

    @pl.when(pl.program_id(1) == 0)
    def _():
        state[...] = jnp.zeros_like(state)

    ri = lax.broadcasted_iota(I32, (C, C), 0)
    ci = lax.broadcasted_iota(I32, (C, C), 1)
    tri_incl = jnp.where(ci <= ri, 1.0, 0.0).astype(BF16)
    r2 = lax.broadcasted_iota(I32, (2 * C, 2 * C), 0)
    c2 = lax.broadcasted_iota(I32, (2 * C, 2 * C), 1)
    same = (r2 >= C) == (c2 >= C)
    strict = same & (c2 < r2)
    incl = same & (c2 <= r2)
    eye = jnp.where(c2 == r2, 1.0, 0.0)
    head0 = lax.broadcasted_iota(I32, (C, LANES), 1) < RWKV_HEAD_DIM

    def stack2(a):
        return jnp.concatenate([jnp.where(head0, a, 0.0), jnp.where(head0, 0.0, a)], axis=0)

    C2 = 2 * C
    cat0 = lambda *a: jnp.concatenate(a, axis=0)
    cat1 = lambda *a: jnp.concatenate(a, axis=1)

    items = []
    for c in range(n_chunks):
        rows = slice(c * C, (c + 1) * C)
        lw = lw_ref[0, rows, :]
        cum = _mm_exact_rhs_left(tri_incl, lw)
        cum_end = cum[C - 1:C, :]
        w_end = jnp.exp(cum_end)
        e_pos = jnp.exp(cum)
        e_neg = jnp.exp(-cum)
        e_prev = jnp.exp(cum - lw)
        e_end = jnp.exp(cum_end - cum)
        kk = kk_ref[0, rows, :]
        k2 = k_ref[0, rows, :]
        pneg = -akk_ref[0, rows, :]
        vv = v_ref[0, rows, :]
        rt = r_ref[0, rows, :] * e_pos
        bt = kk * e_prev
        pt = pneg * e_neg
        kt = k2 * e_neg
        ph = pneg * e_end
        kh = k2 * e_end
        for pp in range(n_pairs):
            sl = slice(pp * LANES, (pp + 1) * LANES)
            items.append(dict(
                c=c, p=pp, w_end=w_end[:, sl],
                bt2=stack2(bt[:, sl]).astype(BF16), rt2=stack2(rt[:, sl]).astype(BF16),
                pk2=cat0(stack2(pt[:, sl]), stack2(kt[:, sl])).astype(BF16),
                phkh2=cat0(stack2(ph[:, sl]), stack2(kh[:, sl])).astype(BF16),
                v2=stack2(vv[:, sl])))
    for it in items:
        ab = _dot(cat0(it['bt2'], it['rt2']), it['pk2'], NT)
        it['a_ab'] = jnp.where(strict, ab[:C2, :C2], 0.0)
        it['a_ak'] = jnp.where(strict, ab[:C2, C2:], 0.0).astype(BF16)
        it['b_rpk'] = cat1(jnp.where(incl, ab[C2:, :C2], 0.0), jnp.where(incl, ab[C2:, C2:], 0.0)).astype(BF16)
        it['tinv'] = eye + it['a_ab']
    for it in items:
        it['apow'] = _mm(it['a_ab'], it['a_ab'])
    for _ in range(int(math.log2(C)) - 1):
        for it in items:
            both = _mm(cat0(it['apow'], it['tinv']), it['apow'])
            it['apow'] = both[:C2]
            it['tinv'] = it['tinv'] + both[C2:]
    for it in items:
        it['akv'] = _dot(it['a_ak'], it['v2'].astype(BF16))
    for it in items:
        tt = _dot(it['tinv'].astype(BF16), cat1(it['bt2'], it['akv'].astype(BF16)))
        it['tb_rt'] = cat0(tt[:, :LANES].astype(BF16), it['rt2'])
        it['tav'] = tt[:, LANES:]
    for it in items:
        pp = it['p']
        s0 = state[pp]
        top = _dot(it['tb_rt'], s0.astype(BF16), NT)
        u2 = top[:C2] + it['tav']
        uv = cat0(u2, it['v2']).astype(BF16)
        y2 = top[C2:] + _dot(it['b_rpk'], uv)
        state[pp] = s0 * it['w_end'] + _dot(uv, it['phkh2'], TN)
        y_ref[0, it['c'] * C:(it['c'] + 1) * C, pp * LANES:(pp + 1) * LANES] = y2[0:C] + y2[C:C2]


def _mm_exact_rhs_left(b_exact_bf16, a):
    h, m, l = _split3(a)
    return _dot(b_exact_bf16, h) + (_dot(b_exact_bf16, m) + _dot(b_exact_bf16, l))


def _scan_call(r, lw, k2, v, kk, akk):
    B, T, W = r.shape
    tb = _tile(T, SCAN_BLOCK)
    spec = pl.BlockSpec((1, tb, W), lambda b, c: (b, c, 0))
    return pl.pallas_call(
        _scan_kernel,
        grid=(B, T // tb),
        in_specs=[spec] * 6,
        out_specs=spec,
        out_shape=jax.ShapeDtypeStruct((B, T, W), F32),
        scratch_shapes=[pltpu.VMEM((W // LANES, 2 * RWKV_HEAD_DIM, LANES), F32)],
        compiler_params=_cparams("arbitrary", "arbitrary"),
        name="scan",
    )(r, lw, k2, v, kk, akk)


def _attn_kernel(q_ref, k_ref, v_ref, o_ref):
    T = q_ref.shape[1]
    tq = min(T, ATTN_TILE)
    row = lax.broadcasted_iota(I32, (tq, tq), 0)
    col = lax.broadcasted_iota(I32, (tq, tq), 1)
    causal = col <= row

    def update(q, kt, vt, carry, mask):
        m_old, l_old, acc = carry
        s = _dot(q, kt, NT)
        if mask:
            s = jnp.where(causal, s, NEG_INF)
        m_new = jnp.maximum(m_old, jnp.max(s, axis=-1, keepdims=True))
        alpha = jnp.exp2(m_old - m_new)
        p = jnp.exp2(s - m_new)
        l_new = alpha * l_old + jnp.sum(p, axis=-1, keepdims=True)
        acc = alpha * acc + _dot(p.astype(BF16), vt)
        return m_new, l_new, acc

    for qi in range(T // tq):
        q = q_ref[0, qi * tq:(qi + 1) * tq, :]
        carry = (jnp.full((tq, 1), NEG_INF, F32), jnp.zeros((tq, 1), F32),
                 jnp.zeros((tq, HEAD_PAD), F32))

        def body(ki, carry, q=q):
            rows = pl.ds(pl.multiple_of(ki * tq, tq), tq)
            return update(q, k_ref[0, rows, :], v_ref[0, rows, :], carry, False)

        carry = lax.fori_loop(0, qi, body, carry, unroll=True)
        diag = slice(qi * tq, (qi + 1) * tq)
        _, l_fin, acc = update(q, k_ref[0, diag, :], v_ref[0, diag, :], carry, True)
        o_ref[0, diag, :] = (acc / l_fin).astype(o_ref.dtype)


def _attn_call(q, k, v):
    B, T, HP = q.shape
    spec = pl.BlockSpec((1, T, HEAD_PAD), lambda b, h: (b, 0, h))
    return pl.pallas_call(
        _attn_kernel,
        grid=(B, MLA_HEADS),
        in_specs=[spec, spec, spec],
        out_specs=spec,
        out_shape=jax.ShapeDtypeStruct((B, T, HP), BF16),
        compiler_params=_cparams("arbitrary", "arbitrary"),
        name="attn",
    )(q, k, v)


def _post_kernel(y_ref, bonus_ref, g_ref, o_ref, x_ref, mod_ref, lnw_ref, lnb_ref,
                 wo_r_ref, wo_m_ref, nffn_ref, x1_ref, h2_ref, h2t_ref):
    y = y_ref[0]
    ones64 = _seg_ones(D_RWKV, RWKV_HEAD_DIM)
    mean = _segsum(y, ones64) * (1.0 / RWKV_HEAD_DIM)
    yc = y - mean
    var = _segsum(yc * yc, ones64) * (1.0 / RWKV_HEAD_DIM)
    yn = yc * lax.rsqrt(var + GN_EPS) * lnw_ref[...] + lnb_ref[...]
    yr = (yn + bonus_ref[0]) * g_ref[0]
    mix = _mm(yr, wo_r_ref[...]) + _dot(o_ref[0], wo_m_ref[...])
    g_a = mod_ref[0, 2:3, :]
    sh_f = mod_ref[0, 3:4, :]
    sc_f = mod_ref[0, 4:5, :]
    x1 = x_ref[0] + g_a * mix
    x1_ref[0] = x1
    ms = jnp.mean(x1 * x1, axis=-1, keepdims=True)
    h2 = x1 * lax.rsqrt(ms + NORM_EPS) * nffn_ref[...] * (1.0 + sc_f) + sh_f
    h2_ref[0] = h2
    tm = h2.shape[0]
    for s in range(SUBLANES):
        h2t_ref[0, pl.ds(s, tm, stride=SUBLANES), :] = h2[:, s * LANES:(s + 1) * LANES]


def _post_call(y, bonus, g, o_pad, x, mod3, ln_w, ln_b, w_out, norm_ffn, tm):
    B, T, D = x.shape
    HP = MLA_HEADS * HEAD_PAD
    wo_r = w_out[:D_RWKV].astype(BF16)
    wo_m = jnp.pad(w_out[D_RWKV:].reshape(MLA_HEADS, V_HEAD_DIM, D),
                   ((0, 0), (0, HEAD_PAD - V_HEAD_DIM), (0, 0))).reshape(HP, D).astype(BF16)
    tok = lambda n: pl.BlockSpec((1, tm, n), lambda b, t: (b, t, 0))
    row = lambda n: pl.BlockSpec((1, n), lambda b, t: (0, 0))
    full = lambda a: pl.BlockSpec(a.shape, lambda b, t: (0,) * a.ndim)
    return pl.pallas_call(
        _post_kernel,
        grid=(B, T // tm),
        in_specs=[tok(D_RWKV), tok(D_RWKV), tok(D_RWKV), tok(HP), tok(D),
                  pl.BlockSpec((1, 6, D), lambda b, t: (b, 0, 0)),
                  row(D_RWKV), row(D_RWKV), full(wo_r), full(wo_m), row(D)],
        out_specs=[tok(D), tok(D),
                   pl.BlockSpec((1, tm * SUBLANES, LANES), lambda b, t: (b, t, 0))],
        out_shape=[jax.ShapeDtypeStruct((B, T, D), F32)] * 2
        + [jax.ShapeDtypeStruct((B, T * SUBLANES, LANES), F32)],
        compiler_params=_cparams("arbitrary", "arbitrary"),
        name="post",
    )(y, bonus, g, o_pad, x, mod3, ln_w.reshape(1, -1), ln_b.reshape(1, -1), wo_r, wo_m,
      norm_ffn.reshape(1, D))


def _first_index(mask, iota, size, axis):
    return jnp.min(jnp.where(mask, iota, size), axis=axis, keepdims=True)


def _route_kernel(h_ref, wr_ref, bias_ref, e_ref, w_ref, rank_ref, cnt_ref, base):
    tr = h_ref.shape[0]
    E = N_EXPERTS

    @pl.when(pl.program_id(0) == 0)
    def _():
        base[...] = jnp.zeros_like(base)

    logits = _mm3(wr_ref[...], h_ref[...], NT)
    scores = _sigmoid(logits)
    sel = scores + bias_ref[...]
    iota_g = lax.broadcasted_iota(I32, (GROUP_SIZE, tr), 0)
    gs_rows = []
    for gi in range(N_GROUPS):
        blk = sel[gi * GROUP_SIZE:(gi + 1) * GROUP_SIZE, :]
        m1 = jnp.max(blk, axis=0, keepdims=True)
        i1 = _first_index(blk == m1, iota_g, GROUP_SIZE, 0)
        m2 = jnp.max(jnp.where(iota_g == i1, NEG_INF, blk), axis=0, keepdims=True)
        gs_rows.append(m1 + m2)
    gs = jnp.concatenate(gs_rows, axis=0)
    iota8 = lax.broadcasted_iota(I32, (N_GROUPS, tr), 0)
    gmask = jnp.zeros((N_GROUPS, tr), jnp.bool_)
    for _ in range(TOPK_GROUPS):
        mg = jnp.max(gs, axis=0, keepdims=True)
        ig = _first_index(gs == mg, iota8, N_GROUPS, 0)
        hit = iota8 == ig
        gmask = gmask | hit
        gs = jnp.where(hit, NEG_INF, gs)
    msel = jnp.concatenate(
        [jnp.where(gmask[gi:gi + 1, :], sel[gi * GROUP_SIZE:(gi + 1) * GROUP_SIZE, :], NEG_INF)
         for gi in range(N_GROUPS)], axis=0)
    iota_e = lax.broadcasted_iota(I32, (E, tr), 0)
    e_rows, w_rows = [], []
    onehot = jnp.zeros((E, tr), F32)
    for _ in range(TOP_K):
        mv = jnp.max(msel, axis=0, keepdims=True)
        ie = _first_index(msel == mv, iota_e, E, 0)
        hit = iota_e == ie
        e_rows.append(ie)
        w_rows.append(jnp.sum(jnp.where(hit, scores, 0.0), axis=0, keepdims=True))
        onehot = jnp.where(hit, 1.0, onehot)
        msel = jnp.where(hit, NEG_INF, msel)
    top_e = jnp.concatenate(e_rows, axis=0)
    wts = jnp.concatenate(w_rows, axis=0)
    wts = wts / jnp.sum(wts, axis=0, keepdims=True) * ROUTED_SCALE
    ti = lax.broadcasted_iota(I32, (tr, tr), 0)
    tj = lax.broadcasted_iota(I32, (tr, tr), 1)
    upper = jnp.where(ti < tj, 1.0, 0.0).astype(BF16)
    pos = _dot(onehot.astype(BF16), upper) + base[...]
    rank_rows = [jnp.sum(jnp.where(iota_e == e_rows[j], pos, 0.0), axis=0, keepdims=True)
                 for j in range(TOP_K)]
    base[...] = base[...] + jnp.sum(onehot, axis=1, keepdims=True)
    e_ref[...] = top_e
    w_ref[...] = wts
    rank_ref[...] = jnp.concatenate(rank_rows, axis=0).astype(I32)
    cnt_ref[...] = base[...].astype(I32)


def _route_call(h2, w_router, router_bias, tr):
    N, D = h2.shape
    E = N_EXPERTS
    out_kn = pl.BlockSpec((TOP_K, tr), lambda i: (0, i))
    return pl.pallas_call(
        _route_kernel,
        grid=(N // tr,),
        in_specs=[pl.BlockSpec((tr, D), lambda i: (i, 0)),
                  pl.BlockSpec((E, D), lambda i: (0, 0)),
                  pl.BlockSpec((E, 1), lambda i: (0, 0))],
        out_specs=[out_kn, out_kn, out_kn, pl.BlockSpec((E, 1), lambda i: (0, 0))],
        out_shape=[jax.ShapeDtypeStruct((TOP_K, N), I32), jax.ShapeDtypeStruct((TOP_K, N), F32),
                   jax.ShapeDtypeStruct((TOP_K, N), I32), jax.ShapeDtypeStruct((E, 1), I32)],
        scratch_shapes=[pltpu.VMEM((E, 1), F32)],
        compiler_params=_cparams("arbitrary"),
        name="route",
    )(h2, w_router.T, router_bias.reshape(E, 1))


def _dest_kernel(e_ref, rank_ref, start_ref, d_ref):
    tr = e_ref.shape[1]
    iota_e = lax.broadcasted_iota(I32, (N_EXPERTS, tr), 0)
    starts = start_ref[...]
    rows = [jnp.sum(jnp.where(iota_e == e_ref[j:j + 1, :], starts, 0), axis=0, keepdims=True)
            for j in range(TOP_K)]
    d_ref[...] = jnp.concatenate(rows, axis=0) + rank_ref[...]


def _dest_call(top_e, rank, pad_starts, tr):
    K, N = top_e.shape
    spec = pl.BlockSpec((K, tr), lambda i: (0, i))
    return pl.pallas_call(
        _dest_kernel,
        grid=(N // tr,),
        in_specs=[spec, spec, pl.BlockSpec((N_EXPERTS, 1), lambda i: (0, 0))],
        out_specs=spec,
        out_shape=jax.ShapeDtypeStruct((K, N), I32),
        compiler_params=_cparams("arbitrary"),
        name="dest",
    )(top_e, rank, pad_starts.reshape(N_EXPERTS, 1))


def _sc_dispatch_call(h2t, dest3, n_rows):
    N = h2t.shape[0]
    info = plsc.get_sparse_core_info()
    n_workers = info.num_cores * info.num_subcores
    n_chunks = N // (SC_CHUNK * n_workers)
    assert n_chunks * SC_CHUNK * n_workers == N
    mesh = plsc.VectorSubcoreMesh(core_axis_name="c", subcore_axis_name="s")

    @functools.partial(
        pl.kernel, mesh=mesh,
        out_type=(jax.ShapeDtypeStruct((n_rows,) + h2t.shape[1:], h2t.dtype),
                  jax.ShapeDtypeStruct((n_rows, LANES), I32)),
        scratch_types=[pltpu.VMEM((TOP_K, SC_CHUNK), I32),
                       pltpu.VMEM((SC_CHUNK,) + h2t.shape[1:], h2t.dtype),
                       pltpu.VMEM((SC_CHUNK, LANES), I32)],
        name="sc_dispatch",
    )
    def scatter_rows(h_hbm, dest_hbm, xs_hbm, tag_hbm, idx_v, rows_v, tag_v):
        wid = lax.axis_index("s") * info.num_cores + lax.axis_index("c")
        zeros = jnp.zeros((info.num_lanes,), I32)

        @pl.loop(0, SC_CHUNK)
        def _(r):
            for l0 in range(0, LANES, info.num_lanes):
                tag_v[r, pl.ds(l0, info.num_lanes)] = zeros

        @pl.loop(0, n_chunks)
        def _(c):
            chunk = wid * n_chunks + c
            base = chunk * SC_CHUNK
            pltpu.sync_copy(dest_hbm.at[chunk], idx_v)
            pltpu.sync_copy(h_hbm.at[pl.ds(base, SC_CHUNK)], rows_v)
            for j in range(TOP_K):
                pltpu.sync_copy(rows_v, xs_hbm.at[idx_v.at[j]])

                @pl.loop(0, SC_CHUNK)
                def _(r):
                    tag_v[r, pl.ds(0, info.num_lanes)] = zeros + ((base + r) * TOP_K + j)

                pltpu.sync_copy(tag_v, tag_hbm.at[idx_v.at[j]])

    return scatter_rows(h2t, dest3)


def _moe_kernel(be_ref, nu_ref, nv_ref, xs_ref, xtag_ref, wgu_ref, wdn_ref, yt_hbm, wgu_bf, wdn_bf,
                ybuf, tag_v, tag_s, sem_tag, sem_rows):
    i = pl.program_id(0)
    n_steps = pl.num_programs(0)
    cur = i % 2
    n_real = yt_hbm.shape[0] - 2 * MOE_ROWS

    def row_copy(buf, r, tag):
        rows = pl.ds(pl.multiple_of(r * SUBLANES, SUBLANES), SUBLANES)
        return pltpu.make_async_copy(ybuf.at[buf, rows, :], yt_hbm.at[tag], sem_rows.at[buf])

    n_used = nu_ref[0]
    prv = 1 - cur

    def wait_sent(buf):
        pltpu.make_async_copy(ybuf.at[buf], ybuf.at[buf], sem_rows.at[buf]).wait()

    def send_prev():
        for r in range(MOE_ROWS):
            row_copy(prv, r, tag_s[prv, r]).start()

    @pl.when((i >= 2) & (i - 2 < n_used))
    def _():
        wait_sent(cur)

    i_blk = jnp.minimum(i, n_steps - 2)
    @pl.when((i < n_used) & ((i == 0) | (be_ref[i_blk] != be_ref[jnp.maximum(i_blk - 1, 0)])))
    def _():
        wgu_bf[...] = wgu_ref[0].astype(BF16)
        wdn_bf[...] = wdn_ref[0].astype(BF16)

    def compute():
        row = lax.broadcasted_iota(I32, (MOE_ROWS, LANES), 0)
        tags = jnp.where(row < nv_ref[i_blk], xtag_ref[...], n_real + cur * MOE_ROWS + row)
        tag_v[...] = tags.astype(F32).T[0:1, :].astype(I32)
        cp = pltpu.make_async_copy(tag_v, tag_s.at[pl.ds(cur, 1)], sem_tag)
        cp.start()
        xb = jnp.concatenate([xs_ref[pl.ds(s, MOE_ROWS, stride=SUBLANES), :]
                              for s in range(SUBLANES)], axis=1).astype(BF16)
        gu = _dot(xb, wgu_bf[...])
        act = _silu(gu[:, :D_EXPERT]) * gu[:, D_EXPERT:]
        y = _dot(act.astype(BF16), wdn_bf[...])
        for s in range(SUBLANES):
            ybuf[cur, pl.ds(s, MOE_ROWS, stride=SUBLANES), :] = y[:, s * LANES:(s + 1) * LANES]
        cp.wait()

    @pl.when((i == 0) & (i < n_used))
    def _():
        compute()

    @pl.when((i > 0) & (i < n_used))
    def _():
        send_prev()
        compute()

    @pl.when((i > 0) & (i == n_used))
    def _():
        send_prev()

    @pl.when((i == n_steps - 1) & (i - 1 < n_used))
    def _():
        wait_sent(prv)


def _moe_call(block_expert, n_used, n_valid, xs, xtag, w_gu, w_dn, n_out_rows):
    P = xs.shape[0] // SUBLANES
    D = SUBLANES * LANES
    nb = P // MOE_ROWS
    blk = lambda i, be, nu, nv: (jnp.minimum(i, nu[0] - 1), 0)
    wblk = lambda i, be, nu, nv: (be[jnp.minimum(i, nu[0] - 1)], 0, 0)
    return pl.pallas_call(
        _moe_kernel,
        grid_spec=pltpu.PrefetchScalarGridSpec(
            num_scalar_prefetch=3,
            grid=(nb + 1,),
            in_specs=[pl.BlockSpec((MOE_ROWS * SUBLANES, LANES), blk),
                      pl.BlockSpec((MOE_ROWS, LANES), blk),
                      pl.BlockSpec((1, D, 2 * D_EXPERT), wblk),
                      pl.BlockSpec((1, D_EXPERT, D), wblk)],
            out_specs=pl.BlockSpec(memory_space=pl.ANY),
            scratch_shapes=[pltpu.VMEM((D, 2 * D_EXPERT), BF16), pltpu.VMEM((D_EXPERT, D), BF16),
                            pltpu.VMEM((2, MOE_ROWS * SUBLANES, LANES), F32),
                            pltpu.VMEM((1, MOE_ROWS), I32), pltpu.SMEM((2, MOE_ROWS), I32),
                            pltpu.SemaphoreType.DMA, pltpu.SemaphoreType.DMA((2,))]),
        out_shape=jax.ShapeDtypeStruct((n_out_rows, SUBLANES, LANES), F32),
        compiler_params=_cparams("arbitrary"),
        name="moe",
    )(block_expert, n_used, n_valid, xs, xtag, w_gu, w_dn)


def _shared_kernel(h_ref, x1_ref, mod_ref, wsg_ref, wsd_ref, o_ref):
    gu = _mm(h_ref[...], wsg_ref[...])
    act = _silu(gu[:, :D_EXPERT]) * gu[:, D_EXPERT:]
    o_ref[...] = x1_ref[...] + mod_ref[0, 5:6, :] * _mm(act, wsd_ref[...])


def _shared_call(h2, x1, mod3, w_sh_gu, w_sh_dn, tokens_per_batch, tm):
    N, D = h2.shape
    tiles_per_batch = tokens_per_batch // tm
    tok = pl.BlockSpec((tm, D), lambda i: (i, 0))
    wsg = w_sh_gu.astype(BF16)
    wsd = w_sh_dn.astype(BF16)
    return pl.pallas_call(
        _shared_kernel,
        grid=(N // tm,),
        in_specs=[tok, tok,
                  pl.BlockSpec((1, 6, D), lambda i: (i // tiles_per_batch, 0, 0)),
                  pl.BlockSpec(wsg.shape, lambda i: (0, 0)),
                  pl.BlockSpec(wsd.shape, lambda i: (0, 0))],
        out_specs=tok,
        out_shape=jax.ShapeDtypeStruct((N, D), F32),
        compiler_params=_cparams("arbitrary"),
        name="shared",
    )(h2, x1, mod3, wsg, wsd)


def _combine_kernel(w_hbm, yt_ref, xs_ref, mod_ref, o_ref, wts, routed, sem_w):
    i = pl.program_id(0)
    tc = xs_ref.shape[0]
    cp_w = pltpu.make_async_copy(w_hbm.at[i], wts, sem_w)
    cp_w.start()
    cp_w.wait()

    def wsum(tt, carry):
        for u in range(SUBLANES):
            t = tt * SUBLANES + u
            first = pl.multiple_of(t * (TOP_K * SUBLANES), SUBLANES)
            acc = yt_ref[pl.ds(first, SUBLANES), :] * wts[0, t]
            for j in range(1, TOP_K):
                acc = acc + yt_ref[pl.ds(first + j * SUBLANES, SUBLANES), :] * wts[j, t]
            routed[pl.ds(pl.multiple_of(t * SUBLANES, SUBLANES), SUBLANES), :] = acc
        return carry

    lax.fori_loop(0, tc // SUBLANES, wsum, 0)
    routed2d = jnp.concatenate([routed[pl.ds(s, tc, stride=SUBLANES), :] for s in range(SUBLANES)],
                               axis=1)
    o_ref[...] = xs_ref[...] + mod_ref[0, 5:6, :] * routed2d


def _combine_call(w3, yt, x_shared, mod3, tokens_per_batch, tc):
    N, D = x_shared.shape
    tiles_per_batch = tokens_per_batch // tc
    tok = pl.BlockSpec((tc, D), lambda i: (i, 0))
    return pl.pallas_call(
        _combine_kernel,
        grid=(N // tc,),
        in_specs=[pl.BlockSpec(memory_space=pl.ANY),
                  pl.BlockSpec((tc * TOP_K * SUBLANES, LANES), lambda i: (i, 0)),
                  tok,
                  pl.BlockSpec((1, 6, D), lambda i: (i // tiles_per_batch, 0, 0))],
        out_specs=tok,
        out_shape=jax.ShapeDtypeStruct((N, D), F32),
        scratch_shapes=[pltpu.SMEM((TOP_K, tc), F32), pltpu.VMEM((tc * SUBLANES, LANES), F32),
                        pltpu.SemaphoreType.DMA],
        compiler_params=_cparams("arbitrary"),
        name="combine",
    )(w3, yt, x_shared, mod3)


def _tile(n, pref):
    t = min(n, pref)
    assert n % t == 0, (n, t)
    return t


def _layer(x, mod3, positions, norm_mix, w_in, rwkv_mu, decay_w0, decay_up, iclr_a0, iclr_up,
           gate_up, rwkv_k_k, rwkv_k_a, rwkv_r_k, ln_x_w, ln_x_b, q_a_norm, w_q_b, kv_a_norm,
           w_kv_b, q_norm, k_norm, w_out, norm_ffn, w_router, router_bias, w_e_gate_up, w_e_down,
           w_sh_gate_up, w_sh_down):
    B, T, D = x.shape
    N = B * T
    assert T % SCAN_CHUNK == 0
    (r, lw, k2, v, kk, akk, g, bonus, q_pad, k_pad, v_pad) = _pre_call(
        x, mod3, positions, norm_mix, w_in, rwkv_mu, decay_w0, decay_up, iclr_a0, iclr_up,
        gate_up, rwkv_k_k, rwkv_k_a, rwkv_r_k, q_a_norm, w_q_b, kv_a_norm, w_kv_b, q_norm, k_norm,
        tm=_tile(T, 512))
    y = _scan_call(r, lw, k2, v, kk, akk)
    o_pad = _attn_call(q_pad, k_pad, v_pad)
    x1, h2, h2t = _post_call(y, bonus, g, o_pad, x, mod3, ln_x_w, ln_x_b, w_out, norm_ffn,
                             tm=_tile(T, 512))
    x1 = x1.reshape(N, D)
    h2 = h2.reshape(N, D)
    h2t = h2t.reshape(N, D // LANES, LANES)

    tr = _tile(N, 512)
    top_e, wts, rank, counts = _route_call(h2, w_router, router_bias, tr)
    counts = counts.reshape(N_EXPERTS)
    padded = (counts + MOE_ROWS - 1) // MOE_ROWS * MOE_ROWS
    pad_ends = jnp.cumsum(padded)
    pad_starts = pad_ends - padded
    n_blocks = (N * TOP_K + N_EXPERTS * (MOE_ROWS - 1)) // MOE_ROWS
    block_expert = jnp.minimum(
        jnp.searchsorted(pad_ends, jnp.arange(n_blocks, dtype=I32) * MOE_ROWS, side='right'),
        N_EXPERTS - 1).astype(I32)
    n_used = (pad_ends[-1:] // MOE_ROWS).astype(I32)
    block_row0 = jnp.arange(n_blocks + 1, dtype=I32) * MOE_ROWS
    be_pad = jnp.concatenate([block_expert, block_expert[-1:]])
    n_valid = jnp.clip((pad_starts + counts)[be_pad] - block_row0, 0, MOE_ROWS).astype(I32)
    dest = _dest_call(top_e, rank, pad_starts.astype(I32), tr)

    dest3 = dest.reshape(TOP_K, N // SC_CHUNK, SC_CHUNK).transpose(1, 0, 2)
    n_rows = n_blocks * MOE_ROWS
    xs, xtag = _sc_dispatch_call(h2t, dest3, n_rows)
    yt = _moe_call(block_expert, n_used, n_valid, xs.reshape(n_rows * SUBLANES, LANES), xtag,
                   w_e_gate_up, w_e_down, N * TOP_K + 2 * MOE_ROWS)
    tc = _tile(T, COMBINE_TILE)
    w3 = wts.reshape(TOP_K, N // tc, tc).transpose(1, 0, 2)
    x_shared = _shared_call(h2, x1, mod3, w_sh_gate_up, w_sh_down, T, _tile(T, 512))
    out = _combine_call(w3, yt.reshape(-1, LANES), x_shared, mod3, T, tc)
    return out.reshape(B, T, D)


def kernel(x, c, positions, ada_w, ada_b, norm_mix, w_in, rwkv_mu, decay_w0, decay_up, iclr_a0, iclr_up, gate_up, rwkv_k_k, rwkv_k_a, rwkv_r_k, ln_x_w, ln_x_b, q_a_norm, w_q_b, kv_a_norm, w_kv_b, q_norm, k_norm, w_out, norm_ffn, w_router, router_bias, w_e_gate_up, w_e_down, w_sh_gate_up, w_sh_down):
    B, T, D = x.shape
    depth = ada_w.shape[0]
    for l in range(depth):
        mod3 = _mod_call(c, ada_w[l], ada_b[l]).reshape(B, 6, D)
        x = _layer(x, mod3, positions, norm_mix[l], w_in[l], rwkv_mu[l], decay_w0[l], decay_up[l],
                   iclr_a0[l], iclr_up[l], gate_up[l], rwkv_k_k[l], rwkv_k_a[l], rwkv_r_k[l],
                   ln_x_w[l], ln_x_b[l], q_a_norm[l], w_q_b[l], kv_a_norm[l], w_kv_b[l],
                   q_norm[l], k_norm[l], w_out[l], norm_ffn[l], w_router[l], router_bias[l],
                   w_e_gate_up[l], w_e_down[l], w_sh_gate_up[l], w_sh_down[l])
    return x
```

```python
import functools
import math

import jax
import jax.numpy as jnp
import numpy as np
from jax import lax
from jax.experimental import pallas as pl
from jax.experimental.pallas import tpu as pltpu
from jax.experimental.pallas import tpu_sc as plsc

F32 = jnp.float32
BF16 = jnp.bfloat16
I32 = jnp.int32

NORM_EPS = 1e-6
GN_EPS = 64e-5
RWKV_HEADS = 8
RWKV_HEAD_DIM = 64
D_RWKV = 512
DECAY_LORA = 32
ICLR_LORA = 32
GATE_LORA = 96
MLA_HEADS = 8
QK_NOPE_DIM = 64
QK_ROPE_DIM = 32
QK_HEAD_DIM = 96
V_HEAD_DIM = 64
Q_LORA_RANK = 256
KV_LORA_RANK = 128
ROPE_THETA = 10000.0
N_EXPERTS = 256
TOP_K = 8
N_GROUPS = 8
TOPK_GROUPS = 4
GROUP_SIZE = N_EXPERTS // N_GROUPS
D_EXPERT = 256
ROUTED_SCALE = 2.5
MOE_ROWS = 512
SC_CHUNK = 64
COMBINE_TILE = 256

LANES = 128
SUBLANES = 8
HEAD_PAD = 128
VMEM_LIMIT = 56 * 1024 * 1024

PRE_SUBTILES = 1
SCAN_CHUNK = 64
SCAN_BLOCK = 512
ATTN_TILE = 512
ATTN_Q_SCALE = QK_HEAD_DIM ** -0.5 * math.log2(math.e)
NEG_INF = float("-inf")


def _cparams(*sem):
    return pltpu.CompilerParams(dimension_semantics=sem, vmem_limit_bytes=VMEM_LIMIT)


def _split2(a):
    hi = a.astype(BF16)
    lo = (a - hi.astype(F32)).astype(BF16)
    return hi, lo


def _split3(a):
    hi = a.astype(BF16)
    r1 = a - hi.astype(F32)
    mid = r1.astype(BF16)
    lo = (r1 - mid.astype(F32)).astype(BF16)
    return hi, mid, lo


def _dot(a, b, dims=None):
    if dims is None:
        return jnp.dot(a, b, preferred_element_type=F32)
    return lax.dot_general(a, b, (dims, ((), ())), preferred_element_type=F32)


def _mm(a, b, dims=None):
    return _dot(a.astype(BF16), b.astype(BF16), dims)


def _mm3(a, b, dims=None):
    ah, al = _split2(a)
    bh, bl = _split2(b)
    return _dot(ah, bh, dims) + (_dot(ah, bl, dims) + _dot(al, bh, dims))


def _mm_exact_rhs(a, b_exact_bf16, dims=None):
    h, m, l = _split3(a)
    return _dot(h, b_exact_bf16, dims) + (_dot(m, b_exact_bf16, dims) + _dot(l, b_exact_bf16, dims))


NT = ((1,), (1,))
TN = ((0,), (0,))


def _sigmoid(z):
    return 1.0 / (1.0 + jnp.exp(-z))


def _silu(z):
    return z * _sigmoid(z)


def _seg_ones(width, seg):
    r = lax.broadcasted_iota(I32, (width, width), 0) // seg
    c = lax.broadcasted_iota(I32, (width, width), 1) // seg
    return jnp.where(r == c, 1.0, 0.0).astype(BF16)


def _segsum(a, ones_bd):
    hi, lo = _split2(a)
    return _dot(hi, ones_bd) + _dot(lo, ones_bd)


def _mod_kernel(c_ref, w_ref, b_ref, o_ref):
    ca = _silu(c_ref[...])
    o_ref[...] = _mm3(ca, w_ref[...]) + b_ref[...]


def _mod_call(c, ada_w, ada_b):
    B, D = c.shape
    n6 = ada_w.shape[1]
    tn = D
    return pl.pallas_call(
        _mod_kernel,
        grid=(n6 // tn,),
        in_specs=[pl.BlockSpec((B, D), lambda j: (0, 0)),
                  pl.BlockSpec((D, tn), lambda j: (0, j)),
                  pl.BlockSpec((1, tn), lambda j: (0, j))],
        out_specs=pl.BlockSpec((B, tn), lambda j: (0, j)),
        out_shape=jax.ShapeDtypeStruct((B, n6), F32),
        compiler_params=_cparams("arbitrary"),
        name="mod",
    )(c, ada_w, ada_b.reshape(1, n6))


def _pre_kernel(x_ref, mod_ref, pos_ref, nmix_ref, wrkv_ref, wlora_ref, wmla_ref,
                mu_rkv_ref, mu_lora_ref, wup_ref, w0_ref, a0_ref, kk_ref, ka_ref, rk_ref,
                qan_ref, wqb_ref, kvan_ref, wkb_ref, wvb_ref, qn_ref, kn_ref, invf_ref,
                r_ref, lw_ref, k_ref, v_ref, kkn_ref, akk_ref, g_ref, bonus_ref,
                q_ref, kout_ref, vout_ref,
                carry_rkv, carry_lora):
    ti = pl.program_id(1)
    tm = x_ref.shape[1]

    @pl.when(ti == 0)
    def _():
        carry_rkv[...] = jnp.zeros_like(carry_rkv)
        carry_lora[...] = jnp.zeros_like(carry_lora)

    ts = tm // PRE_SUBTILES
    sh_a = mod_ref[0, 0:1, :]
    sc_a = mod_ref[0, 1:2, :]
    ones64 = _seg_ones(D_RWKV, RWKV_HEAD_DIM)
    row0 = lax.broadcasted_iota(I32, (ts, 1), 0) == 0
    half = QK_ROPE_DIM // 2
    last_rows = {}

    def sub_tile(s):
        rows = slice(s * ts, (s + 1) * ts)
        xb = x_ref[0, rows, :]
        ms = jnp.mean(xb * xb, axis=-1, keepdims=True)
        h = xb * lax.rsqrt(ms + NORM_EPS) * nmix_ref[...] * (1.0 + sc_a) + sh_a
        hb = h.astype(BF16)
        yield
        u_rkv = _dot(hb, wrkv_ref[...])
        u_lora = _dot(hb, wlora_ref[...])
        u_mla = _dot(hb, wmla_ref[...])
        last_rows[s] = (u_rkv[ts - 1:ts, :], u_lora[ts - 1:ts, :])
        yield

        before_rkv, before_lora = (carry_rkv[...], carry_lora[...]) if s == 0 else last_rows[s - 1]
        prev_rkv = jnp.where(row0, before_rkv, pltpu.roll(u_rkv, 1, 0))
        prev_lora = jnp.where(row0, before_lora, pltpu.roll(u_lora, 1, 0))
        if s == PRE_SUBTILES - 1:
            carry_rkv[...] = u_rkv[ts - 1:ts, :]
            carry_lora[...] = u_lora[ts - 1:ts, :]
        us = u_rkv + (prev_rkv - u_rkv) * mu_rkv_ref[...]
        ul = u_lora + (prev_lora - u_lora) * mu_lora_ref[...]
        r = us[:, 0:D_RWKV]
        k = us[:, D_RWKV:2 * D_RWKV]
        v = us[:, 2 * D_RWKV:3 * D_RWKV]
        lane_l = lax.broadcasted_iota(I32, ul.shape, 1)
        t_in = jnp.where(lane_l < DECAY_LORA, jnp.tanh(ul),
                         jnp.where(lane_l < DECAY_LORA + ICLR_LORA, ul, _sigmoid(ul)))
        yield
        up = _mm(t_in, wup_ref[...])
        yield
        z = w0_ref[...] + up[:, 0:D_RWKV]
        lw = (-math.exp(-0.5)) * _sigmoid(z)
        a = _sigmoid(a0_ref[...] + up[:, D_RWKV:2 * D_RWKV])
        g = up[:, 2 * D_RWKV:3 * D_RWKV]
        kk = k * kk_ref[...]
        k2 = k * (1.0 + (a - 1.0) * ka_ref[...])
        yield
        ss = _segsum(kk * kk, ones64)
        bonus_sum = _segsum(r * k2 * rk_ref[...], ones64)
        yield
        kk = kk * lax.rsqrt(jnp.maximum(ss, 1e-24))
        r_ref[0, rows, :] = r
        lw_ref[0, rows, :] = lw
        k_ref[0, rows, :] = k2
        v_ref[0, rows, :] = v
        kkn_ref[0, rows, :] = kk
        akk_ref[0, rows, :] = a * kk
        g_ref[0, rows, :] = g
        bonus_ref[0, rows, :] = bonus_sum * v
        yield

        q_lat = u_mla[:, 0:Q_LORA_RANK]
        kv_lat = u_mla[:, Q_LORA_RANK:Q_LORA_RANK + KV_LORA_RANK]
        kpe_tile = u_mla[:, Q_LORA_RANK + KV_LORA_RANK:]
        qn = q_lat * lax.rsqrt(jnp.mean(q_lat * q_lat, axis=-1, keepdims=True) + NORM_EPS) * qan_ref[...]
        kvn = kv_lat * lax.rsqrt(jnp.mean(kv_lat * kv_lat, axis=-1, keepdims=True) + NORM_EPS) * kvan_ref[...]
        kvb = kvn.astype(BF16)
        yield
        q_raw = _mm(qn, wqb_ref[...])
        k_raw = _dot(kvb, wkb_ref[...])
        v_pad = _dot(kvb, wvb_ref[...])
        yield
        kpe_h = pltpu.roll(kpe_tile, QK_NOPE_DIM, 1)
        cos_t, s1, s2 = rope_tables(s)

        def tables(gain, scale):
            g_s = gain * scale
            return (cos_t * g_s, s1 * pltpu.roll(g_s, HEAD_PAD - half, 1),
                    s2 * pltpu.roll(g_s, half, 1))

        def norm_rope(xh, tabs):
            c_g, s1_g, s2_g = tabs
            ssq = jnp.sum(xh * xh, axis=-1, keepdims=True) * (1.0 / QK_HEAD_DIM)
            rot = xh * c_g + pltpu.roll(xh, HEAD_PAD - half, 1) * s1_g + pltpu.roll(xh, half, 1) * s2_g
            return rot * lax.rsqrt(ssq + NORM_EPS)

        q_tabs = tables(qn_ref[...], ATTN_Q_SCALE)
        k_tabs = tables(kn_ref[...], 1.0)
        vout_ref[0, rows, :] = v_pad.astype(BF16)
        yield
        for hh in range(MLA_HEADS):
            sl = slice(hh * HEAD_PAD, (hh + 1) * HEAD_PAD)
            q_ref[0, rows, sl] = norm_rope(q_raw[:, sl], q_tabs).astype(BF16)
            kout_ref[0, rows, sl] = norm_rope(k_raw[:, sl] + kpe_h, k_tabs).astype(BF16)
            yield

    def rope_tables(s):
        ang_t = invf_ref[...] * pos_ref[0, :, s * ts:(s + 1) * ts].astype(F32)
        frow = lax.broadcasted_iota(I32, (half, HEAD_PAD), 0)
        flane = lax.broadcasted_iota(I32, (half, HEAD_PAD), 1)
        at_x1 = flane == frow + QK_NOPE_DIM
        at_x2 = flane == frow + QK_NOPE_DIM + half
        e_cos = jnp.where(at_x1 | at_x2, 1.0, 0.0).astype(BF16)
        e_sin = jnp.concatenate([jnp.where(at_x1, -1.0, 0.0), jnp.where(at_x2, 1.0, 0.0)],
                                axis=1).astype(BF16)
        lane = lax.broadcasted_iota(I32, (1, HEAD_PAD), 1)
        off_rope = jnp.where((lane >= QK_NOPE_DIM) & (lane < QK_HEAD_DIM), 0.0, 1.0)
        cos_t = _mm_exact_rhs(jnp.cos(ang_t), e_cos, TN) + off_rope
        sin2 = _mm_exact_rhs(jnp.sin(ang_t), e_sin, TN)
        return cos_t, sin2[:, :HEAD_PAD], sin2[:, HEAD_PAD:]

    live = [sub_tile(s) for s in range(PRE_SUBTILES)]
    while live:
        live = [gen for gen in live if next(gen, "done") != "done"]


def _pad_heads(w, n_heads, width):
    kdim = w.shape[0]
    w = w.reshape(kdim, n_heads, width)
    w = jnp.pad(w, ((0, 0), (0, 0), (0, HEAD_PAD - width)))
    return w.reshape(kdim, n_heads * HEAD_PAD)


def _pre_call(x, mod3, positions, norm_mix, w_in, rwkv_mu, decay_w0, decay_up, iclr_a0, iclr_up,
              gate_up, k_k, k_a, r_k, q_a_norm, w_q_b, kv_a_norm, w_kv_b, q_norm, k_norm, tm):
    B, T, D = x.shape
    n_rkv = 3 * D_RWKV
    n_lora = DECAY_LORA + ICLR_LORA + GATE_LORA
    LORA_PAD = 256
    MLA_PAD = 512
    n_mla = Q_LORA_RANK + KV_LORA_RANK + QK_ROPE_DIM
    w_rkv = w_in[:, :n_rkv].astype(BF16)
    w_lora = jnp.pad(w_in[:, n_rkv:n_rkv + n_lora], ((0, 0), (0, LORA_PAD - n_lora))).astype(BF16)
    w_mla = jnp.pad(w_in[:, n_rkv + n_lora:], ((0, 0), (0, MLA_PAD - n_mla))).astype(BF16)
    mu_rkv = rwkv_mu[:n_rkv].reshape(1, n_rkv)
    mu_lora = jnp.pad(rwkv_mu[n_rkv:], (0, LORA_PAD - n_lora)).reshape(1, LORA_PAD)
    w_up = jnp.zeros((LORA_PAD, n_rkv), F32)
    w_up = w_up.at[0:DECAY_LORA, 0:D_RWKV].set(decay_up)
    w_up = w_up.at[DECAY_LORA:DECAY_LORA + ICLR_LORA, D_RWKV:2 * D_RWKV].set(iclr_up)
    w_up = w_up.at[DECAY_LORA + ICLR_LORA:n_lora, 2 * D_RWKV:].set(gate_up)
    w_up = w_up.astype(BF16)
    w_qb = _pad_heads(w_q_b, MLA_HEADS, QK_HEAD_DIM).astype(BF16)
    w_kv3 = w_kv_b.reshape(KV_LORA_RANK, MLA_HEADS, QK_NOPE_DIM + V_HEAD_DIM)
    w_kb = _pad_heads(w_kv3[:, :, :QK_NOPE_DIM].reshape(KV_LORA_RANK, -1), MLA_HEADS, QK_NOPE_DIM).astype(BF16)
    w_vb = _pad_heads(w_kv3[:, :, QK_NOPE_DIM:].reshape(KV_LORA_RANK, -1), MLA_HEADS, V_HEAD_DIM).astype(BF16)
    qn_pad = jnp.pad(q_norm, (0, HEAD_PAD - QK_HEAD_DIM)).reshape(1, HEAD_PAD)
    kn_pad = jnp.pad(k_norm, (0, HEAD_PAD - QK_HEAD_DIM)).reshape(1, HEAD_PAD)
    inv_freq = ROPE_THETA ** (-jnp.arange(0, QK_ROPE_DIM, 2, dtype=F32) / QK_ROPE_DIM)
    invf = inv_freq.reshape(QK_ROPE_DIM // 2, 1)
    pos3 = positions.reshape(B, 1, T)
    HP = MLA_HEADS * HEAD_PAD

    row = lambda n: pl.BlockSpec((1, n), lambda b, t: (0, 0))
    full = lambda a: pl.BlockSpec(a.shape, lambda b, t: (0,) * a.ndim)
    tok = lambda n: pl.BlockSpec((1, tm, n), lambda b, t: (b, t, 0))
    outs = ([jax.ShapeDtypeStruct((B, T, D_RWKV), F32)] * 8
            + [jax.ShapeDtypeStruct((B, T, HP), BF16)] * 3)
    return pl.pallas_call(
        _pre_kernel,
        grid=(B, T // tm),
        in_specs=[tok(D),
                  pl.BlockSpec((1, 6, D), lambda b, t: (b, 0, 0)),
                  pl.BlockSpec((1, 1, tm), lambda b, t: (b, 0, t)),
                  row(D), full(w_rkv), full(w_lora), full(w_mla),
                  row(n_rkv), row(LORA_PAD), full(w_up), row(D_RWKV), row(D_RWKV),
                  row(D_RWKV), row(D_RWKV), row(D_RWKV),
                  row(Q_LORA_RANK), full(w_qb), row(KV_LORA_RANK), full(w_kb), full(w_vb),
                  row(HEAD_PAD), row(HEAD_PAD), full(invf)],
        out_specs=[tok(D_RWKV)] * 8 + [tok(HP)] * 3,
        out_shape=outs,
        scratch_shapes=[pltpu.VMEM((1, n_rkv), F32), pltpu.VMEM((1, LORA_PAD), F32)],
        compiler_params=_cparams("arbitrary", "arbitrary"),
        name="pre",
    )(x, mod3, pos3, norm_mix.reshape(1, D), w_rkv, w_lora, w_mla, mu_rkv, mu_lora, w_up,
      decay_w0.reshape(1, -1), iclr_a0.reshape(1, -1), k_k.reshape(1, -1), k_a.reshape(1, -1),
      r_k.reshape(1, -1), q_a_norm.reshape(1, -1), w_qb, kv_a_norm.reshape(1, -1), w_kb, w_vb,
      qn_pad, kn_pad, invf)


def _scan_kernel(r_ref, lw_ref, k_ref, v_ref, kk_ref, akk_ref, y_ref, state):
    C = SCAN_CHUNK
    n_chunks = r_ref.shape[1] // C
    n_pairs = r_ref.shape[2] // LANES

    @pl.when(pl.program_id(1) == 0)
    def _():
        state[...] = jnp.zeros_like(state)

    ri = lax.broadcasted_iota(I32, (C, C), 0)
    ci = lax.broadcasted_iota(I32, (C, C), 1)
    tri_incl = jnp.where(ci <= ri, 1.0, 0.0).astype(BF16)
    r2 = lax.broadcasted_iota(I32, (2 * C, 2 * C), 0)
    c2 = lax.broadcasted_iota(I32, (2 * C, 2 * C), 1)
    same = (r2 >= C) == (c2 >= C)
    strict = same & (c2 < r2)
    incl = same & (c2 <= r2)
    eye = jnp.where(c2 == r2, 1.0, 0.0)
    head0 = lax.broadcasted_iota(I32, (C, LANES), 1) < RWKV_HEAD_DIM

    def stack2(a):
        return jnp.concatenate([jnp.where(head0, a, 0.0), jnp.where(head0, 0.0, a)], axis=0)

    C2 = 2 * C
    cat0 = lambda *a: jnp.concatenate(a, axis=0)
    cat1 = lambda *a: jnp.concatenate(a, axis=1)

    items = []
    for c in range(n_chunks):
        rows = slice(c * C, (c + 1) * C)
        lw = lw_ref[0, rows, :]
        cum = _mm_exact_rhs_left(tri_incl, lw)
        cum_end = cum[C - 1:C, :]
        w_end = jnp.exp(cum_end)
        e_pos = jnp.exp(cum)
        e_neg = jnp.exp(-cum)
        e_prev = jnp.exp(cum - lw)
        e_end = jnp.exp(cum_end - cum)
        kk = kk_ref[0, rows, :]
        k2 = k_ref[0, rows, :]
        pneg = -akk_ref[0, rows, :]
        vv = v_ref[0, rows, :]
        rt = r_ref[0, rows, :] * e_pos
        bt = kk * e_prev
        pt = pneg * e_neg
        kt = k2 * e_neg
        ph = pneg * e_end
        kh = k2 * e_end
        for pp in range(n_pairs):
            sl = slice(pp * LANES, (pp + 1) * LANES)
            items.append(dict(
                c=c, p=pp, w_end=w_end[:, sl],
                bt2=stack2(bt[:, sl]).astype(BF16), rt2=stack2(rt[:, sl]).astype(BF16),
                pk2=cat0(stack2(pt[:, sl]), stack2(kt[:, sl])).astype(BF16),
                phkh2=cat0(stack2(ph[:, sl]), stack2(kh[:, sl])).astype(BF16),
                v2=stack2(vv[:, sl])))
    for it in items:
        ab = _dot(cat0(it['bt2'], it['rt2']), it['pk2'], NT)
        it['a_ab'] = jnp.where(strict, ab[:C2, :C2], 0.0)
        it['a_ak'] = jnp.where(strict, ab[:C2, C2:], 0.0).astype(BF16)
        it['b_rpk'] = cat1(jnp.where(incl, ab[C2:, :C2], 0.0), jnp.where(incl, ab[C2:, C2:], 0.0)).astype(BF16)
        it['tinv'] = eye + it['a_ab']
    for it in items:
        it['apow'] = _mm(it['a_ab'], it['a_ab'])
    for _ in range(int(math.log2(C)) - 1):
        for it in items:
            both = _mm(cat0(it['apow'], it['tinv']), it['apow'])
            it['apow'] = both[:C2]
            it['tinv'] = it['tinv'] + both[C2:]
    for it in items:
        it['akv'] = _dot(it['a_ak'], it['v2'].astype(BF16))
    for it in items:
        tt = _dot(it['tinv'].astype(BF16), cat1(it['bt2'], it['akv'].astype(BF16)))
        it['tb_rt'] = cat0(tt[:, :LANES].astype(BF16), it['rt2'])
        it['tav'] = tt[:, LANES:]
    for it in items:
        pp = it['p']
        s0 = state[pp]
        top = _dot(it['tb_rt'], s0.astype(BF16), NT)
        u2 = top[:C2] + it['tav']
        uv = cat0(u2, it['v2']).astype(BF16)
        y2 = top[C2:] + _dot(it['b_rpk'], uv)
        state[pp] = s0 * it['w_end'] + _dot(uv, it['phkh2'], TN)
        y_ref[0, it['c'] * C:(it['c'] + 1) * C, pp * LANES:(pp + 1) * LANES] = y2[0:C] + y2[C:C2]


def _mm_exact_rhs_left(b_exact_bf16, a):
    h, m, l = _split3(a)
    return _dot(b_exact_bf16, h) + (_dot(b_exact_bf16, m) + _dot(b_exact_bf16, l))


def _scan_call(r, lw, k2, v, kk, akk):
    B, T, W = r.shape
    tb = _tile(T, SCAN_BLOCK)
    spec = pl.BlockSpec((1, tb, W), lambda b, c: (b, c, 0))
    return pl.pallas_call(
        _scan_kernel,
        grid=(B, T // tb),
        in_specs=[spec] * 6,
        out_specs=spec,
        out_shape=jax.ShapeDtypeStruct((B, T, W), F32),
        scratch_shapes=[pltpu.VMEM((W // LANES, 2 * RWKV_HEAD_DIM, LANES), F32)],
        compiler_params=_cparams("arbitrary", "arbitrary"),
        name="scan",
    )(r, lw, k2, v, kk, akk)


def _attn_kernel(q_ref, k_ref, v_ref, o_ref):
    T = q_ref.shape[1]
    tq = min(T, ATTN_TILE)
    row = lax.broadcasted_iota(I32, (tq, tq), 0)
    col = lax.broadcasted_iota(I32, (tq, tq), 1)
    causal = col <= row

    def update(q, kt, vt, carry, mask):
        m_old, l_old, acc = carry
        s = _dot(q, kt, NT)
        if mask:
            s = jnp.where(causal, s, NEG_INF)
        m_new = jnp.maximum(m_old, jnp.max(s, axis=-1, keepdims=True))
        alpha = jnp.exp2(m_old - m_new)
        p = jnp.exp2(s - m_new)
        l_new = alpha * l_old + jnp.sum(p, axis=-1, keepdims=True)
        acc = alpha * acc + _dot(p.astype(BF16), vt)
        return m_new, l_new, acc

    for qi in range(T // tq):
        q = q_ref[0, qi * tq:(qi + 1) * tq, :]
        carry = (jnp.full((tq, 1), NEG_INF, F32), jnp.zeros((tq, 1), F32),
                 jnp.zeros((tq, HEAD_PAD), F32))

        def body(ki, carry, q=q):
            rows = pl.ds(pl.multiple_of(ki * tq, tq), tq)
            return update(q, k_ref[0, rows, :], v_ref[0, rows, :], carry, False)

        carry = lax.fori_loop(0, qi, body, carry, unroll=True)
        diag = slice(qi * tq, (qi + 1) * tq)
        _, l_fin, acc = update(q, k_ref[0, diag, :], v_ref[0, diag, :], carry, True)
        o_ref[0, diag, :] = (acc / l_fin).astype(o_ref.dtype)


def _attn_call(q, k, v):
    B, T, HP = q.shape
    spec = pl.BlockSpec((1, T, HEAD_PAD), lambda b, h: (b, 0, h))
    return pl.pallas_call(
        _attn_kernel,
        grid=(B, MLA_HEADS),
        in_specs=[spec, spec, spec],
        out_specs=spec,
        out_shape=jax.ShapeDtypeStruct((B, T, HP), BF16),
        compiler_params=_cparams("arbitrary", "arbitrary"),
        name="attn",
    )(q, k, v)


def _post_kernel(y_ref, bonus_ref, g_ref, o_ref, x_ref, mod_ref, lnw_ref, lnb_ref,
                 wo_r_ref, wo_m_ref, nffn_ref, x1_ref, h2_ref, h2t_ref):
    y = y_ref[0]
    ones64 = _seg_ones(D_RWKV, RWKV_HEAD_DIM)
    mean = _segsum(y, ones64) * (1.0 / RWKV_HEAD_DIM)
    yc = y - mean
    var = _segsum(yc * yc, ones64) * (1.0 / RWKV_HEAD_DIM)
    yn = yc * lax.rsqrt(var + GN_EPS) * lnw_ref[...] + lnb_ref[...]
    yr = (yn + bonus_ref[0]) * g_ref[0]
    mix = _mm(yr, wo_r_ref[...]) + _dot(o_ref[0], wo_m_ref[...])
    g_a = mod_ref[0, 2:3, :]
    sh_f = mod_ref[0, 3:4, :]
    sc_f = mod_ref[0, 4:5, :]
    x1 = x_ref[0] + g_a * mix
    x1_ref[0] = x1
    ms = jnp.mean(x1 * x1, axis=-1, keepdims=True)
    h2 = x1 * lax.rsqrt(ms + NORM_EPS) * nffn_ref[...] * (1.0 + sc_f) + sh_f
    h2_ref[0] = h2
    tm = h2.shape[0]
    for s in range(SUBLANES):
        h2t_ref[0, pl.ds(s, tm, stride=SUBLANES), :] = h2[:, s * LANES:(s + 1) * LANES]


def _post_call(y, bonus, g, o_pad, x, mod3, ln_w, ln_b, w_out, norm_ffn, tm):
    B, T, D = x.shape
    HP = MLA_HEADS * HEAD_PAD
    wo_r = w_out[:D_RWKV].astype(BF16)
    wo_m = jnp.pad(w_out[D_RWKV:].reshape(MLA_HEADS, V_HEAD_DIM, D),
                   ((0, 0), (0, HEAD_PAD - V_HEAD_DIM), (0, 0))).reshape(HP, D).astype(BF16)
    tok = lambda n: pl.BlockSpec((1, tm, n), lambda b, t: (b, t, 0))
    row = lambda n: pl.BlockSpec((1, n), lambda b, t: (0, 0))
    full = lambda a: pl.BlockSpec(a.shape, lambda b, t: (0,) * a.ndim)
    return pl.pallas_call(
        _post_kernel,
        grid=(B, T // tm),
        in_specs=[tok(D_RWKV), tok(D_RWKV), tok(D_RWKV), tok(HP), tok(D),
                  pl.BlockSpec((1, 6, D), lambda b, t: (b, 0, 0)),
                  row(D_RWKV), row(D_RWKV), full(wo_r), full(wo_m), row(D)],
        out_specs=[tok(D), tok(D),
                   pl.BlockSpec((1, tm * SUBLANES, LANES), lambda b, t: (b, t, 0))],
        out_shape=[jax.ShapeDtypeStruct((B, T, D), F32)] * 2
        + [jax.ShapeDtypeStruct((B, T * SUBLANES, LANES), F32)],
        compiler_params=_cparams("arbitrary", "arbitrary"),
        name="post",
    )(y, bonus, g, o_pad, x, mod3, ln_w.reshape(1, -1), ln_b.reshape(1, -1), wo_r, wo_m,
      norm_ffn.reshape(1, D))


def _first_index(mask, iota, size, axis):
    return jnp.min(jnp.where(mask, iota, size), axis=axis, keepdims=True)


def _route_kernel(h_ref, wr_ref, bias_ref, e_ref, w_ref, rank_ref, cnt_ref, base):
    tr = h_ref.shape[0]
    E = N_EXPERTS

    @pl.when(pl.program_id(0) == 0)
    def _():
        base[...] = jnp.zeros_like(base)

    logits = _mm3(wr_ref[...], h_ref[...], NT)
    scores = _sigmoid(logits)
    sel = scores + bias_ref[...]
    iota_g = lax.broadcasted_iota(I32, (GROUP_SIZE, tr), 0)
    gs_rows = []
    for gi in range(N_GROUPS):
        blk = sel[gi * GROUP_SIZE:(gi + 1) * GROUP_SIZE, :]
        m1 = jnp.max(blk, axis=0, keepdims=True)
        i1 = _first_index(blk == m1, iota_g, GROUP_SIZE, 0)
        m2 = jnp.max(jnp.where(iota_g == i1, NEG_INF, blk), axis=0, keepdims=True)
        gs_rows.append(m1 + m2)
    gs = jnp.concatenate(gs_rows, axis=0)
    iota8 = lax.broadcasted_iota(I32, (N_GROUPS, tr), 0)
    gmask = jnp.zeros((N_GROUPS, tr), jnp.bool_)
    for _ in range(TOPK_GROUPS):
        mg = jnp.max(gs, axis=0, keepdims=True)
        ig = _first_index(gs == mg, iota8, N_GROUPS, 0)
        hit = iota8 == ig
        gmask = gmask | hit
        gs = jnp.where(hit, NEG_INF, gs)
    msel = jnp.concatenate(
        [jnp.where(gmask[gi:gi + 1, :], sel[gi * GROUP_SIZE:(gi + 1) * GROUP_SIZE, :], NEG_INF)
         for gi in range(N_GROUPS)], axis=0)
    iota_e = lax.broadcasted_iota(I32, (E, tr), 0)
    e_rows, w_rows = [], []
    onehot = jnp.zeros((E, tr), F32)
    for _ in range(TOP_K):
        mv = jnp.max(msel, axis=0, keepdims=True)
        ie = _first_index(msel == mv, iota_e, E, 0)
        hit = iota_e == ie
        e_rows.append(ie)
        w_rows.append(jnp.sum(jnp.where(hit, scores, 0.0), axis=0, keepdims=True))
        onehot = jnp.where(hit, 1.0, onehot)
        msel = jnp.where(hit, NEG_INF, msel)
    top_e = jnp.concatenate(e_rows, axis=0)
    wts = jnp.concatenate(w_rows, axis=0)
    wts = wts / jnp.sum(wts, axis=0, keepdims=True) * ROUTED_SCALE
    ti = lax.broadcasted_iota(I32, (tr, tr), 0)
    tj = lax.broadcasted_iota(I32, (tr, tr), 1)
    upper = jnp.where(ti < tj, 1.0, 0.0).astype(BF16)
    pos = _dot(onehot.astype(BF16), upper) + base[...]
    rank_rows = [jnp.sum(jnp.where(iota_e == e_rows[j], pos, 0.0), axis=0, keepdims=True)
                 for j in range(TOP_K)]
    base[...] = base[...] + jnp.sum(onehot, axis=1, keepdims=True)
    e_ref[...] = top_e
    w_ref[...] = wts
    rank_ref[...] = jnp.concatenate(rank_rows, axis=0).astype(I32)
    cnt_ref[...] = base[...].astype(I32)


def _route_call(h2, w_router, router_bias, tr):
    N, D = h2.shape
    E = N_EXPERTS
    out_kn = pl.BlockSpec((TOP_K, tr), lambda i: (0, i))
    return pl.pallas_call(
        _route_kernel,
        grid=(N // tr,),
        in_specs=[pl.BlockSpec((tr, D), lambda i: (i, 0)),
                  pl.BlockSpec((E, D), lambda i: (0, 0)),
                  pl.BlockSpec((E, 1), lambda i: (0, 0))],
        out_specs=[out_kn, out_kn, out_kn, pl.BlockSpec((E, 1), lambda i: (0, 0))],
        out_shape=[jax.ShapeDtypeStruct((TOP_K, N), I32), jax.ShapeDtypeStruct((TOP_K, N), F32),
                   jax.ShapeDtypeStruct((TOP_K, N), I32), jax.ShapeDtypeStruct((E, 1), I32)],
        scratch_shapes=[pltpu.VMEM((E, 1), F32)],
        compiler_params=_cparams("arbitrary"),
        name="route",
    )(h2, w_router.T, router_bias.reshape(E, 1))


def _dest_kernel(e_ref, rank_ref, start_ref, d_ref):
    tr = e_ref.shape[1]
    iota_e = lax.broadcasted_iota(I32, (N_EXPERTS, tr), 0)
    starts = start_ref[...]
    rows = [jnp.sum(jnp.where(iota_e == e_ref[j:j + 1, :], starts, 0), axis=0, keepdims=True)
            for j in range(TOP_K)]
    d_ref[...] = jnp.concatenate(rows, axis=0) + rank_ref[...]


def _dest_call(top_e, rank, pad_starts, tr):
    K, N = top_e.shape
    spec = pl.BlockSpec((K, tr), lambda i: (0, i))
    return pl.pallas_call(
        _dest_kernel,
        grid=(N // tr,),
        in_specs=[spec, spec, pl.BlockSpec((N_EXPERTS, 1), lambda i: (0, 0))],
        out_specs=spec,
        out_shape=jax.ShapeDtypeStruct((K, N), I32),
        compiler_params=_cparams("arbitrary"),
        name="dest",
    )(top_e, rank, pad_starts.reshape(N_EXPERTS, 1))


def _sc_dispatch_call(h2t, dest3, n_rows):
    N = h2t.shape[0]
    info = plsc.get_sparse_core_info()
    n_workers = info.num_cores * info.num_subcores
    n_chunks = N // (SC_CHUNK * n_workers)
    assert n_chunks * SC_CHUNK * n_workers == N
    mesh = plsc.VectorSubcoreMesh(core_axis_name="c", subcore_axis_name="s")

    @functools.partial(
        pl.kernel, mesh=mesh,
        out_type=(jax.ShapeDtypeStruct((n_rows,) + h2t.shape[1:], h2t.dtype),
                  jax.ShapeDtypeStruct((n_rows, LANES), I32)),
        scratch_types=[pltpu.VMEM((TOP_K, SC_CHUNK), I32),
                       pltpu.VMEM((SC_CHUNK,) + h2t.shape[1:], h2t.dtype),
                       pltpu.VMEM((SC_CHUNK, LANES), I32)],
        name="sc_dispatch",
    )
    def scatter_rows(h_hbm, dest_hbm, xs_hbm, tag_hbm, idx_v, rows_v, tag_v):
        wid = lax.axis_index("s") * info.num_cores + lax.axis_index("c")
        zeros = jnp.zeros((info.num_lanes,), I32)

        @pl.loop(0, SC_CHUNK)
        def _(r):
            for l0 in range(0, LANES, info.num_lanes):
                tag_v[r, pl.ds(l0, info.num_lanes)] = zeros

        @pl.loop(0, n_chunks)
        def _(c):
            chunk = wid * n_chunks + c
            base = chunk * SC_CHUNK
            pltpu.sync_copy(dest_hbm.at[chunk], idx_v)
            pltpu.sync_copy(h_hbm.at[pl.ds(base, SC_CHUNK)], rows_v)
            for j in range(TOP_K):
                pltpu.sync_copy(rows_v, xs_hbm.at[idx_v.at[j]])

                @pl.loop(0, SC_CHUNK)
                def _(r):
                    tag_v[r, pl.ds(0, info.num_lanes)] = zeros + ((base + r) * TOP_K + j)

                pltpu.sync_copy(tag_v, tag_hbm.at[idx_v.at[j]])

    return scatter_rows(h2t, dest3)


def _moe_kernel(be_ref, nu_ref, nv_ref, xs_ref, xtag_ref, wgu_ref, wdn_ref, yt_hbm, wgu_bf, wdn_bf,
                ybuf, tag_v, tag_s, sem_tag, sem_rows):
    i = pl.program_id(0)
    n_steps = pl.num_programs(0)
    cur = i % 2
    n_real = yt_hbm.shape[0] - 2 * MOE_ROWS

    def row_copy(buf, r, tag):
        rows = pl.ds(pl.multiple_of(r * SUBLANES, SUBLANES), SUBLANES)
        return pltpu.make_async_copy(ybuf.at[buf, rows, :], yt_hbm.at[tag], sem_rows.at[buf])

    n_used = nu_ref[0]
    prv = 1 - cur

    def wait_sent(buf):
        pltpu.make_async_copy(ybuf.at[buf], ybuf.at[buf], sem_rows.at[buf]).wait()

    def send_prev():
        for r in range(MOE_ROWS):
            row_copy(prv, r, tag_s[prv, r]).start()

    @pl.when((i >= 2) & (i - 2 < n_used))
    def _():
        wait_sent(cur)

    i_blk = jnp.minimum(i, n_steps - 2)
    @pl.when((i < n_used) & ((i == 0) | (be_ref[i_blk] != be_ref[jnp.maximum(i_blk - 1, 0)])))
    def _():
        wgu_bf[...] = wgu_ref[0].astype(BF16)
        wdn_bf[...] = wdn_ref[0].astype(BF16)

    def compute():
        row = lax.broadcasted_iota(I32, (MOE_ROWS, LANES), 0)
        tags = jnp.where(row < nv_ref[i_blk], xtag_ref[...], n_real + cur * MOE_ROWS + row)
        tag_v[...] = tags.astype(F32).T[0:1, :].astype(I32)
        cp = pltpu.make_async_copy(tag_v, tag_s.at[pl.ds(cur, 1)], sem_tag)
        cp.start()
        xb = jnp.concatenate([xs_ref[pl.ds(s, MOE_ROWS, stride=SUBLANES), :]
                              for s in range(SUBLANES)], axis=1).astype(BF16)
        gu = _dot(xb, wgu_bf[...])
        act = _silu(gu[:, :D_EXPERT]) * gu[:, D_EXPERT:]
        y = _dot(act.astype(BF16), wdn_bf[...])
        for s in range(SUBLANES):
            ybuf[cur, pl.ds(s, MOE_ROWS, stride=SUBLANES), :] = y[:, s * LANES:(s + 1) * LANES]
        cp.wait()

    @pl.when((i == 0) & (i < n_used))
    def _():
        compute()

    @pl.when((i > 0) & (i < n_used))
    def _():
        send_prev()
        compute()

    @pl.when((i > 0) & (i == n_used))
    def _():
        send_prev()

    @pl.when((i == n_steps - 1) & (i - 1 < n_used))
    def _():
        wait_sent(prv)


def _moe_call(block_expert, n_used, n_valid, xs, xtag, w_gu, w_dn, n_out_rows):
    P = xs.shape[0] // SUBLANES
    D = SUBLANES * LANES
    nb = P // MOE_ROWS
    blk = lambda i, be, nu, nv: (jnp.minimum(i, nu[0] - 1), 0)
    wblk = lambda i, be, nu, nv: (be[jnp.minimum(i, nu[0] - 1)], 0, 0)
    return pl.pallas_call(
        _moe_kernel,
        grid_spec=pltpu.PrefetchScalarGridSpec(
            num_scalar_prefetch=3,
            grid=(nb + 1,),
            in_specs=[pl.BlockSpec((MOE_ROWS * SUBLANES, LANES), blk),
                      pl.BlockSpec((MOE_ROWS, LANES), blk),
                      pl.BlockSpec((1, D, 2 * D_EXPERT), wblk),
                      pl.BlockSpec((1, D_EXPERT, D), wblk)],
            out_specs=pl.BlockSpec(memory_space=pl.ANY),
            scratch_shapes=[pltpu.VMEM((D, 2 * D_EXPERT), BF16), pltpu.VMEM((D_EXPERT, D), BF16),
                            pltpu.VMEM((2, MOE_ROWS * SUBLANES, LANES), F32),
                            pltpu.VMEM((1, MOE_ROWS), I32), pltpu.SMEM((2, MOE_ROWS), I32),
                            pltpu.SemaphoreType.DMA, pltpu.SemaphoreType.DMA((2,))]),
        out_shape=jax.ShapeDtypeStruct((n_out_rows, SUBLANES, LANES), F32),
        compiler_params=_cparams("arbitrary"),
        name="moe",
    )(block_expert, n_used, n_valid, xs, xtag, w_gu, w_dn)


def _combine_kernel(w_hbm, yt_ref, h_ref, x1_ref, mod_ref, wsg_ref, wsd_ref, o_ref,
                    wts, routed, sem_w):
    i = pl.program_id(0)
    tc = h_ref.shape[0]
    cp_w = pltpu.make_async_copy(w_hbm.at[i], wts, sem_w)
    cp_w.start()
    gu = _mm(h_ref[...], wsg_ref[...])
    act = _silu(gu[:, :D_EXPERT]) * gu[:, D_EXPERT:]
    ffn = _mm(act, wsd_ref[...])
    cp_w.wait()

    def wsum(tt, carry):
        for u in range(SUBLANES):
            t = tt * SUBLANES + u
            first = pl.multiple_of(t * (TOP_K * SUBLANES), SUBLANES)
            acc = yt_ref[pl.ds(first, SUBLANES), :] * wts[0, t]
            for j in range(1, TOP_K):
                acc = acc + yt_ref[pl.ds(first + j * SUBLANES, SUBLANES), :] * wts[j, t]
            routed[pl.ds(pl.multiple_of(t * SUBLANES, SUBLANES), SUBLANES), :] = acc
        return carry

    lax.fori_loop(0, tc // SUBLANES, wsum, 0)
    routed2d = jnp.concatenate([routed[pl.ds(s, tc, stride=SUBLANES), :] for s in range(SUBLANES)],
                               axis=1)
    g_f = mod_ref[0, 5:6, :]
    o_ref[...] = x1_ref[...] + g_f * (ffn + routed2d)


def _combine_call(w3, yt, h2, x1, mod3, w_sh_gu, w_sh_dn, tokens_per_batch, tc):
    N, D = h2.shape
    tiles_per_batch = tokens_per_batch // tc
    tok = pl.BlockSpec((tc, D), lambda i: (i, 0))
    wsg = w_sh_gu.astype(BF16)
    wsd = w_sh_dn.astype(BF16)
    return pl.pallas_call(
        _combine_kernel,
        grid=(N // tc,),
        in_specs=[pl.BlockSpec(memory_space=pl.ANY),
                  pl.BlockSpec((tc * TOP_K * SUBLANES, LANES), lambda i: (i, 0)),
                  tok, tok,
                  pl.BlockSpec((1, 6, D), lambda i: (i // tiles_per_batch, 0, 0)),
                  pl.BlockSpec(wsg.shape, lambda i: (0, 0)),
                  pl.BlockSpec(wsd.shape, lambda i: (0, 0))],
        out_specs=tok,
        out_shape=jax.ShapeDtypeStruct((N, D), F32),
        scratch_shapes=[pltpu.SMEM((TOP_K, tc), F32), pltpu.VMEM((tc * SUBLANES, LANES), F32),
                        pltpu.SemaphoreType.DMA],
        compiler_params=_cparams("arbitrary"),
        name="combine",
    )(w3, yt, h2, x1, mod3, wsg, wsd)


def _tile(n, pref):
    t = min(n, pref)
    assert n % t == 0, (n, t)
    return t


def _layer(x, mod3, positions, norm_mix, w_in, rwkv_mu, decay_w0, decay_up, iclr_a0, iclr_up,
           gate_up, rwkv_k_k, rwkv_k_a, rwkv_r_k, ln_x_w, ln_x_b, q_a_norm, w_q_b, kv_a_norm,
           w_kv_b, q_norm, k_norm, w_out, norm_ffn, w_router, router_bias, w_e_gate_up, w_e_down,
           w_sh_gate_up, w_sh_down):
    B, T, D = x.shape
    N = B * T
    assert T % SCAN_CHUNK == 0
    (r, lw, k2, v, kk, akk, g, bonus, q_pad, k_pad, v_pad) = _pre_call(
        x, mod3, positions, norm_mix, w_in, rwkv_mu, decay_w0, decay_up, iclr_a0, iclr_up,
        gate_up, rwkv_k_k, rwkv_k_a, rwkv_r_k, q_a_norm, w_q_b, kv_a_norm, w_kv_b, q_norm, k_norm,
        tm=_tile(T, 512))
    y = _scan_call(r, lw, k2, v, kk, akk)
    o_pad = _attn_call(q_pad, k_pad, v_pad)
    x1, h2, h2t = _post_call(y, bonus, g, o_pad, x, mod3, ln_x_w, ln_x_b, w_out, norm_ffn,
                             tm=_tile(T, 512))
    x1 = x1.reshape(N, D)
    h2 = h2.reshape(N, D)
    h2t = h2t.reshape(N, D // LANES, LANES)

    tr = _tile(N, 512)
    top_e, wts, rank, counts = _route_call(h2, w_router, router_bias, tr)
    counts = counts.reshape(N_EXPERTS)
    padded = (counts + MOE_ROWS - 1) // MOE_ROWS * MOE_ROWS
    pad_ends = jnp.cumsum(padded)
    pad_starts = pad_ends - padded
    n_blocks = (N * TOP_K + N_EXPERTS * (MOE_ROWS - 1)) // MOE_ROWS
    block_expert = jnp.minimum(
        jnp.searchsorted(pad_ends, jnp.arange(n_blocks, dtype=I32) * MOE_ROWS, side='right'),
        N_EXPERTS - 1).astype(I32)
    n_used = (pad_ends[-1:] // MOE_ROWS).astype(I32)
    block_row0 = jnp.arange(n_blocks + 1, dtype=I32) * MOE_ROWS
    be_pad = jnp.concatenate([block_expert, block_expert[-1:]])
    n_valid = jnp.clip((pad_starts + counts)[be_pad] - block_row0, 0, MOE_ROWS).astype(I32)
    dest = _dest_call(top_e, rank, pad_starts.astype(I32), tr)

    dest3 = dest.reshape(TOP_K, N // SC_CHUNK, SC_CHUNK).transpose(1, 0, 2)
    n_rows = n_blocks * MOE_ROWS
    xs, xtag = _sc_dispatch_call(h2t, dest3, n_rows)
    yt = _moe_call(block_expert, n_used, n_valid, xs.reshape(n_rows * SUBLANES, LANES), xtag,
                   w_e_gate_up, w_e_down, N * TOP_K + 2 * MOE_ROWS)
    tc = _tile(T, COMBINE_TILE)
    w3 = wts.reshape(TOP_K, N // tc, tc).transpose(1, 0, 2)
    out = _combine_call(w3, yt.reshape(-1, LANES), h2, x1, mod3, w_sh_gate_up,
                        w_sh_down, T, tc)
    return out.reshape(B, T, D)


def kernel(x, c, positions, ada_w, ada_b, norm_mix, w_in, rwkv_mu, decay_w0, decay_up, iclr_a0, iclr_up, gate_up, rwkv_k_k, rwkv_k_a, rwkv_r_k, ln_x_w, ln_x_b, q_a_norm, w_q_b, kv_a_norm, w_kv_b, q_norm, k_norm, w_out, norm_ffn, w_router, router_bias, w_e_gate_up, w_e_down, w_sh_gate_up, w_sh_down):
    B, T, D = x.shape
    depth = ada_w.shape[0]
    for l in range(depth):
        mod3 = _mod_call(c, ada_w[l], ada_b[l]).reshape(B, 6, D)
        x = _layer(x, mod3, positions, norm_mix[l], w_in[l], rwkv_mu[l], decay_w0[l], decay_up[l],
                   iclr_a0[l], iclr_up[l], gate_up[l], rwkv_k_k[l], rwkv_k_a[l], rwkv_r_k[l],
                   ln_x_w[l], ln_x_b[l], q_a_norm[l], w_q_b[l], kv_a_norm[l], w_kv_b[l],
                   q_norm[l], k_norm[l], w_out[l], norm_ffn[l], w_router[l], router_bias[l],
                   w_e_gate_up[l], w_e_down[l], w_sh_gate_up[l], w_sh_down[l])
    return x
```

```python
import functools
import math

import jax
import jax.numpy as jnp
import numpy as np
from jax import lax
from jax.experimental import pallas as pl
from jax.experimental.pallas import tpu as pltpu
from jax.experimental.pallas import tpu_sc as plsc

F32 = jnp.float32
BF16 = jnp.bfloat16
I32 = jnp.int32

NORM_EPS = 1e-6
GN_EPS = 64e-5
RWKV_HEADS = 8
RWKV_HEAD_DIM = 64
D_RWKV = 512
DECAY_LORA = 32
ICLR_LORA = 32
GATE_LORA = 96
MLA_HEADS = 8
QK_NOPE_DIM = 64
QK_ROPE_DIM = 32
QK_HEAD_DIM = 96
V_HEAD_DIM = 64
Q_LORA_RANK = 256
KV_LORA_RANK = 128
ROPE_THETA = 10000.0
N_EXPERTS = 256
TOP_K = 8
N_GROUPS = 8
TOPK_GROUPS = 4
GROUP_SIZE = N_EXPERTS // N_GROUPS
D_EXPERT = 256
ROUTED_SCALE = 2.5
MOE_ROWS = 512
SC_CHUNK = 64
COMBINE_TILE = 256

LANES = 128
SUBLANES = 8
HEAD_PAD = 128
VMEM_LIMIT = 56 * 1024 * 1024

PRE_SUBTILES = 1
SCAN_CHUNK = 64
SCAN_BLOCK = 512
ATTN_TILE = 512
ATTN_Q_SCALE = QK_HEAD_DIM ** -0.5 * math.log2(math.e)
NEG_INF = float("-inf")


def _cparams(*sem):
    return pltpu.CompilerParams(dimension_semantics=sem, vmem_limit_bytes=VMEM_LIMIT)


def _split2(a):
    hi = a.astype(BF16)
    lo = (a - hi.astype(F32)).astype(BF16)
    return hi, lo


def _split3(a):
    hi = a.astype(BF16)
    r1 = a - hi.astype(F32)
    mid = r1.astype(BF16)
    lo = (r1 - mid.astype(F32)).astype(BF16)
    return hi, mid, lo


def _dot(a, b, dims=None):
    if dims is None:
        return jnp.dot(a, b, preferred_element_type=F32)
    return lax.dot_general(a, b, (dims, ((), ())), preferred_element_type=F32)


def _mm(a, b, dims=None):
    return _dot(a.astype(BF16), b.astype(BF16), dims)


def _mm3(a, b, dims=None):
    ah, al = _split2(a)
    bh, bl = _split2(b)
    return _dot(ah, bh, dims) + (_dot(ah, bl, dims) + _dot(al, bh, dims))


def _mm_exact_rhs(a, b_exact_bf16, dims=None):
    h, m, l = _split3(a)
    return _dot(h, b_exact_bf16, dims) + (_dot(m, b_exact_bf16, dims) + _dot(l, b_exact_bf16, dims))


NT = ((1,), (1,))
TN = ((0,), (0,))


def _sigmoid(z):
    return 1.0 / (1.0 + jnp.exp(-z))


def _silu(z):
    return z * _sigmoid(z)


def _seg_ones(width, seg):
    r = lax.broadcasted_iota(I32, (width, width), 0) // seg
    c = lax.broadcasted_iota(I32, (width, width), 1) // seg
    return jnp.where(r == c, 1.0, 0.0).astype(BF16)


def _segsum(a, ones_bd):
    hi, lo = _split2(a)
    return _dot(hi, ones_bd) + _dot(lo, ones_bd)


def _mod_kernel(c_ref, w_ref, b_ref, o_ref):
    ca = _silu(c_ref[...])
    o_ref[...] = _mm3(ca, w_ref[...]) + b_ref[...]


def _mod_call(c, ada_w, ada_b):
    B, D = c.shape
    n6 = ada_w.shape[1]
    tn = D
    return pl.pallas_call(
        _mod_kernel,
        grid=(n6 // tn,),
        in_specs=[pl.BlockSpec((B, D), lambda j: (0, 0)),
                  pl.BlockSpec((D, tn), lambda j: (0, j)),
                  pl.BlockSpec((1, tn), lambda j: (0, j))],
        out_specs=pl.BlockSpec((B, tn), lambda j: (0, j)),
        out_shape=jax.ShapeDtypeStruct((B, n6), F32),
        compiler_params=_cparams("arbitrary"),
        name="mod",
    )(c, ada_w, ada_b.reshape(1, n6))


def _pre_kernel(x_ref, mod_ref, pos_ref, nmix_ref, wrkv_ref, wlora_ref, wmla_ref,
                mu_rkv_ref, mu_lora_ref, wup_ref, w0_ref, a0_ref, kk_ref, ka_ref, rk_ref,
                qan_ref, wqb_ref, kvan_ref, wkb_ref, wvb_ref, qn_ref, kn_ref, invf_ref,
                r_ref, lw_ref, k_ref, v_ref, kkn_ref, akk_ref, g_ref, bonus_ref,
                q_ref, kout_ref, vout_ref,
                carry_rkv, carry_lora):
    ti = pl.program_id(1)
    tm = x_ref.shape[1]

    @pl.when(ti == 0)
    def _():
        carry_rkv[...] = jnp.zeros_like(carry_rkv)
        carry_lora[...] = jnp.zeros_like(carry_lora)

    ts = tm // PRE_SUBTILES
    sh_a = mod_ref[0, 0:1, :]
    sc_a = mod_ref[0, 1:2, :]
    ones64 = _seg_ones(D_RWKV, RWKV_HEAD_DIM)
    row0 = lax.broadcasted_iota(I32, (ts, 1), 0) == 0
    half = QK_ROPE_DIM // 2
    last_rows = {}

    def sub_tile(s):
        rows = slice(s * ts, (s + 1) * ts)
        xb = x_ref[0, rows, :]
        ms = jnp.mean(xb * xb, axis=-1, keepdims=True)
        h = xb * lax.rsqrt(ms + NORM_EPS) * nmix_ref[...] * (1.0 + sc_a) + sh_a
        hb = h.astype(BF16)
        yield
        u_rkv = _dot(hb, wrkv_ref[...])
        u_lora = _dot(hb, wlora_ref[...])
        u_mla = _dot(hb, wmla_ref[...])
        last_rows[s] = (u_rkv[ts - 1:ts, :], u_lora[ts - 1:ts, :])
        yield

        before_rkv, before_lora = (carry_rkv[...], carry_lora[...]) if s == 0 else last_rows[s - 1]
        prev_rkv = jnp.where(row0, before_rkv, pltpu.roll(u_rkv, 1, 0))
        prev_lora = jnp.where(row0, before_lora, pltpu.roll(u_lora, 1, 0))
        if s == PRE_SUBTILES - 1:
            carry_rkv[...] = u_rkv[ts - 1:ts, :]
            carry_lora[...] = u_lora[ts - 1:ts, :]
        us = u_rkv + (prev_rkv - u_rkv) * mu_rkv_ref[...]
        ul = u_lora + (prev_lora - u_lora) * mu_lora_ref[...]
        r = us[:, 0:D_RWKV]
        k = us[:, D_RWKV:2 * D_RWKV]
        v = us[:, 2 * D_RWKV:3 * D_RWKV]
        lane_l = lax.broadcasted_iota(I32, ul.shape, 1)
        t_in = jnp.where(lane_l < DECAY_LORA, jnp.tanh(ul),
                         jnp.where(lane_l < DECAY_LORA + ICLR_LORA, ul, _sigmoid(ul)))
        yield
        up = _mm(t_in, wup_ref[...])
        yield
        z = w0_ref[...] + up[:, 0:D_RWKV]
        lw = (-math.exp(-0.5)) * _sigmoid(z)
        a = _sigmoid(a0_ref[...] + up[:, D_RWKV:2 * D_RWKV])
        g = up[:, 2 * D_RWKV:3 * D_RWKV]
        kk = k * kk_ref[...]
        k2 = k * (1.0 + (a - 1.0) * ka_ref[...])
        yield
        ss = _segsum(kk * kk, ones64)
        bonus_sum = _segsum(r * k2 * rk_ref[...], ones64)
        yield
        kk = kk * lax.rsqrt(jnp.maximum(ss, 1e-24))
        r_ref[0, rows, :] = r
        lw_ref[0, rows, :] = lw
        k_ref[0, rows, :] = k2
        v_ref[0, rows, :] = v
        kkn_ref[0, rows, :] = kk
        akk_ref[0, rows, :] = a * kk
        g_ref[0, rows, :] = g
        bonus_ref[0, rows, :] = bonus_sum * v
        yield

        q_lat = u_mla[:, 0:Q_LORA_RANK]
        kv_lat = u_mla[:, Q_LORA_RANK:Q_LORA_RANK + KV_LORA_RANK]
        kpe_tile = u_mla[:, Q_LORA_RANK + KV_LORA_RANK:]
        qn = q_lat * lax.rsqrt(jnp.mean(q_lat * q_lat, axis=-1, keepdims=True) + NORM_EPS) * qan_ref[...]
        kvn = kv_lat * lax.rsqrt(jnp.mean(kv_lat * kv_lat, axis=-1, keepdims=True) + NORM_EPS) * kvan_ref[...]
        kvb = kvn.astype(BF16)
        yield
        q_raw = _mm(qn, wqb_ref[...])
        k_raw = _dot(kvb, wkb_ref[...])
        v_pad = _dot(kvb, wvb_ref[...])
        yield
        kpe_h = pltpu.roll(kpe_tile, QK_NOPE_DIM, 1)
        cos_t, s1, s2 = rope_tables(s)

        def tables(gain, scale):
            g_s = gain * scale
            return (cos_t * g_s, s1 * pltpu.roll(g_s, HEAD_PAD - half, 1),
                    s2 * pltpu.roll(g_s, half, 1))

        def norm_rope(xh, tabs):
            c_g, s1_g, s2_g = tabs
            ssq = jnp.sum(xh * xh, axis=-1, keepdims=True) * (1.0 / QK_HEAD_DIM)
            rot = xh * c_g + pltpu.roll(xh, HEAD_PAD - half, 1) * s1_g + pltpu.roll(xh, half, 1) * s2_g
            return rot * lax.rsqrt(ssq + NORM_EPS)

        q_tabs = tables(qn_ref[...], ATTN_Q_SCALE)
        k_tabs = tables(kn_ref[...], 1.0)
        vout_ref[0, rows, :] = v_pad.astype(BF16)
        yield
        for hh in range(MLA_HEADS):
            sl = slice(hh * HEAD_PAD, (hh + 1) * HEAD_PAD)
            q_ref[0, rows, sl] = norm_rope(q_raw[:, sl], q_tabs).astype(BF16)
            kout_ref[0, rows, sl] = norm_rope(k_raw[:, sl] + kpe_h, k_tabs).astype(BF16)
            yield

    def rope_tables(s):
        ang_t = invf_ref[...] * pos_ref[0, :, s * ts:(s + 1) * ts].astype(F32)
        frow = lax.broadcasted_iota(I32, (half, HEAD_PAD), 0)
        flane = lax.broadcasted_iota(I32, (half, HEAD_PAD), 1)
        at_x1 = flane == frow + QK_NOPE_DIM
        at_x2 = flane == frow + QK_NOPE_DIM + half
        e_cos = jnp.where(at_x1 | at_x2, 1.0, 0.0).astype(BF16)
        e_sin = jnp.concatenate([jnp.where(at_x1, -1.0, 0.0), jnp.where(at_x2, 1.0, 0.0)],
                                axis=1).astype(BF16)
        lane = lax.broadcasted_iota(I32, (1, HEAD_PAD), 1)
        off_rope = jnp.where((lane >= QK_NOPE_DIM) & (lane < QK_HEAD_DIM), 0.0, 1.0)
        cos_t = _mm_exact_rhs(jnp.cos(ang_t), e_cos, TN) + off_rope
        sin2 = _mm_exact_rhs(jnp.sin(ang_t), e_sin, TN)
        return cos_t, sin2[:, :HEAD_PAD], sin2[:, HEAD_PAD:]

    live = [sub_tile(s) for s in range(PRE_SUBTILES)]
    while live:
        live = [gen for gen in live if next(gen, "done") != "done"]


def _pad_heads(w, n_heads, width):
    kdim = w.shape[0]
    w = w.reshape(kdim, n_heads, width)
    w = jnp.pad(w, ((0, 0), (0, 0), (0, HEAD_PAD - width)))
    return w.reshape(kdim, n_heads * HEAD_PAD)


def _pre_call(x, mod3, positions, norm_mix, w_in, rwkv_mu, decay_w0, decay_up, iclr_a0, iclr_up,
              gate_up, k_k, k_a, r_k, q_a_norm, w_q_b, kv_a_norm, w_kv_b, q_norm, k_norm, tm):
    B, T, D = x.shape
    n_rkv = 3 * D_RWKV
    n_lora = DECAY_LORA + ICLR_LORA + GATE_LORA
    LORA_PAD = 256
    MLA_PAD = 512
    n_mla = Q_LORA_RANK + KV_LORA_RANK + QK_ROPE_DIM
    w_rkv = w_in[:, :n_rkv].astype(BF16)
    w_lora = jnp.pad(w_in[:, n_rkv:n_rkv + n_lora], ((0, 0), (0, LORA_PAD - n_lora))).astype(BF16)
    w_mla = jnp.pad(w_in[:, n_rkv + n_lora:], ((0, 0), (0, MLA_PAD - n_mla))).astype(BF16)
    mu_rkv = rwkv_mu[:n_rkv].reshape(1, n_rkv)
    mu_lora = jnp.pad(rwkv_mu[n_rkv:], (0, LORA_PAD - n_lora)).reshape(1, LORA_PAD)
    w_up = jnp.zeros((LORA_PAD, n_rkv), F32)
    w_up = w_up.at[0:DECAY_LORA, 0:D_RWKV].set(decay_up)
    w_up = w_up.at[DECAY_LORA:DECAY_LORA + ICLR_LORA, D_RWKV:2 * D_RWKV].set(iclr_up)
    w_up = w_up.at[DECAY_LORA + ICLR_LORA:n_lora, 2 * D_RWKV:].set(gate_up)
    w_up = w_up.astype(BF16)
    w_qb = _pad_heads(w_q_b, MLA_HEADS, QK_HEAD_DIM).astype(BF16)
    w_kv3 = w_kv_b.reshape(KV_LORA_RANK, MLA_HEADS, QK_NOPE_DIM + V_HEAD_DIM)
    w_kb = _pad_heads(w_kv3[:, :, :QK_NOPE_DIM].reshape(KV_LORA_RANK, -1), MLA_HEADS, QK_NOPE_DIM).astype(BF16)
    w_vb = _pad_heads(w_kv3[:, :, QK_NOPE_DIM:].reshape(KV_LORA_RANK, -1), MLA_HEADS, V_HEAD_DIM).astype(BF16)
    qn_pad = jnp.pad(q_norm, (0, HEAD_PAD - QK_HEAD_DIM)).reshape(1, HEAD_PAD)
    kn_pad = jnp.pad(k_norm, (0, HEAD_PAD - QK_HEAD_DIM)).reshape(1, HEAD_PAD)
    inv_freq = ROPE_THETA ** (-jnp.arange(0, QK_ROPE_DIM, 2, dtype=F32) / QK_ROPE_DIM)
    invf = inv_freq.reshape(QK_ROPE_DIM // 2, 1)
    pos3 = positions.reshape(B, 1, T)
    HP = MLA_HEADS * HEAD_PAD

    row = lambda n: pl.BlockSpec((1, n), lambda b, t: (0, 0))
    full = lambda a: pl.BlockSpec(a.shape, lambda b, t: (0,) * a.ndim)
    tok = lambda n: pl.BlockSpec((1, tm, n), lambda b, t: (b, t, 0))
    outs = ([jax.ShapeDtypeStruct((B, T, D_RWKV), F32)] * 8
            + [jax.ShapeDtypeStruct((B, T, HP), BF16)] * 3)
    return pl.pallas_call(
        _pre_kernel,
        grid=(B, T // tm),
        in_specs=[tok(D),
                  pl.BlockSpec((1, 6, D), lambda b, t: (b, 0, 0)),
                  pl.BlockSpec((1, 1, tm), lambda b, t: (b, 0, t)),
                  row(D), full(w_rkv), full(w_lora), full(w_mla),
                  row(n_rkv), row(LORA_PAD), full(w_up), row(D_RWKV), row(D_RWKV),
                  row(D_RWKV), row(D_RWKV), row(D_RWKV),
                  row(Q_LORA_RANK), full(w_qb), row(KV_LORA_RANK), full(w_kb), full(w_vb),
                  row(HEAD_PAD), row(HEAD_PAD), full(invf)],
        out_specs=[tok(D_RWKV)] * 8 + [tok(HP)] * 3,
        out_shape=outs,
        scratch_shapes=[pltpu.VMEM((1, n_rkv), F32), pltpu.VMEM((1, LORA_PAD), F32)],
        compiler_params=_cparams("arbitrary", "arbitrary"),
        name="pre",
    )(x, mod3, pos3, norm_mix.reshape(1, D), w_rkv, w_lora, w_mla, mu_rkv, mu_lora, w_up,
      decay_w0.reshape(1, -1), iclr_a0.reshape(1, -1), k_k.reshape(1, -1), k_a.reshape(1, -1),
      r_k.reshape(1, -1), q_a_norm.reshape(1, -1), w_qb, kv_a_norm.reshape(1, -1), w_kb, w_vb,
      qn_pad, kn_pad, invf)


def _scan_kernel(r_ref, lw_ref, k_ref, v_ref, kk_ref, akk_ref, y_ref, state):
    C = SCAN_CHUNK
    n_chunks = r_ref.shape[1] // C
    n_pairs = r_ref.shape[2] // LANES

    @pl.when(pl.program_id(1) == 0)
    def _():
        state[...] = jnp.zeros_like(state)

    ri = lax.broadcasted_iota(I32, (C, C), 0)
    ci = lax.broadcasted_iota(I32, (C, C), 1)
    tri_incl = jnp.where(ci <= ri, 1.0, 0.0).astype(BF16)
    r2 = lax.broadcasted_iota(I32, (2 * C, 2 * C), 0)
    c2 = lax.broadcasted_iota(I32, (2 * C, 2 * C), 1)
    same = (r2 >= C) == (c2 >= C)
    strict = same & (c2 < r2)
    incl = same & (c2 <= r2)
    eye = jnp.where(c2 == r2, 1.0, 0.0)
    head0 = lax.broadcasted_iota(I32, (C, LANES), 1) < RWKV_HEAD_DIM

    def stack2(a):
        return jnp.concatenate([jnp.where(head0, a, 0.0), jnp.where(head0, 0.0, a)], axis=0)

    C2 = 2 * C
    cat0 = lambda *a: jnp.concatenate(a, axis=0)
    cat1 = lambda *a: jnp.concatenate(a, axis=1)

    items = []
    for c in range(n_chunks):
        rows = slice(c * C, (c + 1) * C)
        lw = lw_ref[0, rows, :]
        cum = _mm_exact_rhs_left(tri_incl, lw)
        cum_end = cum[C - 1:C, :]
        w_end = jnp.exp(cum_end)
        e_pos = jnp.exp(cum)
        e_neg = jnp.exp(-cum)
        e_prev = jnp.exp(cum - lw)
        e_end = jnp.exp(cum_end - cum)
        kk = kk_ref[0, rows, :]
        k2 = k_ref[0, rows, :]
        pneg = -akk_ref[0, rows, :]
        vv = v_ref[0, rows, :]
        rt = r_ref[0, rows, :] * e_pos
        bt = kk * e_prev
        pt = pneg * e_neg
        kt = k2 * e_neg
        ph = pneg * e_end
        kh = k2 * e_end
        for pp in range(n_pairs):
            sl = slice(pp * LANES, (pp + 1) * LANES)
            items.append(dict(
                c=c, p=pp, w_end=w_end[:, sl],
                bt2=stack2(bt[:, sl]).astype(BF16), rt2=stack2(rt[:, sl]).astype(BF16),
                pk2=cat0(stack2(pt[:, sl]), stack2(kt[:, sl])).astype(BF16),
                phkh2=cat0(stack2(ph[:, sl]), stack2(kh[:, sl])).astype(BF16),
                v2=stack2(vv[:, sl])))
    for it in items:
        ab = _dot(cat0(it['bt2'], it['rt2']), it['pk2'], NT)
        it['a_ab'] = jnp.where(strict, ab[:C2, :C2], 0.0)
        it['a_ak'] = jnp.where(strict, ab[:C2, C2:], 0.0).astype(BF16)
        it['b_rpk'] = cat1(jnp.where(incl, ab[C2:, :C2], 0.0), jnp.where(incl, ab[C2:, C2:], 0.0)).astype(BF16)
        it['tinv'] = eye + it['a_ab']
    for it in items:
        it['apow'] = _mm(it['a_ab'], it['a_ab'])
    for _ in range(int(math.log2(C)) - 1):
        for it in items:
            both = _mm(cat0(it['apow'], it['tinv']), it['apow'])
            it['apow'] = both[:C2]
            it['tinv'] = it['tinv'] + both[C2:]
    for it in items:
        it['akv'] = _dot(it['a_ak'], it['v2'].astype(BF16))
    for it in items:
        tt = _dot(it['tinv'].astype(BF16), cat1(it['bt2'], it['akv'].astype(BF16)))
        it['tb_rt'] = cat0(tt[:, :LANES].astype(BF16), it['rt2'])
        it['tav'] = tt[:, LANES:]
    for it in items:
        pp = it['p']
        s0 = state[pp]
        top = _dot(it['tb_rt'], s0.astype(BF16), NT)
        u2 = top[:C2] + it['tav']
        uv = cat0(u2, it['v2']).astype(BF16)
        y2 = top[C2:] + _dot(it['b_rpk'], uv)
        state[pp] = s0 * it['w_end'] + _dot(uv, it['phkh2'], TN)
        y_ref[0, it['c'] * C:(it['c'] + 1) * C, pp * LANES:(pp + 1) * LANES] = y2[0:C] + y2[C:C2]


def _mm_exact_rhs_left(b_exact_bf16, a):
    h, m, l = _split3(a)
    return _dot(b_exact_bf16, h) + (_dot(b_exact_bf16, m) + _dot(b_exact_bf16, l))


def _scan_call(r, lw, k2, v, kk, akk):
    B, T, W = r.shape
    tb = _tile(T, SCAN_BLOCK)
    spec = pl.BlockSpec((1, tb, W), lambda b, c: (b, c, 0))
    return pl.pallas_call(
        _scan_kernel,
        grid=(B, T // tb),
        in_specs=[spec] * 6,
        out_specs=spec,
        out_shape=jax.ShapeDtypeStruct((B, T, W), F32),
        scratch_shapes=[pltpu.VMEM((W // LANES, 2 * RWKV_HEAD_DIM, LANES), F32)],
        compiler_params=_cparams("arbitrary", "arbitrary"),
        name="scan",
    )(r, lw, k2, v, kk, akk)


def _attn_kernel(q_ref, k_ref, v_ref, o_ref):
    T = q_ref.shape[1]
    tq = min(T, ATTN_TILE)
    row = lax.broadcasted_iota(I32, (tq, tq), 0)
    col = lax.broadcasted_iota(I32, (tq, tq), 1)
    causal = col <= row

    def update(q, kt, vt, carry, mask):
        m_old, l_old, acc = carry
        s = _dot(q, kt, NT)
        if mask:
            s = jnp.where(causal, s, NEG_INF)
        m_new = jnp.maximum(m_old, jnp.max(s, axis=-1, keepdims=True))
        alpha = jnp.exp2(m_old - m_new)
        p = jnp.exp2(s - m_new)
        l_new = alpha * l_old + jnp.sum(p, axis=-1, keepdims=True)
        acc = alpha * acc + _dot(p.astype(BF16), vt)
        return m_new, l_new, acc

    for qi in range(T // tq):
        q = q_ref[0, qi * tq:(qi + 1) * tq, :]
        carry = (jnp.full((tq, 1), NEG_INF, F32), jnp.zeros((tq, 1), F32),
                 jnp.zeros((tq, HEAD_PAD), F32))

        def body(ki, carry, q=q):
            rows = pl.ds(pl.multiple_of(ki * tq, tq), tq)
            return update(q, k_ref[0, rows, :], v_ref[0, rows, :], carry, False)

        carry = lax.fori_loop(0, qi, body, carry, unroll=True)
        diag = slice(qi * tq, (qi + 1) * tq)
        _, l_fin, acc = update(q, k_ref[0, diag, :], v_ref[0, diag, :], carry, True)
        o_ref[0, diag, :] = (acc / l_fin).astype(o_ref.dtype)


def _attn_call(q, k, v):
    B, T, HP = q.shape
    spec = pl.BlockSpec((1, T, HEAD_PAD), lambda b, h: (b, 0, h))
    return pl.pallas_call(
        _attn_kernel,
        grid=(B, MLA_HEADS),
        in_specs=[spec, spec, spec],
        out_specs=spec,
        out_shape=jax.ShapeDtypeStruct((B, T, HP), BF16),
        compiler_params=_cparams("arbitrary", "arbitrary"),
        name="attn",
    )(q, k, v)


def _post_kernel(y_ref, bonus_ref, g_ref, o_ref, x_ref, mod_ref, lnw_ref, lnb_ref,
                 wo_r_ref, wo_m_ref, nffn_ref, x1_ref, h2_ref, h2t_ref):
    y = y_ref[0]
    ones64 = _seg_ones(D_RWKV, RWKV_HEAD_DIM)
    mean = _segsum(y, ones64) * (1.0 / RWKV_HEAD_DIM)
    yc = y - mean
    var = _segsum(yc * yc, ones64) * (1.0 / RWKV_HEAD_DIM)
    yn = yc * lax.rsqrt(var + GN_EPS) * lnw_ref[...] + lnb_ref[...]
    yr = (yn + bonus_ref[0]) * g_ref[0]
    mix = _mm(yr, wo_r_ref[...]) + _dot(o_ref[0], wo_m_ref[...])
    g_a = mod_ref[0, 2:3, :]
    sh_f = mod_ref[0, 3:4, :]
    sc_f = mod_ref[0, 4:5, :]
    x1 = x_ref[0] + g_a * mix
    x1_ref[0] = x1
    ms = jnp.mean(x1 * x1, axis=-1, keepdims=True)
    h2 = x1 * lax.rsqrt(ms + NORM_EPS) * nffn_ref[...] * (1.0 + sc_f) + sh_f
    h2_ref[0] = h2
    tm = h2.shape[0]
    for s in range(SUBLANES):
        h2t_ref[0, pl.ds(s, tm, stride=SUBLANES), :] = h2[:, s * LANES:(s + 1) * LANES]


def _post_call(y, bonus, g, o_pad, x, mod3, ln_w, ln_b, w_out, norm_ffn, tm):
    B, T, D = x.shape
    HP = MLA_HEADS * HEAD_PAD
    wo_r = w_out[:D_RWKV].astype(BF16)
    wo_m = jnp.pad(w_out[D_RWKV:].reshape(MLA_HEADS, V_HEAD_DIM, D),
                   ((0, 0), (0, HEAD_PAD - V_HEAD_DIM), (0, 0))).reshape(HP, D).astype(BF16)
    tok = lambda n: pl.BlockSpec((1, tm, n), lambda b, t: (b, t, 0))
    row = lambda n: pl.BlockSpec((1, n), lambda b, t: (0, 0))
    full = lambda a: pl.BlockSpec(a.shape, lambda b, t: (0,) * a.ndim)
    return pl.pallas_call(
        _post_kernel,
        grid=(B, T // tm),
        in_specs=[tok(D_RWKV), tok(D_RWKV), tok(D_RWKV), tok(HP), tok(D),
                  pl.BlockSpec((1, 6, D), lambda b, t: (b, 0, 0)),
                  row(D_RWKV), row(D_RWKV), full(wo_r), full(wo_m), row(D)],
        out_specs=[tok(D), tok(D),
                   pl.BlockSpec((1, tm * SUBLANES, LANES), lambda b, t: (b, t, 0))],
        out_shape=[jax.ShapeDtypeStruct((B, T, D), F32)] * 2
        + [jax.ShapeDtypeStruct((B, T * SUBLANES, LANES), F32)],
        compiler_params=_cparams("arbitrary", "arbitrary"),
        name="post",
    )(y, bonus, g, o_pad, x, mod3, ln_w.reshape(1, -1), ln_b.reshape(1, -1), wo_r, wo_m,
      norm_ffn.reshape(1, D))


def _first_index(mask, iota, size, axis):
    return jnp.min(jnp.where(mask, iota, size), axis=axis, keepdims=True)


def _route_kernel(h_ref, wr_ref, bias_ref, e_ref, w_ref, rank_ref, cnt_ref, base):
    tr = h_ref.shape[0]
    E = N_EXPERTS

    @pl.when(pl.program_id(0) == 0)
    def _():
        base[...] = jnp.zeros_like(base)

    logits = _mm3(wr_ref[...], h_ref[...], NT)
    scores = _sigmoid(logits)
    sel = scores + bias_ref[...]
    iota_g = lax.broadcasted_iota(I32, (GROUP_SIZE, tr), 0)
    gs_rows = []
    for gi in range(N_GROUPS):
        blk = sel[gi * GROUP_SIZE:(gi + 1) * GROUP_SIZE, :]
        m1 = jnp.max(blk, axis=0, keepdims=True)
        i1 = _first_index(blk == m1, iota_g, GROUP_SIZE, 0)
        m2 = jnp.max(jnp.where(iota_g == i1, NEG_INF, blk), axis=0, keepdims=True)
        gs_rows.append(m1 + m2)
    gs = jnp.concatenate(gs_rows, axis=0)
    iota8 = lax.broadcasted_iota(I32, (N_GROUPS, tr), 0)
    gmask = jnp.zeros((N_GROUPS, tr), jnp.bool_)
    for _ in range(TOPK_GROUPS):
        mg = jnp.max(gs, axis=0, keepdims=True)
        ig = _first_index(gs == mg, iota8, N_GROUPS, 0)
        hit = iota8 == ig
        gmask = gmask | hit
        gs = jnp.where(hit, NEG_INF, gs)
    msel = jnp.concatenate(
        [jnp.where(gmask[gi:gi + 1, :], sel[gi * GROUP_SIZE:(gi + 1) * GROUP_SIZE, :], NEG_INF)
         for gi in range(N_GROUPS)], axis=0)
    iota_e = lax.broadcasted_iota(I32, (E, tr), 0)
    e_rows, w_rows = [], []
    onehot = jnp.zeros((E, tr), F32)
    for _ in range(TOP_K):
        mv = jnp.max(msel, axis=0, keepdims=True)
        ie = _first_index(msel == mv, iota_e, E, 0)
        hit = iota_e == ie
        e_rows.append(ie)
        w_rows.append(jnp.sum(jnp.where(hit, scores, 0.0), axis=0, keepdims=True))
        onehot = jnp.where(hit, 1.0, onehot)
        msel = jnp.where(hit, NEG_INF, msel)
    top_e = jnp.concatenate(e_rows, axis=0)
    wts = jnp.concatenate(w_rows, axis=0)
    wts = wts / jnp.sum(wts, axis=0, keepdims=True) * ROUTED_SCALE
    ti = lax.broadcasted_iota(I32, (tr, tr), 0)
    tj = lax.broadcasted_iota(I32, (tr, tr), 1)
    upper = jnp.where(ti < tj, 1.0, 0.0).astype(BF16)
    pos = _dot(onehot.astype(BF16), upper) + base[...]
    rank_rows = [jnp.sum(jnp.where(iota_e == e_rows[j], pos, 0.0), axis=0, keepdims=True)
                 for j in range(TOP_K)]
    base[...] = base[...] + jnp.sum(onehot, axis=1, keepdims=True)
    e_ref[...] = top_e
    w_ref[...] = wts
    rank_ref[...] = jnp.concatenate(rank_rows, axis=0).astype(I32)
    cnt_ref[...] = base[...].astype(I32)


def _route_call(h2, w_router, router_bias, tr):
    N, D = h2.shape
    E = N_EXPERTS
    out_kn = pl.BlockSpec((TOP_K, tr), lambda i: (0, i))
    return pl.pallas_call(
        _route_kernel,
        grid=(N // tr,),
        in_specs=[pl.BlockSpec((tr, D), lambda i: (i, 0)),
                  pl.BlockSpec((E, D), lambda i: (0, 0)),
                  pl.BlockSpec((E, 1), lambda i: (0, 0))],
        out_specs=[out_kn, out_kn, out_kn, pl.BlockSpec((E, 1), lambda i: (0, 0))],
        out_shape=[jax.ShapeDtypeStruct((TOP_K, N), I32), jax.ShapeDtypeStruct((TOP_K, N), F32),
                   jax.ShapeDtypeStruct((TOP_K, N), I32), jax.ShapeDtypeStruct((E, 1), I32)],
        scratch_shapes=[pltpu.VMEM((E, 1), F32)],
        compiler_params=_cparams("arbitrary"),
        name="route",
    )(h2, w_router.T, router_bias.reshape(E, 1))


def _dest_kernel(e_ref, rank_ref, start_ref, d_ref):
    tr = e_ref.shape[1]
    iota_e = lax.broadcasted_iota(I32, (N_EXPERTS, tr), 0)
    starts = start_ref[...]
    rows = [jnp.sum(jnp.where(iota_e == e_ref[j:j + 1, :], starts, 0), axis=0, keepdims=True)
            for j in range(TOP_K)]
    d_ref[...] = jnp.concatenate(rows, axis=0) + rank_ref[...]


def _dest_call(top_e, rank, pad_starts, tr):
    K, N = top_e.shape
    spec = pl.BlockSpec((K, tr), lambda i: (0, i))
    return pl.pallas_call(
        _dest_kernel,
        grid=(N // tr,),
        in_specs=[spec, spec, pl.BlockSpec((N_EXPERTS, 1), lambda i: (0, 0))],
        out_specs=spec,
        out_shape=jax.ShapeDtypeStruct((K, N), I32),
        compiler_params=_cparams("arbitrary"),
        name="dest",
    )(top_e, rank, pad_starts.reshape(N_EXPERTS, 1))


def _sc_dispatch_call(h2t, dest3, n_rows):
    N = h2t.shape[0]
    info = plsc.get_sparse_core_info()
    n_workers = info.num_cores * info.num_subcores
    n_chunks = N // (SC_CHUNK * n_workers)
    assert n_chunks * SC_CHUNK * n_workers == N
    mesh = plsc.VectorSubcoreMesh(core_axis_name="c", subcore_axis_name="s")

    @functools.partial(
        pl.kernel, mesh=mesh,
        out_type=(jax.ShapeDtypeStruct((n_rows,) + h2t.shape[1:], h2t.dtype),
                  jax.ShapeDtypeStruct((n_rows, LANES), I32)),
        scratch_types=[pltpu.VMEM((TOP_K, SC_CHUNK), I32),
                       pltpu.VMEM((SC_CHUNK,) + h2t.shape[1:], h2t.dtype),
                       pltpu.VMEM((SC_CHUNK, LANES), I32)],
        name="sc_dispatch",
    )
    def scatter_rows(h_hbm, dest_hbm, xs_hbm, tag_hbm, idx_v, rows_v, tag_v):
        wid = lax.axis_index("s") * info.num_cores + lax.axis_index("c")
        zeros = jnp.zeros((info.num_lanes,), I32)

        @pl.loop(0, SC_CHUNK)
        def _(r):
            for l0 in range(0, LANES, info.num_lanes):
                tag_v[r, pl.ds(l0, info.num_lanes)] = zeros

        @pl.loop(0, n_chunks)
        def _(c):
            chunk = wid * n_chunks + c
            base = chunk * SC_CHUNK
            pltpu.sync_copy(dest_hbm.at[chunk], idx_v)
            pltpu.sync_copy(h_hbm.at[pl.ds(base, SC_CHUNK)], rows_v)
            for j in range(TOP_K):
                pltpu.sync_copy(rows_v, xs_hbm.at[idx_v.at[j]])

                @pl.loop(0, SC_CHUNK)
                def _(r):
                    tag_v[r, pl.ds(0, info.num_lanes)] = zeros + ((base + r) * TOP_K + j)

                pltpu.sync_copy(tag_v, tag_hbm.at[idx_v.at[j]])

    return scatter_rows(h2t, dest3)


def _moe_kernel(be_ref, nu_ref, nv_ref, xs_ref, xtag_ref, wgu_ref, wdn_ref, yt_hbm, wgu_bf, wdn_bf,
                ybuf, tag_v, tag_s, sem_tag, sem_rows):
    i = pl.program_id(0)
    n_steps = pl.num_programs(0)
    cur = i % 2
    n_real = yt_hbm.shape[0] - 2 * MOE_ROWS

    def row_copy(buf, r, tag):
        rows = pl.ds(pl.multiple_of(r * SUBLANES, SUBLANES), SUBLANES)
        return pltpu.make_async_copy(ybuf.at[buf, rows, :], yt_hbm.at[tag], sem_rows.at[buf])

    n_used = nu_ref[0]
    prv = 1 - cur

    def wait_sent(buf):
        pltpu.make_async_copy(ybuf.at[buf], ybuf.at[buf], sem_rows.at[buf]).wait()

    def send_prev():
        for r in range(MOE_ROWS):
            row_copy(prv, r, tag_s[prv, r]).start()

    @pl.when((i >= 2) & (i - 2 < n_used))
    def _():
        wait_sent(cur)

    i_blk = jnp.minimum(i, n_steps - 2)
    @pl.when((i < n_used) & ((i == 0) | (be_ref[i_blk] != be_ref[jnp.maximum(i_blk - 1, 0)])))
    def _():
        wgu_bf[...] = wgu_ref[0].astype(BF16)
        wdn_bf[...] = wdn_ref[0].astype(BF16)

    def compute():
        row = lax.broadcasted_iota(I32, (MOE_ROWS, LANES), 0)
        tags = jnp.where(row < nv_ref[i_blk], xtag_ref[...], n_real + cur * MOE_ROWS + row)
        tag_v[...] = tags.astype(F32).T[0:1, :].astype(I32)
        cp = pltpu.make_async_copy(tag_v, tag_s.at[pl.ds(cur, 1)], sem_tag)
        cp.start()
        xb = jnp.concatenate([xs_ref[pl.ds(s, MOE_ROWS, stride=SUBLANES), :]
                              for s in range(SUBLANES)], axis=1).astype(BF16)
        gu = _dot(xb, wgu_bf[...])
        act = _silu(gu[:, :D_EXPERT]) * gu[:, D_EXPERT:]
        y = _dot(act.astype(BF16), wdn_bf[...])
        for s in range(SUBLANES):
            ybuf[cur, pl.ds(s, MOE_ROWS, stride=SUBLANES), :] = y[:, s * LANES:(s + 1) * LANES]
        cp.wait()

    @pl.when((i == 0) & (i < n_used))
    def _():
        compute()

    @pl.when((i > 0) & (i < n_used))
    def _():
        send_prev()
        compute()

    @pl.when((i > 0) & (i == n_used))
    def _():
        send_prev()

    @pl.when((i == n_steps - 1) & (i - 1 < n_used))
    def _():
        wait_sent(prv)


def _moe_call(block_expert, n_used, n_valid, xs, xtag, w_gu, w_dn, n_out_rows):
    P = xs.shape[0] // SUBLANES
    D = SUBLANES * LANES
    nb = P // MOE_ROWS
    blk = lambda i, be, nu, nv: (jnp.minimum(i, nu[0] - 1), 0)
    wblk = lambda i, be, nu, nv: (be[jnp.minimum(i, nu[0] - 1)], 0, 0)
    return pl.pallas_call(
        _moe_kernel,
        grid_spec=pltpu.PrefetchScalarGridSpec(
            num_scalar_prefetch=3,
            grid=(nb + 1,),
            in_specs=[pl.BlockSpec((MOE_ROWS * SUBLANES, LANES), blk),
                      pl.BlockSpec((MOE_ROWS, LANES), blk),
                      pl.BlockSpec((1, D, 2 * D_EXPERT), wblk),
                      pl.BlockSpec((1, D_EXPERT, D), wblk)],
            out_specs=pl.BlockSpec(memory_space=pl.ANY),
            scratch_shapes=[pltpu.VMEM((D, 2 * D_EXPERT), BF16), pltpu.VMEM((D_EXPERT, D), BF16),
                            pltpu.VMEM((2, MOE_ROWS * SUBLANES, LANES), F32),
                            pltpu.VMEM((1, MOE_ROWS), I32), pltpu.SMEM((2, MOE_ROWS), I32),
                            pltpu.SemaphoreType.DMA, pltpu.SemaphoreType.DMA((2,))]),
        out_shape=jax.ShapeDtypeStruct((n_out_rows, SUBLANES, LANES), F32),
        compiler_params=_cparams("arbitrary"),
        name="moe",
    )(block_expert, n_used, n_valid, xs, xtag, w_gu, w_dn)


def _combine_kernel(w_hbm, yt_ref, h_ref, x1_ref, mod_ref, wsg_ref, wsd_ref, o_ref,
                    wts, routed, sem_w):
    i = pl.program_id(0)
    tc = h_ref.shape[0]
    cp_w = pltpu.make_async_copy(w_hbm.at[i], wts, sem_w)
    cp_w.start()
    gu = _mm(h_ref[...], wsg_ref[...])
    act = _silu(gu[:, :D_EXPERT]) * gu[:, D_EXPERT:]
    ffn = _mm(act, wsd_ref[...])
    cp_w.wait()

    def wsum(tt, carry):
        for u in range(SUBLANES):
            t = tt * SUBLANES + u
            first = pl.multiple_of(t * (TOP_K * SUBLANES), SUBLANES)
            acc = yt_ref[pl.ds(first, SUBLANES), :] * wts[0, t]
            for j in range(1, TOP_K):
                acc = acc + yt_ref[pl.ds(first + j * SUBLANES, SUBLANES), :] * wts[j, t]
            routed[pl.ds(pl.multiple_of(t * SUBLANES, SUBLANES), SUBLANES), :] = acc
        return carry

    lax.fori_loop(0, tc // SUBLANES, wsum, 0)
    routed2d = jnp.concatenate([routed[pl.ds(s, tc, stride=SUBLANES), :] for s in range(SUBLANES)],
                               axis=1)
    g_f = mod_ref[0, 5:6, :]
    o_ref[...] = x1_ref[...] + g_f * (ffn + routed2d)


def _combine_call(w3, yt, h2, x1, mod3, w_sh_gu, w_sh_dn, tokens_per_batch, tc):
    N, D = h2.shape
    tiles_per_batch = tokens_per_batch // tc
    tok = pl.BlockSpec((tc, D), lambda i: (i, 0))
    wsg = w_sh_gu.astype(BF16)
    wsd = w_sh_dn.astype(BF16)
    return pl.pallas_call(
        _combine_kernel,
        grid=(N // tc,),
        in_specs=[pl.BlockSpec(memory_space=pl.ANY),
                  pl.BlockSpec((tc * TOP_K * SUBLANES, LANES), lambda i: (i, 0)),
                  tok, tok,
                  pl.BlockSpec((1, 6, D), lambda i: (i // tiles_per_batch, 0, 0)),
                  pl.BlockSpec(wsg.shape, lambda i: (0, 0)),
                  pl.BlockSpec(wsd.shape, lambda i: (0, 0))],
        out_specs=tok,
        out_shape=jax.ShapeDtypeStruct((N, D), F32),
        scratch_shapes=[pltpu.SMEM((TOP_K, tc), F32), pltpu.VMEM((tc * SUBLANES, LANES), F32),
                        pltpu.SemaphoreType.DMA],
        compiler_params=_cparams("arbitrary"),
        name="combine",
    )(w3, yt, h2, x1, mod3, wsg, wsd)


def _tile(n, pref):
    t = min(n, pref)
    assert n % t == 0, (n, t)
    return t


def _layer(x, mod3, positions, norm_mix, w_in, rwkv_mu, decay_w0, decay_up, iclr_a0, iclr_up,
           gate_up, rwkv_k_k, rwkv_k_a, rwkv_r_k, ln_x_w, ln_x_b, q_a_norm, w_q_b, kv_a_norm,
           w_kv_b, q_norm, k_norm, w_out, norm_ffn, w_router, router_bias, w_e_gate_up, w_e_down,
           w_sh_gate_up, w_sh_down):
    B, T, D = x.shape
    N = B * T
    assert T % SCAN_CHUNK == 0
    (r, lw, k2, v, kk, akk, g, bonus, q_pad, k_pad, v_pad) = _pre_call(
        x, mod3, positions, norm_mix, w_in, rwkv_mu, decay_w0, decay_up, iclr_a0, iclr_up,
        gate_up, rwkv_k_k, rwkv_k_a, rwkv_r_k, q_a_norm, w_q_b, kv_a_norm, w_kv_b, q_norm, k_norm,
        tm=_tile(T, 512))
    y = _scan_call(r, lw, k2, v, kk, akk)
    o_pad = _attn_call(q_pad, k_pad, v_pad)
    x1, h2, h2t = _post_call(y, bonus, g, o_pad, x, mod3, ln_x_w, ln_x_b, w_out, norm_ffn,
                             tm=_tile(T, 512))
    x1 = x1.reshape(N, D)
    h2 = h2.reshape(N, D)
    h2t = h2t.reshape(N, D // LANES, LANES)

    tr = _tile(N, 512)
    top_e, wts, rank, counts = _route_call(h2, w_router, router_bias, tr)
    counts = counts.reshape(N_EXPERTS)
    padded = (counts + MOE_ROWS - 1) // MOE_ROWS * MOE_ROWS
    pad_ends = jnp.cumsum(padded)
    pad_starts = pad_ends - padded
    n_blocks = (N * TOP_K + N_EXPERTS * (MOE_ROWS - 1)) // MOE_ROWS
    block_row0 = jnp.arange(n_blocks + 1, dtype=I32) * MOE_ROWS
    block_expert = jnp.minimum(
        jnp.sum((pad_ends[None, :] <= block_row0[:, None]).astype(I32), axis=1), N_EXPERTS - 1)
    n_used = (pad_ends[-1:] // MOE_ROWS).astype(I32)
    own = block_expert[:, None] == jnp.arange(N_EXPERTS, dtype=I32)[None, :]
    run_end = jnp.sum(jnp.where(own, (pad_starts + counts)[None, :], 0), axis=1)
    n_valid = jnp.clip(run_end - block_row0, 0, MOE_ROWS).astype(I32)
    dest = _dest_call(top_e, rank, pad_starts.astype(I32), tr)

    dest3 = dest.reshape(TOP_K, N // SC_CHUNK, SC_CHUNK).transpose(1, 0, 2)
    n_rows = n_blocks * MOE_ROWS
    xs, xtag = _sc_dispatch_call(h2t, dest3, n_rows)
    yt = _moe_call(block_expert, n_used, n_valid, xs.reshape(n_rows * SUBLANES, LANES), xtag,
                   w_e_gate_up, w_e_down, N * TOP_K + 2 * MOE_ROWS)
    tc = _tile(T, COMBINE_TILE)
    w3 = wts.reshape(TOP_K, N // tc, tc).transpose(1, 0, 2)
    out = _combine_call(w3, yt.reshape(-1, LANES), h2, x1, mod3, w_sh_gate_up,
                        w_sh_down, T, tc)
    return out.reshape(B, T, D)


def kernel(x, c, positions, ada_w, ada_b, norm_mix, w_in, rwkv_mu, decay_w0, decay_up, iclr_a0, iclr_up, gate_up, rwkv_k_k, rwkv_k_a, rwkv_r_k, ln_x_w, ln_x_b, q_a_norm, w_q_b, kv_a_norm, w_kv_b, q_norm, k_norm, w_out, norm_ffn, w_router, router_bias, w_e_gate_up, w_e_down, w_sh_gate_up, w_sh_down):
    B, T, D = x.shape
    depth = ada_w.shape[0]
    for l in range(depth):
        mod3 = _mod_call(c, ada_w[l], ada_b[l]).reshape(B, 6, D)
        x = _layer(x, mod3, positions, norm_mix[l], w_in[l], rwkv_mu[l], decay_w0[l], decay_up[l],
                   iclr_a0[l], iclr_up[l], gate_up[l], rwkv_k_k[l], rwkv_k_a[l], rwkv_r_k[l],
                   ln_x_w[l], ln_x_b[l], q_a_norm[l], w_q_b[l], kv_a_norm[l], w_kv_b[l],
                   q_norm[l], k_norm[l], w_out[l], norm_ffn[l], w_router[l], router_bias[l],
                   w_e_gate_up[l], w_e_down[l], w_sh_gate_up[l], w_sh_down[l])
    return x
```

```python
import functools
import math

import jax
import jax.numpy as jnp
import numpy as np
from jax import lax
from jax.experimental import pallas as pl
from jax.experimental.pallas import tpu as pltpu
from jax.experimental.pallas import tpu_sc as plsc

F32 = jnp.float32
BF16 = jnp.bfloat16
I32 = jnp.int32

NORM_EPS = 1e-6
GN_EPS = 64e-5
RWKV_HEADS = 8
RWKV_HEAD_DIM = 64
D_RWKV = 512
DECAY_LORA = 32
ICLR_LORA = 32
GATE_LORA = 96
MLA_HEADS = 8
QK_NOPE_DIM = 64
QK_ROPE_DIM = 32
QK_HEAD_DIM = 96
V_HEAD_DIM = 64
Q_LORA_RANK = 256
KV_LORA_RANK = 128
ROPE_THETA = 10000.0
N_EXPERTS = 256
TOP_K = 8
N_GROUPS = 8
TOPK_GROUPS = 4
GROUP_SIZE = N_EXPERTS // N_GROUPS
D_EXPERT = 256
ROUTED_SCALE = 2.5
MOE_ROWS = 512
SC_CHUNK = 64
COMBINE_TILE = 256

LANES = 128
SUBLANES = 8
HEAD_PAD = 128
VMEM_LIMIT = 56 * 1024 * 1024

PRE_SUBTILES = 1
SCAN_CHUNK = 64
SCAN_BLOCK = 512
ATTN_TILE = 256
ATTN_Q_SCALE = QK_HEAD_DIM ** -0.5 * math.log2(math.e)
NEG_INF = float("-inf")


def _cparams(*sem):
    return pltpu.CompilerParams(dimension_semantics=sem, vmem_limit_bytes=VMEM_LIMIT)


def _split2(a):
    hi = a.astype(BF16)
    lo = (a - hi.astype(F32)).astype(BF16)
    return hi, lo


def _split3(a):
    hi = a.astype(BF16)
    r1 = a - hi.astype(F32)
    mid = r1.astype(BF16)
    lo = (r1 - mid.astype(F32)).astype(BF16)
    return hi, mid, lo


def _dot(a, b, dims=None):
    if dims is None:
        return jnp.dot(a, b, preferred_element_type=F32)
    return lax.dot_general(a, b, (dims, ((), ())), preferred_element_type=F32)


def _mm(a, b, dims=None):
    return _dot(a.astype(BF16), b.astype(BF16), dims)


def _mm3(a, b, dims=None):
    ah, al = _split2(a)
    bh, bl = _split2(b)
    return _dot(ah, bh, dims) + (_dot(ah, bl, dims) + _dot(al, bh, dims))


def _mm_exact_rhs(a, b_exact_bf16, dims=None):
    h, m, l = _split3(a)
    return _dot(h, b_exact_bf16, dims) + (_dot(m, b_exact_bf16, dims) + _dot(l, b_exact_bf16, dims))


NT = ((1,), (1,))
TN = ((0,), (0,))


def _sigmoid(z):
    return 1.0 / (1.0 + jnp.exp(-z))


def _silu(z):
    return z * _sigmoid(z)


def _seg_ones(width, seg):
    r = lax.broadcasted_iota(I32, (width, width), 0) // seg
    c = lax.broadcasted_iota(I32, (width, width), 1) // seg
    return jnp.where(r == c, 1.0, 0.0).astype(BF16)


def _segsum(a, ones_bd):
    hi, lo = _split2(a)
    return _dot(hi, ones_bd) + _dot(lo, ones_bd)


def _mod_kernel(c_ref, w_ref, b_ref, o_ref):
    ca = _silu(c_ref[...])
    o_ref[...] = _mm3(ca, w_ref[...]) + b_ref[...]


def _mod_call(c, ada_w, ada_b):
    B, D = c.shape
    n6 = ada_w.shape[1]
    tn = D
    return pl.pallas_call(
        _mod_kernel,
        grid=(n6 // tn,),
        in_specs=[pl.BlockSpec((B, D), lambda j: (0, 0)),
                  pl.BlockSpec((D, tn), lambda j: (0, j)),
                  pl.BlockSpec((1, tn), lambda j: (0, j))],
        out_specs=pl.BlockSpec((B, tn), lambda j: (0, j)),
        out_shape=jax.ShapeDtypeStruct((B, n6), F32),
        compiler_params=_cparams("arbitrary"),
        name="mod",
    )(c, ada_w, ada_b.reshape(1, n6))


def _pre_kernel(x_ref, mod_ref, pos_ref, nmix_ref, wrkv_ref, wlora_ref, wmla_ref,
                mu_rkv_ref, mu_lora_ref, wup_ref, w0_ref, a0_ref, kk_ref, ka_ref, rk_ref,
                qan_ref, wqb_ref, kvan_ref, wkb_ref, wvb_ref, qn_ref, kn_ref, invf_ref,
                r_ref, lw_ref, k_ref, v_ref, kkn_ref, akk_ref, g_ref, bonus_ref,
                q_ref, kout_ref, vout_ref,
                carry_rkv, carry_lora):
    ti = pl.program_id(1)
    tm = x_ref.shape[1]

    @pl.when(ti == 0)
    def _():
        carry_rkv[...] = jnp.zeros_like(carry_rkv)
        carry_lora[...] = jnp.zeros_like(carry_lora)

    ts = tm // PRE_SUBTILES
    sh_a = mod_ref[0, 0:1, :]
    sc_a = mod_ref[0, 1:2, :]
    ones64 = _seg_ones(D_RWKV, RWKV_HEAD_DIM)
    row0 = lax.broadcasted_iota(I32, (ts, 1), 0) == 0
    half = QK_ROPE_DIM // 2
    last_rows = {}

    def sub_tile(s):
        rows = slice(s * ts, (s + 1) * ts)
        xb = x_ref[0, rows, :]
        ms = jnp.mean(xb * xb, axis=-1, keepdims=True)
        h = xb * lax.rsqrt(ms + NORM_EPS) * nmix_ref[...] * (1.0 + sc_a) + sh_a
        hb = h.astype(BF16)
        yield
        u_rkv = _dot(hb, wrkv_ref[...])
        u_lora = _dot(hb, wlora_ref[...])
        u_mla = _dot(hb, wmla_ref[...])
        last_rows[s] = (u_rkv[ts - 1:ts, :], u_lora[ts - 1:ts, :])
        yield

        before_rkv, before_lora = (carry_rkv[...], carry_lora[...]) if s == 0 else last_rows[s - 1]
        prev_rkv = jnp.where(row0, before_rkv, pltpu.roll(u_rkv, 1, 0))
        prev_lora = jnp.where(row0, before_lora, pltpu.roll(u_lora, 1, 0))
        if s == PRE_SUBTILES - 1:
            carry_rkv[...] = u_rkv[ts - 1:ts, :]
            carry_lora[...] = u_lora[ts - 1:ts, :]
        us = u_rkv + (prev_rkv - u_rkv) * mu_rkv_ref[...]
        ul = u_lora + (prev_lora - u_lora) * mu_lora_ref[...]
        r = us[:, 0:D_RWKV]
        k = us[:, D_RWKV:2 * D_RWKV]
        v = us[:, 2 * D_RWKV:3 * D_RWKV]
        lane_l = lax.broadcasted_iota(I32, ul.shape, 1)
        t_in = jnp.where(lane_l < DECAY_LORA, jnp.tanh(ul),
                         jnp.where(lane_l < DECAY_LORA + ICLR_LORA, ul, _sigmoid(ul)))
        yield
        up = _mm(t_in, wup_ref[...])
        yield
        z = w0_ref[...] + up[:, 0:D_RWKV]
        lw = (-math.exp(-0.5)) * _sigmoid(z)
        a = _sigmoid(a0_ref[...] + up[:, D_RWKV:2 * D_RWKV])
        g = up[:, 2 * D_RWKV:3 * D_RWKV]
        kk = k * kk_ref[...]
        k2 = k * (1.0 + (a - 1.0) * ka_ref[...])
        yield
        ss = _segsum(kk * kk, ones64)
        bonus_sum = _segsum(r * k2 * rk_ref[...], ones64)
        yield
        kk = kk * lax.rsqrt(jnp.maximum(ss, 1e-24))
        r_ref[0, rows, :] = r
        lw_ref[0, rows, :] = lw
        k_ref[0, rows, :] = k2
        v_ref[0, rows, :] = v
        kkn_ref[0, rows, :] = kk
        akk_ref[0, rows, :] = a * kk
        g_ref[0, rows, :] = g
        bonus_ref[0, rows, :] = bonus_sum * v
        yield

        q_lat = u_mla[:, 0:Q_LORA_RANK]
        kv_lat = u_mla[:, Q_LORA_RANK:Q_LORA_RANK + KV_LORA_RANK]
        kpe_tile = u_mla[:, Q_LORA_RANK + KV_LORA_RANK:]
        qn = q_lat * lax.rsqrt(jnp.mean(q_lat * q_lat, axis=-1, keepdims=True) + NORM_EPS) * qan_ref[...]
        kvn = kv_lat * lax.rsqrt(jnp.mean(kv_lat * kv_lat, axis=-1, keepdims=True) + NORM_EPS) * kvan_ref[...]
        kvb = kvn.astype(BF16)
        yield
        q_raw = _mm(qn, wqb_ref[...])
        k_raw = _dot(kvb, wkb_ref[...])
        v_pad = _dot(kvb, wvb_ref[...])
        yield
        kpe_h = pltpu.roll(kpe_tile, QK_NOPE_DIM, 1)
        cos_t, s1, s2 = rope_tables(s)

        def tables(gain, scale):
            g_s = gain * scale
            return (cos_t * g_s, s1 * pltpu.roll(g_s, HEAD_PAD - half, 1),
                    s2 * pltpu.roll(g_s, half, 1))

        def norm_rope(xh, tabs):
            c_g, s1_g, s2_g = tabs
            ssq = jnp.sum(xh * xh, axis=-1, keepdims=True) * (1.0 / QK_HEAD_DIM)
            rot = xh * c_g + pltpu.roll(xh, HEAD_PAD - half, 1) * s1_g + pltpu.roll(xh, half, 1) * s2_g
            return rot * lax.rsqrt(ssq + NORM_EPS)

        q_tabs = tables(qn_ref[...], ATTN_Q_SCALE)
        k_tabs = tables(kn_ref[...], 1.0)
        vout_ref[0, rows, :] = v_pad.astype(BF16)
        yield
        for hh in range(MLA_HEADS):
            sl = slice(hh * HEAD_PAD, (hh + 1) * HEAD_PAD)
            q_ref[0, rows, sl] = norm_rope(q_raw[:, sl], q_tabs).astype(BF16)
            kout_ref[0, rows, sl] = norm_rope(k_raw[:, sl] + kpe_h, k_tabs).astype(BF16)
            yield

    def rope_tables(s):
        ang_t = invf_ref[...] * pos_ref[0, :, s * ts:(s + 1) * ts].astype(F32)
        frow = lax.broadcasted_iota(I32, (half, HEAD_PAD), 0)
        flane = lax.broadcasted_iota(I32, (half, HEAD_PAD), 1)
        at_x1 = flane == frow + QK_NOPE_DIM
        at_x2 = flane == frow + QK_NOPE_DIM + half
        e_cos = jnp.where(at_x1 | at_x2, 1.0, 0.0).astype(BF16)
        e_sin = jnp.concatenate([jnp.where(at_x1, -1.0, 0.0), jnp.where(at_x2, 1.0, 0.0)],
                                axis=1).astype(BF16)
        lane = lax.broadcasted_iota(I32, (1, HEAD_PAD), 1)
        off_rope = jnp.where((lane >= QK_NOPE_DIM) & (lane < QK_HEAD_DIM), 0.0, 1.0)
        cos_t = _mm_exact_rhs(jnp.cos(ang_t), e_cos, TN) + off_rope
        sin2 = _mm_exact_rhs(jnp.sin(ang_t), e_sin, TN)
        return cos_t, sin2[:, :HEAD_PAD], sin2[:, HEAD_PAD:]

    live = [sub_tile(s) for s in range(PRE_SUBTILES)]
    while live:
        live = [gen for gen in live if next(gen, "done") != "done"]


def _pad_heads(w, n_heads, width):
    kdim = w.shape[0]
    w = w.reshape(kdim, n_heads, width)
    w = jnp.pad(w, ((0, 0), (0, 0), (0, HEAD_PAD - width)))
    return w.reshape(kdim, n_heads * HEAD_PAD)


def _pre_call(x, mod3, positions, norm_mix, w_in, rwkv_mu, decay_w0, decay_up, iclr_a0, iclr_up,
              gate_up, k_k, k_a, r_k, q_a_norm, w_q_b, kv_a_norm, w_kv_b, q_norm, k_norm, tm):
    B, T, D = x.shape
    n_rkv = 3 * D_RWKV
    n_lora = DECAY_LORA + ICLR_LORA + GATE_LORA
    LORA_PAD = 256
    MLA_PAD = 512
    n_mla = Q_LORA_RANK + KV_LORA_RANK + QK_ROPE_DIM
    w_rkv = w_in[:, :n_rkv].astype(BF16)
    w_lora = jnp.pad(w_in[:, n_rkv:n_rkv + n_lora], ((0, 0), (0, LORA_PAD - n_lora))).astype(BF16)
    w_mla = jnp.pad(w_in[:, n_rkv + n_lora:], ((0, 0), (0, MLA_PAD - n_mla))).astype(BF16)
    mu_rkv = rwkv_mu[:n_rkv].reshape(1, n_rkv)
    mu_lora = jnp.pad(rwkv_mu[n_rkv:], (0, LORA_PAD - n_lora)).reshape(1, LORA_PAD)
    w_up = jnp.zeros((LORA_PAD, n_rkv), F32)
    w_up = w_up.at[0:DECAY_LORA, 0:D_RWKV].set(decay_up)
    w_up = w_up.at[DECAY_LORA:DECAY_LORA + ICLR_LORA, D_RWKV:2 * D_RWKV].set(iclr_up)
    w_up = w_up.at[DECAY_LORA + ICLR_LORA:n_lora, 2 * D_RWKV:].set(gate_up)
    w_up = w_up.astype(BF16)
    w_qb = _pad_heads(w_q_b, MLA_HEADS, QK_HEAD_DIM).astype(BF16)
    w_kv3 = w_kv_b.reshape(KV_LORA_RANK, MLA_HEADS, QK_NOPE_DIM + V_HEAD_DIM)
    w_kb = _pad_heads(w_kv3[:, :, :QK_NOPE_DIM].reshape(KV_LORA_RANK, -1), MLA_HEADS, QK_NOPE_DIM).astype(BF16)
    w_vb = _pad_heads(w_kv3[:, :, QK_NOPE_DIM:].reshape(KV_LORA_RANK, -1), MLA_HEADS, V_HEAD_DIM).astype(BF16)
    qn_pad = jnp.pad(q_norm, (0, HEAD_PAD - QK_HEAD_DIM)).reshape(1, HEAD_PAD)
    kn_pad = jnp.pad(k_norm, (0, HEAD_PAD - QK_HEAD_DIM)).reshape(1, HEAD_PAD)
    inv_freq = ROPE_THETA ** (-jnp.arange(0, QK_ROPE_DIM, 2, dtype=F32) / QK_ROPE_DIM)
    invf = inv_freq.reshape(QK_ROPE_DIM // 2, 1)
    pos3 = positions.reshape(B, 1, T)
    HP = MLA_HEADS * HEAD_PAD

    row = lambda n: pl.BlockSpec((1, n), lambda b, t: (0, 0))
    full = lambda a: pl.BlockSpec(a.shape, lambda b, t: (0,) * a.ndim)
    tok = lambda n: pl.BlockSpec((1, tm, n), lambda b, t: (b, t, 0))
    outs = ([jax.ShapeDtypeStruct((B, T, D_RWKV), F32)] * 8
            + [jax.ShapeDtypeStruct((B, T, HP), BF16)] * 3)
    return pl.pallas_call(
        _pre_kernel,
        grid=(B, T // tm),
        in_specs=[tok(D),
                  pl.BlockSpec((1, 6, D), lambda b, t: (b, 0, 0)),
                  pl.BlockSpec((1, 1, tm), lambda b, t: (b, 0, t)),
                  row(D), full(w_rkv), full(w_lora), full(w_mla),
                  row(n_rkv), row(LORA_PAD), full(w_up), row(D_RWKV), row(D_RWKV),
                  row(D_RWKV), row(D_RWKV), row(D_RWKV),
                  row(Q_LORA_RANK), full(w_qb), row(KV_LORA_RANK), full(w_kb), full(w_vb),
                  row(HEAD_PAD), row(HEAD_PAD), full(invf)],
        out_specs=[tok(D_RWKV)] * 8 + [tok(HP)] * 3,
        out_shape=outs,
        scratch_shapes=[pltpu.VMEM((1, n_rkv), F32), pltpu.VMEM((1, LORA_PAD), F32)],
        compiler_params=_cparams("arbitrary", "arbitrary"),
        name="pre",
    )(x, mod3, pos3, norm_mix.reshape(1, D), w_rkv, w_lora, w_mla, mu_rkv, mu_lora, w_up,
      decay_w0.reshape(1, -1), iclr_a0.reshape(1, -1), k_k.reshape(1, -1), k_a.reshape(1, -1),
      r_k.reshape(1, -1), q_a_norm.reshape(1, -1), w_qb, kv_a_norm.reshape(1, -1), w_kb, w_vb,
      qn_pad, kn_pad, invf)


def _scan_kernel(r_ref, lw_ref, k_ref, v_ref, kk_ref, akk_ref, y_ref, state):
    C = SCAN_CHUNK
    n_chunks = r_ref.shape[1] // C
    n_pairs = r_ref.shape[2] // LANES

    @pl.when(pl.program_id(1) == 0)
    def _():
        state[...] = jnp.zeros_like(state)

    ri = lax.broadcasted_iota(I32, (C, C), 0)
    ci = lax.broadcasted_iota(I32, (C, C), 1)
    tri_incl = jnp.where(ci <= ri, 1.0, 0.0).astype(BF16)
    r2 = lax.broadcasted_iota(I32, (2 * C, 2 * C), 0)
    c2 = lax.broadcasted_iota(I32, (2 * C, 2 * C), 1)
    same = (r2 >= C) == (c2 >= C)
    strict = same & (c2 < r2)
    incl = same & (c2 <= r2)
    eye = jnp.where(c2 == r2, 1.0, 0.0)
    head0 = lax.broadcasted_iota(I32, (C, LANES), 1) < RWKV_HEAD_DIM

    def stack2(a):
        return jnp.concatenate([jnp.where(head0, a, 0.0), jnp.where(head0, 0.0, a)], axis=0)

    C2 = 2 * C
    cat0 = lambda *a: jnp.concatenate(a, axis=0)
    cat1 = lambda *a: jnp.concatenate(a, axis=1)

    items = []
    for c in range(n_chunks):
        rows = slice(c * C, (c + 1) * C)
        lw = lw_ref[0, rows, :]
        cum = _mm_exact_rhs_left(tri_incl, lw)
        cum_end = cum[C - 1:C, :]
        w_end = jnp.exp(cum_end)
        e_pos = jnp.exp(cum)
        e_neg = jnp.exp(-cum)
        e_prev = jnp.exp(cum - lw)
        e_end = jnp.exp(cum_end - cum)
        kk = kk_ref[0, rows, :]
        k2 = k_ref[0, rows, :]
        pneg = -akk_ref[0, rows, :]
        vv = v_ref[0, rows, :]
        rt = r_ref[0, rows, :] * e_pos
        bt = kk * e_prev
        pt = pneg * e_neg
        kt = k2 * e_neg
        ph = pneg * e_end
        kh = k2 * e_end
        for pp in range(n_pairs):
            sl = slice(pp * LANES, (pp + 1) * LANES)
            items.append(dict(
                c=c, p=pp, w_end=w_end[:, sl],
                bt2=stack2(bt[:, sl]).astype(BF16), rt2=stack2(rt[:, sl]).astype(BF16),
                pk2=cat0(stack2(pt[:, sl]), stack2(kt[:, sl])).astype(BF16),
                phkh2=cat0(stack2(ph[:, sl]), stack2(kh[:, sl])).astype(BF16),
                v2=stack2(vv[:, sl])))
    for it in items:
        ab = _dot(cat0(it['bt2'], it['rt2']), it['pk2'], NT)
        it['a_ab'] = jnp.where(strict, ab[:C2, :C2], 0.0)
        it['a_ak'] = jnp.where(strict, ab[:C2, C2:], 0.0).astype(BF16)
        it['b_rpk'] = cat1(jnp.where(incl, ab[C2:, :C2], 0.0), jnp.where(incl, ab[C2:, C2:], 0.0)).astype(BF16)
        it['tinv'] = eye + it['a_ab']
    for it in items:
        it['apow'] = _mm(it['a_ab'], it['a_ab'])
    for _ in range(int(math.log2(C)) - 1):
        for it in items:
            both = _mm(cat0(it['apow'], it['tinv']), it['apow'])
            it['apow'] = both[:C2]
            it['tinv'] = it['tinv'] + both[C2:]
    for it in items:
        it['akv'] = _dot(it['a_ak'], it['v2'].astype(BF16))
    for it in items:
        tt = _dot(it['tinv'].astype(BF16), cat1(it['bt2'], it['akv'].astype(BF16)))
        it['tb_rt'] = cat0(tt[:, :LANES].astype(BF16), it['rt2'])
        it['tav'] = tt[:, LANES:]
    for it in items:
        pp = it['p']
        s0 = state[pp]
        top = _dot(it['tb_rt'], s0.astype(BF16), NT)
        u2 = top[:C2] + it['tav']
        uv = cat0(u2, it['v2']).astype(BF16)
        y2 = top[C2:] + _dot(it['b_rpk'], uv)
        state[pp] = s0 * it['w_end'] + _dot(uv, it['phkh2'], TN)
        y_ref[0, it['c'] * C:(it['c'] + 1) * C, pp * LANES:(pp + 1) * LANES] = y2[0:C] + y2[C:C2]


def _mm_exact_rhs_left(b_exact_bf16, a):
    h, m, l = _split3(a)
    return _dot(b_exact_bf16, h) + (_dot(b_exact_bf16, m) + _dot(b_exact_bf16, l))


def _scan_call(r, lw, k2, v, kk, akk):
    B, T, W = r.shape
    tb = _tile(T, SCAN_BLOCK)
    spec = pl.BlockSpec((1, tb, W), lambda b, c: (b, c, 0))
    return pl.pallas_call(
        _scan_kernel,
        grid=(B, T // tb),
        in_specs=[spec] * 6,
        out_specs=spec,
        out_shape=jax.ShapeDtypeStruct((B, T, W), F32),
        scratch_shapes=[pltpu.VMEM((W // LANES, 2 * RWKV_HEAD_DIM, LANES), F32)],
        compiler_params=_cparams("arbitrary", "arbitrary"),
        name="scan",
    )(r, lw, k2, v, kk, akk)


def _attn_kernel(q_ref, k_ref, v_ref, o_ref):
    T = q_ref.shape[1]
    tq = min(T, ATTN_TILE)
    row = lax.broadcasted_iota(I32, (tq, tq), 0)
    col = lax.broadcasted_iota(I32, (tq, tq), 1)
    causal = col <= row

    def update(q, kt, vt, carry, mask):
        m_old, l_old, acc = carry
        s = _dot(q, kt, NT)
        if mask:
            s = jnp.where(causal, s, NEG_INF)
        m_new = jnp.maximum(m_old, jnp.max(s, axis=-1, keepdims=True))
        alpha = jnp.exp2(m_old - m_new)
        p = jnp.exp2(s - m_new)
        l_new = alpha * l_old + jnp.sum(p, axis=-1, keepdims=True)
        acc = alpha * acc + _dot(p.astype(BF16), vt)
        return m_new, l_new, acc

    for qi in range(T // tq):
        q = q_ref[0, qi * tq:(qi + 1) * tq, :]
        carry = (jnp.full((tq, 1), NEG_INF, F32), jnp.zeros((tq, 1), F32),
                 jnp.zeros((tq, HEAD_PAD), F32))

        def body(ki, carry, q=q):
            rows = pl.ds(pl.multiple_of(ki * tq, tq), tq)
            return update(q, k_ref[0, rows, :], v_ref[0, rows, :], carry, False)

        carry = lax.fori_loop(0, qi, body, carry, unroll=True)
        diag = slice(qi * tq, (qi + 1) * tq)
        _, l_fin, acc = update(q, k_ref[0, diag, :], v_ref[0, diag, :], carry, True)
        o_ref[0, diag, :] = (acc / l_fin).astype(o_ref.dtype)


def _attn_call(q, k, v):
    B, T, HP = q.shape
    spec = pl.BlockSpec((1, T, HEAD_PAD), lambda b, h: (b, 0, h))
    return pl.pallas_call(
        _attn_kernel,
        grid=(B, MLA_HEADS),
        in_specs=[spec, spec, spec],
        out_specs=spec,
        out_shape=jax.ShapeDtypeStruct((B, T, HP), BF16),
        compiler_params=_cparams("arbitrary", "arbitrary"),
        name="attn",
    )(q, k, v)


def _post_kernel(y_ref, bonus_ref, g_ref, o_ref, x_ref, mod_ref, lnw_ref, lnb_ref,
                 wo_r_ref, wo_m_ref, nffn_ref, x1_ref, h2_ref, h2t_ref):
    y = y_ref[0]
    ones64 = _seg_ones(D_RWKV, RWKV_HEAD_DIM)
    mean = _segsum(y, ones64) * (1.0 / RWKV_HEAD_DIM)
    yc = y - mean
    var = _segsum(yc * yc, ones64) * (1.0 / RWKV_HEAD_DIM)
    yn = yc * lax.rsqrt(var + GN_EPS) * lnw_ref[...] + lnb_ref[...]
    yr = (yn + bonus_ref[0]) * g_ref[0]
    mix = _mm(yr, wo_r_ref[...]) + _dot(o_ref[0], wo_m_ref[...])
    g_a = mod_ref[0, 2:3, :]
    sh_f = mod_ref[0, 3:4, :]
    sc_f = mod_ref[0, 4:5, :]
    x1 = x_ref[0] + g_a * mix
    x1_ref[0] = x1
    ms = jnp.mean(x1 * x1, axis=-1, keepdims=True)
    h2 = x1 * lax.rsqrt(ms + NORM_EPS) * nffn_ref[...] * (1.0 + sc_f) + sh_f
    h2_ref[0] = h2
    tm = h2.shape[0]
    for s in range(SUBLANES):
        h2t_ref[0, pl.ds(s, tm, stride=SUBLANES), :] = h2[:, s * LANES:(s + 1) * LANES]


def _post_call(y, bonus, g, o_pad, x, mod3, ln_w, ln_b, w_out, norm_ffn, tm):
    B, T, D = x.shape
    HP = MLA_HEADS * HEAD_PAD
    wo_r = w_out[:D_RWKV].astype(BF16)
    wo_m = jnp.pad(w_out[D_RWKV:].reshape(MLA_HEADS, V_HEAD_DIM, D),
                   ((0, 0), (0, HEAD_PAD - V_HEAD_DIM), (0, 0))).reshape(HP, D).astype(BF16)
    tok = lambda n: pl.BlockSpec((1, tm, n), lambda b, t: (b, t, 0))
    row = lambda n: pl.BlockSpec((1, n), lambda b, t: (0, 0))
    full = lambda a: pl.BlockSpec(a.shape, lambda b, t: (0,) * a.ndim)
    return pl.pallas_call(
        _post_kernel,
        grid=(B, T // tm),
        in_specs=[tok(D_RWKV), tok(D_RWKV), tok(D_RWKV), tok(HP), tok(D),
                  pl.BlockSpec((1, 6, D), lambda b, t: (b, 0, 0)),
                  row(D_RWKV), row(D_RWKV), full(wo_r), full(wo_m), row(D)],
        out_specs=[tok(D), tok(D),
                   pl.BlockSpec((1, tm * SUBLANES, LANES), lambda b, t: (b, t, 0))],
        out_shape=[jax.ShapeDtypeStruct((B, T, D), F32)] * 2
        + [jax.ShapeDtypeStruct((B, T * SUBLANES, LANES), F32)],
        compiler_params=_cparams("arbitrary", "arbitrary"),
        name="post",
    )(y, bonus, g, o_pad, x, mod3, ln_w.reshape(1, -1), ln_b.reshape(1, -1), wo_r, wo_m,
      norm_ffn.reshape(1, D))


def _first_index(mask, iota, size, axis):
    return jnp.min(jnp.where(mask, iota, size), axis=axis, keepdims=True)


def _route_kernel(h_ref, wr_ref, bias_ref, e_ref, w_ref, rank_ref, cnt_ref, base):
    tr = h_ref.shape[0]
    E = N_EXPERTS

    @pl.when(pl.program_id(0) == 0)
    def _():
        base[...] = jnp.zeros_like(base)

    logits = _mm3(wr_ref[...], h_ref[...], NT)
    scores = _sigmoid(logits)
    sel = scores + bias_ref[...]
    iota_g = lax.broadcasted_iota(I32, (GROUP_SIZE, tr), 0)
    gs_rows = []
    for gi in range(N_GROUPS):
        blk = sel[gi * GROUP_SIZE:(gi + 1) * GROUP_SIZE, :]
        m1 = jnp.max(blk, axis=0, keepdims=True)
        i1 = _first_index(blk == m1, iota_g, GROUP_SIZE, 0)
        m2 = jnp.max(jnp.where(iota_g == i1, NEG_INF, blk), axis=0, keepdims=True)
        gs_rows.append(m1 + m2)
    gs = jnp.concatenate(gs_rows, axis=0)
    iota8 = lax.broadcasted_iota(I32, (N_GROUPS, tr), 0)
    gmask = jnp.zeros((N_GROUPS, tr), jnp.bool_)
    for _ in range(TOPK_GROUPS):
        mg = jnp.max(gs, axis=0, keepdims=True)
        ig = _first_index(gs == mg, iota8, N_GROUPS, 0)
        hit = iota8 == ig
        gmask = gmask | hit
        gs = jnp.where(hit, NEG_INF, gs)
    msel = jnp.concatenate(
        [jnp.where(gmask[gi:gi + 1, :], sel[gi * GROUP_SIZE:(gi + 1) * GROUP_SIZE, :], NEG_INF)
         for gi in range(N_GROUPS)], axis=0)
    iota_e = lax.broadcasted_iota(I32, (E, tr), 0)
    e_rows, w_rows = [], []
    onehot = jnp.zeros((E, tr), F32)
    for _ in range(TOP_K):
        mv = jnp.max(msel, axis=0, keepdims=True)
        ie = _first_index(msel == mv, iota_e, E, 0)
        hit = iota_e == ie
        e_rows.append(ie)
        w_rows.append(jnp.sum(jnp.where(hit, scores, 0.0), axis=0, keepdims=True))
        onehot = jnp.where(hit, 1.0, onehot)
        msel = jnp.where(hit, NEG_INF, msel)
    top_e = jnp.concatenate(e_rows, axis=0)
    wts = jnp.concatenate(w_rows, axis=0)
    wts = wts / jnp.sum(wts, axis=0, keepdims=True) * ROUTED_SCALE
    ti = lax.broadcasted_iota(I32, (tr, tr), 0)
    tj = lax.broadcasted_iota(I32, (tr, tr), 1)
    upper = jnp.where(ti < tj, 1.0, 0.0).astype(BF16)
    pos = _dot(onehot.astype(BF16), upper) + base[...]
    rank_rows = [jnp.sum(jnp.where(iota_e == e_rows[j], pos, 0.0), axis=0, keepdims=True)
                 for j in range(TOP_K)]
    base[...] = base[...] + jnp.sum(onehot, axis=1, keepdims=True)
    e_ref[...] = top_e
    w_ref[...] = wts
    rank_ref[...] = jnp.concatenate(rank_rows, axis=0).astype(I32)
    cnt_ref[...] = base[...].astype(I32)


def _route_call(h2, w_router, router_bias, tr):
    N, D = h2.shape
    E = N_EXPERTS
    out_kn = pl.BlockSpec((TOP_K, tr), lambda i: (0, i))
    return pl.pallas_call(
        _route_kernel,
        grid=(N // tr,),
        in_specs=[pl.BlockSpec((tr, D), lambda i: (i, 0)),
                  pl.BlockSpec((E, D), lambda i: (0, 0)),
                  pl.BlockSpec((E, 1), lambda i: (0, 0))],
        out_specs=[out_kn, out_kn, out_kn, pl.BlockSpec((E, 1), lambda i: (0, 0))],
        out_shape=[jax.ShapeDtypeStruct((TOP_K, N), I32), jax.ShapeDtypeStruct((TOP_K, N), F32),
                   jax.ShapeDtypeStruct((TOP_K, N), I32), jax.ShapeDtypeStruct((E, 1), I32)],
        scratch_shapes=[pltpu.VMEM((E, 1), F32)],
        compiler_params=_cparams("arbitrary"),
        name="route",
    )(h2, w_router.T, router_bias.reshape(E, 1))


def _dest_kernel(e_ref, rank_ref, start_ref, d_ref):
    tr = e_ref.shape[1]
    iota_e = lax.broadcasted_iota(I32, (N_EXPERTS, tr), 0)
    starts = start_ref[...]
    rows = [jnp.sum(jnp.where(iota_e == e_ref[j:j + 1, :], starts, 0), axis=0, keepdims=True)
            for j in range(TOP_K)]
    d_ref[...] = jnp.concatenate(rows, axis=0) + rank_ref[...]


def _dest_call(top_e, rank, pad_starts, tr):
    K, N = top_e.shape
    spec = pl.BlockSpec((K, tr), lambda i: (0, i))
    return pl.pallas_call(
        _dest_kernel,
        grid=(N // tr,),
        in_specs=[spec, spec, pl.BlockSpec((N_EXPERTS, 1), lambda i: (0, 0))],
        out_specs=spec,
        out_shape=jax.ShapeDtypeStruct((K, N), I32),
        compiler_params=_cparams("arbitrary"),
        name="dest",
    )(top_e, rank, pad_starts.reshape(N_EXPERTS, 1))


def _sc_dispatch_call(h2t, dest3, n_rows):
    N = h2t.shape[0]
    info = plsc.get_sparse_core_info()
    n_workers = info.num_cores * info.num_subcores
    n_chunks = N // (SC_CHUNK * n_workers)
    assert n_chunks * SC_CHUNK * n_workers == N
    mesh = plsc.VectorSubcoreMesh(core_axis_name="c", subcore_axis_name="s")

    @functools.partial(
        pl.kernel, mesh=mesh,
        out_type=(jax.ShapeDtypeStruct((n_rows,) + h2t.shape[1:], h2t.dtype),
                  jax.ShapeDtypeStruct((n_rows, LANES), I32)),
        scratch_types=[pltpu.VMEM((TOP_K, SC_CHUNK), I32),
                       pltpu.VMEM((SC_CHUNK,) + h2t.shape[1:], h2t.dtype),
                       pltpu.VMEM((SC_CHUNK, LANES), I32)],
        name="sc_dispatch",
    )
    def scatter_rows(h_hbm, dest_hbm, xs_hbm, tag_hbm, idx_v, rows_v, tag_v):
        wid = lax.axis_index("s") * info.num_cores + lax.axis_index("c")
        zeros = jnp.zeros((info.num_lanes,), I32)

        @pl.loop(0, SC_CHUNK)
        def _(r):
            for l0 in range(0, LANES, info.num_lanes):
                tag_v[r, pl.ds(l0, info.num_lanes)] = zeros

        @pl.loop(0, n_chunks)
        def _(c):
            chunk = wid * n_chunks + c
            base = chunk * SC_CHUNK
            pltpu.sync_copy(dest_hbm.at[chunk], idx_v)
            pltpu.sync_copy(h_hbm.at[pl.ds(base, SC_CHUNK)], rows_v)
            for j in range(TOP_K):
                pltpu.sync_copy(rows_v, xs_hbm.at[idx_v.at[j]])

                @pl.loop(0, SC_CHUNK)
                def _(r):
                    tag_v[r, pl.ds(0, info.num_lanes)] = zeros + ((base + r) * TOP_K + j)

                pltpu.sync_copy(tag_v, tag_hbm.at[idx_v.at[j]])

    return scatter_rows(h2t, dest3)


def _moe_kernel(be_ref, nu_ref, nv_ref, xs_ref, xtag_ref, wgu_ref, wdn_ref, yt_hbm, wgu_bf, wdn_bf,
                ybuf, tag_v, tag_s, sem_tag, sem_rows):
    i = pl.program_id(0)
    n_steps = pl.num_programs(0)
    cur = i % 2
    n_real = yt_hbm.shape[0] - 2 * MOE_ROWS

    def row_copy(buf, r, tag):
        rows = pl.ds(pl.multiple_of(r * SUBLANES, SUBLANES), SUBLANES)
        return pltpu.make_async_copy(ybuf.at[buf, rows, :], yt_hbm.at[tag], sem_rows.at[buf])

    n_used = nu_ref[0]
    prv = 1 - cur

    def wait_sent(buf):
        pltpu.make_async_copy(ybuf.at[buf], ybuf.at[buf], sem_rows.at[buf]).wait()

    def send_prev():
        for r in range(MOE_ROWS):
            row_copy(prv, r, tag_s[prv, r]).start()

    @pl.when((i >= 2) & (i - 2 < n_used))
    def _():
        wait_sent(cur)

    i_blk = jnp.minimum(i, n_steps - 2)
    @pl.when((i < n_used) & ((i == 0) | (be_ref[i_blk] != be_ref[jnp.maximum(i_blk - 1, 0)])))
    def _():
        wgu_bf[...] = wgu_ref[0].astype(BF16)
        wdn_bf[...] = wdn_ref[0].astype(BF16)

    def compute():
        row = lax.broadcasted_iota(I32, (MOE_ROWS, LANES), 0)
        tags = jnp.where(row < nv_ref[i_blk], xtag_ref[...], n_real + cur * MOE_ROWS + row)
        tag_v[...] = tags.astype(F32).T[0:1, :].astype(I32)
        cp = pltpu.make_async_copy(tag_v, tag_s.at[pl.ds(cur, 1)], sem_tag)
        cp.start()
        xb = jnp.concatenate([xs_ref[pl.ds(s, MOE_ROWS, stride=SUBLANES), :]
                              for s in range(SUBLANES)], axis=1).astype(BF16)
        gu = _dot(xb, wgu_bf[...])
        act = _silu(gu[:, :D_EXPERT]) * gu[:, D_EXPERT:]
        y = _dot(act.astype(BF16), wdn_bf[...])
        for s in range(SUBLANES):
            ybuf[cur, pl.ds(s, MOE_ROWS, stride=SUBLANES), :] = y[:, s * LANES:(s + 1) * LANES]
        cp.wait()

    @pl.when((i == 0) & (i < n_used))
    def _():
        compute()

    @pl.when((i > 0) & (i < n_used))
    def _():
        send_prev()
        compute()

    @pl.when((i > 0) & (i == n_used))
    def _():
        send_prev()

    @pl.when((i == n_steps - 1) & (i - 1 < n_used))
    def _():
        wait_sent(prv)


def _moe_call(block_expert, n_used, n_valid, xs, xtag, w_gu, w_dn, n_out_rows):
    P = xs.shape[0] // SUBLANES
    D = SUBLANES * LANES
    nb = P // MOE_ROWS
    blk = lambda i, be, nu, nv: (jnp.minimum(i, nu[0] - 1), 0)
    wblk = lambda i, be, nu, nv: (be[jnp.minimum(i, nu[0] - 1)], 0, 0)
    return pl.pallas_call(
        _moe_kernel,
        grid_spec=pltpu.PrefetchScalarGridSpec(
            num_scalar_prefetch=3,
            grid=(nb + 1,),
            in_specs=[pl.BlockSpec((MOE_ROWS * SUBLANES, LANES), blk),
                      pl.BlockSpec((MOE_ROWS, LANES), blk),
                      pl.BlockSpec((1, D, 2 * D_EXPERT), wblk),
                      pl.BlockSpec((1, D_EXPERT, D), wblk)],
            out_specs=pl.BlockSpec(memory_space=pl.ANY),
            scratch_shapes=[pltpu.VMEM((D, 2 * D_EXPERT), BF16), pltpu.VMEM((D_EXPERT, D), BF16),
                            pltpu.VMEM((2, MOE_ROWS * SUBLANES, LANES), F32),
                            pltpu.VMEM((1, MOE_ROWS), I32), pltpu.SMEM((2, MOE_ROWS), I32),
                            pltpu.SemaphoreType.DMA, pltpu.SemaphoreType.DMA((2,))]),
        out_shape=jax.ShapeDtypeStruct((n_out_rows, SUBLANES, LANES), F32),
        compiler_params=_cparams("arbitrary"),
        name="moe",
    )(block_expert, n_used, n_valid, xs, xtag, w_gu, w_dn)


def _combine_kernel(w_hbm, yt_ref, h_ref, x1_ref, mod_ref, wsg_ref, wsd_ref, o_ref,
                    wts, routed, sem_w):
    i = pl.program_id(0)
    tc = h_ref.shape[0]
    cp_w = pltpu.make_async_copy(w_hbm.at[i], wts, sem_w)
    cp_w.start()
    gu = _mm(h_ref[...], wsg_ref[...])
    act = _silu(gu[:, :D_EXPERT]) * gu[:, D_EXPERT:]
    ffn = _mm(act, wsd_ref[...])
    cp_w.wait()

    def wsum(tt, carry):
        for u in range(SUBLANES):
            t = tt * SUBLANES + u
            first = pl.multiple_of(t * (TOP_K * SUBLANES), SUBLANES)
            acc = yt_ref[pl.ds(first, SUBLANES), :] * wts[0, t]
            for j in range(1, TOP_K):
                acc = acc + yt_ref[pl.ds(first + j * SUBLANES, SUBLANES), :] * wts[j, t]
            routed[pl.ds(pl.multiple_of(t * SUBLANES, SUBLANES), SUBLANES), :] = acc
        return carry

    lax.fori_loop(0, tc // SUBLANES, wsum, 0)
    routed2d = jnp.concatenate([routed[pl.ds(s, tc, stride=SUBLANES), :] for s in range(SUBLANES)],
                               axis=1)
    g_f = mod_ref[0, 5:6, :]
    o_ref[...] = x1_ref[...] + g_f * (ffn + routed2d)


def _combine_call(w3, yt, h2, x1, mod3, w_sh_gu, w_sh_dn, tokens_per_batch, tc):
    N, D = h2.shape
    tiles_per_batch = tokens_per_batch // tc
    tok = pl.BlockSpec((tc, D), lambda i: (i, 0))
    wsg = w_sh_gu.astype(BF16)
    wsd = w_sh_dn.astype(BF16)
    return pl.pallas_call(
        _combine_kernel,
        grid=(N // tc,),
        in_specs=[pl.BlockSpec(memory_space=pl.ANY),
                  pl.BlockSpec((tc * TOP_K * SUBLANES, LANES), lambda i: (i, 0)),
                  tok, tok,
                  pl.BlockSpec((1, 6, D), lambda i: (i // tiles_per_batch, 0, 0)),
                  pl.BlockSpec(wsg.shape, lambda i: (0, 0)),
                  pl.BlockSpec(wsd.shape, lambda i: (0, 0))],
        out_specs=tok,
        out_shape=jax.ShapeDtypeStruct((N, D), F32),
        scratch_shapes=[pltpu.SMEM((TOP_K, tc), F32), pltpu.VMEM((tc * SUBLANES, LANES), F32),
                        pltpu.SemaphoreType.DMA],
        compiler_params=_cparams("arbitrary"),
        name="combine",
    )(w3, yt, h2, x1, mod3, wsg, wsd)


def _tile(n, pref):
    t = min(n, pref)
    assert n % t == 0, (n, t)
    return t


def _layer(x, mod3, positions, norm_mix, w_in, rwkv_mu, decay_w0, decay_up, iclr_a0, iclr_up,
           gate_up, rwkv_k_k, rwkv_k_a, rwkv_r_k, ln_x_w, ln_x_b, q_a_norm, w_q_b, kv_a_norm,
           w_kv_b, q_norm, k_norm, w_out, norm_ffn, w_router, router_bias, w_e_gate_up, w_e_down,
           w_sh_gate_up, w_sh_down):
    B, T, D = x.shape
    N = B * T
    assert T % SCAN_CHUNK == 0
    (r, lw, k2, v, kk, akk, g, bonus, q_pad, k_pad, v_pad) = _pre_call(
        x, mod3, positions, norm_mix, w_in, rwkv_mu, decay_w0, decay_up, iclr_a0, iclr_up,
        gate_up, rwkv_k_k, rwkv_k_a, rwkv_r_k, q_a_norm, w_q_b, kv_a_norm, w_kv_b, q_norm, k_norm,
        tm=_tile(T, 512))
    y = _scan_call(r, lw, k2, v, kk, akk)
    o_pad = _attn_call(q_pad, k_pad, v_pad)
    x1, h2, h2t = _post_call(y, bonus, g, o_pad, x, mod3, ln_x_w, ln_x_b, w_out, norm_ffn,
                             tm=_tile(T, 512))
    x1 = x1.reshape(N, D)
    h2 = h2.reshape(N, D)
    h2t = h2t.reshape(N, D // LANES, LANES)

    tr = _tile(N, 512)
    top_e, wts, rank, counts = _route_call(h2, w_router, router_bias, tr)
    counts = counts.reshape(N_EXPERTS)
    padded = (counts + MOE_ROWS - 1) // MOE_ROWS * MOE_ROWS
    pad_ends = jnp.cumsum(padded)
    pad_starts = pad_ends - padded
    n_blocks = (N * TOP_K + N_EXPERTS * (MOE_ROWS - 1)) // MOE_ROWS
    block_row0 = jnp.arange(n_blocks + 1, dtype=I32) * MOE_ROWS
    block_expert = jnp.minimum(
        jnp.sum((pad_ends[None, :] <= block_row0[:, None]).astype(I32), axis=1), N_EXPERTS - 1)
    n_used = (pad_ends[-1:] // MOE_ROWS).astype(I32)
    own = block_expert[:, None] == jnp.arange(N_EXPERTS, dtype=I32)[None, :]
    run_end = jnp.sum(jnp.where(own, (pad_starts + counts)[None, :], 0), axis=1)
    n_valid = jnp.clip(run_end - block_row0, 0, MOE_ROWS).astype(I32)
    dest = _dest_call(top_e, rank, pad_starts.astype(I32), tr)

    dest3 = dest.reshape(TOP_K, N // SC_CHUNK, SC_CHUNK).transpose(1, 0, 2)
    n_rows = n_blocks * MOE_ROWS
    xs, xtag = _sc_dispatch_call(h2t, dest3, n_rows)
    yt = _moe_call(block_expert, n_used, n_valid, xs.reshape(n_rows * SUBLANES, LANES), xtag,
                   w_e_gate_up, w_e_down, N * TOP_K + 2 * MOE_ROWS)
    tc = _tile(T, COMBINE_TILE)
    w3 = wts.reshape(TOP_K, N // tc, tc).transpose(1, 0, 2)
    out = _combine_call(w3, yt.reshape(-1, LANES), h2, x1, mod3, w_sh_gate_up,
                        w_sh_down, T, tc)
    return out.reshape(B, T, D)


def kernel(x, c, positions, ada_w, ada_b, norm_mix, w_in, rwkv_mu, decay_w0, decay_up, iclr_a0, iclr_up, gate_up, rwkv_k_k, rwkv_k_a, rwkv_r_k, ln_x_w, ln_x_b, q_a_norm, w_q_b, kv_a_norm, w_kv_b, q_norm, k_norm, w_out, norm_ffn, w_router, router_bias, w_e_gate_up, w_e_down, w_sh_gate_up, w_sh_down):
    B, T, D = x.shape
    depth = ada_w.shape[0]
    for l in range(depth):
        mod3 = _mod_call(c, ada_w[l], ada_b[l]).reshape(B, 6, D)
        x = _layer(x, mod3, positions, norm_mix[l], w_in[l], rwkv_mu[l], decay_w0[l], decay_up[l],
                   iclr_a0[l], iclr_up[l], gate_up[l], rwkv_k_k[l], rwkv_k_a[l], rwkv_r_k[l],
                   ln_x_w[l], ln_x_b[l], q_a_norm[l], w_q_b[l], kv_a_norm[l], w_kv_b[l],
                   q_norm[l], k_norm[l], w_out[l], norm_ffn[l], w_router[l], router_bias[l],
                   w_e_gate_up[l], w_e_down[l], w_sh_gate_up[l], w_sh_down[l])
    return x
```

```python
import functools
import math

import jax
import jax.numpy as jnp
import numpy as np
from jax import lax
from jax.experimental import pallas as pl
from jax.experimental.pallas import tpu as pltpu
from jax.experimental.pallas import tpu_sc as plsc

F32 = jnp.float32
BF16 = jnp.bfloat16
I32 = jnp.int32

NORM_EPS = 1e-6
GN_EPS = 64e-5
RWKV_HEADS = 8
RWKV_HEAD_DIM = 64
D_RWKV = 512
DECAY_LORA = 32
ICLR_LORA = 32
GATE_LORA = 96
MLA_HEADS = 8
QK_NOPE_DIM = 64
QK_ROPE_DIM = 32
QK_HEAD_DIM = 96
V_HEAD_DIM = 64
Q_LORA_RANK = 256
KV_LORA_RANK = 128
ROPE_THETA = 10000.0
N_EXPERTS = 256
TOP_K = 8
N_GROUPS = 8
TOPK_GROUPS = 4
GROUP_SIZE = N_EXPERTS // N_GROUPS
D_EXPERT = 256
ROUTED_SCALE = 2.5
MOE_ROWS = 512
SC_CHUNK = 64
COMBINE_TILE = 256

LANES = 128
SUBLANES = 8
HEAD_PAD = 128
VMEM_LIMIT = 56 * 1024 * 1024

PRE_SUBTILES = 1
SCAN_CHUNK = 64
SCAN_BLOCK = 512
ATTN_TILE = 256
ATTN_Q_SCALE = QK_HEAD_DIM ** -0.5 * math.log2(math.e)
NEG_INF = float("-inf")


def _cparams(*sem):
    return pltpu.CompilerParams(dimension_semantics=sem, vmem_limit_bytes=VMEM_LIMIT)


def _split2(a):
    hi = a.astype(BF16)
    lo = (a - hi.astype(F32)).astype(BF16)
    return hi, lo


def _split3(a):
    hi = a.astype(BF16)
    r1 = a - hi.astype(F32)
    mid = r1.astype(BF16)
    lo = (r1 - mid.astype(F32)).astype(BF16)
    return hi, mid, lo


def _dot(a, b, dims=None):
    if dims is None:
        return jnp.dot(a, b, preferred_element_type=F32)
    return lax.dot_general(a, b, (dims, ((), ())), preferred_element_type=F32)


def _mm(a, b, dims=None):
    return _dot(a.astype(BF16), b.astype(BF16), dims)


def _mm3(a, b, dims=None):
    ah, al = _split2(a)
    bh, bl = _split2(b)
    return _dot(ah, bh, dims) + (_dot(ah, bl, dims) + _dot(al, bh, dims))


def _mm_exact_rhs(a, b_exact_bf16, dims=None):
    h, m, l = _split3(a)
    return _dot(h, b_exact_bf16, dims) + (_dot(m, b_exact_bf16, dims) + _dot(l, b_exact_bf16, dims))


NT = ((1,), (1,))
TN = ((0,), (0,))


def _sigmoid(z):
    return 1.0 / (1.0 + jnp.exp(-z))


def _silu(z):
    return z * _sigmoid(z)


def _seg_ones(width, seg):
    r = lax.broadcasted_iota(I32, (width, width), 0) // seg
    c = lax.broadcasted_iota(I32, (width, width), 1) // seg
    return jnp.where(r == c, 1.0, 0.0).astype(BF16)


def _segsum(a, ones_bd):
    hi, lo = _split2(a)
    return _dot(hi, ones_bd) + _dot(lo, ones_bd)


def _mod_kernel(c_ref, w_ref, b_ref, o_ref):
    ca = _silu(c_ref[...])
    o_ref[...] = _mm3(ca, w_ref[...]) + b_ref[...]


def _mod_call(c, ada_w, ada_b):
    B, D = c.shape
    n6 = ada_w.shape[1]
    tn = D
    return pl.pallas_call(
        _mod_kernel,
        grid=(n6 // tn,),
        in_specs=[pl.BlockSpec((B, D), lambda j: (0, 0)),
                  pl.BlockSpec((D, tn), lambda j: (0, j)),
                  pl.BlockSpec((1, tn), lambda j: (0, j))],
        out_specs=pl.BlockSpec((B, tn), lambda j: (0, j)),
        out_shape=jax.ShapeDtypeStruct((B, n6), F32),
        compiler_params=_cparams("arbitrary"),
        name="mod",
    )(c, ada_w, ada_b.reshape(1, n6))


def _pre_kernel(x_ref, mod_ref, pos_ref, nmix_ref, wrkv_ref, wlora_ref, wmla_ref,
                mu_rkv_ref, mu_lora_ref, wup_ref, w0_ref, a0_ref, kk_ref, ka_ref, rk_ref,
                qan_ref, wqb_ref, kvan_ref, wkb_ref, wvb_ref, qn_ref, kn_ref, invf_ref,
                r_ref, lw_ref, k_ref, v_ref, kkn_ref, akk_ref, g_ref, bonus_ref,
                q_ref, kout_ref, vout_ref,
                carry_rkv, carry_lora):
    ti = pl.program_id(1)
    tm = x_ref.shape[1]

    @pl.when(ti == 0)
    def _():
        carry_rkv[...] = jnp.zeros_like(carry_rkv)
        carry_lora[...] = jnp.zeros_like(carry_lora)

    ts = tm // PRE_SUBTILES
    sh_a = mod_ref[0, 0:1, :]
    sc_a = mod_ref[0, 1:2, :]
    ones64 = _seg_ones(D_RWKV, RWKV_HEAD_DIM)
    row0 = lax.broadcasted_iota(I32, (ts, 1), 0) == 0
    half = QK_ROPE_DIM // 2
    last_rows = {}

    def sub_tile(s):
        rows = slice(s * ts, (s + 1) * ts)
        xb = x_ref[0, rows, :]
        ms = jnp.mean(xb * xb, axis=-1, keepdims=True)
        h = xb * lax.rsqrt(ms + NORM_EPS) * nmix_ref[...] * (1.0 + sc_a) + sh_a
        hb = h.astype(BF16)
        yield
        u_rkv = _dot(hb, wrkv_ref[...])
        u_lora = _dot(hb, wlora_ref[...])
        u_mla = _dot(hb, wmla_ref[...])
        last_rows[s] = (u_rkv[ts - 1:ts, :], u_lora[ts - 1:ts, :])
        yield

        before_rkv, before_lora = (carry_rkv[...], carry_lora[...]) if s == 0 else last_rows[s - 1]
        prev_rkv = jnp.where(row0, before_rkv, pltpu.roll(u_rkv, 1, 0))
        prev_lora = jnp.where(row0, before_lora, pltpu.roll(u_lora, 1, 0))
        if s == PRE_SUBTILES - 1:
            carry_rkv[...] = u_rkv[ts - 1:ts, :]
            carry_lora[...] = u_lora[ts - 1:ts, :]
        us = u_rkv + (prev_rkv - u_rkv) * mu_rkv_ref[...]
        ul = u_lora + (prev_lora - u_lora) * mu_lora_ref[...]
        r = us[:, 0:D_RWKV]
        k = us[:, D_RWKV:2 * D_RWKV]
        v = us[:, 2 * D_RWKV:3 * D_RWKV]
        lane_l = lax.broadcasted_iota(I32, ul.shape, 1)
        t_in = jnp.where(lane_l < DECAY_LORA, jnp.tanh(ul),
                         jnp.where(lane_l < DECAY_LORA + ICLR_LORA, ul, _sigmoid(ul)))
        yield
        up = _mm(t_in, wup_ref[...])
        yield
        z = w0_ref[...] + up[:, 0:D_RWKV]
        lw = (-math.exp(-0.5)) * _sigmoid(z)
        a = _sigmoid(a0_ref[...] + up[:, D_RWKV:2 * D_RWKV])
        g = up[:, 2 * D_RWKV:3 * D_RWKV]
        kk = k * kk_ref[...]
        k2 = k * (1.0 + (a - 1.0) * ka_ref[...])
        yield
        ss = _segsum(kk * kk, ones64)
        bonus_sum = _segsum(r * k2 * rk_ref[...], ones64)
        yield
        kk = kk * lax.rsqrt(jnp.maximum(ss, 1e-24))
        r_ref[0, rows, :] = r
        lw_ref[0, rows, :] = lw
        k_ref[0, rows, :] = k2
        v_ref[0, rows, :] = v
        kkn_ref[0, rows, :] = kk
        akk_ref[0, rows, :] = a * kk
        g_ref[0, rows, :] = g
        bonus_ref[0, rows, :] = bonus_sum * v
        yield

        q_lat = u_mla[:, 0:Q_LORA_RANK]
        kv_lat = u_mla[:, Q_LORA_RANK:Q_LORA_RANK + KV_LORA_RANK]
        kpe_tile = u_mla[:, Q_LORA_RANK + KV_LORA_RANK:]
        qn = q_lat * lax.rsqrt(jnp.mean(q_lat * q_lat, axis=-1, keepdims=True) + NORM_EPS) * qan_ref[...]
        kvn = kv_lat * lax.rsqrt(jnp.mean(kv_lat * kv_lat, axis=-1, keepdims=True) + NORM_EPS) * kvan_ref[...]
        kvb = kvn.astype(BF16)
        yield
        q_raw = _mm(qn, wqb_ref[...])
        k_raw = _dot(kvb, wkb_ref[...])
        v_pad = _dot(kvb, wvb_ref[...])
        yield
        kpe_h = pltpu.roll(kpe_tile, QK_NOPE_DIM, 1)
        cos_t, s1, s2 = rope_tables(s)

        def tables(gain, scale):
            g_s = gain * scale
            return (cos_t * g_s, s1 * pltpu.roll(g_s, HEAD_PAD - half, 1),
                    s2 * pltpu.roll(g_s, half, 1))

        def norm_rope(xh, tabs):
            c_g, s1_g, s2_g = tabs
            ssq = jnp.sum(xh * xh, axis=-1, keepdims=True) * (1.0 / QK_HEAD_DIM)
            rot = xh * c_g + pltpu.roll(xh, HEAD_PAD - half, 1) * s1_g + pltpu.roll(xh, half, 1) * s2_g
            return rot * lax.rsqrt(ssq + NORM_EPS)

        q_tabs = tables(qn_ref[...], ATTN_Q_SCALE)
        k_tabs = tables(kn_ref[...], 1.0)
        vout_ref[0, rows, :] = v_pad.astype(BF16)
        yield
        for hh in range(MLA_HEADS):
            sl = slice(hh * HEAD_PAD, (hh + 1) * HEAD_PAD)
            q_ref[0, rows, sl] = norm_rope(q_raw[:, sl], q_tabs).astype(BF16)
            kout_ref[0, rows, sl] = norm_rope(k_raw[:, sl] + kpe_h, k_tabs).astype(BF16)
            yield

    def rope_tables(s):
        ang_t = invf_ref[...] * pos_ref[0, :, s * ts:(s + 1) * ts].astype(F32)
        frow = lax.broadcasted_iota(I32, (half, HEAD_PAD), 0)
        flane = lax.broadcasted_iota(I32, (half, HEAD_PAD), 1)
        at_x1 = flane == frow + QK_NOPE_DIM
        at_x2 = flane == frow + QK_NOPE_DIM + half
        e_cos = jnp.where(at_x1 | at_x2, 1.0, 0.0).astype(BF16)
        e_sin = jnp.concatenate([jnp.where(at_x1, -1.0, 0.0), jnp.where(at_x2, 1.0, 0.0)],
                                axis=1).astype(BF16)
        lane = lax.broadcasted_iota(I32, (1, HEAD_PAD), 1)
        off_rope = jnp.where((lane >= QK_NOPE_DIM) & (lane < QK_HEAD_DIM), 0.0, 1.0)
        cos_t = _mm_exact_rhs(jnp.cos(ang_t), e_cos, TN) + off_rope
        sin2 = _mm_exact_rhs(jnp.sin(ang_t), e_sin, TN)
        return cos_t, sin2[:, :HEAD_PAD], sin2[:, HEAD_PAD:]

    live = [sub_tile(s) for s in range(PRE_SUBTILES)]
    while live:
        live = [gen for gen in live if next(gen, "done") != "done"]


def _pad_heads(w, n_heads, width):
    kdim = w.shape[0]
    w = w.reshape(kdim, n_heads, width)
    w = jnp.pad(w, ((0, 0), (0, 0), (0, HEAD_PAD - width)))
    return w.reshape(kdim, n_heads * HEAD_PAD)


def _pre_call(x, mod3, positions, norm_mix, w_in, rwkv_mu, decay_w0, decay_up, iclr_a0, iclr_up,
              gate_up, k_k, k_a, r_k, q_a_norm, w_q_b, kv_a_norm, w_kv_b, q_norm, k_norm, tm):
    B, T, D = x.shape
    n_rkv = 3 * D_RWKV
    n_lora = DECAY_LORA + ICLR_LORA + GATE_LORA
    LORA_PAD = 256
    MLA_PAD = 512
    n_mla = Q_LORA_RANK + KV_LORA_RANK + QK_ROPE_DIM
    w_rkv = w_in[:, :n_rkv].astype(BF16)
    w_lora = jnp.pad(w_in[:, n_rkv:n_rkv + n_lora], ((0, 0), (0, LORA_PAD - n_lora))).astype(BF16)
    w_mla = jnp.pad(w_in[:, n_rkv + n_lora:], ((0, 0), (0, MLA_PAD - n_mla))).astype(BF16)
    mu_rkv = rwkv_mu[:n_rkv].reshape(1, n_rkv)
    mu_lora = jnp.pad(rwkv_mu[n_rkv:], (0, LORA_PAD - n_lora)).reshape(1, LORA_PAD)
    w_up = jnp.zeros((LORA_PAD, n_rkv), F32)
    w_up = w_up.at[0:DECAY_LORA, 0:D_RWKV].set(decay_up)
    w_up = w_up.at[DECAY_LORA:DECAY_LORA + ICLR_LORA, D_RWKV:2 * D_RWKV].set(iclr_up)
    w_up = w_up.at[DECAY_LORA + ICLR_LORA:n_lora, 2 * D_RWKV:].set(gate_up)
    w_up = w_up.astype(BF16)
    w_qb = _pad_heads(w_q_b, MLA_HEADS, QK_HEAD_DIM).astype(BF16)
    w_kv3 = w_kv_b.reshape(KV_LORA_RANK, MLA_HEADS, QK_NOPE_DIM + V_HEAD_DIM)
    w_kb = _pad_heads(w_kv3[:, :, :QK_NOPE_DIM].reshape(KV_LORA_RANK, -1), MLA_HEADS, QK_NOPE_DIM).astype(BF16)
    w_vb = _pad_heads(w_kv3[:, :, QK_NOPE_DIM:].reshape(KV_LORA_RANK, -1), MLA_HEADS, V_HEAD_DIM).astype(BF16)
    qn_pad = jnp.pad(q_norm, (0, HEAD_PAD - QK_HEAD_DIM)).reshape(1, HEAD_PAD)
    kn_pad = jnp.pad(k_norm, (0, HEAD_PAD - QK_HEAD_DIM)).reshape(1, HEAD_PAD)
    inv_freq = ROPE_THETA ** (-jnp.arange(0, QK_ROPE_DIM, 2, dtype=F32) / QK_ROPE_DIM)
    invf = inv_freq.reshape(QK_ROPE_DIM // 2, 1)
    pos3 = positions.reshape(B, 1, T)
    HP = MLA_HEADS * HEAD_PAD

    row = lambda n: pl.BlockSpec((1, n), lambda b, t: (0, 0))
    full = lambda a: pl.BlockSpec(a.shape, lambda b, t: (0,) * a.ndim)
    tok = lambda n: pl.BlockSpec((1, tm, n), lambda b, t: (b, t, 0))
    outs = ([jax.ShapeDtypeStruct((B, T, D_RWKV), F32)] * 8
            + [jax.ShapeDtypeStruct((B, T, HP), BF16)] * 3)
    return pl.pallas_call(
        _pre_kernel,
        grid=(B, T // tm),
        in_specs=[tok(D),
                  pl.BlockSpec((1, 6, D), lambda b, t: (b, 0, 0)),
                  pl.BlockSpec((1, 1, tm), lambda b, t: (b, 0, t)),
                  row(D), full(w_rkv), full(w_lora), full(w_mla),
                  row(n_rkv), row(LORA_PAD), full(w_up), row(D_RWKV), row(D_RWKV),
                  row(D_RWKV), row(D_RWKV), row(D_RWKV),
                  row(Q_LORA_RANK), full(w_qb), row(KV_LORA_RANK), full(w_kb), full(w_vb),
                  row(HEAD_PAD), row(HEAD_PAD), full(invf)],
        out_specs=[tok(D_RWKV)] * 8 + [tok(HP)] * 3,
        out_shape=outs,
        scratch_shapes=[pltpu.VMEM((1, n_rkv), F32), pltpu.VMEM((1, LORA_PAD), F32)],
        compiler_params=_cparams("arbitrary", "arbitrary"),
        name="pre",
    )(x, mod3, pos3, norm_mix.reshape(1, D), w_rkv, w_lora, w_mla, mu_rkv, mu_lora, w_up,
      decay_w0.reshape(1, -1), iclr_a0.reshape(1, -1), k_k.reshape(1, -1), k_a.reshape(1, -1),
      r_k.reshape(1, -1), q_a_norm.reshape(1, -1), w_qb, kv_a_norm.reshape(1, -1), w_kb, w_vb,
      qn_pad, kn_pad, invf)


def _scan_kernel(r_ref, lw_ref, k_ref, v_ref, kk_ref, akk_ref, y_ref, state):
    C = SCAN_CHUNK
    n_chunks = r_ref.shape[1] // C
    n_pairs = r_ref.shape[2] // LANES

    @pl.when(pl.program_id(1) == 0)
    def _():
        state[...] = jnp.zeros_like(state)

    ri = lax.broadcasted_iota(I32, (C, C), 0)
    ci = lax.broadcasted_iota(I32, (C, C), 1)
    tri_incl = jnp.where(ci <= ri, 1.0, 0.0).astype(BF16)
    r2 = lax.broadcasted_iota(I32, (2 * C, 2 * C), 0)
    c2 = lax.broadcasted_iota(I32, (2 * C, 2 * C), 1)
    same = (r2 >= C) == (c2 >= C)
    strict = same & (c2 < r2)
    incl = same & (c2 <= r2)
    eye = jnp.where(c2 == r2, 1.0, 0.0)
    head0 = lax.broadcasted_iota(I32, (C, LANES), 1) < RWKV_HEAD_DIM

    def stack2(a):
        return jnp.concatenate([jnp.where(head0, a, 0.0), jnp.where(head0, 0.0, a)], axis=0)

    C2 = 2 * C
    cat0 = lambda *a: jnp.concatenate(a, axis=0)
    cat1 = lambda *a: jnp.concatenate(a, axis=1)

    items = []
    for c in range(n_chunks):
        rows = slice(c * C, (c + 1) * C)
        lw = lw_ref[0, rows, :]
        cum = _mm_exact_rhs_left(tri_incl, lw)
        cum_end = cum[C - 1:C, :]
        w_end = jnp.exp(cum_end)
        e_pos = jnp.exp(cum)
        e_neg = jnp.exp(-cum)
        e_prev = jnp.exp(cum - lw)
        e_end = jnp.exp(cum_end - cum)
        kk = kk_ref[0, rows, :]
        k2 = k_ref[0, rows, :]
        pneg = -akk_ref[0, rows, :]
        vv = v_ref[0, rows, :]
        rt = r_ref[0, rows, :] * e_pos
        bt = kk * e_prev
        pt = pneg * e_neg
        kt = k2 * e_neg
        ph = pneg * e_end
        kh = k2 * e_end
        for pp in range(n_pairs):
            sl = slice(pp * LANES, (pp + 1) * LANES)
            items.append(dict(
                c=c, p=pp, w_end=w_end[:, sl],
                bt2=stack2(bt[:, sl]).astype(BF16), rt2=stack2(rt[:, sl]).astype(BF16),
                pk2=cat0(stack2(pt[:, sl]), stack2(kt[:, sl])).astype(BF16),
                phkh2=cat0(stack2(ph[:, sl]), stack2(kh[:, sl])).astype(BF16),
                v2=stack2(vv[:, sl])))
    for it in items:
        ab = _dot(cat0(it['bt2'], it['rt2']), it['pk2'], NT)
        it['a_ab'] = jnp.where(strict, ab[:C2, :C2], 0.0)
        it['a_ak'] = jnp.where(strict, ab[:C2, C2:], 0.0).astype(BF16)
        it['b_rpk'] = cat1(jnp.where(incl, ab[C2:, :C2], 0.0), jnp.where(incl, ab[C2:, C2:], 0.0)).astype(BF16)
        it['tinv'] = eye + it['a_ab']
    for it in items:
        it['apow'] = _mm(it['a_ab'], it['a_ab'])
    for _ in range(int(math.log2(C)) - 1):
        for it in items:
            both = _mm(cat0(it['apow'], it['tinv']), it['apow'])
            it['apow'] = both[:C2]
            it['tinv'] = it['tinv'] + both[C2:]
    for it in items:
        it['akv'] = _dot(it['a_ak'], it['v2'].astype(BF16))
    for it in items:
        tt = _dot(it['tinv'].astype(BF16), cat1(it['bt2'], it['akv'].astype(BF16)))
        it['tb_rt'] = cat0(tt[:, :LANES].astype(BF16), it['rt2'])
        it['tav'] = tt[:, LANES:]
    for it in items:
        pp = it['p']
        s0 = state[pp]
        top = _dot(it['tb_rt'], s0.astype(BF16), NT)
        u2 = top[:C2] + it['tav']
        uv = cat0(u2, it['v2']).astype(BF16)
        y2 = top[C2:] + _dot(it['b_rpk'], uv)
        state[pp] = s0 * it['w_end'] + _dot(uv, it['phkh2'], TN)
        y_ref[0, it['c'] * C:(it['c'] + 1) * C, pp * LANES:(pp + 1) * LANES] = y2[0:C] + y2[C:C2]


def _mm_exact_rhs_left(b_exact_bf16, a):
    h, m, l = _split3(a)
    return _dot(b_exact_bf16, h) + (_dot(b_exact_bf16, m) + _dot(b_exact_bf16, l))


def _scan_call(r, lw, k2, v, kk, akk):
    B, T, W = r.shape
    tb = _tile(T, SCAN_BLOCK)
    spec = pl.BlockSpec((1, tb, W), lambda b, c: (b, c, 0))
    return pl.pallas_call(
        _scan_kernel,
        grid=(B, T // tb),
        in_specs=[spec] * 6,
        out_specs=spec,
        out_shape=jax.ShapeDtypeStruct((B, T, W), F32),
        scratch_shapes=[pltpu.VMEM((W // LANES, 2 * RWKV_HEAD_DIM, LANES), F32)],
        compiler_params=_cparams("arbitrary", "arbitrary"),
        name="scan",
    )(r, lw, k2, v, kk, akk)


def _attn_kernel(q_ref, k_ref, v_ref, o_ref):
    T = q_ref.shape[1]
    tq = min(T, ATTN_TILE)
    row = lax.broadcasted_iota(I32, (tq, tq), 0)
    col = lax.broadcasted_iota(I32, (tq, tq), 1)
    causal = col <= row

    def update(q, kt, vt, carry, mask):
        m_old, l_old, acc = carry
        s = _dot(q, kt, NT)
        if mask:
            s = jnp.where(causal, s, NEG_INF)
        m_new = jnp.maximum(m_old, jnp.max(s, axis=-1, keepdims=True))
        alpha = jnp.exp2(m_old - m_new)
        p = jnp.exp2(s - m_new)
        l_new = alpha * l_old + jnp.sum(p, axis=-1, keepdims=True)
        acc = alpha * acc + _dot(p.astype(BF16), vt)
        return m_new, l_new, acc

    for qi in range(T // tq):
        q = q_ref[0, qi * tq:(qi + 1) * tq, :]
        carry = (jnp.full((tq, 1), NEG_INF, F32), jnp.zeros((tq, 1), F32),
                 jnp.zeros((tq, HEAD_PAD), F32))

        def body(ki, carry, q=q):
            rows = pl.ds(pl.multiple_of(ki * tq, tq), tq)
            return update(q, k_ref[0, rows, :], v_ref[0, rows, :], carry, False)

        carry = lax.fori_loop(0, qi, body, carry, unroll=True)
        diag = slice(qi * tq, (qi + 1) * tq)
        _, l_fin, acc = update(q, k_ref[0, diag, :], v_ref[0, diag, :], carry, True)
        o_ref[0, diag, :] = (acc / l_fin).astype(o_ref.dtype)


def _attn_call(q, k, v):
    B, T, HP = q.shape
    spec = pl.BlockSpec((1, T, HEAD_PAD), lambda b, h: (b, 0, h))
    return pl.pallas_call(
        _attn_kernel,
        grid=(B, MLA_HEADS),
        in_specs=[spec, spec, spec],
        out_specs=spec,
        out_shape=jax.ShapeDtypeStruct((B, T, HP), BF16),
        compiler_params=_cparams("arbitrary", "arbitrary"),
        name="attn",
    )(q, k, v)


def _post_kernel(y_ref, bonus_ref, g_ref, o_ref, x_ref, mod_ref, lnw_ref, lnb_ref,
                 wo_r_ref, wo_m_ref, nffn_ref, x1_ref, h2_ref, h2t_ref):
    y = y_ref[0]
    ones64 = _seg_ones(D_RWKV, RWKV_HEAD_DIM)
    mean = _segsum(y, ones64) * (1.0 / RWKV_HEAD_DIM)
    yc = y - mean
    var = _segsum(yc * yc, ones64) * (1.0 / RWKV_HEAD_DIM)
    yn = yc * lax.rsqrt(var + GN_EPS) * lnw_ref[...] + lnb_ref[...]
    yr = (yn + bonus_ref[0]) * g_ref[0]
    mix = _mm(yr, wo_r_ref[...]) + _dot(o_ref[0], wo_m_ref[...])
    g_a = mod_ref[0, 2:3, :]
    sh_f = mod_ref[0, 3:4, :]
    sc_f = mod_ref[0, 4:5, :]
    x1 = x_ref[0] + g_a * mix
    x1_ref[0] = x1
    ms = jnp.mean(x1 * x1, axis=-1, keepdims=True)
    h2 = x1 * lax.rsqrt(ms + NORM_EPS) * nffn_ref[...] * (1.0 + sc_f) + sh_f
    h2_ref[0] = h2
    tm = h2.shape[0]
    for s in range(SUBLANES):
        h2t_ref[0, pl.ds(s, tm, stride=SUBLANES), :] = h2[:, s * LANES:(s + 1) * LANES]


def _post_call(y, bonus, g, o_pad, x, mod3, ln_w, ln_b, w_out, norm_ffn, tm):
    B, T, D = x.shape
    HP = MLA_HEADS * HEAD_PAD
    wo_r = w_out[:D_RWKV].astype(BF16)
    wo_m = jnp.pad(w_out[D_RWKV:].reshape(MLA_HEADS, V_HEAD_DIM, D),
                   ((0, 0), (0, HEAD_PAD - V_HEAD_DIM), (0, 0))).reshape(HP, D).astype(BF16)
    tok = lambda n: pl.BlockSpec((1, tm, n), lambda b, t: (b, t, 0))
    row = lambda n: pl.BlockSpec((1, n), lambda b, t: (0, 0))
    full = lambda a: pl.BlockSpec(a.shape, lambda b, t: (0,) * a.ndim)
    return pl.pallas_call(
        _post_kernel,
        grid=(B, T // tm),
        in_specs=[tok(D_RWKV), tok(D_RWKV), tok(D_RWKV), tok(HP), tok(D),
                  pl.BlockSpec((1, 6, D), lambda b, t: (b, 0, 0)),
                  row(D_RWKV), row(D_RWKV), full(wo_r), full(wo_m), row(D)],
        out_specs=[tok(D), tok(D),
                   pl.BlockSpec((1, tm * SUBLANES, LANES), lambda b, t: (b, t, 0))],
        out_shape=[jax.ShapeDtypeStruct((B, T, D), F32)] * 2
        + [jax.ShapeDtypeStruct((B, T * SUBLANES, LANES), F32)],
        compiler_params=_cparams("arbitrary", "arbitrary"),
        name="post",
    )(y, bonus, g, o_pad, x, mod3, ln_w.reshape(1, -1), ln_b.reshape(1, -1), wo_r, wo_m,
      norm_ffn.reshape(1, D))


def _first_index(mask, iota, size, axis):
    return jnp.min(jnp.where(mask, iota, size), axis=axis, keepdims=True)


def _route_kernel(h_ref, wr_ref, bias_ref, e_ref, w_ref, rank_ref, cnt_ref, base):
    tr = h_ref.shape[0]
    E = N_EXPERTS

    @pl.when(pl.program_id(0) == 0)
    def _():
        base[...] = jnp.zeros_like(base)

    logits = _mm3(wr_ref[...], h_ref[...], NT)
    scores = _sigmoid(logits)
    sel = scores + bias_ref[...]
    iota_g = lax.broadcasted_iota(I32, (GROUP_SIZE, tr), 0)
    gs_rows = []
    for gi in range(N_GROUPS):
        blk = sel[gi * GROUP_SIZE:(gi + 1) * GROUP_SIZE, :]
        m1 = jnp.max(blk, axis=0, keepdims=True)
        i1 = _first_index(blk == m1, iota_g, GROUP_SIZE, 0)
        m2 = jnp.max(jnp.where(iota_g == i1, NEG_INF, blk), axis=0, keepdims=True)
        gs_rows.append(m1 + m2)
    gs = jnp.concatenate(gs_rows, axis=0)
    iota8 = lax.broadcasted_iota(I32, (N_GROUPS, tr), 0)
    gmask = jnp.zeros((N_GROUPS, tr), jnp.bool_)
    for _ in range(TOPK_GROUPS):
        mg = jnp.max(gs, axis=0, keepdims=True)
        ig = _first_index(gs == mg, iota8, N_GROUPS, 0)
        hit = iota8 == ig
        gmask = gmask | hit
        gs = jnp.where(hit, NEG_INF, gs)
    msel = jnp.concatenate(
        [jnp.where(gmask[gi:gi + 1, :], sel[gi * GROUP_SIZE:(gi + 1) * GROUP_SIZE, :], NEG_INF)
         for gi in range(N_GROUPS)], axis=0)
    iota_e = lax.broadcasted_iota(I32, (E, tr), 0)
    e_rows, w_rows = [], []
    onehot = jnp.zeros((E, tr), F32)
    for _ in range(TOP_K):
        mv = jnp.max(msel, axis=0, keepdims=True)
        ie = _first_index(msel == mv, iota_e, E, 0)
        hit = iota_e == ie
        e_rows.append(ie)
        w_rows.append(jnp.sum(jnp.where(hit, scores, 0.0), axis=0, keepdims=True))
        onehot = jnp.where(hit, 1.0, onehot)
        msel = jnp.where(hit, NEG_INF, msel)
    top_e = jnp.concatenate(e_rows, axis=0)
    wts = jnp.concatenate(w_rows, axis=0)
    wts = wts / jnp.sum(wts, axis=0, keepdims=True) * ROUTED_SCALE
    ti = lax.broadcasted_iota(I32, (tr, tr), 0)
    tj = lax.broadcasted_iota(I32, (tr, tr), 1)
    upper = jnp.where(ti < tj, 1.0, 0.0).astype(BF16)
    pos = _dot(onehot.astype(BF16), upper) + base[...]
    rank_rows = [jnp.sum(jnp.where(iota_e == e_rows[j], pos, 0.0), axis=0, keepdims=True)
                 for j in range(TOP_K)]
    base[...] = base[...] + jnp.sum(onehot, axis=1, keepdims=True)
    e_ref[...] = top_e
    w_ref[...] = wts
    rank_ref[...] = jnp.concatenate(rank_rows, axis=0).astype(I32)
    cnt_ref[...] = base[...].astype(I32)


def _route_call(h2, w_router, router_bias, tr):
    N, D = h2.shape
    E = N_EXPERTS
    out_kn = pl.BlockSpec((TOP_K, tr), lambda i: (0, i))
    return pl.pallas_call(
        _route_kernel,
        grid=(N // tr,),
        in_specs=[pl.BlockSpec((tr, D), lambda i: (i, 0)),
                  pl.BlockSpec((E, D), lambda i: (0, 0)),
                  pl.BlockSpec((E, 1), lambda i: (0, 0))],
        out_specs=[out_kn, out_kn, out_kn, pl.BlockSpec((E, 1), lambda i: (0, 0))],
        out_shape=[jax.ShapeDtypeStruct((TOP_K, N), I32), jax.ShapeDtypeStruct((TOP_K, N), F32),
                   jax.ShapeDtypeStruct((TOP_K, N), I32), jax.ShapeDtypeStruct((E, 1), I32)],
        scratch_shapes=[pltpu.VMEM((E, 1), F32)],
        compiler_params=_cparams("arbitrary"),
        name="route",
    )(h2, w_router.T, router_bias.reshape(E, 1))


def _dest_kernel(e_ref, rank_ref, start_ref, d_ref):
    tr = e_ref.shape[1]
    iota_e = lax.broadcasted_iota(I32, (N_EXPERTS, tr), 0)
    starts = start_ref[...]
    rows = [jnp.sum(jnp.where(iota_e == e_ref[j:j + 1, :], starts, 0), axis=0, keepdims=True)
            for j in range(TOP_K)]
    d_ref[...] = jnp.concatenate(rows, axis=0) + rank_ref[...]


def _dest_call(top_e, rank, pad_starts, tr):
    K, N = top_e.shape
    spec = pl.BlockSpec((K, tr), lambda i: (0, i))
    return pl.pallas_call(
        _dest_kernel,
        grid=(N // tr,),
        in_specs=[spec, spec, pl.BlockSpec((N_EXPERTS, 1), lambda i: (0, 0))],
        out_specs=spec,
        out_shape=jax.ShapeDtypeStruct((K, N), I32),
        compiler_params=_cparams("arbitrary"),
        name="dest",
    )(top_e, rank, pad_starts.reshape(N_EXPERTS, 1))


def _sc_dispatch_call(h2t, dest3, n_rows):
    N = h2t.shape[0]
    info = plsc.get_sparse_core_info()
    n_workers = info.num_cores * info.num_subcores
    n_chunks = N // (SC_CHUNK * n_workers)
    assert n_chunks * SC_CHUNK * n_workers == N
    mesh = plsc.VectorSubcoreMesh(core_axis_name="c", subcore_axis_name="s")

    @functools.partial(
        pl.kernel, mesh=mesh,
        out_type=(jax.ShapeDtypeStruct((n_rows,) + h2t.shape[1:], h2t.dtype),
                  jax.ShapeDtypeStruct((n_rows, LANES), I32)),
        scratch_types=[pltpu.VMEM((TOP_K, SC_CHUNK), I32),
                       pltpu.VMEM((SC_CHUNK,) + h2t.shape[1:], h2t.dtype),
                       pltpu.VMEM((SC_CHUNK, LANES), I32)],
        name="sc_dispatch",
    )
    def scatter_rows(h_hbm, dest_hbm, xs_hbm, tag_hbm, idx_v, rows_v, tag_v):
        wid = lax.axis_index("s") * info.num_cores + lax.axis_index("c")
        zeros = jnp.zeros((info.num_lanes,), I32)

        @pl.loop(0, SC_CHUNK)
        def _(r):
            for l0 in range(0, LANES, info.num_lanes):
                tag_v[r, pl.ds(l0, info.num_lanes)] = zeros

        @pl.loop(0, n_chunks)
        def _(c):
            chunk = wid * n_chunks + c
            base = chunk * SC_CHUNK
            pltpu.sync_copy(dest_hbm.at[chunk], idx_v)
            pltpu.sync_copy(h_hbm.at[pl.ds(base, SC_CHUNK)], rows_v)
            for j in range(TOP_K):
                pltpu.sync_copy(rows_v, xs_hbm.at[idx_v.at[j]])

                @pl.loop(0, SC_CHUNK)
                def _(r):
                    tag_v[r, pl.ds(0, info.num_lanes)] = zeros + ((base + r) * TOP_K + j)

                pltpu.sync_copy(tag_v, tag_hbm.at[idx_v.at[j]])

    return scatter_rows(h2t, dest3)


def _moe_kernel(be_ref, nu_ref, nv_ref, xs_ref, xtag_ref, wgu_ref, wdn_ref, yt_hbm, wgu_bf, wdn_bf,
                ybuf, tag_v, tag_s, n_sent, sem_tag, sem_rows):
    i = pl.program_id(0)
    n_steps = pl.num_programs(0)
    cur = i % 2
    n_real = yt_hbm.shape[0] - 2 * MOE_ROWS

    def row_copy(buf, r, tag):
        rows = pl.ds(pl.multiple_of(r * SUBLANES, SUBLANES), SUBLANES)
        return pltpu.make_async_copy(ybuf.at[buf, rows, :], yt_hbm.at[tag], sem_rows.at[buf])

    n_used = nu_ref[0]
    prv = 1 - cur

    n_prev = nv_ref[jnp.maximum(i - 1, 0)]
    full_prev = n_prev == MOE_ROWS

    def wait_sent(buf):
        @pl.when(n_sent[buf] == MOE_ROWS)
        def _():
            pltpu.make_async_copy(ybuf.at[buf], ybuf.at[buf], sem_rows.at[buf]).wait()

        @pl.when(n_sent[buf] < MOE_ROWS)
        def _():
            def one(k, carry):
                row_copy(buf, 0, 0).wait()
                return carry
            lax.fori_loop(0, n_sent[buf], one, 0)

    def send_prev():
        for r in range(MOE_ROWS):
            row_copy(prv, r, tag_s[prv, r]).start()
        n_sent[prv] = MOE_ROWS

    def send_prev_partial():
        def one(r, carry):
            row_copy(prv, r, tag_s[prv, r]).start()
            return carry
        lax.fori_loop(0, n_prev, one, 0)
        n_sent[prv] = n_prev

    @pl.when((i >= 2) & (i - 2 < n_used))
    def _():
        wait_sent(cur)

    i_blk = jnp.minimum(i, n_steps - 2)
    @pl.when((i < n_used) & ((i == 0) | (be_ref[i_blk] != be_ref[jnp.maximum(i_blk - 1, 0)])))
    def _():
        wgu_bf[...] = wgu_ref[0].astype(BF16)
        wdn_bf[...] = wdn_ref[0].astype(BF16)

    def compute():
        row = lax.broadcasted_iota(I32, (MOE_ROWS, LANES), 0)
        tags = jnp.where(row < nv_ref[i_blk], xtag_ref[...], n_real + cur * MOE_ROWS + row)
        tag_v[...] = tags.astype(F32).T[0:1, :].astype(I32)
        cp = pltpu.make_async_copy(tag_v, tag_s.at[pl.ds(cur, 1)], sem_tag)
        cp.start()
        xb = jnp.concatenate([xs_ref[pl.ds(s, MOE_ROWS, stride=SUBLANES), :]
                              for s in range(SUBLANES)], axis=1).astype(BF16)
        gu = _dot(xb, wgu_bf[...])
        act = _silu(gu[:, :D_EXPERT]) * gu[:, D_EXPERT:]
        y = _dot(act.astype(BF16), wdn_bf[...])
        for s in range(SUBLANES):
            ybuf[cur, pl.ds(s, MOE_ROWS, stride=SUBLANES), :] = y[:, s * LANES:(s + 1) * LANES]
        cp.wait()

    @pl.when((i == 0) & (i < n_used))
    def _():
        compute()

    @pl.when((i > 0) & (i < n_used) & full_prev)
    def _():
        send_prev()
        compute()

    @pl.when((i > 0) & (i <= n_used) & jnp.logical_not(full_prev))
    def _():
        send_prev_partial()

    @pl.when((i > 0) & (i < n_used) & jnp.logical_not(full_prev))
    def _():
        compute()

    @pl.when((i > 0) & (i == n_used) & full_prev)
    def _():
        send_prev()

    @pl.when((i == n_steps - 1) & (i - 1 < n_used))
    def _():
        wait_sent(prv)


def _moe_call(block_expert, n_used, n_valid, xs, xtag, w_gu, w_dn, n_out_rows):
    P = xs.shape[0] // SUBLANES
    D = SUBLANES * LANES
    nb = P // MOE_ROWS
    blk = lambda i, be, nu, nv: (jnp.minimum(i, nu[0] - 1), 0)
    wblk = lambda i, be, nu, nv: (be[jnp.minimum(i, nu[0] - 1)], 0, 0)
    return pl.pallas_call(
        _moe_kernel,
        grid_spec=pltpu.PrefetchScalarGridSpec(
            num_scalar_prefetch=3,
            grid=(nb + 1,),
            in_specs=[pl.BlockSpec((MOE_ROWS * SUBLANES, LANES), blk),
                      pl.BlockSpec((MOE_ROWS, LANES), blk),
                      pl.BlockSpec((1, D, 2 * D_EXPERT), wblk),
                      pl.BlockSpec((1, D_EXPERT, D), wblk)],
            out_specs=pl.BlockSpec(memory_space=pl.ANY),
            scratch_shapes=[pltpu.VMEM((D, 2 * D_EXPERT), BF16), pltpu.VMEM((D_EXPERT, D), BF16),
                            pltpu.VMEM((2, MOE_ROWS * SUBLANES, LANES), F32),
                            pltpu.VMEM((1, MOE_ROWS), I32), pltpu.SMEM((2, MOE_ROWS), I32),
                            pltpu.SMEM((2,), I32),
                            pltpu.SemaphoreType.DMA, pltpu.SemaphoreType.DMA((2,))]),
        out_shape=jax.ShapeDtypeStruct((n_out_rows, SUBLANES, LANES), F32),
        compiler_params=_cparams("arbitrary"),
        name="moe",
    )(block_expert, n_used, n_valid, xs, xtag, w_gu, w_dn)


def _combine_kernel(w_hbm, yt_ref, h_ref, x1_ref, mod_ref, wsg_ref, wsd_ref, o_ref,
                    wts, routed, sem_w):
    i = pl.program_id(0)
    tc = h_ref.shape[0]
    cp_w = pltpu.make_async_copy(w_hbm.at[i], wts, sem_w)
    cp_w.start()
    gu = _mm(h_ref[...], wsg_ref[...])
    act = _silu(gu[:, :D_EXPERT]) * gu[:, D_EXPERT:]
    ffn = _mm(act, wsd_ref[...])
    cp_w.wait()

    def wsum(tt, carry):
        for u in range(SUBLANES):
            t = tt * SUBLANES + u
            first = pl.multiple_of(t * (TOP_K * SUBLANES), SUBLANES)
            acc = yt_ref[pl.ds(first, SUBLANES), :] * wts[0, t]
            for j in range(1, TOP_K):
                acc = acc + yt_ref[pl.ds(first + j * SUBLANES, SUBLANES), :] * wts[j, t]
            routed[pl.ds(pl.multiple_of(t * SUBLANES, SUBLANES), SUBLANES), :] = acc
        return carry

    lax.fori_loop(0, tc // SUBLANES, wsum, 0)
    routed2d = jnp.concatenate([routed[pl.ds(s, tc, stride=SUBLANES), :] for s in range(SUBLANES)],
                               axis=1)
    g_f = mod_ref[0, 5:6, :]
    o_ref[...] = x1_ref[...] + g_f * (ffn + routed2d)


def _combine_call(w3, yt, h2, x1, mod3, w_sh_gu, w_sh_dn, tokens_per_batch, tc):
    N, D = h2.shape
    tiles_per_batch = tokens_per_batch // tc
    tok = pl.BlockSpec((tc, D), lambda i: (i, 0))
    wsg = w_sh_gu.astype(BF16)
    wsd = w_sh_dn.astype(BF16)
    return pl.pallas_call(
        _combine_kernel,
        grid=(N // tc,),
        in_specs=[pl.BlockSpec(memory_space=pl.ANY),
                  pl.BlockSpec((tc * TOP_K * SUBLANES, LANES), lambda i: (i, 0)),
                  tok, tok,
                  pl.BlockSpec((1, 6, D), lambda i: (i // tiles_per_batch, 0, 0)),
                  pl.BlockSpec(wsg.shape, lambda i: (0, 0)),
                  pl.BlockSpec(wsd.shape, lambda i: (0, 0))],
        out_specs=tok,
        out_shape=jax.ShapeDtypeStruct((N, D), F32),
        scratch_shapes=[pltpu.SMEM((TOP_K, tc), F32), pltpu.VMEM((tc * SUBLANES, LANES), F32),
                        pltpu.SemaphoreType.DMA],
        compiler_params=_cparams("arbitrary"),
        name="combine",
    )(w3, yt, h2, x1, mod3, wsg, wsd)


def _tile(n, pref):
    t = min(n, pref)
    assert n % t == 0, (n, t)
    return t


def _layer(x, mod3, positions, norm_mix, w_in, rwkv_mu, decay_w0, decay_up, iclr_a0, iclr_up,
           gate_up, rwkv_k_k, rwkv_k_a, rwkv_r_k, ln_x_w, ln_x_b, q_a_norm, w_q_b, kv_a_norm,
           w_kv_b, q_norm, k_norm, w_out, norm_ffn, w_router, router_bias, w_e_gate_up, w_e_down,
           w_sh_gate_up, w_sh_down):
    B, T, D = x.shape
    N = B * T
    assert T % SCAN_CHUNK == 0
    (r, lw, k2, v, kk, akk, g, bonus, q_pad, k_pad, v_pad) = _pre_call(
        x, mod3, positions, norm_mix, w_in, rwkv_mu, decay_w0, decay_up, iclr_a0, iclr_up,
        gate_up, rwkv_k_k, rwkv_k_a, rwkv_r_k, q_a_norm, w_q_b, kv_a_norm, w_kv_b, q_norm, k_norm,
        tm=_tile(T, 512))
    y = _scan_call(r, lw, k2, v, kk, akk)
    o_pad = _attn_call(q_pad, k_pad, v_pad)
    x1, h2, h2t = _post_call(y, bonus, g, o_pad, x, mod3, ln_x_w, ln_x_b, w_out, norm_ffn,
                             tm=_tile(T, 512))
    x1 = x1.reshape(N, D)
    h2 = h2.reshape(N, D)
    h2t = h2t.reshape(N, D // LANES, LANES)

    tr = _tile(N, 512)
    top_e, wts, rank, counts = _route_call(h2, w_router, router_bias, tr)
    counts = counts.reshape(N_EXPERTS)
    padded = (counts + MOE_ROWS - 1) // MOE_ROWS * MOE_ROWS
    pad_ends = jnp.cumsum(padded)
    pad_starts = pad_ends - padded
    n_blocks = (N * TOP_K + N_EXPERTS * (MOE_ROWS - 1)) // MOE_ROWS
    block_row0 = jnp.arange(n_blocks + 1, dtype=I32) * MOE_ROWS
    block_expert = jnp.minimum(
        jnp.sum((pad_ends[None, :] <= block_row0[:, None]).astype(I32), axis=1), N_EXPERTS - 1)
    n_used = (pad_ends[-1:] // MOE_ROWS).astype(I32)
    own = block_expert[:, None] == jnp.arange(N_EXPERTS, dtype=I32)[None, :]
    run_end = jnp.sum(jnp.where(own, (pad_starts + counts)[None, :], 0), axis=1)
    n_valid = jnp.clip(run_end - block_row0, 0, MOE_ROWS).astype(I32)
    dest = _dest_call(top_e, rank, pad_starts.astype(I32), tr)

    dest3 = dest.reshape(TOP_K, N // SC_CHUNK, SC_CHUNK).transpose(1, 0, 2)
    n_rows = n_blocks * MOE_ROWS
    xs, xtag = _sc_dispatch_call(h2t, dest3, n_rows)
    yt = _moe_call(block_expert, n_used, n_valid, xs.reshape(n_rows * SUBLANES, LANES), xtag,
                   w_e_gate_up, w_e_down, N * TOP_K + 2 * MOE_ROWS)
    tc = _tile(T, COMBINE_TILE)
    w3 = wts.reshape(TOP_K, N // tc, tc).transpose(1, 0, 2)
    out = _combine_call(w3, yt.reshape(-1, LANES), h2, x1, mod3, w_sh_gate_up,
                        w_sh_down, T, tc)
    return out.reshape(B, T, D)


def kernel(x, c, positions, ada_w, ada_b, norm_mix, w_in, rwkv_mu, decay_w0, decay_up, iclr_a0, iclr_up, gate_up, rwkv_k_k, rwkv_k_a, rwkv_r_k, ln_x_w, ln_x_b, q_a_norm, w_q_b, kv_a_norm, w_kv_b, q_norm, k_norm, w_out, norm_ffn, w_router, router_bias, w_e_gate_up, w_e_down, w_sh_gate_up, w_sh_down):
    B, T, D = x.shape
    depth = ada_w.shape[0]
    for l in range(depth):
        mod3 = _mod_call(c, ada_w[l], ada_b[l]).reshape(B, 6, D)
        x = _layer(x, mod3, positions, norm_mix[l], w_in[l], rwkv_mu[l], decay_w0[l], decay_up[l],
                   iclr_a0[l], iclr_up[l], gate_up[l], rwkv_k_k[l], rwkv_k_a[l], rwkv_r_k[l],
                   ln_x_w[l], ln_x_b[l], q_a_norm[l], w_q_b[l], kv_a_norm[l], w_kv_b[l],
                   q_norm[l], k_norm[l], w_out[l], norm_ffn[l], w_router[l], router_bias[l],
                   w_e_gate_up[l], w_e_down[l], w_sh_gate_up[l], w_sh_down[l])
    return x
```

```python
import functools
import math

import jax
import jax.numpy as jnp
import numpy as np
from jax import lax
from jax.experimental import pallas as pl
from jax.experimental.pallas import tpu as pltpu
from jax.experimental.pallas import tpu_sc as plsc

F32 = jnp.float32
BF16 = jnp.bfloat16
I32 = jnp.int32

NORM_EPS = 1e-6
GN_EPS = 64e-5
RWKV_HEADS = 8
RWKV_HEAD_DIM = 64
D_RWKV = 512
DECAY_LORA = 32
ICLR_LORA = 32
GATE_LORA = 96
MLA_HEADS = 8
QK_NOPE_DIM = 64
QK_ROPE_DIM = 32
QK_HEAD_DIM = 96
V_HEAD_DIM = 64
Q_LORA_RANK = 256
KV_LORA_RANK = 128
ROPE_THETA = 10000.0
N_EXPERTS = 256
TOP_K = 8
N_GROUPS = 8
TOPK_GROUPS = 4
GROUP_SIZE = N_EXPERTS // N_GROUPS
D_EXPERT = 256
ROUTED_SCALE = 2.5
MOE_ROWS = 512
SC_CHUNK = 64
COMBINE_TILE = 256

LANES = 128
SUBLANES = 8
HEAD_PAD = 128
VMEM_LIMIT = 56 * 1024 * 1024

PRE_SUBTILES = 1
SCAN_CHUNK = 64
SCAN_BLOCK = 512
ATTN_TILE = 256
ATTN_Q_SCALE = QK_HEAD_DIM ** -0.5 * math.log2(math.e)
NEG_INF = float("-inf")


def _cparams(*sem):
    return pltpu.CompilerParams(dimension_semantics=sem, vmem_limit_bytes=VMEM_LIMIT)


def _split2(a):
    hi = a.astype(BF16)
    lo = (a - hi.astype(F32)).astype(BF16)
    return hi, lo


def _split3(a):
    hi = a.astype(BF16)
    r1 = a - hi.astype(F32)
    mid = r1.astype(BF16)
    lo = (r1 - mid.astype(F32)).astype(BF16)
    return hi, mid, lo


def _dot(a, b, dims=None):
    if dims is None:
        return jnp.dot(a, b, preferred_element_type=F32)
    return lax.dot_general(a, b, (dims, ((), ())), preferred_element_type=F32)


def _mm(a, b, dims=None):
    return _dot(a.astype(BF16), b.astype(BF16), dims)


def _mm3(a, b, dims=None):
    ah, al = _split2(a)
    bh, bl = _split2(b)
    return _dot(ah, bh, dims) + (_dot(ah, bl, dims) + _dot(al, bh, dims))


def _mm_exact_rhs(a, b_exact_bf16, dims=None):
    h, m, l = _split3(a)
    return _dot(h, b_exact_bf16, dims) + (_dot(m, b_exact_bf16, dims) + _dot(l, b_exact_bf16, dims))


NT = ((1,), (1,))
TN = ((0,), (0,))


def _sigmoid(z):
    return 1.0 / (1.0 + jnp.exp(-z))


def _silu(z):
    return z * _sigmoid(z)


def _seg_ones(width, seg):
    r = lax.broadcasted_iota(I32, (width, width), 0) // seg
    c = lax.broadcasted_iota(I32, (width, width), 1) // seg
    return jnp.where(r == c, 1.0, 0.0).astype(BF16)


def _segsum(a, ones_bd):
    hi, lo = _split2(a)
    return _dot(hi, ones_bd) + _dot(lo, ones_bd)


def _mod_kernel(c_ref, w_ref, b_ref, o_ref):
    ca = _silu(c_ref[...])
    o_ref[...] = _mm3(ca, w_ref[...]) + b_ref[...]


def _mod_call(c, ada_w, ada_b):
    B, D = c.shape
    n6 = ada_w.shape[1]
    tn = D
    return pl.pallas_call(
        _mod_kernel,
        grid=(n6 // tn,),
        in_specs=[pl.BlockSpec((B, D), lambda j: (0, 0)),
                  pl.BlockSpec((D, tn), lambda j: (0, j)),
                  pl.BlockSpec((1, tn), lambda j: (0, j))],
        out_specs=pl.BlockSpec((B, tn), lambda j: (0, j)),
        out_shape=jax.ShapeDtypeStruct((B, n6), F32),
        compiler_params=_cparams("arbitrary"),
        name="mod",
    )(c, ada_w, ada_b.reshape(1, n6))


def _pre_kernel(x_ref, mod_ref, pos_ref, nmix_ref, wrkv_ref, wlora_ref, wmla_ref,
                mu_rkv_ref, mu_lora_ref, wup_ref, w0_ref, a0_ref, kk_ref, ka_ref, rk_ref,
                qan_ref, wqb_ref, kvan_ref, wkb_ref, wvb_ref, qn_ref, kn_ref, invf_ref,
                r_ref, lw_ref, k_ref, v_ref, kkn_ref, akk_ref, g_ref, bonus_ref,
                q_ref, kout_ref, vout_ref,
                carry_rkv, carry_lora):
    ti = pl.program_id(1)
    tm = x_ref.shape[1]

    @pl.when(ti == 0)
    def _():
        carry_rkv[...] = jnp.zeros_like(carry_rkv)
        carry_lora[...] = jnp.zeros_like(carry_lora)

    ts = tm // PRE_SUBTILES
    sh_a = mod_ref[0, 0:1, :]
    sc_a = mod_ref[0, 1:2, :]
    ones64 = _seg_ones(D_RWKV, RWKV_HEAD_DIM)
    row0 = lax.broadcasted_iota(I32, (ts, 1), 0) == 0
    half = QK_ROPE_DIM // 2
    last_rows = {}

    def sub_tile(s):
        rows = slice(s * ts, (s + 1) * ts)
        xb = x_ref[0, rows, :]
        ms = jnp.mean(xb * xb, axis=-1, keepdims=True)
        h = xb * lax.rsqrt(ms + NORM_EPS) * nmix_ref[...] * (1.0 + sc_a) + sh_a
        hb = h.astype(BF16)
        yield
        u_rkv = _dot(hb, wrkv_ref[...])
        u_lora = _dot(hb, wlora_ref[...])
        u_mla = _dot(hb, wmla_ref[...])
        last_rows[s] = (u_rkv[ts - 1:ts, :], u_lora[ts - 1:ts, :])
        yield

        before_rkv, before_lora = (carry_rkv[...], carry_lora[...]) if s == 0 else last_rows[s - 1]
        prev_rkv = jnp.where(row0, before_rkv, pltpu.roll(u_rkv, 1, 0))
        prev_lora = jnp.where(row0, before_lora, pltpu.roll(u_lora, 1, 0))
        if s == PRE_SUBTILES - 1:
            carry_rkv[...] = u_rkv[ts - 1:ts, :]
            carry_lora[...] = u_lora[ts - 1:ts, :]
        us = u_rkv + (prev_rkv - u_rkv) * mu_rkv_ref[...]
        ul = u_lora + (prev_lora - u_lora) * mu_lora_ref[...]
        r = us[:, 0:D_RWKV]
        k = us[:, D_RWKV:2 * D_RWKV]
        v = us[:, 2 * D_RWKV:3 * D_RWKV]
        lane_l = lax.broadcasted_iota(I32, ul.shape, 1)
        t_in = jnp.where(lane_l < DECAY_LORA, jnp.tanh(ul),
                         jnp.where(lane_l < DECAY_LORA + ICLR_LORA, ul, _sigmoid(ul)))
        yield
        up = _mm(t_in, wup_ref[...])
        yield
        z = w0_ref[...] + up[:, 0:D_RWKV]
        lw = (-math.exp(-0.5)) * _sigmoid(z)
        a = _sigmoid(a0_ref[...] + up[:, D_RWKV:2 * D_RWKV])
        g = up[:, 2 * D_RWKV:3 * D_RWKV]
        kk = k * kk_ref[...]
        k2 = k * (1.0 + (a - 1.0) * ka_ref[...])
        yield
        ss = _segsum(kk * kk, ones64)
        bonus_sum = _segsum(r * k2 * rk_ref[...], ones64)
        yield
        kk = kk * lax.rsqrt(jnp.maximum(ss, 1e-24))
        r_ref[0, rows, :] = r
        lw_ref[0, rows, :] = lw
        k_ref[0, rows, :] = k2
        v_ref[0, rows, :] = v
        kkn_ref[0, rows, :] = kk
        akk_ref[0, rows, :] = a * kk
        g_ref[0, rows, :] = g
        bonus_ref[0, rows, :] = bonus_sum * v
        yield

        q_lat = u_mla[:, 0:Q_LORA_RANK]
        kv_lat = u_mla[:, Q_LORA_RANK:Q_LORA_RANK + KV_LORA_RANK]
        kpe_tile = u_mla[:, Q_LORA_RANK + KV_LORA_RANK:]
        qn = q_lat * lax.rsqrt(jnp.mean(q_lat * q_lat, axis=-1, keepdims=True) + NORM_EPS) * qan_ref[...]
        kvn = kv_lat * lax.rsqrt(jnp.mean(kv_lat * kv_lat, axis=-1, keepdims=True) + NORM_EPS) * kvan_ref[...]
        kvb = kvn.astype(BF16)
        yield
        q_raw = _mm(qn, wqb_ref[...])
        k_raw = _dot(kvb, wkb_ref[...])
        v_pad = _dot(kvb, wvb_ref[...])
        yield
        kpe_h = pltpu.roll(kpe_tile, QK_NOPE_DIM, 1)
        cos_t, s1, s2 = rope_tables(s)

        def tables(gain, scale):
            g_s = gain * scale
            return (cos_t * g_s, s1 * pltpu.roll(g_s, HEAD_PAD - half, 1),
                    s2 * pltpu.roll(g_s, half, 1))

        def norm_rope(xh, tabs):
            c_g, s1_g, s2_g = tabs
            ssq = jnp.sum(xh * xh, axis=-1, keepdims=True) * (1.0 / QK_HEAD_DIM)
            rot = xh * c_g + pltpu.roll(xh, HEAD_PAD - half, 1) * s1_g + pltpu.roll(xh, half, 1) * s2_g
            return rot * lax.rsqrt(ssq + NORM_EPS)

        q_tabs = tables(qn_ref[...], ATTN_Q_SCALE)
        k_tabs = tables(kn_ref[...], 1.0)
        vout_ref[0, rows, :] = v_pad.astype(BF16)
        yield
        for hh in range(MLA_HEADS):
            sl = slice(hh * HEAD_PAD, (hh + 1) * HEAD_PAD)
            q_ref[0, rows, sl] = norm_rope(q_raw[:, sl], q_tabs).astype(BF16)
            kout_ref[0, rows, sl] = norm_rope(k_raw[:, sl] + kpe_h, k_tabs).astype(BF16)
            yield

    def rope_tables(s):
        ang_t = invf_ref[...] * pos_ref[0, :, s * ts:(s + 1) * ts].astype(F32)
        frow = lax.broadcasted_iota(I32, (half, HEAD_PAD), 0)
        flane = lax.broadcasted_iota(I32, (half, HEAD_PAD), 1)
        at_x1 = flane == frow + QK_NOPE_DIM
        at_x2 = flane == frow + QK_NOPE_DIM + half
        e_cos = jnp.where(at_x1 | at_x2, 1.0, 0.0).astype(BF16)
        e_sin = jnp.concatenate([jnp.where(at_x1, -1.0, 0.0), jnp.where(at_x2, 1.0, 0.0)],
                                axis=1).astype(BF16)
        lane = lax.broadcasted_iota(I32, (1, HEAD_PAD), 1)
        off_rope = jnp.where((lane >= QK_NOPE_DIM) & (lane < QK_HEAD_DIM), 0.0, 1.0)
        cos_t = _mm_exact_rhs(jnp.cos(ang_t), e_cos, TN) + off_rope
        sin2 = _mm_exact_rhs(jnp.sin(ang_t), e_sin, TN)
        return cos_t, sin2[:, :HEAD_PAD], sin2[:, HEAD_PAD:]

    live = [sub_tile(s) for s in range(PRE_SUBTILES)]
    while live:
        live = [gen for gen in live if next(gen, "done") != "done"]


def _pad_heads(w, n_heads, width):
    kdim = w.shape[0]
    w = w.reshape(kdim, n_heads, width)
    w = jnp.pad(w, ((0, 0), (0, 0), (0, HEAD_PAD - width)))
    return w.reshape(kdim, n_heads * HEAD_PAD)


def _pre_call(x, mod3, positions, norm_mix, w_in, rwkv_mu, decay_w0, decay_up, iclr_a0, iclr_up,
              gate_up, k_k, k_a, r_k, q_a_norm, w_q_b, kv_a_norm, w_kv_b, q_norm, k_norm, tm):
    B, T, D = x.shape
    n_rkv = 3 * D_RWKV
    n_lora = DECAY_LORA + ICLR_LORA + GATE_LORA
    LORA_PAD = 256
    MLA_PAD = 512
    n_mla = Q_LORA_RANK + KV_LORA_RANK + QK_ROPE_DIM
    w_rkv = w_in[:, :n_rkv].astype(BF16)
    w_lora = jnp.pad(w_in[:, n_rkv:n_rkv + n_lora], ((0, 0), (0, LORA_PAD - n_lora))).astype(BF16)
    w_mla = jnp.pad(w_in[:, n_rkv + n_lora:], ((0, 0), (0, MLA_PAD - n_mla))).astype(BF16)
    mu_rkv = rwkv_mu[:n_rkv].reshape(1, n_rkv)
    mu_lora = jnp.pad(rwkv_mu[n_rkv:], (0, LORA_PAD - n_lora)).reshape(1, LORA_PAD)
    w_up = jnp.zeros((LORA_PAD, n_rkv), F32)
    w_up = w_up.at[0:DECAY_LORA, 0:D_RWKV].set(decay_up)
    w_up = w_up.at[DECAY_LORA:DECAY_LORA + ICLR_LORA, D_RWKV:2 * D_RWKV].set(iclr_up)
    w_up = w_up.at[DECAY_LORA + ICLR_LORA:n_lora, 2 * D_RWKV:].set(gate_up)
    w_up = w_up.astype(BF16)
    w_qb = _pad_heads(w_q_b, MLA_HEADS, QK_HEAD_DIM).astype(BF16)
    w_kv3 = w_kv_b.reshape(KV_LORA_RANK, MLA_HEADS, QK_NOPE_DIM + V_HEAD_DIM)
    w_kb = _pad_heads(w_kv3[:, :, :QK_NOPE_DIM].reshape(KV_LORA_RANK, -1), MLA_HEADS, QK_NOPE_DIM).astype(BF16)
    w_vb = _pad_heads(w_kv3[:, :, QK_NOPE_DIM:].reshape(KV_LORA_RANK, -1), MLA_HEADS, V_HEAD_DIM).astype(BF16)
    qn_pad = jnp.pad(q_norm, (0, HEAD_PAD - QK_HEAD_DIM)).reshape(1, HEAD_PAD)
    kn_pad = jnp.pad(k_norm, (0, HEAD_PAD - QK_HEAD_DIM)).reshape(1, HEAD_PAD)
    inv_freq = ROPE_THETA ** (-jnp.arange(0, QK_ROPE_DIM, 2, dtype=F32) / QK_ROPE_DIM)
    invf = inv_freq.reshape(QK_ROPE_DIM // 2, 1)
    pos3 = positions.reshape(B, 1, T)
    HP = MLA_HEADS * HEAD_PAD

    row = lambda n: pl.BlockSpec((1, n), lambda b, t: (0, 0))
    full = lambda a: pl.BlockSpec(a.shape, lambda b, t: (0,) * a.ndim)
    tok = lambda n: pl.BlockSpec((1, tm, n), lambda b, t: (b, t, 0))
    outs = ([jax.ShapeDtypeStruct((B, T, D_RWKV), F32)] * 8
            + [jax.ShapeDtypeStruct((B, T, HP), BF16)] * 3)
    return pl.pallas_call(
        _pre_kernel,
        grid=(B, T // tm),
        in_specs=[tok(D),
                  pl.BlockSpec((1, 6, D), lambda b, t: (b, 0, 0)),
                  pl.BlockSpec((1, 1, tm), lambda b, t: (b, 0, t)),
                  row(D), full(w_rkv), full(w_lora), full(w_mla),
                  row(n_rkv), row(LORA_PAD), full(w_up), row(D_RWKV), row(D_RWKV),
                  row(D_RWKV), row(D_RWKV), row(D_RWKV),
                  row(Q_LORA_RANK), full(w_qb), row(KV_LORA_RANK), full(w_kb), full(w_vb),
                  row(HEAD_PAD), row(HEAD_PAD), full(invf)],
        out_specs=[tok(D_RWKV)] * 8 + [tok(HP)] * 3,
        out_shape=outs,
        scratch_shapes=[pltpu.VMEM((1, n_rkv), F32), pltpu.VMEM((1, LORA_PAD), F32)],
        compiler_params=_cparams("arbitrary", "arbitrary"),
        name="pre",
    )(x, mod3, pos3, norm_mix.reshape(1, D), w_rkv, w_lora, w_mla, mu_rkv, mu_lora, w_up,
      decay_w0.reshape(1, -1), iclr_a0.reshape(1, -1), k_k.reshape(1, -1), k_a.reshape(1, -1),
      r_k.reshape(1, -1), q_a_norm.reshape(1, -1), w_qb, kv_a_norm.reshape(1, -1), w_kb, w_vb,
      qn_pad, kn_pad, invf)


def _scan_kernel(r_ref, lw_ref, k_ref, v_ref, kk_ref, akk_ref, y_ref, state):
    C = SCAN_CHUNK
    n_chunks = r_ref.shape[1] // C
    n_pairs = r_ref.shape[2] // LANES

    @pl.when(pl.program_id(1) == 0)
    def _():
        state[...] = jnp.zeros_like(state)

    ri = lax.broadcasted_iota(I32, (C, C), 0)
    ci = lax.broadcasted_iota(I32, (C, C), 1)
    tri_incl = jnp.where(ci <= ri, 1.0, 0.0).astype(BF16)
    r2 = lax.broadcasted_iota(I32, (2 * C, 2 * C), 0)
    c2 = lax.broadcasted_iota(I32, (2 * C, 2 * C), 1)
    same = (r2 >= C) == (c2 >= C)
    strict = same & (c2 < r2)
    incl = same & (c2 <= r2)
    eye = jnp.where(c2 == r2, 1.0, 0.0)
    head0 = lax.broadcasted_iota(I32, (C, LANES), 1) < RWKV_HEAD_DIM

    def stack2(a):
        return jnp.concatenate([jnp.where(head0, a, 0.0), jnp.where(head0, 0.0, a)], axis=0)

    C2 = 2 * C
    cat0 = lambda *a: jnp.concatenate(a, axis=0)
    cat1 = lambda *a: jnp.concatenate(a, axis=1)

    items = []
    for c in range(n_chunks):
        rows = slice(c * C, (c + 1) * C)
        lw = lw_ref[0, rows, :]
        cum = _mm_exact_rhs_left(tri_incl, lw)
        cum_end = cum[C - 1:C, :]
        w_end = jnp.exp(cum_end)
        e_pos = jnp.exp(cum)
        e_neg = jnp.exp(-cum)
        e_prev = jnp.exp(cum - lw)
        e_end = jnp.exp(cum_end - cum)
        kk = kk_ref[0, rows, :]
        k2 = k_ref[0, rows, :]
        pneg = -akk_ref[0, rows, :]
        vv = v_ref[0, rows, :]
        rt = r_ref[0, rows, :] * e_pos
        bt = kk * e_prev
        pt = pneg * e_neg
        kt = k2 * e_neg
        ph = pneg * e_end
        kh = k2 * e_end
        for pp in range(n_pairs):
            sl = slice(pp * LANES, (pp + 1) * LANES)
            items.append(dict(
                c=c, p=pp, w_end=w_end[:, sl],
                bt2=stack2(bt[:, sl]).astype(BF16), rt2=stack2(rt[:, sl]).astype(BF16),
                pk2=cat0(stack2(pt[:, sl]), stack2(kt[:, sl])).astype(BF16),
                phkh2=cat0(stack2(ph[:, sl]), stack2(kh[:, sl])).astype(BF16),
                v2=stack2(vv[:, sl])))
    for it in items:
        ab = _dot(cat0(it['bt2'], it['rt2']), it['pk2'], NT)
        it['a_ab'] = jnp.where(strict, ab[:C2, :C2], 0.0)
        it['a_ak'] = jnp.where(strict, ab[:C2, C2:], 0.0).astype(BF16)
        it['b_rpk'] = cat1(jnp.where(incl, ab[C2:, :C2], 0.0), jnp.where(incl, ab[C2:, C2:], 0.0)).astype(BF16)
        it['tinv'] = eye + it['a_ab']
    for it in items:
        it['apow'] = _mm(it['a_ab'], it['a_ab'])
    for _ in range(int(math.log2(C)) - 1):
        for it in items:
            both = _mm(cat0(it['apow'], it['tinv']), it['apow'])
            it['apow'] = both[:C2]
            it['tinv'] = it['tinv'] + both[C2:]
    for it in items:
        it['akv'] = _dot(it['a_ak'], it['v2'].astype(BF16))
    for it in items:
        tt = _dot(it['tinv'].astype(BF16), cat1(it['bt2'], it['akv'].astype(BF16)))
        it['tb_rt'] = cat0(tt[:, :LANES].astype(BF16), it['rt2'])
        it['tav'] = tt[:, LANES:]
    for it in items:
        pp = it['p']
        s0 = state[pp]
        top = _dot(it['tb_rt'], s0.astype(BF16), NT)
        u2 = top[:C2] + it['tav']
        uv = cat0(u2, it['v2']).astype(BF16)
        y2 = top[C2:] + _dot(it['b_rpk'], uv)
        state[pp] = s0 * it['w_end'] + _dot(uv, it['phkh2'], TN)
        y_ref[0, it['c'] * C:(it['c'] + 1) * C, pp * LANES:(pp + 1) * LANES] = y2[0:C] + y2[C:C2]


def _mm_exact_rhs_left(b_exact_bf16, a):
    h, m, l = _split3(a)
    return _dot(b_exact_bf16, h) + (_dot(b_exact_bf16, m) + _dot(b_exact_bf16, l))


def _scan_call(r, lw, k2, v, kk, akk):
    B, T, W = r.shape
    tb = _tile(T, SCAN_BLOCK)
    spec = pl.BlockSpec((1, tb, W), lambda b, c: (b, c, 0))
    return pl.pallas_call(
        _scan_kernel,
        grid=(B, T // tb),
        in_specs=[spec] * 6,
        out_specs=spec,
        out_shape=jax.ShapeDtypeStruct((B, T, W), F32),
        scratch_shapes=[pltpu.VMEM((W // LANES, 2 * RWKV_HEAD_DIM, LANES), F32)],
        compiler_params=_cparams("arbitrary", "arbitrary"),
        name="scan",
    )(r, lw, k2, v, kk, akk)


def _attn_kernel(q_ref, k_ref, v_ref, o_ref):
    T = q_ref.shape[1]
    tq = min(T, ATTN_TILE)
    row = lax.broadcasted_iota(I32, (tq, tq), 0)
    col = lax.broadcasted_iota(I32, (tq, tq), 1)
    causal = col <= row

    def update(q, kt, vt, carry, mask):
        m_old, l_old, acc = carry
        s = _dot(q, kt, NT)
        if mask:
            s = jnp.where(causal, s, NEG_INF)
        m_new = jnp.maximum(m_old, jnp.max(s, axis=-1, keepdims=True))
        alpha = jnp.exp2(m_old - m_new)
        p = jnp.exp2(s - m_new)
        l_new = alpha * l_old + jnp.sum(p, axis=-1, keepdims=True)
        acc = alpha * acc + _dot(p.astype(BF16), vt)
        return m_new, l_new, acc

    for qi in range(T // tq):
        q = q_ref[0, qi * tq:(qi + 1) * tq, :]
        carry = (jnp.full((tq, 1), NEG_INF, F32), jnp.zeros((tq, 1), F32),
                 jnp.zeros((tq, HEAD_PAD), F32))

        def body(ki, carry, q=q):
            rows = pl.ds(pl.multiple_of(ki * tq, tq), tq)
            return update(q, k_ref[0, rows, :], v_ref[0, rows, :], carry, False)

        carry = lax.fori_loop(0, qi, body, carry, unroll=True)
        diag = slice(qi * tq, (qi + 1) * tq)
        _, l_fin, acc = update(q, k_ref[0, diag, :], v_ref[0, diag, :], carry, True)
        o_ref[0, diag, :] = (acc / l_fin).astype(o_ref.dtype)


def _attn_call(q, k, v):
    B, T, HP = q.shape
    spec = pl.BlockSpec((1, T, HEAD_PAD), lambda b, h: (b, 0, h))
    return pl.pallas_call(
        _attn_kernel,
        grid=(B, MLA_HEADS),
        in_specs=[spec, spec, spec],
        out_specs=spec,
        out_shape=jax.ShapeDtypeStruct((B, T, HP), BF16),
        compiler_params=_cparams("arbitrary", "arbitrary"),
        name="attn",
    )(q, k, v)


def _post_kernel(y_ref, bonus_ref, g_ref, o_ref, x_ref, mod_ref, lnw_ref, lnb_ref,
                 wo_r_ref, wo_m_ref, nffn_ref, x1_ref, h2_ref, h2t_ref):
    y = y_ref[0]
    ones64 = _seg_ones(D_RWKV, RWKV_HEAD_DIM)
    mean = _segsum(y, ones64) * (1.0 / RWKV_HEAD_DIM)
    yc = y - mean
    var = _segsum(yc * yc, ones64) * (1.0 / RWKV_HEAD_DIM)
    yn = yc * lax.rsqrt(var + GN_EPS) * lnw_ref[...] + lnb_ref[...]
    yr = (yn + bonus_ref[0]) * g_ref[0]
    mix = _mm(yr, wo_r_ref[...]) + _dot(o_ref[0], wo_m_ref[...])
    g_a = mod_ref[0, 2:3, :]
    sh_f = mod_ref[0, 3:4, :]
    sc_f = mod_ref[0, 4:5, :]
    x1 = x_ref[0] + g_a * mix
    x1_ref[0] = x1
    ms = jnp.mean(x1 * x1, axis=-1, keepdims=True)
    h2 = x1 * lax.rsqrt(ms + NORM_EPS) * nffn_ref[...] * (1.0 + sc_f) + sh_f
    h2_ref[0] = h2
    tm = h2.shape[0]
    for s in range(SUBLANES):
        h2t_ref[0, pl.ds(s, tm, stride=SUBLANES), :] = h2[:, s * LANES:(s + 1) * LANES]


def _post_call(y, bonus, g, o_pad, x, mod3, ln_w, ln_b, w_out, norm_ffn, tm):
    B, T, D = x.shape
    HP = MLA_HEADS * HEAD_PAD
    wo_r = w_out[:D_RWKV].astype(BF16)
    wo_m = jnp.pad(w_out[D_RWKV:].reshape(MLA_HEADS, V_HEAD_DIM, D),
                   ((0, 0), (0, HEAD_PAD - V_HEAD_DIM), (0, 0))).reshape(HP, D).astype(BF16)
    tok = lambda n: pl.BlockSpec((1, tm, n), lambda b, t: (b, t, 0))
    row = lambda n: pl.BlockSpec((1, n), lambda b, t: (0, 0))
    full = lambda a: pl.BlockSpec(a.shape, lambda b, t: (0,) * a.ndim)
    return pl.pallas_call(
        _post_kernel,
        grid=(B, T // tm),
        in_specs=[tok(D_RWKV), tok(D_RWKV), tok(D_RWKV), tok(HP), tok(D),
                  pl.BlockSpec((1, 6, D), lambda b, t: (b, 0, 0)),
                  row(D_RWKV), row(D_RWKV), full(wo_r), full(wo_m), row(D)],
        out_specs=[tok(D), tok(D),
                   pl.BlockSpec((1, tm * SUBLANES, LANES), lambda b, t: (b, t, 0))],
        out_shape=[jax.ShapeDtypeStruct((B, T, D), F32)] * 2
        + [jax.ShapeDtypeStruct((B, T * SUBLANES, LANES), F32)],
        compiler_params=_cparams("arbitrary", "arbitrary"),
        name="post",
    )(y, bonus, g, o_pad, x, mod3, ln_w.reshape(1, -1), ln_b.reshape(1, -1), wo_r, wo_m,
      norm_ffn.reshape(1, D))


def _first_index(mask, iota, size, axis):
    return jnp.min(jnp.where(mask, iota, size), axis=axis, keepdims=True)


def _route_kernel(h_ref, wr_ref, bias_ref, e_ref, w_ref, rank_ref, cnt_ref, base):
    tr = h_ref.shape[0]
    E = N_EXPERTS

    @pl.when(pl.program_id(0) == 0)
    def _():
        base[...] = jnp.zeros_like(base)

    logits = _mm3(wr_ref[...], h_ref[...], NT)
    scores = _sigmoid(logits)
    sel = scores + bias_ref[...]
    iota_g = lax.broadcasted_iota(I32, (GROUP_SIZE, tr), 0)
    gs_rows = []
    for gi in range(N_GROUPS):
        blk = sel[gi * GROUP_SIZE:(gi + 1) * GROUP_SIZE, :]
        m1 = jnp.max(blk, axis=0, keepdims=True)
        i1 = _first_index(blk == m1, iota_g, GROUP_SIZE, 0)
        m2 = jnp.max(jnp.where(iota_g == i1, NEG_INF, blk), axis=0, keepdims=True)
        gs_rows.append(m1 + m2)
    gs = jnp.concatenate(gs_rows, axis=0)
    iota8 = lax.broadcasted_iota(I32, (N_GROUPS, tr), 0)
    gmask = jnp.zeros((N_GROUPS, tr), jnp.bool_)
    for _ in range(TOPK_GROUPS):
        mg = jnp.max(gs, axis=0, keepdims=True)
        ig = _first_index(gs == mg, iota8, N_GROUPS, 0)
        hit = iota8 == ig
        gmask = gmask | hit
        gs = jnp.where(hit, NEG_INF, gs)
    msel = jnp.concatenate(
        [jnp.where(gmask[gi:gi + 1, :], sel[gi * GROUP_SIZE:(gi + 1) * GROUP_SIZE, :], NEG_INF)
         for gi in range(N_GROUPS)], axis=0)
    iota_e = lax.broadcasted_iota(I32, (E, tr), 0)
    e_rows, w_rows = [], []
    onehot = jnp.zeros((E, tr), F32)
    for _ in range(TOP_K):
        mv = jnp.max(msel, axis=0, keepdims=True)
        ie = _first_index(msel == mv, iota_e, E, 0)
        hit = iota_e == ie
        e_rows.append(ie)
        w_rows.append(jnp.sum(jnp.where(hit, scores, 0.0), axis=0, keepdims=True))
        onehot = jnp.where(hit, 1.0, onehot)
        msel = jnp.where(hit, NEG_INF, msel)
    top_e = jnp.concatenate(e_rows, axis=0)
    wts = jnp.concatenate(w_rows, axis=0)
    wts = wts / jnp.sum(wts, axis=0, keepdims=True) * ROUTED_SCALE
    ti = lax.broadcasted_iota(I32, (tr, tr), 0)
    tj = lax.broadcasted_iota(I32, (tr, tr), 1)
    upper = jnp.where(ti < tj, 1.0, 0.0).astype(BF16)
    pos = _dot(onehot.astype(BF16), upper) + base[...]
    rank_rows = [jnp.sum(jnp.where(iota_e == e_rows[j], pos, 0.0), axis=0, keepdims=True)
                 for j in range(TOP_K)]
    base[...] = base[...] + jnp.sum(onehot, axis=1, keepdims=True)
    e_ref[...] = top_e
    w_ref[...] = wts
    rank_ref[...] = jnp.concatenate(rank_rows, axis=0).astype(I32)
    cnt_ref[...] = base[...].astype(I32)


def _route_call(h2, w_router, router_bias, tr):
    N, D = h2.shape
    E = N_EXPERTS
    out_kn = pl.BlockSpec((TOP_K, tr), lambda i: (0, i))
    return pl.pallas_call(
        _route_kernel,
        grid=(N // tr,),
        in_specs=[pl.BlockSpec((tr, D), lambda i: (i, 0)),
                  pl.BlockSpec((E, D), lambda i: (0, 0)),
                  pl.BlockSpec((E, 1), lambda i: (0, 0))],
        out_specs=[out_kn, out_kn, out_kn, pl.BlockSpec((E, 1), lambda i: (0, 0))],
        out_shape=[jax.ShapeDtypeStruct((TOP_K, N), I32), jax.ShapeDtypeStruct((TOP_K, N), F32),
                   jax.ShapeDtypeStruct((TOP_K, N), I32), jax.ShapeDtypeStruct((E, 1), I32)],
        scratch_shapes=[pltpu.VMEM((E, 1), F32)],
        compiler_params=_cparams("arbitrary"),
        name="route",
    )(h2, w_router.T, router_bias.reshape(E, 1))


def _dest_kernel(e_ref, rank_ref, start_ref, d_ref):
    tr = e_ref.shape[1]
    iota_e = lax.broadcasted_iota(I32, (N_EXPERTS, tr), 0)
    starts = start_ref[...]
    rows = [jnp.sum(jnp.where(iota_e == e_ref[j:j + 1, :], starts, 0), axis=0, keepdims=True)
            for j in range(TOP_K)]
    d_ref[...] = jnp.concatenate(rows, axis=0) + rank_ref[...]


def _dest_call(top_e, rank, pad_starts, tr):
    K, N = top_e.shape
    spec = pl.BlockSpec((K, tr), lambda i: (0, i))
    return pl.pallas_call(
        _dest_kernel,
        grid=(N // tr,),
        in_specs=[spec, spec, pl.BlockSpec((N_EXPERTS, 1), lambda i: (0, 0))],
        out_specs=spec,
        out_shape=jax.ShapeDtypeStruct((K, N), I32),
        compiler_params=_cparams("arbitrary"),
        name="dest",
    )(top_e, rank, pad_starts.reshape(N_EXPERTS, 1))


def _sc_dispatch_call(h2t, dest3, n_rows):
    N = h2t.shape[0]
    info = plsc.get_sparse_core_info()
    n_workers = info.num_cores * info.num_subcores
    n_chunks = N // (SC_CHUNK * n_workers)
    assert n_chunks * SC_CHUNK * n_workers == N
    mesh = plsc.VectorSubcoreMesh(core_axis_name="c", subcore_axis_name="s")

    @functools.partial(
        pl.kernel, mesh=mesh,
        out_type=(jax.ShapeDtypeStruct((n_rows,) + h2t.shape[1:], h2t.dtype),
                  jax.ShapeDtypeStruct((n_rows, LANES), I32)),
        scratch_types=[pltpu.VMEM((TOP_K, SC_CHUNK), I32),
                       pltpu.VMEM((SC_CHUNK,) + h2t.shape[1:], h2t.dtype),
                       pltpu.VMEM((SC_CHUNK, LANES), I32)],
        name="sc_dispatch",
    )
    def scatter_rows(h_hbm, dest_hbm, xs_hbm, tag_hbm, idx_v, rows_v, tag_v):
        wid = lax.axis_index("s") * info.num_cores + lax.axis_index("c")
        zeros = jnp.zeros((info.num_lanes,), I32)

        @pl.loop(0, SC_CHUNK)
        def _(r):
            for l0 in range(0, LANES, info.num_lanes):
                tag_v[r, pl.ds(l0, info.num_lanes)] = zeros

        @pl.loop(0, n_chunks)
        def _(c):
            chunk = wid * n_chunks + c
            base = chunk * SC_CHUNK
            pltpu.sync_copy(dest_hbm.at[chunk], idx_v)
            pltpu.sync_copy(h_hbm.at[pl.ds(base, SC_CHUNK)], rows_v)
            for j in range(TOP_K):
                pltpu.sync_copy(rows_v, xs_hbm.at[idx_v.at[j]])

                @pl.loop(0, SC_CHUNK)
                def _(r):
                    tag_v[r, pl.ds(0, info.num_lanes)] = zeros + ((base + r) * TOP_K + j)

                pltpu.sync_copy(tag_v, tag_hbm.at[idx_v.at[j]])

    return scatter_rows(h2t, dest3)


def _moe_kernel(be_ref, nu_ref, nv_ref, xs_ref, xtag_ref, wgu_ref, wdn_ref, yt_hbm, wgu_bf, wdn_bf,
                ybuf, tag_v, tag_s, sem_tag, sem_rows):
    i = pl.program_id(0)
    n_steps = pl.num_programs(0)
    cur = i % 2
    n_real = yt_hbm.shape[0] - 2 * MOE_ROWS

    def row_copy(buf, r, tag):
        rows = pl.ds(pl.multiple_of(r * SUBLANES, SUBLANES), SUBLANES)
        return pltpu.make_async_copy(ybuf.at[buf, rows, :], yt_hbm.at[tag], sem_rows.at[buf])

    n_used = nu_ref[0]
    prv = 1 - cur

    def wait_sent(buf):
        pltpu.make_async_copy(ybuf.at[buf], ybuf.at[buf], sem_rows.at[buf]).wait()

    def send_prev():
        for r in range(MOE_ROWS):
            row_copy(prv, r, tag_s[prv, r]).start(priority=r % 2)

    @pl.when((i >= 2) & (i - 2 < n_used))
    def _():
        wait_sent(cur)

    i_blk = jnp.minimum(i, n_steps - 2)
    @pl.when((i < n_used) & ((i == 0) | (be_ref[i_blk] != be_ref[jnp.maximum(i_blk - 1, 0)])))
    def _():
        wgu_bf[...] = wgu_ref[0].astype(BF16)
        wdn_bf[...] = wdn_ref[0].astype(BF16)

    def compute():
        row = lax.broadcasted_iota(I32, (MOE_ROWS, LANES), 0)
        tags = jnp.where(row < nv_ref[i_blk], xtag_ref[...], n_real + cur * MOE_ROWS + row)
        tag_v[...] = tags.astype(F32).T[0:1, :].astype(I32)
        cp = pltpu.make_async_copy(tag_v, tag_s.at[pl.ds(cur, 1)], sem_tag)
        cp.start()
        xb = jnp.concatenate([xs_ref[pl.ds(s, MOE_ROWS, stride=SUBLANES), :]
                              for s in range(SUBLANES)], axis=1).astype(BF16)
        gu = _dot(xb, wgu_bf[...])
        act = _silu(gu[:, :D_EXPERT]) * gu[:, D_EXPERT:]
        y = _dot(act.astype(BF16), wdn_bf[...])
        for s in range(SUBLANES):
            ybuf[cur, pl.ds(s, MOE_ROWS, stride=SUBLANES), :] = y[:, s * LANES:(s + 1) * LANES]
        cp.wait()

    @pl.when((i == 0) & (i < n_used))
    def _():
        compute()

    @pl.when((i > 0) & (i < n_used))
    def _():
        send_prev()
        compute()

    @pl.when((i > 0) & (i == n_used))
    def _():
        send_prev()

    @pl.when((i == n_steps - 1) & (i - 1 < n_used))
    def _():
        wait_sent(prv)


def _moe_call(block_expert, n_used, n_valid, xs, xtag, w_gu, w_dn, n_out_rows):
    P = xs.shape[0] // SUBLANES
    D = SUBLANES * LANES
    nb = P // MOE_ROWS
    blk = lambda i, be, nu, nv: (jnp.minimum(i, nu[0] - 1), 0)
    wblk = lambda i, be, nu, nv: (be[jnp.minimum(i, nu[0] - 1)], 0, 0)
    return pl.pallas_call(
        _moe_kernel,
        grid_spec=pltpu.PrefetchScalarGridSpec(
            num_scalar_prefetch=3,
            grid=(nb + 1,),
            in_specs=[pl.BlockSpec((MOE_ROWS * SUBLANES, LANES), blk),
                      pl.BlockSpec((MOE_ROWS, LANES), blk),
                      pl.BlockSpec((1, D, 2 * D_EXPERT), wblk),
                      pl.BlockSpec((1, D_EXPERT, D), wblk)],
            out_specs=pl.BlockSpec(memory_space=pl.ANY),
            scratch_shapes=[pltpu.VMEM((D, 2 * D_EXPERT), BF16), pltpu.VMEM((D_EXPERT, D), BF16),
                            pltpu.VMEM((2, MOE_ROWS * SUBLANES, LANES), F32),
                            pltpu.VMEM((1, MOE_ROWS), I32), pltpu.SMEM((2, MOE_ROWS), I32),
                            pltpu.SemaphoreType.DMA, pltpu.SemaphoreType.DMA((2,))]),
        out_shape=jax.ShapeDtypeStruct((n_out_rows, SUBLANES, LANES), F32),
        compiler_params=_cparams("arbitrary"),
        name="moe",
    )(block_expert, n_used, n_valid, xs, xtag, w_gu, w_dn)


def _combine_kernel(w_hbm, yt_ref, h_ref, x1_ref, mod_ref, wsg_ref, wsd_ref, o_ref,
                    wts, routed, sem_w):
    i = pl.program_id(0)
    tc = h_ref.shape[0]
    cp_w = pltpu.make_async_copy(w_hbm.at[i], wts, sem_w)
    cp_w.start()
    gu = _mm(h_ref[...], wsg_ref[...])
    act = _silu(gu[:, :D_EXPERT]) * gu[:, D_EXPERT:]
    ffn = _mm(act, wsd_ref[...])
    cp_w.wait()

    def wsum(tt, carry):
        for u in range(SUBLANES):
            t = tt * SUBLANES + u
            first = pl.multiple_of(t * (TOP_K * SUBLANES), SUBLANES)
            acc = yt_ref[pl.ds(first, SUBLANES), :] * wts[0, t]
            for j in range(1, TOP_K):
                acc = acc + yt_ref[pl.ds(first + j * SUBLANES, SUBLANES), :] * wts[j, t]
            routed[pl.ds(pl.multiple_of(t * SUBLANES, SUBLANES), SUBLANES), :] = acc
        return carry

    lax.fori_loop(0, tc // SUBLANES, wsum, 0)
    routed2d = jnp.concatenate([routed[pl.ds(s, tc, stride=SUBLANES), :] for s in range(SUBLANES)],
                               axis=1)
    g_f = mod_ref[0, 5:6, :]
    o_ref[...] = x1_ref[...] + g_f * (ffn + routed2d)


def _combine_call(w3, yt, h2, x1, mod3, w_sh_gu, w_sh_dn, tokens_per_batch, tc):
    N, D = h2.shape
    tiles_per_batch = tokens_per_batch // tc
    tok = pl.BlockSpec((tc, D), lambda i: (i, 0))
    wsg = w_sh_gu.astype(BF16)
    wsd = w_sh_dn.astype(BF16)
    return pl.pallas_call(
        _combine_kernel,
        grid=(N // tc,),
        in_specs=[pl.BlockSpec(memory_space=pl.ANY),
                  pl.BlockSpec((tc * TOP_K * SUBLANES, LANES), lambda i: (i, 0)),
                  tok, tok,
                  pl.BlockSpec((1, 6, D), lambda i: (i // tiles_per_batch, 0, 0)),
                  pl.BlockSpec(wsg.shape, lambda i: (0, 0)),
                  pl.BlockSpec(wsd.shape, lambda i: (0, 0))],
        out_specs=tok,
        out_shape=jax.ShapeDtypeStruct((N, D), F32),
        scratch_shapes=[pltpu.SMEM((TOP_K, tc), F32), pltpu.VMEM((tc * SUBLANES, LANES), F32),
                        pltpu.SemaphoreType.DMA],
        compiler_params=_cparams("arbitrary"),
        name="combine",
    )(w3, yt, h2, x1, mod3, wsg, wsd)


def _tile(n, pref):
    t = min(n, pref)
    assert n % t == 0, (n, t)
    return t


def _layer(x, mod3, positions, norm_mix, w_in, rwkv_mu, decay_w0, decay_up, iclr_a0, iclr_up,
           gate_up, rwkv_k_k, rwkv_k_a, rwkv_r_k, ln_x_w, ln_x_b, q_a_norm, w_q_b, kv_a_norm,
           w_kv_b, q_norm, k_norm, w_out, norm_ffn, w_router, router_bias, w_e_gate_up, w_e_down,
           w_sh_gate_up, w_sh_down):
    B, T, D = x.shape
    N = B * T
    assert T % SCAN_CHUNK == 0
    (r, lw, k2, v, kk, akk, g, bonus, q_pad, k_pad, v_pad) = _pre_call(
        x, mod3, positions, norm_mix, w_in, rwkv_mu, decay_w0, decay_up, iclr_a0, iclr_up,
        gate_up, rwkv_k_k, rwkv_k_a, rwkv_r_k, q_a_norm, w_q_b, kv_a_norm, w_kv_b, q_norm, k_norm,
        tm=_tile(T, 512))
    y = _scan_call(r, lw, k2, v, kk, akk)
    o_pad = _attn_call(q_pad, k_pad, v_pad)
    x1, h2, h2t = _post_call(y, bonus, g, o_pad, x, mod3, ln_x_w, ln_x_b, w_out, norm_ffn,
                             tm=_tile(T, 512))
    x1 = x1.reshape(N, D)
    h2 = h2.reshape(N, D)
    h2t = h2t.reshape(N, D // LANES, LANES)

    tr = _tile(N, 512)
    top_e, wts, rank, counts = _route_call(h2, w_router, router_bias, tr)
    counts = counts.reshape(N_EXPERTS)
    padded = (counts + MOE_ROWS - 1) // MOE_ROWS * MOE_ROWS
    pad_ends = jnp.cumsum(padded)
    pad_starts = pad_ends - padded
    n_blocks = (N * TOP_K + N_EXPERTS * (MOE_ROWS - 1)) // MOE_ROWS
    block_row0 = jnp.arange(n_blocks + 1, dtype=I32) * MOE_ROWS
    block_expert = jnp.minimum(
        jnp.sum((pad_ends[None, :] <= block_row0[:, None]).astype(I32), axis=1), N_EXPERTS - 1)
    n_used = (pad_ends[-1:] // MOE_ROWS).astype(I32)
    own = block_expert[:, None] == jnp.arange(N_EXPERTS, dtype=I32)[None, :]
    run_end = jnp.sum(jnp.where(own, (pad_starts + counts)[None, :], 0), axis=1)
    n_valid = jnp.clip(run_end - block_row0, 0, MOE_ROWS).astype(I32)
    dest = _dest_call(top_e, rank, pad_starts.astype(I32), tr)

    dest3 = dest.reshape(TOP_K, N // SC_CHUNK, SC_CHUNK).transpose(1, 0, 2)
    n_rows = n_blocks * MOE_ROWS
    xs, xtag = _sc_dispatch_call(h2t, dest3, n_rows)
    yt = _moe_call(block_expert, n_used, n_valid, xs.reshape(n_rows * SUBLANES, LANES), xtag,
                   w_e_gate_up, w_e_down, N * TOP_K + 2 * MOE_ROWS)
    tc = _tile(T, COMBINE_TILE)
    w3 = wts.reshape(TOP_K, N // tc, tc).transpose(1, 0, 2)
    out = _combine_call(w3, yt.reshape(-1, LANES), h2, x1, mod3, w_sh_gate_up,
                        w_sh_down, T, tc)
    return out.reshape(B, T, D)


def kernel(x, c, positions, ada_w, ada_b, norm_mix, w_in, rwkv_mu, decay_w0, decay_up, iclr_a0, iclr_up, gate_up, rwkv_k_k, rwkv_k_a, rwkv_r_k, ln_x_w, ln_x_b, q_a_norm, w_q_b, kv_a_norm, w_kv_b, q_norm, k_norm, w_out, norm_ffn, w_router, router_bias, w_e_gate_up, w_e_down, w_sh_gate_up, w_sh_down):
    B, T, D = x.shape
    depth = ada_w.shape[0]
    for l in range(depth):
        mod3 = _mod_call(c, ada_w[l], ada_b[l]).reshape(B, 6, D)
        x = _layer(x, mod3, positions, norm_mix[l], w_in[l], rwkv_mu[l], decay_w0[l], decay_up[l],
                   iclr_a0[l], iclr_up[l], gate_up[l], rwkv_k_k[l], rwkv_k_a[l], rwkv_r_k[l],
                   ln_x_w[l], ln_x_b[l], q_a_norm[l], w_q_b[l], kv_a_norm[l], w_kv_b[l],
                   q_norm[l], k_norm[l], w_out[l], norm_ffn[l], w_router[l], router_bias[l],
                   w_e_gate_up[l], w_e_down[l], w_sh_gate_up[l], w_sh_down[l])
    return x
```

```python
import functools
import math

import jax
import jax.numpy as jnp
import numpy as np
from jax import lax
from jax.experimental import pallas as pl
from jax.experimental.pallas import tpu as pltpu
from jax.experimental.pallas import tpu_sc as plsc

F32 = jnp.float32
BF16 = jnp.bfloat16
I32 = jnp.int32

NORM_EPS = 1e-6
GN_EPS = 64e-5
RWKV_HEADS = 8
RWKV_HEAD_DIM = 64
D_RWKV = 512
DECAY_LORA = 32
ICLR_LORA = 32
GATE_LORA = 96
MLA_HEADS = 8
QK_NOPE_DIM = 64
QK_ROPE_DIM = 32
QK_HEAD_DIM = 96
V_HEAD_DIM = 64
Q_LORA_RANK = 256
KV_LORA_RANK = 128
ROPE_THETA = 10000.0
N_EXPERTS = 256
TOP_K = 8
N_GROUPS = 8
TOPK_GROUPS = 4
GROUP_SIZE = N_EXPERTS // N_GROUPS
D_EXPERT = 256
ROUTED_SCALE = 2.5
MOE_ROWS = 512
SC_CHUNK = 64
COMBINE_TILE = 256

LANES = 128
SUBLANES = 8
HEAD_PAD = 128
VMEM_LIMIT = 56 * 1024 * 1024

PRE_SUBTILES = 1
SCAN_CHUNK = 64
SCAN_BLOCK = 512
ATTN_TILE = 256
ATTN_Q_SCALE = QK_HEAD_DIM ** -0.5 * math.log2(math.e)
NEG_INF = float("-inf")


def _cparams(*sem):
    return pltpu.CompilerParams(dimension_semantics=sem, vmem_limit_bytes=VMEM_LIMIT)


def _split2(a):
    hi = a.astype(BF16)
    lo = (a - hi.astype(F32)).astype(BF16)
    return hi, lo


def _split3(a):
    hi = a.astype(BF16)
    r1 = a - hi.astype(F32)
    mid = r1.astype(BF16)
    lo = (r1 - mid.astype(F32)).astype(BF16)
    return hi, mid, lo


def _dot(a, b, dims=None):
    if dims is None:
        return jnp.dot(a, b, preferred_element_type=F32)
    return lax.dot_general(a, b, (dims, ((), ())), preferred_element_type=F32)


def _mm(a, b, dims=None):
    return _dot(a.astype(BF16), b.astype(BF16), dims)


def _mm3(a, b, dims=None):
    ah, al = _split2(a)
    bh, bl = _split2(b)
    return _dot(ah, bh, dims) + (_dot(ah, bl, dims) + _dot(al, bh, dims))


def _mm_exact_rhs(a, b_exact_bf16, dims=None):
    h, m, l = _split3(a)
    return _dot(h, b_exact_bf16, dims) + (_dot(m, b_exact_bf16, dims) + _dot(l, b_exact_bf16, dims))


NT = ((1,), (1,))
TN = ((0,), (0,))


def _sigmoid(z):
    return 1.0 / (1.0 + jnp.exp(-z))


def _silu(z):
    return z * _sigmoid(z)


def _seg_ones(width, seg):
    r = lax.broadcasted_iota(I32, (width, width), 0) // seg
    c = lax.broadcasted_iota(I32, (width, width), 1) // seg
    return jnp.where(r == c, 1.0, 0.0).astype(BF16)


def _segsum(a, ones_bd):
    hi, lo = _split2(a)
    return _dot(hi, ones_bd) + _dot(lo, ones_bd)


def _mod_kernel(c_ref, w_ref, b_ref, o_ref):
    ca = _silu(c_ref[...])
    o_ref[...] = _mm3(ca, w_ref[...]) + b_ref[...]


def _mod_call(c, ada_w, ada_b):
    B, D = c.shape
    n6 = ada_w.shape[1]
    tn = D
    return pl.pallas_call(
        _mod_kernel,
        grid=(n6 // tn,),
        in_specs=[pl.BlockSpec((B, D), lambda j: (0, 0)),
                  pl.BlockSpec((D, tn), lambda j: (0, j)),
                  pl.BlockSpec((1, tn), lambda j: (0, j))],
        out_specs=pl.BlockSpec((B, tn), lambda j: (0, j)),
        out_shape=jax.ShapeDtypeStruct((B, n6), F32),
        compiler_params=_cparams("arbitrary"),
        name="mod",
    )(c, ada_w, ada_b.reshape(1, n6))


def _pre_kernel(x_ref, mod_ref, pos_ref, nmix_ref, wrkv_ref, wlora_ref, wmla_ref,
                mu_rkv_ref, mu_lora_ref, wup_ref, w0_ref, a0_ref, kk_ref, ka_ref, rk_ref,
                qan_ref, wqb_ref, kvan_ref, wkb_ref, wvb_ref, qn_ref, kn_ref, invf_ref,
                r_ref, lw_ref, k_ref, v_ref, kkn_ref, akk_ref, g_ref, bonus_ref,
                q_ref, kout_ref, vout_ref,
                carry_rkv, carry_lora):
    ti = pl.program_id(1)
    tm = x_ref.shape[1]

    @pl.when(ti == 0)
    def _():
        carry_rkv[...] = jnp.zeros_like(carry_rkv)
        carry_lora[...] = jnp.zeros_like(carry_lora)

    ts = tm // PRE_SUBTILES
    sh_a = mod_ref[0, 0:1, :]
    sc_a = mod_ref[0, 1:2, :]
    ones64 = _seg_ones(D_RWKV, RWKV_HEAD_DIM)
    row0 = lax.broadcasted_iota(I32, (ts, 1), 0) == 0
    half = QK_ROPE_DIM // 2
    last_rows = {}

    def sub_tile(s):
        rows = slice(s * ts, (s + 1) * ts)
        xb = x_ref[0, rows, :]
        ms = jnp.mean(xb * xb, axis=-1, keepdims=True)
        h = xb * lax.rsqrt(ms + NORM_EPS) * nmix_ref[...] * (1.0 + sc_a) + sh_a
        hb = h.astype(BF16)
        yield
        u_rkv = _dot(hb, wrkv_ref[...])
        u_lora = _dot(hb, wlora_ref[...])
        u_mla = _dot(hb, wmla_ref[...])
        last_rows[s] = (u_rkv[ts - 1:ts, :], u_lora[ts - 1:ts, :])
        yield

        before_rkv, before_lora = (carry_rkv[...], carry_lora[...]) if s == 0 else last_rows[s - 1]
        prev_rkv = jnp.where(row0, before_rkv, pltpu.roll(u_rkv, 1, 0))
        prev_lora = jnp.where(row0, before_lora, pltpu.roll(u_lora, 1, 0))
        if s == PRE_SUBTILES - 1:
            carry_rkv[...] = u_rkv[ts - 1:ts, :]
            carry_lora[...] = u_lora[ts - 1:ts, :]
        us = u_rkv + (prev_rkv - u_rkv) * mu_rkv_ref[...]
        ul = u_lora + (prev_lora - u_lora) * mu_lora_ref[...]
        r = us[:, 0:D_RWKV]
        k = us[:, D_RWKV:2 * D_RWKV]
        v = us[:, 2 * D_RWKV:3 * D_RWKV]
        lane_l = lax.broadcasted_iota(I32, ul.shape, 1)
        t_in = jnp.where(lane_l < DECAY_LORA, jnp.tanh(ul),
                         jnp.where(lane_l < DECAY_LORA + ICLR_LORA, ul, _sigmoid(ul)))
        yield
        up = _mm(t_in, wup_ref[...])
        yield
        z = w0_ref[...] + up[:, 0:D_RWKV]
        lw = (-math.exp(-0.5)) * _sigmoid(z)
        a = _sigmoid(a0_ref[...] + up[:, D_RWKV:2 * D_RWKV])
        g = up[:, 2 * D_RWKV:3 * D_RWKV]
        kk = k * kk_ref[...]
        k2 = k * (1.0 + (a - 1.0) * ka_ref[...])
        yield
        ss = _segsum(kk * kk, ones64)
        bonus_sum = _segsum(r * k2 * rk_ref[...], ones64)
        yield
        kk = kk * lax.rsqrt(jnp.maximum(ss, 1e-24))
        r_ref[0, rows, :] = r
        lw_ref[0, rows, :] = lw
        k_ref[0, rows, :] = k2
        v_ref[0, rows, :] = v
        kkn_ref[0, rows, :] = kk
        akk_ref[0, rows, :] = a * kk
        g_ref[0, rows, :] = g
        bonus_ref[0, rows, :] = bonus_sum * v
        yield

        q_lat = u_mla[:, 0:Q_LORA_RANK]
        kv_lat = u_mla[:, Q_LORA_RANK:Q_LORA_RANK + KV_LORA_RANK]
        kpe_tile = u_mla[:, Q_LORA_RANK + KV_LORA_RANK:]
        qn = q_lat * lax.rsqrt(jnp.mean(q_lat * q_lat, axis=-1, keepdims=True) + NORM_EPS) * qan_ref[...]
        kvn = kv_lat * lax.rsqrt(jnp.mean(kv_lat * kv_lat, axis=-1, keepdims=True) + NORM_EPS) * kvan_ref[...]
        kvb = kvn.astype(BF16)
        yield
        q_raw = _mm(qn, wqb_ref[...])
        k_raw = _dot(kvb, wkb_ref[...])
        v_pad = _dot(kvb, wvb_ref[...])
        yield
        kpe_h = pltpu.roll(kpe_tile, QK_NOPE_DIM, 1)
        cos_t, s1, s2 = rope_tables(s)

        def tables(gain, scale):
            g_s = gain * scale
            return (cos_t * g_s, s1 * pltpu.roll(g_s, HEAD_PAD - half, 1),
                    s2 * pltpu.roll(g_s, half, 1))

        def norm_rope(xh, tabs):
            c_g, s1_g, s2_g = tabs
            ssq = jnp.sum(xh * xh, axis=-1, keepdims=True) * (1.0 / QK_HEAD_DIM)
            rot = xh * c_g + pltpu.roll(xh, HEAD_PAD - half, 1) * s1_g + pltpu.roll(xh, half, 1) * s2_g
            return rot * lax.rsqrt(ssq + NORM_EPS)

        q_tabs = tables(qn_ref[...], ATTN_Q_SCALE)
        k_tabs = tables(kn_ref[...], 1.0)
        vout_ref[0, rows, :] = v_pad.astype(BF16)
        yield
        for hh in range(MLA_HEADS):
            sl = slice(hh * HEAD_PAD, (hh + 1) * HEAD_PAD)
            q_ref[0, rows, sl] = norm_rope(q_raw[:, sl], q_tabs).astype(BF16)
            kout_ref[0, rows, sl] = norm_rope(k_raw[:, sl] + kpe_h, k_tabs).astype(BF16)
            yield

    def rope_tables(s):
        ang_t = invf_ref[...] * pos_ref[0, :, s * ts:(s + 1) * ts].astype(F32)
        frow = lax.broadcasted_iota(I32, (half, HEAD_PAD), 0)
        flane = lax.broadcasted_iota(I32, (half, HEAD_PAD), 1)
        at_x1 = flane == frow + QK_NOPE_DIM
        at_x2 = flane == frow + QK_NOPE_DIM + half
        e_cos = jnp.where(at_x1 | at_x2, 1.0, 0.0).astype(BF16)
        e_sin = jnp.concatenate([jnp.where(at_x1, -1.0, 0.0), jnp.where(at_x2, 1.0, 0.0)],
                                axis=1).astype(BF16)
        lane = lax.broadcasted_iota(I32, (1, HEAD_PAD), 1)
        off_rope = jnp.where((lane >= QK_NOPE_DIM) & (lane < QK_HEAD_DIM), 0.0, 1.0)
        cos_t = _mm_exact_rhs(jnp.cos(ang_t), e_cos, TN) + off_rope
        sin2 = _mm_exact_rhs(jnp.sin(ang_t), e_sin, TN)
        return cos_t, sin2[:, :HEAD_PAD], sin2[:, HEAD_PAD:]

    live = [sub_tile(s) for s in range(PRE_SUBTILES)]
    while live:
        live = [gen for gen in live if next(gen, "done") != "done"]


def _pad_heads(w, n_heads, width):
    kdim = w.shape[0]
    w = w.reshape(kdim, n_heads, width)
    w = jnp.pad(w, ((0, 0), (0, 0), (0, HEAD_PAD - width)))
    return w.reshape(kdim, n_heads * HEAD_PAD)


def _pre_call(x, mod3, positions, norm_mix, w_in, rwkv_mu, decay_w0, decay_up, iclr_a0, iclr_up,
              gate_up, k_k, k_a, r_k, q_a_norm, w_q_b, kv_a_norm, w_kv_b, q_norm, k_norm, tm):
    B, T, D = x.shape
    n_rkv = 3 * D_RWKV
    n_lora = DECAY_LORA + ICLR_LORA + GATE_LORA
    LORA_PAD = 256
    MLA_PAD = 512
    n_mla = Q_LORA_RANK + KV_LORA_RANK + QK_ROPE_DIM
    w_rkv = w_in[:, :n_rkv].astype(BF16)
    w_lora = jnp.pad(w_in[:, n_rkv:n_rkv + n_lora], ((0, 0), (0, LORA_PAD - n_lora))).astype(BF16)
    w_mla = jnp.pad(w_in[:, n_rkv + n_lora:], ((0, 0), (0, MLA_PAD - n_mla))).astype(BF16)
    mu_rkv = rwkv_mu[:n_rkv].reshape(1, n_rkv)
    mu_lora = jnp.pad(rwkv_mu[n_rkv:], (0, LORA_PAD - n_lora)).reshape(1, LORA_PAD)
    w_up = jnp.zeros((LORA_PAD, n_rkv), F32)
    w_up = w_up.at[0:DECAY_LORA, 0:D_RWKV].set(decay_up)
    w_up = w_up.at[DECAY_LORA:DECAY_LORA + ICLR_LORA, D_RWKV:2 * D_RWKV].set(iclr_up)
    w_up = w_up.at[DECAY_LORA + ICLR_LORA:n_lora, 2 * D_RWKV:].set(gate_up)
    w_up = w_up.astype(BF16)
    w_qb = _pad_heads(w_q_b, MLA_HEADS, QK_HEAD_DIM).astype(BF16)
    w_kv3 = w_kv_b.reshape(KV_LORA_RANK, MLA_HEADS, QK_NOPE_DIM + V_HEAD_DIM)
    w_kb = _pad_heads(w_kv3[:, :, :QK_NOPE_DIM].reshape(KV_LORA_RANK, -1), MLA_HEADS, QK_NOPE_DIM).astype(BF16)
    w_vb = _pad_heads(w_kv3[:, :, QK_NOPE_DIM:].reshape(KV_LORA_RANK, -1), MLA_HEADS, V_HEAD_DIM).astype(BF16)
    qn_pad = jnp.pad(q_norm, (0, HEAD_PAD - QK_HEAD_DIM)).reshape(1, HEAD_PAD)
    kn_pad = jnp.pad(k_norm, (0, HEAD_PAD - QK_HEAD_DIM)).reshape(1, HEAD_PAD)
    inv_freq = ROPE_THETA ** (-jnp.arange(0, QK_ROPE_DIM, 2, dtype=F32) / QK_ROPE_DIM)
    invf = inv_freq.reshape(QK_ROPE_DIM // 2, 1)
    pos3 = positions.reshape(B, 1, T)
    HP = MLA_HEADS * HEAD_PAD

    row = lambda n: pl.BlockSpec((1, n), lambda b, t: (0, 0))
    full = lambda a: pl.BlockSpec(a.shape, lambda b, t: (0,) * a.ndim)
    tok = lambda n: pl.BlockSpec((1, tm, n), lambda b, t: (b, t, 0))
    outs = ([jax.ShapeDtypeStruct((B, T, D_RWKV), F32)] * 8
            + [jax.ShapeDtypeStruct((B, T, HP), BF16)] * 3)
    return pl.pallas_call(
        _pre_kernel,
        grid=(B, T // tm),
        in_specs=[tok(D),
                  pl.BlockSpec((1, 6, D), lambda b, t: (b, 0, 0)),
                  pl.BlockSpec((1, 1, tm), lambda b, t: (b, 0, t)),
                  row(D), full(w_rkv), full(w_lora), full(w_mla),
                  row(n_rkv), row(LORA_PAD), full(w_up), row(D_RWKV), row(D_RWKV),
                  row(D_RWKV), row(D_RWKV), row(D_RWKV),
                  row(Q_LORA_RANK), full(w_qb), row(KV_LORA_RANK), full(w_kb), full(w_vb),
                  row(HEAD_PAD), row(HEAD_PAD), full(invf)],
        out_specs=[tok(D_RWKV)] * 8 + [tok(HP)] * 3,
        out_shape=outs,
        scratch_shapes=[pltpu.VMEM((1, n_rkv), F32), pltpu.VMEM((1, LORA_PAD), F32)],
        compiler_params=_cparams("arbitrary", "arbitrary"),
        name="pre",
    )(x, mod3, pos3, norm_mix.reshape(1, D), w_rkv, w_lora, w_mla, mu_rkv, mu_lora, w_up,
      decay_w0.reshape(1, -1), iclr_a0.reshape(1, -1), k_k.reshape(1, -1), k_a.reshape(1, -1),
      r_k.reshape(1, -1), q_a_norm.reshape(1, -1), w_qb, kv_a_norm.reshape(1, -1), w_kb, w_vb,
      qn_pad, kn_pad, invf)


def _scan_kernel(r_ref, lw_ref, k_ref, v_ref, kk_ref, akk_ref, y_ref, state):
    C = SCAN_CHUNK
    n_chunks = r_ref.shape[1] // C
    n_pairs = r_ref.shape[2] // LANES

    @pl.when(pl.program_id(1) == 0)
    def _():
        state[...] = jnp.zeros_like(state)

    ri = lax.broadcasted_iota(I32, (C, C), 0)
    ci = lax.broadcasted_iota(I32, (C, C), 1)
    tri_incl = jnp.where(ci <= ri, 1.0, 0.0).astype(BF16)
    r2 = lax.broadcasted_iota(I32, (2 * C, 2 * C), 0)
    c2 = lax.broadcasted_iota(I32, (2 * C, 2 * C), 1)
    same = (r2 >= C) == (c2 >= C)
    strict = same & (c2 < r2)
    incl = same & (c2 <= r2)
    eye = jnp.where(c2 == r2, 1.0, 0.0)
    head0 = lax.broadcasted_iota(I32, (C, LANES), 1) < RWKV_HEAD_DIM

    def stack2(a):
        return jnp.concatenate([jnp.where(head0, a, 0.0), jnp.where(head0, 0.0, a)], axis=0)

    C2 = 2 * C
    cat0 = lambda *a: jnp.concatenate(a, axis=0)
    cat1 = lambda *a: jnp.concatenate(a, axis=1)

    items = []
    for c in range(n_chunks):
        rows = slice(c * C, (c + 1) * C)
        lw = lw_ref[0, rows, :]
        cum = _mm_exact_rhs_left(tri_incl, lw)
        cum_end = cum[C - 1:C, :]
        w_end = jnp.exp(cum_end)
        e_pos = jnp.exp(cum)
        e_neg = jnp.exp(-cum)
        e_prev = jnp.exp(cum - lw)
        e_end = jnp.exp(cum_end - cum)
        kk = kk_ref[0, rows, :]
        k2 = k_ref[0, rows, :]
        pneg = -akk_ref[0, rows, :]
        vv = v_ref[0, rows, :]
        rt = r_ref[0, rows, :] * e_pos
        bt = kk * e_prev
        pt = pneg * e_neg
        kt = k2 * e_neg
        ph = pneg * e_end
        kh = k2 * e_end
        for pp in range(n_pairs):
            sl = slice(pp * LANES, (pp + 1) * LANES)
            items.append(dict(
                c=c, p=pp, w_end=w_end[:, sl],
                bt2=stack2(bt[:, sl]).astype(BF16), rt2=stack2(rt[:, sl]).astype(BF16),
                pk2=cat0(stack2(pt[:, sl]), stack2(kt[:, sl])).astype(BF16),
                phkh2=cat0(stack2(ph[:, sl]), stack2(kh[:, sl])).astype(BF16),
                v2=stack2(vv[:, sl])))
    for it in items:
        ab = _dot(cat0(it['bt2'], it['rt2']), it['pk2'], NT)
        it['a_ab'] = jnp.where(strict, ab[:C2, :C2], 0.0)
        it['a_ak'] = jnp.where(strict, ab[:C2, C2:], 0.0).astype(BF16)
        it['b_rpk'] = cat1(jnp.where(incl, ab[C2:, :C2], 0.0), jnp.where(incl, ab[C2:, C2:], 0.0)).astype(BF16)
        it['tinv'] = eye + it['a_ab']
    for it in items:
        it['apow'] = _mm(it['a_ab'], it['a_ab'])
    for _ in range(int(math.log2(C)) - 1):
        for it in items:
            both = _mm(cat0(it['apow'], it['tinv']), it['apow'])
            it['apow'] = both[:C2]
            it['tinv'] = it['tinv'] + both[C2:]
    for it in items:
        it['akv'] = _dot(it['a_ak'], it['v2'].astype(BF16))
    for it in items:
        tt = _dot(it['tinv'].astype(BF16), cat1(it['bt2'], it['akv'].astype(BF16)))
        it['tb_rt'] = cat0(tt[:, :LANES].astype(BF16), it['rt2'])
        it['tav'] = tt[:, LANES:]
    for it in items:
        pp = it['p']
        s0 = state[pp]
        top = _dot(it['tb_rt'], s0.astype(BF16), NT)
        u2 = top[:C2] + it['tav']
        uv = cat0(u2, it['v2']).astype(BF16)
        y2 = top[C2:] + _dot(it['b_rpk'], uv)
        state[pp] = s0 * it['w_end'] + _dot(uv, it['phkh2'], TN)
        y_ref[0, it['c'] * C:(it['c'] + 1) * C, pp * LANES:(pp + 1) * LANES] = y2[0:C] + y2[C:C2]


def _mm_exact_rhs_left(b_exact_bf16, a):
    h, m, l = _split3(a)
    return _dot(b_exact_bf16, h) + (_dot(b_exact_bf16, m) + _dot(b_exact_bf16, l))


def _scan_call(r, lw, k2, v, kk, akk):
    B, T, W = r.shape
    tb = _tile(T, SCAN_BLOCK)
    spec = pl.BlockSpec((1, tb, W), lambda b, c: (b, c, 0))
    return pl.pallas_call(
        _scan_kernel,
        grid=(B, T // tb),
        in_specs=[spec] * 6,
        out_specs=spec,
        out_shape=jax.ShapeDtypeStruct((B, T, W), F32),
        scratch_shapes=[pltpu.VMEM((W // LANES, 2 * RWKV_HEAD_DIM, LANES), F32)],
        compiler_params=_cparams("arbitrary", "arbitrary"),
        name="scan",
    )(r, lw, k2, v, kk, akk)


def _attn_kernel(q_ref, k_ref, v_ref, o_ref):
    T = q_ref.shape[1]
    tq = min(T, ATTN_TILE)
    row = lax.broadcasted_iota(I32, (tq, tq), 0)
    col = lax.broadcasted_iota(I32, (tq, tq), 1)
    causal = col <= row

    def update(q, kt, vt, carry, mask):
        m_old, l_old, acc = carry
        s = _dot(q, kt, NT)
        if mask:
            s = jnp.where(causal, s, NEG_INF)
        m_new = jnp.maximum(m_old, jnp.max(s, axis=-1, keepdims=True))
        alpha = jnp.exp2(m_old - m_new)
        p = jnp.exp2(s - m_new)
        l_new = alpha * l_old + jnp.sum(p, axis=-1, keepdims=True)
        acc = alpha * acc + _dot(p.astype(BF16), vt)
        return m_new, l_new, acc

    for qi in range(T // tq):
        q = q_ref[0, qi * tq:(qi + 1) * tq, :]
        carry = (jnp.full((tq, 1), NEG_INF, F32), jnp.zeros((tq, 1), F32),
                 jnp.zeros((tq, HEAD_PAD), F32))

        def body(ki, carry, q=q):
            rows = pl.ds(pl.multiple_of(ki * tq, tq), tq)
            return update(q, k_ref[0, rows, :], v_ref[0, rows, :], carry, False)

        carry = lax.fori_loop(0, qi, body, carry, unroll=True)
        diag = slice(qi * tq, (qi + 1) * tq)
        _, l_fin, acc = update(q, k_ref[0, diag, :], v_ref[0, diag, :], carry, True)
        o_ref[0, diag, :] = (acc / l_fin).astype(o_ref.dtype)


def _attn_call(q, k, v):
    B, T, HP = q.shape
    spec = pl.BlockSpec((1, T, HEAD_PAD), lambda b, h: (b, 0, h))
    return pl.pallas_call(
        _attn_kernel,
        grid=(B, MLA_HEADS),
        in_specs=[spec, spec, spec],
        out_specs=spec,
        out_shape=jax.ShapeDtypeStruct((B, T, HP), BF16),
        compiler_params=_cparams("arbitrary", "arbitrary"),
        name="attn",
    )(q, k, v)


def _post_kernel(y_ref, bonus_ref, g_ref, o_ref, x_ref, mod_ref, lnw_ref, lnb_ref,
                 wo_r_ref, wo_m_ref, nffn_ref, x1_ref, h2_ref, h2t_ref):
    y = y_ref[0]
    ones64 = _seg_ones(D_RWKV, RWKV_HEAD_DIM)
    mean = _segsum(y, ones64) * (1.0 / RWKV_HEAD_DIM)
    yc = y - mean
    var = _segsum(yc * yc, ones64) * (1.0 / RWKV_HEAD_DIM)
    yn = yc * lax.rsqrt(var + GN_EPS) * lnw_ref[...] + lnb_ref[...]
    yr = (yn + bonus_ref[0]) * g_ref[0]
    mix = _mm(yr, wo_r_ref[...]) + _dot(o_ref[0], wo_m_ref[...])
    g_a = mod_ref[0, 2:3, :]
    sh_f = mod_ref[0, 3:4, :]
    sc_f = mod_ref[0, 4:5, :]
    x1 = x_ref[0] + g_a * mix
    x1_ref[0] = x1
    ms = jnp.mean(x1 * x1, axis=-1, keepdims=True)
    h2 = x1 * lax.rsqrt(ms + NORM_EPS) * nffn_ref[...] * (1.0 + sc_f) + sh_f
    h2_ref[0] = h2
    tm = h2.shape[0]
    for s in range(SUBLANES):
        h2t_ref[0, pl.ds(s, tm, stride=SUBLANES), :] = h2[:, s * LANES:(s + 1) * LANES]


def _post_call(y, bonus, g, o_pad, x, mod3, ln_w, ln_b, w_out, norm_ffn, tm):
    B, T, D = x.shape
    HP = MLA_HEADS * HEAD_PAD
    wo_r = w_out[:D_RWKV].astype(BF16)
    wo_m = jnp.pad(w_out[D_RWKV:].reshape(MLA_HEADS, V_HEAD_DIM, D),
                   ((0, 0), (0, HEAD_PAD - V_HEAD_DIM), (0, 0))).reshape(HP, D).astype(BF16)
    tok = lambda n: pl.BlockSpec((1, tm, n), lambda b, t: (b, t, 0))
    row = lambda n: pl.BlockSpec((1, n), lambda b, t: (0, 0))
    full = lambda a: pl.BlockSpec(a.shape, lambda b, t: (0,) * a.ndim)
    return pl.pallas_call(
        _post_kernel,
        grid=(B, T // tm),
        in_specs=[tok(D_RWKV), tok(D_RWKV), tok(D_RWKV), tok(HP), tok(D),
                  pl.BlockSpec((1, 6, D), lambda b, t: (b, 0, 0)),
                  row(D_RWKV), row(D_RWKV), full(wo_r), full(wo_m), row(D)],
        out_specs=[tok(D), tok(D),
                   pl.BlockSpec((1, tm * SUBLANES, LANES), lambda b, t: (b, t, 0))],
        out_shape=[jax.ShapeDtypeStruct((B, T, D), F32)] * 2
        + [jax.ShapeDtypeStruct((B, T * SUBLANES, LANES), F32)],
        compiler_params=_cparams("arbitrary", "arbitrary"),
        name="post",
    )(y, bonus, g, o_pad, x, mod3, ln_w.reshape(1, -1), ln_b.reshape(1, -1), wo_r, wo_m,
      norm_ffn.reshape(1, D))


def _first_index(mask, iota, size, axis):
    return jnp.min(jnp.where(mask, iota, size), axis=axis, keepdims=True)


def _route_kernel(h_ref, wr_ref, bias_ref, e_ref, w_ref, rank_ref, cnt_ref, base):
    tr = h_ref.shape[0]
    E = N_EXPERTS

    @pl.when(pl.program_id(0) == 0)
    def _():
        base[...] = jnp.zeros_like(base)

    logits = _mm3(wr_ref[...], h_ref[...], NT)
    scores = _sigmoid(logits)
    sel = scores + bias_ref[...]
    iota_g = lax.broadcasted_iota(I32, (GROUP_SIZE, tr), 0)
    gs_rows = []
    for gi in range(N_GROUPS):
        blk = sel[gi * GROUP_SIZE:(gi + 1) * GROUP_SIZE, :]
        m1 = jnp.max(blk, axis=0, keepdims=True)
        i1 = _first_index(blk == m1, iota_g, GROUP_SIZE, 0)
        m2 = jnp.max(jnp.where(iota_g == i1, NEG_INF, blk), axis=0, keepdims=True)
        gs_rows.append(m1 + m2)
    gs = jnp.concatenate(gs_rows, axis=0)
    iota8 = lax.broadcasted_iota(I32, (N_GROUPS, tr), 0)
    gmask = jnp.zeros((N_GROUPS, tr), jnp.bool_)
    for _ in range(TOPK_GROUPS):
        mg = jnp.max(gs, axis=0, keepdims=True)
        ig = _first_index(gs == mg, iota8, N_GROUPS, 0)
        hit = iota8 == ig
        gmask = gmask | hit
        gs = jnp.where(hit, NEG_INF, gs)
    msel = jnp.concatenate(
        [jnp.where(gmask[gi:gi + 1, :], sel[gi * GROUP_SIZE:(gi + 1) * GROUP_SIZE, :], NEG_INF)
         for gi in range(N_GROUPS)], axis=0)
    iota_e = lax.broadcasted_iota(I32, (E, tr), 0)
    e_rows, w_rows = [], []
    onehot = jnp.zeros((E, tr), F32)
    for _ in range(TOP_K):
        mv = jnp.max(msel, axis=0, keepdims=True)
        ie = _first_index(msel == mv, iota_e, E, 0)
        hit = iota_e == ie
        e_rows.append(ie)
        w_rows.append(jnp.sum(jnp.where(hit, scores, 0.0), axis=0, keepdims=True))
        onehot = jnp.where(hit, 1.0, onehot)
        msel = jnp.where(hit, NEG_INF, msel)
    top_e = jnp.concatenate(e_rows, axis=0)
    wts = jnp.concatenate(w_rows, axis=0)
    wts = wts / jnp.sum(wts, axis=0, keepdims=True) * ROUTED_SCALE
    ti = lax.broadcasted_iota(I32, (tr, tr), 0)
    tj = lax.broadcasted_iota(I32, (tr, tr), 1)
    upper = jnp.where(ti < tj, 1.0, 0.0).astype(BF16)
    pos = _dot(onehot.astype(BF16), upper) + base[...]
    rank_rows = [jnp.sum(jnp.where(iota_e == e_rows[j], pos, 0.0), axis=0, keepdims=True)
                 for j in range(TOP_K)]
    base[...] = base[...] + jnp.sum(onehot, axis=1, keepdims=True)
    e_ref[...] = top_e
    w_ref[...] = wts
    rank_ref[...] = jnp.concatenate(rank_rows, axis=0).astype(I32)
    cnt_ref[...] = base[...].astype(I32)


def _route_call(h2, w_router, router_bias, tr):
    N, D = h2.shape
    E = N_EXPERTS
    out_kn = pl.BlockSpec((TOP_K, tr), lambda i: (0, i))
    return pl.pallas_call(
        _route_kernel,
        grid=(N // tr,),
        in_specs=[pl.BlockSpec((tr, D), lambda i: (i, 0)),
                  pl.BlockSpec((E, D), lambda i: (0, 0)),
                  pl.BlockSpec((E, 1), lambda i: (0, 0))],
        out_specs=[out_kn, out_kn, out_kn, pl.BlockSpec((E, 1), lambda i: (0, 0))],
        out_shape=[jax.ShapeDtypeStruct((TOP_K, N), I32), jax.ShapeDtypeStruct((TOP_K, N), F32),
                   jax.ShapeDtypeStruct((TOP_K, N), I32), jax.ShapeDtypeStruct((E, 1), I32)],
        scratch_shapes=[pltpu.VMEM((E, 1), F32)],
        compiler_params=_cparams("arbitrary"),
        name="route",
    )(h2, w_router.T, router_bias.reshape(E, 1))


def _dest_kernel(e_ref, rank_ref, start_ref, d_ref):
    tr = e_ref.shape[1]
    iota_e = lax.broadcasted_iota(I32, (N_EXPERTS, tr), 0)
    starts = start_ref[...]
    rows = [jnp.sum(jnp.where(iota_e == e_ref[j:j + 1, :], starts, 0), axis=0, keepdims=True)
            for j in range(TOP_K)]
    d_ref[...] = jnp.concatenate(rows, axis=0) + rank_ref[...]


def _dest_call(top_e, rank, pad_starts, tr):
    K, N = top_e.shape
    spec = pl.BlockSpec((K, tr), lambda i: (0, i))
    return pl.pallas_call(
        _dest_kernel,
        grid=(N // tr,),
        in_specs=[spec, spec, pl.BlockSpec((N_EXPERTS, 1), lambda i: (0, 0))],
        out_specs=spec,
        out_shape=jax.ShapeDtypeStruct((K, N), I32),
        compiler_params=_cparams("arbitrary"),
        name="dest",
    )(top_e, rank, pad_starts.reshape(N_EXPERTS, 1))


def _sc_dispatch_call(h2t, dest3, n_rows):
    N = h2t.shape[0]
    info = plsc.get_sparse_core_info()
    n_workers = info.num_cores * info.num_subcores
    n_chunks = N // (SC_CHUNK * n_workers)
    assert n_chunks * SC_CHUNK * n_workers == N
    mesh = plsc.VectorSubcoreMesh(core_axis_name="c", subcore_axis_name="s")

    @functools.partial(
        pl.kernel, mesh=mesh,
        out_type=(jax.ShapeDtypeStruct((n_rows,) + h2t.shape[1:], h2t.dtype),
                  jax.ShapeDtypeStruct((n_rows, LANES), I32)),
        scratch_types=[pltpu.VMEM((TOP_K, SC_CHUNK), I32),
                       pltpu.VMEM((SC_CHUNK,) + h2t.shape[1:], h2t.dtype),
                       pltpu.VMEM((SC_CHUNK, LANES), I32)],
        name="sc_dispatch",
    )
    def scatter_rows(h_hbm, dest_hbm, xs_hbm, tag_hbm, idx_v, rows_v, tag_v):
        wid = lax.axis_index("s") * info.num_cores + lax.axis_index("c")
        zeros = jnp.zeros((info.num_lanes,), I32)

        @pl.loop(0, SC_CHUNK)
        def _(r):
            for l0 in range(0, LANES, info.num_lanes):
                tag_v[r, pl.ds(l0, info.num_lanes)] = zeros

        @pl.loop(0, n_chunks)
        def _(c):
            chunk = wid * n_chunks + c
            base = chunk * SC_CHUNK
            pltpu.sync_copy(dest_hbm.at[chunk], idx_v)
            pltpu.sync_copy(h_hbm.at[pl.ds(base, SC_CHUNK)], rows_v)
            for j in range(TOP_K):
                pltpu.sync_copy(rows_v, xs_hbm.at[idx_v.at[j]])

                @pl.loop(0, SC_CHUNK)
                def _(r):
                    tag_v[r, pl.ds(0, info.num_lanes)] = zeros + ((base + r) * TOP_K + j)

                pltpu.sync_copy(tag_v, tag_hbm.at[idx_v.at[j]])

    return scatter_rows(h2t, dest3)


def _moe_kernel(be_ref, nu_ref, nv_ref, xs_ref, xtag_ref, wgu_ref, wdn_ref, yt_hbm, wgu_bf, wdn_bf,
                ybuf, tag_v, tag_s, sem_tag, sem_rows):
    i = pl.program_id(0)
    n_steps = pl.num_programs(0)
    cur = i % 2
    n_real = yt_hbm.shape[0] - 2 * MOE_ROWS

    def row_copy(buf, r, tag):
        rows = pl.ds(r * SUBLANES, SUBLANES)
        return pltpu.make_async_copy(ybuf.at[buf, rows, :], yt_hbm.at[tag], sem_rows.at[buf])

    n_used = nu_ref[0]

    def wait_sent(buf):
        pltpu.make_async_copy(ybuf.at[buf], ybuf.at[buf], sem_rows.at[buf]).wait()

    def send_prev(prv):
        for r in range(MOE_ROWS):
            row_copy(prv, r, tag_s[prv, r]).start(priority=r % 2)

    for par in range(2):
        @pl.when((cur == par) & (i >= 2) & (i - 2 < n_used))
        def _():
            wait_sent(par)

    i_blk = jnp.minimum(i, n_steps - 2)
    @pl.when((i < n_used) & ((i == 0) | (be_ref[i_blk] != be_ref[jnp.maximum(i_blk - 1, 0)])))
    def _():
        wgu_bf[...] = wgu_ref[0].astype(BF16)
        wdn_bf[...] = wdn_ref[0].astype(BF16)

    def compute(cur):
        row = lax.broadcasted_iota(I32, (MOE_ROWS, LANES), 0)
        tags = jnp.where(row < nv_ref[i_blk], xtag_ref[...], n_real + cur * MOE_ROWS + row)
        tag_v[...] = tags.astype(F32).T[0:1, :].astype(I32)
        cp = pltpu.make_async_copy(tag_v, tag_s.at[pl.ds(cur, 1)], sem_tag)
        cp.start()
        xb = jnp.concatenate([xs_ref[pl.ds(s, MOE_ROWS, stride=SUBLANES), :]
                              for s in range(SUBLANES)], axis=1).astype(BF16)
        gu = _dot(xb, wgu_bf[...])
        act = _silu(gu[:, :D_EXPERT]) * gu[:, D_EXPERT:]
        y = _dot(act.astype(BF16), wdn_bf[...])
        for s in range(SUBLANES):
            ybuf[cur, pl.ds(s, MOE_ROWS, stride=SUBLANES), :] = y[:, s * LANES:(s + 1) * LANES]
        cp.wait()

    @pl.when((i == 0) & (i < n_used))
    def _():
        compute(0)

    for par in range(2):
        @pl.when((cur == par) & (i > 0) & (i < n_used))
        def _():
            send_prev(1 - par)
            compute(par)

        @pl.when((cur == par) & (i > 0) & (i == n_used))
        def _():
            send_prev(1 - par)

        @pl.when((cur == par) & (i == n_steps - 1) & (i - 1 < n_used))
        def _():
            wait_sent(1 - par)


def _moe_call(block_expert, n_used, n_valid, xs, xtag, w_gu, w_dn, n_out_rows):
    P = xs.shape[0] // SUBLANES
    D = SUBLANES * LANES
    nb = P // MOE_ROWS
    blk = lambda i, be, nu, nv: (jnp.minimum(i, nu[0] - 1), 0)
    wblk = lambda i, be, nu, nv: (be[jnp.minimum(i, nu[0] - 1)], 0, 0)
    return pl.pallas_call(
        _moe_kernel,
        grid_spec=pltpu.PrefetchScalarGridSpec(
            num_scalar_prefetch=3,
            grid=(nb + 1,),
            in_specs=[pl.BlockSpec((MOE_ROWS * SUBLANES, LANES), blk),
                      pl.BlockSpec((MOE_ROWS, LANES), blk),
                      pl.BlockSpec((1, D, 2 * D_EXPERT), wblk),
                      pl.BlockSpec((1, D_EXPERT, D), wblk)],
            out_specs=pl.BlockSpec(memory_space=pl.ANY),
            scratch_shapes=[pltpu.VMEM((D, 2 * D_EXPERT), BF16), pltpu.VMEM((D_EXPERT, D), BF16),
                            pltpu.VMEM((2, MOE_ROWS * SUBLANES, LANES), F32),
                            pltpu.VMEM((1, MOE_ROWS), I32), pltpu.SMEM((2, MOE_ROWS), I32),
                            pltpu.SemaphoreType.DMA, pltpu.SemaphoreType.DMA((2,))]),
        out_shape=jax.ShapeDtypeStruct((n_out_rows, SUBLANES, LANES), F32),
        compiler_params=_cparams("arbitrary"),
        name="moe",
    )(block_expert, n_used, n_valid, xs, xtag, w_gu, w_dn)


def _combine_kernel(w_hbm, yt_ref, h_ref, x1_ref, mod_ref, wsg_ref, wsd_ref, o_ref,
                    wts, routed, sem_w):
    i = pl.program_id(0)
    tc = h_ref.shape[0]
    cp_w = pltpu.make_async_copy(w_hbm.at[i], wts, sem_w)
    cp_w.start()
    gu = _mm(h_ref[...], wsg_ref[...])
    act = _silu(gu[:, :D_EXPERT]) * gu[:, D_EXPERT:]
    ffn = _mm(act, wsd_ref[...])
    cp_w.wait()

    def wsum(tt, carry):
        for u in range(SUBLANES):
            t = tt * SUBLANES + u
            first = pl.multiple_of(t * (TOP_K * SUBLANES), SUBLANES)
            acc = yt_ref[pl.ds(first, SUBLANES), :] * wts[0, t]
            for j in range(1, TOP_K):
                acc = acc + yt_ref[pl.ds(first + j * SUBLANES, SUBLANES), :] * wts[j, t]
            routed[pl.ds(pl.multiple_of(t * SUBLANES, SUBLANES), SUBLANES), :] = acc
        return carry

    lax.fori_loop(0, tc // SUBLANES, wsum, 0)
    routed2d = jnp.concatenate([routed[pl.ds(s, tc, stride=SUBLANES), :] for s in range(SUBLANES)],
                               axis=1)
    g_f = mod_ref[0, 5:6, :]
    o_ref[...] = x1_ref[...] + g_f * (ffn + routed2d)


def _combine_call(w3, yt, h2, x1, mod3, w_sh_gu, w_sh_dn, tokens_per_batch, tc):
    N, D = h2.shape
    tiles_per_batch = tokens_per_batch // tc
    tok = pl.BlockSpec((tc, D), lambda i: (i, 0))
    wsg = w_sh_gu.astype(BF16)
    wsd = w_sh_dn.astype(BF16)
    return pl.pallas_call(
        _combine_kernel,
        grid=(N // tc,),
        in_specs=[pl.BlockSpec(memory_space=pl.ANY),
                  pl.BlockSpec((tc * TOP_K * SUBLANES, LANES), lambda i: (i, 0)),
                  tok, tok,
                  pl.BlockSpec((1, 6, D), lambda i: (i // tiles_per_batch, 0, 0)),
                  pl.BlockSpec(wsg.shape, lambda i: (0, 0)),
                  pl.BlockSpec(wsd.shape, lambda i: (0, 0))],
        out_specs=tok,
        out_shape=jax.ShapeDtypeStruct((N, D), F32),
        scratch_shapes=[pltpu.SMEM((TOP_K, tc), F32), pltpu.VMEM((tc * SUBLANES, LANES), F32),
                        pltpu.SemaphoreType.DMA],
        compiler_params=_cparams("arbitrary"),
        name="combine",
    )(w3, yt, h2, x1, mod3, wsg, wsd)


def _tile(n, pref):
    t = min(n, pref)
    assert n % t == 0, (n, t)
    return t


def _layer(x, mod3, positions, norm_mix, w_in, rwkv_mu, decay_w0, decay_up, iclr_a0, iclr_up,
           gate_up, rwkv_k_k, rwkv_k_a, rwkv_r_k, ln_x_w, ln_x_b, q_a_norm, w_q_b, kv_a_norm,
           w_kv_b, q_norm, k_norm, w_out, norm_ffn, w_router, router_bias, w_e_gate_up, w_e_down,
           w_sh_gate_up, w_sh_down):
    B, T, D = x.shape
    N = B * T
    assert T % SCAN_CHUNK == 0
    (r, lw, k2, v, kk, akk, g, bonus, q_pad, k_pad, v_pad) = _pre_call(
        x, mod3, positions, norm_mix, w_in, rwkv_mu, decay_w0, decay_up, iclr_a0, iclr_up,
        gate_up, rwkv_k_k, rwkv_k_a, rwkv_r_k, q_a_norm, w_q_b, kv_a_norm, w_kv_b, q_norm, k_norm,
        tm=_tile(T, 512))
    y = _scan_call(r, lw, k2, v, kk, akk)
    o_pad = _attn_call(q_pad, k_pad, v_pad)
    x1, h2, h2t = _post_call(y, bonus, g, o_pad, x, mod3, ln_x_w, ln_x_b, w_out, norm_ffn,
                             tm=_tile(T, 512))
    x1 = x1.reshape(N, D)
    h2 = h2.reshape(N, D)
    h2t = h2t.reshape(N, D // LANES, LANES)

    tr = _tile(N, 512)
    top_e, wts, rank, counts = _route_call(h2, w_router, router_bias, tr)
    counts = counts.reshape(N_EXPERTS)
    padded = (counts + MOE_ROWS - 1) // MOE_ROWS * MOE_ROWS
    pad_ends = jnp.cumsum(padded)
    pad_starts = pad_ends - padded
    n_blocks = (N * TOP_K + N_EXPERTS * (MOE_ROWS - 1)) // MOE_ROWS
    block_row0 = jnp.arange(n_blocks + 1, dtype=I32) * MOE_ROWS
    block_expert = jnp.minimum(
        jnp.sum((pad_ends[None, :] <= block_row0[:, None]).astype(I32), axis=1), N_EXPERTS - 1)
    n_used = (pad_ends[-1:] // MOE_ROWS).astype(I32)
    own = block_expert[:, None] == jnp.arange(N_EXPERTS, dtype=I32)[None, :]
    run_end = jnp.sum(jnp.where(own, (pad_starts + counts)[None, :], 0), axis=1)
    n_valid = jnp.clip(run_end - block_row0, 0, MOE_ROWS).astype(I32)
    dest = _dest_call(top_e, rank, pad_starts.astype(I32), tr)

    dest3 = dest.reshape(TOP_K, N // SC_CHUNK, SC_CHUNK).transpose(1, 0, 2)
    n_rows = n_blocks * MOE_ROWS
    xs, xtag = _sc_dispatch_call(h2t, dest3, n_rows)
    yt = _moe_call(block_expert, n_used, n_valid, xs.reshape(n_rows * SUBLANES, LANES), xtag,
                   w_e_gate_up, w_e_down, N * TOP_K + 2 * MOE_ROWS)
    tc = _tile(T, COMBINE_TILE)
    w3 = wts.reshape(TOP_K, N // tc, tc).transpose(1, 0, 2)
    out = _combine_call(w3, yt.reshape(-1, LANES), h2, x1, mod3, w_sh_gate_up,
                        w_sh_down, T, tc)
    return out.reshape(B, T, D)


def kernel(x, c, positions, ada_w, ada_b, norm_mix, w_in, rwkv_mu, decay_w0, decay_up, iclr_a0, iclr_up, gate_up, rwkv_k_k, rwkv_k_a, rwkv_r_k, ln_x_w, ln_x_b, q_a_norm, w_q_b, kv_a_norm, w_kv_b, q_norm, k_norm, w_out, norm_ffn, w_router, router_bias, w_e_gate_up, w_e_down, w_sh_gate_up, w_sh_down):
    B, T, D = x.shape
    depth = ada_w.shape[0]
    for l in range(depth):
        mod3 = _mod_call(c, ada_w[l], ada_b[l]).reshape(B, 6, D)
        x = _layer(x, mod3, positions, norm_mix[l], w_in[l], rwkv_mu[l], decay_w0[l], decay_up[l],
                   iclr_a0[l], iclr_up[l], gate_up[l], rwkv_k_k[l], rwkv_k_a[l], rwkv_r_k[l],
                   ln_x_w[l], ln_x_b[l], q_a_norm[l], w_q_b[l], kv_a_norm[l], w_kv_b[l],
                   q_norm[l], k_norm[l], w_out[l], norm_ffn[l], w_router[l], router_bias[l],
                   w_e_gate_up[l], w_e_down[l], w_sh_gate_up[l], w_sh_down[l])
    return x
```
